```python
import functools
import jax, jax.numpy as jnp
from jax import lax
import numpy as np

D_MODEL = 1024
BATCH = 2
SEQ = 8192
DEPTH = 2
DEC_BATCH = 32
DEC_SEQ = 1
PAST_LEN = 16384
PAGE_SIZE = 128

A_HEADS = 4
A_DK = 128
A_DV = 128
HGRN_CHUNK = 64
B_HEADS = 8
B_KV_HEADS = 2
B_GROUP = B_HEADS // B_KV_HEADS
B_HD = 64
L_CMP = 32
D_CMP = 16
CMP_HID = 64
L_SLC = 64
N_SEL = 16
WINDOW = 512
Q_BLOCK = 128
SEL_FORCE = 1e9
CONV_DIM = D_MODEL
CONV_W = 3
D_FF = 2816
EPS = 1e-6
N_A = (DEPTH + 1) // 2
N_C = DEPTH // 2
AB_MIX = A_HEADS * A_DV + B_HEADS * B_HD
KV_W = 2 * B_KV_HEADS * B_HD
AB_SPLIT = (A_HEADS * A_DK, A_HEADS * A_DK, A_HEADS * A_DV, A_HEADS * A_DV,
            B_HEADS * B_HD, KV_W, KV_W, KV_W, 3 * B_HEADS)
AB_IN = sum(AB_SPLIT)

kernel_name = 'hgrn2_nsa_shortconv_convffn_step'


def rmsnorm(x, w):
    xf = x.astype(jnp.float32)
    y = xf * lax.rsqrt(jnp.mean(xf * xf, axis=-1, keepdims=True) + EPS)
    return (y * w.astype(jnp.float32)).astype(x.dtype)


def causal_dwconv(u, buf, w):
    t = u.shape[1]
    ext = jnp.concatenate([buf.astype(u.dtype), u], axis=1)
    y = ext[:, 0:t] * w[0]
    for j in range(1, CONV_W):
        y = y + ext[:, j:j + t] * w[j]
    return y, ext[:, t:]


def masked_softmax(s, mask):
    s = jnp.where(mask, s, -jnp.inf)
    m = jnp.max(s, axis=-1, keepdims=True)
    m = jnp.where(jnp.isfinite(m), m, 0.0)
    e = jnp.exp(s - m)
    d = jnp.sum(e, axis=-1, keepdims=True)
    return e / jnp.where(d > 0, d, 1.0)


def alibi_slopes():
    h = jnp.arange(1, B_HEADS + 1, dtype=jnp.float32)
    return (2.0 ** (-8.0 * h / B_HEADS)).reshape(B_KV_HEADS, B_GROUP)


def split_ab(proj):
    parts, off = [], 0
    for n in AB_SPLIT:
        parts.append(proj[..., off:off + n])
        off += n
    return parts


def hgrn2_chunk(S, inp):
    q, la, k, v = inp
    A = jnp.cumsum(la, axis=2)
    c = q.shape[2]
    o_inter = jnp.einsum('bhtk,bhkv->bhtv', q * jnp.exp(A), S)
    causal = jnp.tril(jnp.ones((c, c), dtype=bool))
    diff = A[:, :, :, None, :] - A[:, :, None, :, :]
    decay = jnp.exp(jnp.where(causal[None, None, :, :, None], diff, -jnp.inf))
    att = jnp.einsum('bhtk,bhtsk,bhsk->bhts', q, decay, k)
    o = o_inter + jnp.einsum('bhts,bhsv->bhtv', att, v)
    a_end = A[:, :, -1:, :]
    S_new = jnp.exp(a_end[:, :, 0, :])[..., None] * S + jnp.einsum('bhsk,bhsv->bhkv', k * jnp.exp(a_end - A), v)
    return S_new, o


def hgrn2_mixer(qa, fl, ia, ga, lb, norm_w, S0, chunk):
    b, t = qa.shape[:2]

    def heads(z):
        return z.reshape(b, t, A_HEADS, -1).transpose(0, 2, 1, 3).astype(jnp.float32)

    q, v = heads(qa), heads(ia)
    lbh = lb.reshape(A_HEADS, A_DK)[None, :, None, :]
    f = lbh + (1.0 - lbh) * jax.nn.sigmoid(heads(fl))
    la = jnp.log(f)
    k = 1.0 - f
    n = t // chunk

    def to_chunks(z):
        return z.reshape(b, A_HEADS, n, chunk, z.shape[-1]).transpose(2, 0, 1, 3, 4)

    S_fin, o = lax.scan(hgrn2_chunk, S0.astype(jnp.float32),
                        (to_chunks(q), to_chunks(la), to_chunks(k), to_chunks(v)))
    o = o.transpose(1, 0, 3, 2, 4).reshape(b, t, A_HEADS, A_DV)
    o = o * lax.rsqrt(jnp.mean(o * o, axis=-1, keepdims=True) + EPS)
    o = o * norm_w.reshape(A_HEADS, A_DV).astype(jnp.float32)
    o = o * jax.nn.sigmoid(ga.reshape(b, t, A_HEADS, A_DV).astype(jnp.float32))
    return o.reshape(b, t, A_HEADS * A_DV).astype(qa.dtype), S_fin


def nsa_compress(kv, pe, w1, w2):
    b, t = kv.shape[:2]
    n16 = t // D_CMP
    r_n = L_CMP // D_CMP
    nc = n16 - r_n + 1
    ch = kv.reshape(b, n16, D_CMP, 2, B_KV_HEADS, B_HD)
    w1r = w1.reshape(2, L_CMP, B_HD, CMP_HID)
    pre = jnp.einsum('kid,kidm->km', pe, w1r)[None, None, :, None, :]
    for r in range(r_n):
        part = jnp.einsum('bnikhd,kidm->bnkhm', ch, w1r[:, r * D_CMP:(r + 1) * D_CMP])
        pre = pre + part[:, r:r + nc]
    hid = jax.nn.silu(pre)
    return jnp.einsum('bnkhm,kmd->bnkhd', hid, w2)


def shared_attention(qg, qpos, k, v, kpos, mask, slopes):
    s = jnp.einsum('btkgd,bnkd->bkgtn', qg, k).astype(jnp.float32) * (B_HD ** -0.5)
    dist = (qpos[:, None] - kpos[None, :]).astype(jnp.float32)
    s = s - slopes[None, :, :, None, None] * dist
    p = masked_softmax(s, mask)
    o = jnp.einsum('bkgtn,bnkd->btkgd', p.astype(v.dtype), v)
    return o, p


def gathered_attention(qg, qpos, k, v, kpos, slopes):
    s = jnp.einsum('btkgd,bktnd->bkgtn', qg, k).astype(jnp.float32) * (B_HD ** -0.5)
    dist = (qpos[None, None, :, None] - kpos).astype(jnp.float32)
    s = s - slopes[None, :, :, None, None] * dist[:, :, None]
    mask = (kpos <= qpos[None, None, :, None])[:, :, None]
    p = masked_softmax(s, mask)
    return jnp.einsum('bkgtn,bktnd->btkgd', p.astype(v.dtype), v)


def select_blocks(p_cmp, qpos, n_cmp, n_blk, n_sel):
    imp = jnp.sum(p_cmp, axis=2)
    ci = jnp.arange(n_cmp)[:, None] * D_CMP
    sj = jnp.arange(n_blk)[None, :] * L_SLC
    cover = ((ci < sj + L_SLC) & (ci + L_CMP > sj)).astype(jnp.float32)
    score = jnp.einsum('bktc,cj->bktj', imp, cover)
    jb = jnp.arange(n_blk)[None, :]
    cur = (qpos // L_SLC)[:, None]
    forced = (jb == 0) | (jb == cur) | (jb == cur - 1)
    score = jnp.where(forced, SEL_FORCE, score)
    score = jnp.where(jb > cur, -jnp.inf, score)
    _, idx = lax.top_k(score, n_sel)
    return idx


def nsa_core(q, qpos, gates, comp, gather_fn, n_blk, wkv, wpos, slopes):
    b, tq = q.shape[:2]
    qg = q.reshape(b, tq, B_KV_HEADS, B_GROUP, B_HD)
    n_cmp = comp.shape[1]
    cpos = jnp.arange(n_cmp) * D_CMP + (L_CMP - 1)
    cmask = cpos[None, :] <= qpos[:, None]
    o_cmp, p_cmp = shared_attention(qg, qpos, comp[:, :, 0], comp[:, :, 1], cpos, cmask, slopes)
    idx = select_blocks(p_cmp, qpos, n_cmp, n_blk, min(N_SEL, n_blk))
    k_sel, v_sel = gather_fn(idx)
    nk = idx.shape[-1] * L_SLC
    kpos = (idx[..., None] * L_SLC + jnp.arange(L_SLC)).reshape(b, B_KV_HEADS, tq, nk)
    o_slc = gathered_attention(qg, qpos, k_sel.reshape(b, B_KV_HEADS, tq, nk, B_HD),
                               v_sel.reshape(b, B_KV_HEADS, tq, nk, B_HD), kpos, slopes)
    wmask = (wpos[None, :] <= qpos[:, None]) & (wpos[None, :] > qpos[:, None] - WINDOW) & (wpos[None, :] >= 0)
    o_win, _ = shared_attention(qg, qpos, wkv[:, :, 0], wkv[:, :, 1], wpos, wmask, slopes)
    g = jax.nn.sigmoid(gates.astype(jnp.float32)).reshape(b, tq, B_KV_HEADS, B_GROUP, 3)
    o = g[..., 0:1] * o_cmp + g[..., 1:2] * o_slc + g[..., 2:3] * o_win
    return o.reshape(b, tq, B_HEADS * B_HD).astype(q.dtype)


def gather_local_blocks(blocks, idx):
    bi = jnp.arange(blocks.shape[0])[:, None, None, None]
    gi = jnp.arange(B_KV_HEADS)[None, :, None, None]
    r = blocks[bi, idx, :, :, gi, :]
    return r[..., 0, :], r[..., 1, :]


def gather_paged_blocks(pool, page_table, li, new_blocks, nb_past, idx):
    b = idx.shape[0]
    bpp = PAGE_SIZE // L_SLC
    bi = jnp.arange(b)[:, None, None, None]
    gi = jnp.arange(B_KV_HEADS)[None, :, None, None]
    li_idx = jnp.full((1, 1, 1, 1), li, dtype=jnp.int32)
    jp = jnp.clip(idx, 0, nb_past - 1)
    phys = page_table[bi, jp // bpp]
    sub = jp % bpp
    pool_r = pool.reshape(pool.shape[0], bpp, L_SLC, *pool.shape[2:])
    r_past = pool_r[phys, sub, :, li_idx, :, gi, :]
    jn = jnp.clip(idx - nb_past, 0, new_blocks.shape[1] - 1)
    r_new = new_blocks[bi, jn, :, :, gi, :]
    r = jnp.where((idx >= nb_past)[..., None, None, None], r_new, r_past.astype(r_new.dtype))
    return r[..., 0, :], r[..., 1, :]


def nsa_prompt(q, gates, cmp_kv, slc_kv, win_kv, pe, w1, w2, slopes):
    b, s = q.shape[:2]
    comp = nsa_compress(cmp_kv, pe, w1, w2)
    nb = s // L_SLC
    blocks = slc_kv.reshape(b, nb, L_SLC, 2, B_KV_HEADS, B_HD)
    gather_fn = functools.partial(gather_local_blocks, blocks)
    win_pad = jnp.pad(win_kv, ((0, 0), (WINDOW, 0), (0, 0), (0, 0), (0, 0)))

    def one_block(qb):
        q0 = qb * Q_BLOCK
        q_blk = lax.dynamic_slice_in_dim(q, q0, Q_BLOCK, axis=1)
        g_blk = lax.dynamic_slice_in_dim(gates, q0, Q_BLOCK, axis=1)
        w_blk = lax.dynamic_slice_in_dim(win_pad, q0, WINDOW + Q_BLOCK, axis=1)
        qpos = q0 + jnp.arange(Q_BLOCK)
        wpos = q0 - WINDOW + jnp.arange(WINDOW + Q_BLOCK)
        return nsa_core(q_blk, qpos, g_blk, comp, gather_fn, nb, w_blk, wpos, slopes)

    out = lax.map(one_block, jnp.arange(s // Q_BLOCK))
    return out.transpose(1, 0, 2, 3).reshape(b, s, B_HEADS * B_HD)


def nsa_sample(q, gates, cmp_new, slc_new, win_new, cache_cmp_kv, cache_slc_kv, win_buf, page_table, li,
               pe, w1, w2, slopes):
    bd, t = q.shape[:2]
    n_pages = page_table.shape[1]
    past = n_pages * PAGE_SIZE
    li_idx = jnp.full((1, 1), li, dtype=jnp.int32)
    past_cmp = cache_cmp_kv[page_table, :, li_idx].reshape(bd, past, 2, B_KV_HEADS, B_HD)
    rows = jnp.concatenate([past_cmp.astype(cmp_new.dtype), cmp_new], axis=1)
    rows = jnp.pad(rows, ((0, 0), (0, (-(past + t)) % D_CMP), (0, 0), (0, 0), (0, 0)))
    comp = nsa_compress(rows, pe, w1, w2)
    nb_past = past // L_SLC
    n_new = -(-t // L_SLC)
    new_blocks = jnp.pad(slc_new, ((0, 0), (0, n_new * L_SLC - t), (0, 0), (0, 0), (0, 0)))
    new_blocks = new_blocks.reshape(bd, n_new, L_SLC, 2, B_KV_HEADS, B_HD)
    gather_fn = functools.partial(gather_paged_blocks, cache_slc_kv, page_table, li, new_blocks, nb_past)
    wb = win_buf.shape[1]
    wkv = jnp.concatenate([win_buf.astype(win_new.dtype), win_new], axis=1)
    wpos = past - wb + jnp.arange(wb + t)
    qpos = past + jnp.arange(t)
    o = nsa_core(q, qpos, gates, comp, gather_fn, nb_past + n_new, wkv, wpos, slopes)
    return o, wkv[:, t:]


def short_conv_mixer(h, w_in, conv_w, w_out, buf):
    xin, cg, bg = jnp.split(h @ w_in, 3, axis=-1)
    y, new_buf = causal_dwconv(cg * xin, buf, conv_w)
    return (bg * y) @ w_out, new_buf


def conv_ffn(h, w_up, conv_w, w_down, buf):
    u, new_buf = causal_dwconv(h @ w_up, buf, conv_w)
    a, g = jnp.split(u, 2, axis=-1)
    return (jax.nn.silu(a) * g) @ w_down, new_buf


def setup_inputs(seed: int = 0) -> dict:
    key = jax.random.key(seed)
    ks = iter(jax.random.split(key, 32))

    def nrm(shape, scale):
        return jax.random.normal(next(ks), shape, jnp.float32) * scale

    def gain(shape):
        return 1.0 + nrm(shape, 0.05)

    n_pages = PAST_LEN // PAGE_SIZE
    n_used = DEC_BATCH * n_pages
    n_pool = (5 * n_used + 3) // 4
    wb = min(WINDOW, PAST_LEN)
    page_table = jax.random.permutation(next(ks), n_pool)[:n_used].reshape(DEC_BATCH, n_pages).astype(jnp.int32)
    return {
        'x_prompt': nrm((BATCH, SEQ, D_MODEL), 1.0),
        'x_sample': nrm((DEC_BATCH, DEC_SEQ, D_MODEL), 1.0),
        'cache_cmp_kv': nrm((n_pool, PAGE_SIZE, N_A, 2, B_KV_HEADS, B_HD), 1.0),
        'cache_slc_kv': nrm((n_pool, PAGE_SIZE, N_A, 2, B_KV_HEADS, B_HD), 1.0),
        'state_win_kv': nrm((N_A, DEC_BATCH, wb, 2, B_KV_HEADS, B_HD), 1.0),
        'state_hgrn': nrm((N_A, DEC_BATCH, A_HEADS, A_DK, A_DV), 0.5),
        'state_sconv': nrm((N_C, DEC_BATCH, CONV_W - 1, CONV_DIM), 1.0),
        'state_ffn': nrm((DEPTH, DEC_BATCH, CONV_W - 1, 2 * D_FF), 1.0),
        'page_table': page_table,
        'norm_mix': gain((DEPTH, D_MODEL)),
        'norm_ffn': gain((DEPTH, D_MODEL)),
        'norm_final': gain((D_MODEL,)),
        'ab_w_in': nrm((N_A, D_MODEL, AB_IN), D_MODEL ** -0.5),
        'ab_w_out': nrm((N_A, AB_MIX, D_MODEL), AB_MIX ** -0.5),
        'hgrn_lb_logits': nrm((N_A + 1, A_HEADS * A_DK), 0.5),
        'hgrn_norm': gain((N_A, A_HEADS * A_DV)),
        'cmp_pe': nrm((N_A, 2, L_CMP, B_HD), 0.5),
        'cmp_w1': nrm((N_A, 2, L_CMP * B_HD, CMP_HID), (L_CMP * B_HD) ** -0.5),
        'cmp_w2': nrm((N_A, 2, CMP_HID, B_HD), CMP_HID ** -0.5),
        'c_w_in': nrm((N_C, D_MODEL, 3 * CONV_DIM), D_MODEL ** -0.5),
        'c_conv': nrm((N_C, CONV_W, CONV_DIM), CONV_W ** -0.5),
        'c_w_out': nrm((N_C, CONV_DIM, D_MODEL), CONV_DIM ** -0.5),
        'ffn_up': nrm((DEPTH, D_MODEL, 2 * D_FF), D_MODEL ** -0.5),
        'ffn_conv': nrm((DEPTH, CONV_W, 2 * D_FF), CONV_W ** -0.5),
        'ffn_down': nrm((DEPTH, D_FF, D_MODEL), D_FF ** -0.5),
    }


def reference(x_prompt, x_sample, cache_cmp_kv, cache_slc_kv, state_win_kv, state_hgrn, state_sconv, state_ffn,
              page_table, norm_mix, norm_ffn, norm_final, ab_w_in, ab_w_out, hgrn_lb_logits, hgrn_norm,
              cmp_pe, cmp_w1, cmp_w2, c_w_in, c_conv, c_w_out, ffn_up, ffn_conv, ffn_down):
    slopes = alibi_slopes()
    lbs = jnp.cumsum(jax.nn.softmax(hgrn_lb_logits.astype(jnp.float32), axis=0), axis=0)
    b, s = x_prompt.shape[:2]
    bd, t = x_sample.shape[:2]
    wb = state_win_kv.shape[2]
    xp, xs = x_prompt, x_sample
    p_cmp, p_slc, p_win, p_hg, p_sc, p_ff = [], [], [], [], [], []
    s_cmp, s_slc, s_win, s_hg, s_sc, s_ff = [], [], [], [], [], []
    for l in range(DEPTH):
        if l % 2 == 0:
            la = l // 2
            qa, fl, ia, ga, qb, kc, ksl, kw, gb = split_ab(rmsnorm(xp, norm_mix[l]) @ ab_w_in[la])
            kc, ksl, kw = [z.reshape(b, s, 2, B_KV_HEADS, B_HD) for z in (kc, ksl, kw)]
            oa, hg = hgrn2_mixer(qa, fl, ia, ga, lbs[la], hgrn_norm[la],
                                 jnp.zeros((b, A_HEADS, A_DK, A_DV), jnp.float32), HGRN_CHUNK)
            ob = nsa_prompt(qb, gb, kc, ksl, kw, cmp_pe[la], cmp_w1[la], cmp_w2[la], slopes)
            xp = xp + jnp.concatenate([oa, ob], axis=-1) @ ab_w_out[la]
            p_cmp.append(kc)
            p_slc.append(ksl)
            p_win.append(jnp.pad(kw, ((0, 0), (max(wb - s, 0), 0), (0, 0), (0, 0), (0, 0)))[:, -wb:])
            p_hg.append(hg)
            qa, fl, ia, ga, qb, kc, ksl, kw, gb = split_ab(rmsnorm(xs, norm_mix[l]) @ ab_w_in[la])
            kc, ksl, kw = [z.reshape(bd, t, 2, B_KV_HEADS, B_HD) for z in (kc, ksl, kw)]
            oa, hg = hgrn2_mixer(qa, fl, ia, ga, lbs[la], hgrn_norm[la], state_hgrn[la], t)
            ob, win = nsa_sample(qb, gb, kc, ksl, kw, cache_cmp_kv, cache_slc_kv, state_win_kv[la], page_table, la,
                                 cmp_pe[la], cmp_w1[la], cmp_w2[la], slopes)
            xs = xs + jnp.concatenate([oa, ob], axis=-1) @ ab_w_out[la]
            s_cmp.append(kc)
            s_slc.append(ksl)
            s_win.append(win)
            s_hg.append(hg)
        else:
            lc = l // 2
            yp, bp = short_conv_mixer(rmsnorm(xp, norm_mix[l]), c_w_in[lc], c_conv[lc], c_w_out[lc],
                                      jnp.zeros((b, CONV_W - 1, CONV_DIM), xp.dtype))
            xp = xp + yp
            p_sc.append(bp)
            ys, bs = short_conv_mixer(rmsnorm(xs, norm_mix[l]), c_w_in[lc], c_conv[lc], c_w_out[lc], state_sconv[lc])
            xs = xs + ys
            s_sc.append(bs)
        yp, bp = conv_ffn(rmsnorm(xp, norm_ffn[l]), ffn_up[l], ffn_conv[l], ffn_down[l],
                          jnp.zeros((b, CONV_W - 1, 2 * D_FF), xp.dtype))
        xp = xp + yp
        p_ff.append(bp)
        ys, bs = conv_ffn(rmsnorm(xs, norm_ffn[l]), ffn_up[l], ffn_conv[l], ffn_down[l], state_ffn[l])
        xs = xs + ys
        s_ff.append(bs)
    y_prompt = rmsnorm(xp, norm_final)
    y_sample = rmsnorm(xs, norm_final)
    return (y_prompt, y_sample,
            jnp.stack(p_cmp, axis=2), jnp.stack(p_slc, axis=2), jnp.stack(p_win, axis=0),
            jnp.stack(p_hg, axis=0), jnp.stack(p_sc, axis=0), jnp.stack(p_ff, axis=0),
            jnp.stack(s_cmp, axis=2), jnp.stack(s_slc, axis=2), jnp.stack(s_win, axis=0),
            jnp.stack(s_hg, axis=0), jnp.stack(s_sc, axis=0), jnp.stack(s_ff, axis=0))
```

```python
import functools

import jax
import jax.numpy as jnp
from jax import lax
from jax.experimental import pallas as pl
from jax.experimental.pallas import tpu as pltpu

F32 = jnp.float32
BF16 = jnp.bfloat16
NEG_INF = float("-inf")

D_MODEL = 1024
PAGE_SIZE = 128
A_HEADS = 4
A_DK = 128
A_DV = 128
B_HEADS = 8
B_KV_HEADS = 2
B_GROUP = B_HEADS // B_KV_HEADS
B_HD = 64
L_CMP = 32
D_CMP = 16
CMP_HID = 64
L_SLC = 64
SLC_SHIFT = 6
N_SEL = 16
WINDOW = 512
SEL_FORCE = 1e9
CONV_W = 3
EPS = 1e-6
KV_W = 2 * B_KV_HEADS * B_HD
HG_W = 4 * A_HEADS * A_DK
QB_W = B_HEADS * B_HD
GATE_PAD = 128
IN_PAD = HG_W + QB_W + 3 * KV_W + B_KV_HEADS * GATE_PAD

V7X_VMEM_BYTES = 64 * 2**20


def _cparams(sem, vmem_mb):
    assert vmem_mb * 2**20 < V7X_VMEM_BYTES
    return pltpu.CompilerParams(dimension_semantics=sem, vmem_limit_bytes=vmem_mb * 2**20)


def _dot(a, b):
    return jnp.dot(a, b, preferred_element_type=F32)


def _dot_nt(a, b):
    return lax.dot_general(a, b, (((1,), (1,)), ((), ())), preferred_element_type=F32)


def _dot_tn(a, b):
    return lax.dot_general(a, b, (((0,), (0,)), ((), ())), preferred_element_type=F32)


def _split3(x):
    hi = x.astype(BF16)
    r1 = x - hi.astype(F32)
    mid = r1.astype(BF16)
    lo = (r1 - mid.astype(F32)).astype(BF16)
    return hi, mid, lo


def _dot3_lhs(x, w):
    hi, mid, lo = _split3(x)
    return _dot(hi, w) + _dot(mid, w) + _dot(lo, w)


def _dot3_rhs(w, x):
    hi, mid, lo = _split3(x)
    return _dot(w, hi) + _dot(w, mid) + _dot(w, lo)


def _rms(x, w):
    return x * lax.rsqrt(jnp.mean(x * x, axis=-1, keepdims=True) + EPS) * w


def _masked_softmax(s, mask):
    s = jnp.where(mask, s, NEG_INF)
    m = jnp.max(s, axis=-1, keepdims=True)
    m = jnp.where(m > NEG_INF, m, 0.0)
    e = jnp.exp(s - m)
    d = jnp.sum(e, axis=-1, keepdims=True)
    return e * (1.0 / jnp.where(d > 0, d, 1.0))


def _iota(shape, dim):
    return lax.broadcasted_iota(jnp.int32, shape, dim)


def _top_blocks(score, jb, n_sel):
    big = float(score.shape[-1])
    jbf = jb.astype(F32)

    def body(_, carry):
        sc, sel = carry
        m = jnp.max(sc, axis=-1, keepdims=True)
        idx = jnp.min(jnp.where(sc == m, jbf, big), axis=-1, keepdims=True)
        pick = jbf == idx
        return jnp.where(pick, NEG_INF, sc), jnp.where(pick, 1.0, sel)

    _, sel = lax.fori_loop(0, n_sel, body, (score, jnp.zeros_like(score)))
    return sel


def _inproj_kernel(x_ref, nw_ref, w_ref, hg_ref, qb_ref, kc_ref, ksl_ref, kw_ref, gb_ref):
    xn = _rms(x_ref[...], nw_ref[...]).astype(BF16)

    def proj(c0, n):
        return _dot(xn, w_ref[:, c0:c0 + n])

    hg_ref[...] = proj(0, HG_W)
    qb_ref[...] = (proj(HG_W, QB_W) * (B_HD ** -0.5)).astype(BF16)
    c = HG_W + QB_W
    kc_ref[...] = proj(c, KV_W)
    ksl_ref[...] = proj(c + KV_W, KV_W)
    kw_ref[...] = proj(c + 2 * KV_W, KV_W)
    gb_ref[...] = proj(c + 3 * KV_W, B_KV_HEADS * GATE_PAD)


def _inproj(x, nw, w_pad, tm):
    m = x.shape[0]
    row = lambda n: pl.BlockSpec((tm, n), lambda i: (i, 0))
    full = lambda a: pl.BlockSpec(a.shape, lambda i: (0,) * a.ndim)
    widths = (HG_W, QB_W, KV_W, KV_W, KV_W, B_KV_HEADS * GATE_PAD)
    dtypes = (F32, BF16, F32, F32, F32, F32)
    return pl.pallas_call(
        _inproj_kernel,
        grid=(m // tm,),
        in_specs=[row(D_MODEL), full(nw), full(w_pad)],
        out_specs=[row(n) for n in widths],
        out_shape=[jax.ShapeDtypeStruct((m, n), d) for n, d in zip(widths, dtypes)],
        compiler_params=_cparams(("parallel",), 48),
        name="inproj",
    )(x, nw, w_pad)


def _lower_bound(logit_ref, la):
    lg = logit_ref[...]
    e = jnp.exp(lg - jnp.max(lg, axis=0, keepdims=True))
    return jnp.sum(e[:la + 1], axis=0, keepdims=True) / jnp.sum(e, axis=0, keepdims=True)


def _hgrn_post(o, nw, ga):
    o = o * lax.rsqrt(jnp.mean(o * o, axis=-1, keepdims=True) + EPS)
    return o * nw * jax.nn.sigmoid(ga)


def _hgrn_prompt_kernel(hg_ref, lg_ref, nw_ref, oa_ref, st_ref, s_ref, *, la, chunk, sub):
    ci = pl.program_id(1)

    @pl.when(ci == 0)
    def _():
        s_ref[...] = jnp.zeros_like(s_ref)

    lb = _lower_bound(lg_ref, la)
    tri = (_iota((chunk, chunk), 0) >= _iota((chunk, chunk), 1)).astype(BF16)
    trow = _iota((chunk, 1), 0)
    srow = _iota((sub, 1), 0)
    lane = _iota((sub, chunk), 1)
    for h in range(A_HEADS):
        hs = slice(h * A_DK, (h + 1) * A_DK)
        q = hg_ref[:, h * A_DK:(h + 1) * A_DK]
        fl = hg_ref[:, 512 + h * A_DK:512 + (h + 1) * A_DK]
        v = hg_ref[:, 1024 + h * A_DV:1024 + (h + 1) * A_DV]
        ga = hg_ref[:, 1536 + h * A_DV:1536 + (h + 1) * A_DV]
        lbh = lb[:, hs]
        f = lbh + (1.0 - lbh) * jax.nn.sigmoid(fl)
        a = _dot3_rhs(tri, jnp.log(f))
        k = 1.0 - f
        st = s_ref[h]
        vb = v.astype(BF16)
        o = _dot_nt((q * jnp.exp(a)).astype(BF16), st.astype(BF16))
        rows = []
        for i in range(chunk // sub):
            r0 = i * sub
            qi, ai = q[r0:r0 + sub], a[r0:r0 + sub]
            att = jnp.zeros((sub, chunk), F32)
            for s in range(sub):
                dec = jnp.exp(jnp.where(srow >= s, ai - a[r0 + s:r0 + s + 1], NEG_INF))
                col = jnp.sum(qi * dec * k[r0 + s:r0 + s + 1], axis=-1, keepdims=True)
                att = jnp.where(lane == r0 + s, col, att)
            if i > 0:
                ref_a = a[r0 - 1:r0]
                qt = qi * jnp.exp(ai - ref_a)
                kt = k * jnp.exp(jnp.where(trow < r0, ref_a - a, NEG_INF))
                att = att + _dot_nt(qt.astype(BF16), kt.astype(BF16))
            rows.append(att)
        att = jnp.concatenate(rows, axis=0)
        o = o + _dot(att.astype(BF16), vb)
        a_end = a[chunk - 1:chunk]
        khat = k * jnp.exp(a_end - a)
        s_ref[h] = st * jnp.exp(a_end) + _dot_tn(vb, khat.astype(BF16))
        oa_ref[:, hs] = _hgrn_post(o, nw_ref[:, hs], ga).astype(BF16)

    @pl.when(ci == pl.num_programs(1) - 1)
    def _():
        for h in range(A_HEADS):
            st_ref[0, h] = s_ref[h].T


def _hgrn_prompt(hg, lb_logits, nw, la, b, s, chunk=64, sub=16):
    n = s // chunk
    full = lambda a: pl.BlockSpec(a.shape, lambda i, j: (0,) * a.ndim)
    return pl.pallas_call(
        functools.partial(_hgrn_prompt_kernel, la=la, chunk=chunk, sub=sub),
        grid=(b, n),
        in_specs=[pl.BlockSpec((chunk, HG_W), lambda i, j: (i * n + j, 0)), full(lb_logits), full(nw)],
        out_specs=[pl.BlockSpec((chunk, A_HEADS * A_DV), lambda i, j: (i * n + j, 0)),
                   pl.BlockSpec((1, A_HEADS, A_DK, A_DV), lambda i, j: (i, 0, 0, 0))],
        out_shape=[jax.ShapeDtypeStruct((b * s, A_HEADS * A_DV), BF16),
                   jax.ShapeDtypeStruct((b, A_HEADS, A_DK, A_DV), F32)],
        scratch_shapes=[pltpu.VMEM((A_HEADS, A_DV, A_DK), F32)],
        compiler_params=_cparams(("parallel", "arbitrary"), 32),
        name="hgrn_prompt",
    )(hg, lb_logits, nw)


def _hgrn_sample_kernel(hg_ref, s0_ref, lg_ref, nw_ref, oa_ref, st_ref, *, la):
    lb = _lower_bound(lg_ref, la)
    eye = (_iota((A_DK, A_DK), 0) == _iota((A_DK, A_DK), 1)).astype(F32)

    def col(r):
        return jnp.sum(eye * r, axis=1, keepdims=True)

    for h in range(A_HEADS):
        hs = slice(h * A_DK, (h + 1) * A_DK)
        q = hg_ref[0, :, h * A_DK:(h + 1) * A_DK]
        fl = hg_ref[0, :, 512 + h * A_DK:512 + (h + 1) * A_DK]
        v = hg_ref[0, :, 1024 + h * A_DV:1024 + (h + 1) * A_DV]
        ga = hg_ref[0, :, 1536 + h * A_DV:1536 + (h + 1) * A_DV]
        lbh = lb[:, hs]
        f = lbh + (1.0 - lbh) * jax.nn.sigmoid(fl)
        sn = col(f) * s0_ref[0, h] + col(1.0 - f) * v
        st_ref[0, h] = sn
        o = jnp.sum(col(q) * sn, axis=0, keepdims=True)
        oa_ref[0, :, hs] = _hgrn_post(o, nw_ref[:, hs], ga).astype(BF16)


def _hgrn_sample(hg, s0, lb_logits, nw, la):
    bd = hg.shape[0]
    full = lambda a: pl.BlockSpec(a.shape, lambda i: (0,) * a.ndim)
    st_spec = pl.BlockSpec((1, A_HEADS, A_DK, A_DV), lambda i: (i, 0, 0, 0))
    oa, st = pl.pallas_call(
        functools.partial(_hgrn_sample_kernel, la=la),
        grid=(bd,),
        in_specs=[pl.BlockSpec((1, 1, HG_W), lambda i: (i, 0, 0)), st_spec, full(lb_logits), full(nw)],
        out_specs=[pl.BlockSpec((1, 1, A_HEADS * A_DV), lambda i: (i, 0, 0)), st_spec],
        out_shape=[jax.ShapeDtypeStruct((bd, 1, A_HEADS * A_DV), BF16),
                   jax.ShapeDtypeStruct(s0.shape, F32)],
        compiler_params=_cparams(("parallel",), 16),
        name="hgrn_sample",
    )(hg.reshape(bd, 1, HG_W), s0, lb_logits, nw)
    return oa.reshape(bd, A_HEADS * A_DV), st


def _cmp_weights(pe, w1, w2, packed):
    eye = jnp.eye(B_KV_HEADS, dtype=F32)
    w1r = w1.reshape(2, L_CMP, B_HD, CMP_HID)
    w1bd = jnp.einsum("kidm,hH->ikhdHm", w1r, eye).reshape(L_CMP, 2, KV_W // 2, KV_W // 2)
    pe_b = jnp.broadcast_to(pe.transpose(1, 0, 2)[:, :, None, :], (L_CMP, 2, B_KV_HEADS, B_HD))
    out = "khmHKd" if packed else "khmKHd"
    w2bd = jnp.einsum("kmd,kK,hH->" + out, w2, eye, eye).reshape(KV_W, KV_W)
    return pe_b.reshape(L_CMP, KV_W), w1bd.astype(BF16), w2bd.astype(BF16)


def _chunk_preacts(load, n, pe_ref, w1_ref):
    half = KV_W // 2
    p0 = [jnp.zeros((n, half), F32) for _ in range(2)]
    p1 = [jnp.zeros((n, half), F32) for _ in range(2)]
    for i in range(D_CMP):
        for kv in range(2):
            x = load(i, kv)
            ls = slice(kv * half, (kv + 1) * half)
            p0[kv] = p0[kv] + _dot((x + pe_ref[i:i + 1, ls]).astype(BF16), w1_ref[i, kv])
            p1[kv] = p1[kv] + _dot((x + pe_ref[D_CMP + i:D_CMP + i + 1, ls]).astype(BF16), w1_ref[D_CMP + i, kv])
    return jnp.concatenate(p0, axis=1), jnp.concatenate(p1, axis=1)


def _compress_kernel(kc_ref, pe_ref, w1_ref, w2_ref, out_ref):
    n = out_ref.shape[1]
    p0, p1 = _chunk_preacts(lambda i, kv: kc_ref[0, pl.ds(2 * i + kv, n, stride=2 * D_CMP), :], n, pe_ref, w1_ref)
    pre = p0 + pltpu.roll(p1, n - 1, axis=0)
    hid = pre * jax.nn.sigmoid(pre)
    out_ref[0] = _dot(hid.astype(BF16), w2_ref[...]).astype(BF16)


def _compress_prompt(kc, pe_b, w1bd, w2bd):
    b, s2, _ = kc.shape
    n = s2 // (2 * D_CMP)
    full = lambda a: pl.BlockSpec(a.shape, lambda i: (0,) * a.ndim)
    return pl.pallas_call(
        _compress_kernel,
        grid=(b,),
        in_specs=[pl.BlockSpec((1, s2, KV_W // 2), lambda i: (i, 0, 0)), full(pe_b), full(w1bd), full(w2bd)],
        out_specs=pl.BlockSpec((1, n, KV_W), lambda i: (i, 0, 0)),
        out_shape=jax.ShapeDtypeStruct((b, n, KV_W), BF16),
        compiler_params=_cparams(("parallel",), 48),
        name="compress_prompt",
    )(kc, pe_b, w1bd, w2bd)


def _nsa_prompt_kernel(sl_ref, q_ref, g_ref, comp_ref, slc_ref, win_ref, cover_ref, o_ref,
                       m_ref, l_ref, acc_ref, *, tq, tk):
    kvh = pl.program_id(1)
    q0 = pl.program_id(2) * tq
    qpos = q0 + _iota((tq, 1), 0)
    lo = _iota((tq, 128), 1) < B_HD
    qf = q_ref[0].astype(F32)
    qp = []
    for g in range(B_GROUP):
        pair = qf[:, 128 * (g // 2):128 * (g // 2) + 128]
        if g % 2:
            pair = pltpu.roll(pair, B_HD, axis=1)
        qp.append(jnp.where(lo, pair, 0.0).astype(BF16))
    slopes = [sl_ref[kvh * B_GROUP + g] for g in range(B_GROUP)]

    comp = comp_ref[0]
    nc = comp.shape[0]
    cpos = _iota((1, nc), 1) * D_CMP + (L_CMP - 1)
    cmask = cpos <= qpos
    cdist = (qpos - cpos).astype(F32)
    imp = jnp.zeros((tq, nc), F32)
    o_cmp = []
    for g in range(B_GROUP):
        p = _masked_softmax(_dot_nt(qp[g], comp) - slopes[g] * cdist, cmask)
        imp = imp + p
        o_cmp.append(_dot(p.astype(BF16), comp))

    nblk = cover_ref.shape[1]
    jb = _iota((1, nblk), 1)
    cur = qpos >> SLC_SHIFT
    score = _dot3_lhs(imp, cover_ref[...])
    score = jnp.where((jb == 0) | (jb == cur) | (jb == cur - 1), SEL_FORCE, score)
    score = jnp.where(jb > cur, NEG_INF, score)
    sel = _top_blocks(score, jb, min(N_SEL, nblk)).astype(BF16)

    m_ref[...] = jnp.full_like(m_ref, NEG_INF)
    l_ref[...] = jnp.zeros_like(l_ref)
    acc_ref[...] = jnp.zeros_like(acc_ref)
    bpt = tk // L_SLC

    def tile(kt, carry):
        k0 = pl.multiple_of(kt * tk, tk)
        kv = slc_ref[0, 0, pl.ds(k0, tk), :]
        expand = (_iota((nblk, tk), 0) == kt * bpt + (_iota((nblk, tk), 1) >> SLC_SHIFT)).astype(BF16)
        kpos = k0 + _iota((1, tk), 1)
        mask = (_dot(sel, expand) > 0.5) & (kpos <= qpos)
        dist = (qpos - kpos).astype(F32)
        for g in range(B_GROUP):
            s = jnp.where(mask, _dot_nt(qp[g], kv) - slopes[g] * dist, NEG_INF)
            m_old = m_ref[g]
            m_new = jnp.maximum(m_old, jnp.max(s, axis=-1, keepdims=True))
            m_safe = jnp.where(m_new > NEG_INF, m_new, 0.0)
            alpha = jnp.exp(m_old - m_safe)
            p = jnp.exp(s - m_safe)
            l_ref[g] = alpha * l_ref[g] + jnp.sum(p, axis=-1, keepdims=True)
            acc_ref[g] = alpha * acc_ref[g] + _dot(p.astype(BF16), kv)
            m_ref[g] = m_new
        return carry

    lax.fori_loop(0, (q0 + tq - 1) // tk + 1, tile, 0)

    start = pl.multiple_of(jnp.maximum(q0 - WINDOW, 0), 128)
    kvw = win_ref[0, 0, pl.ds(start, WINDOW + tq), :]
    wpos = start + _iota((1, WINDOW + tq), 1)
    wmask = (wpos <= qpos) & (wpos > qpos - WINDOW)
    wdist = (qpos - wpos).astype(F32)

    gate = jax.nn.sigmoid(g_ref[0])
    outs = []
    for g in range(B_GROUP):
        l = l_ref[g]
        o_slc = acc_ref[g] * (1.0 / jnp.where(l > 0, l, 1.0))
        pw = _masked_softmax(_dot_nt(qp[g], kvw) - slopes[g] * wdist, wmask)
        o_win = _dot(pw.astype(BF16), kvw)
        outs.append(gate[:, 3 * g:3 * g + 1] * o_cmp[g] + gate[:, 3 * g + 1:3 * g + 2] * o_slc
                    + gate[:, 3 * g + 2:3 * g + 3] * o_win)
    for j in range(B_GROUP // 2):
        o_ref[0, :, 128 * j:128 * j + 128] = jnp.where(
            lo, pltpu.roll(outs[2 * j], B_HD, axis=1), outs[2 * j + 1]).astype(BF16)


def _nsa_prompt(slopes, qb, gb, comp, slc_kv, win_kv, cover, tq=128, tk=512):
    b, s, _ = qb.shape
    half = QB_W // B_KV_HEADS
    grid_spec = pltpu.PrefetchScalarGridSpec(
        num_scalar_prefetch=1,
        grid=(b, B_KV_HEADS, s // tq),
        in_specs=[
            pl.BlockSpec((1, tq, half), lambda i, h, t, sl: (i, t, h)),
            pl.BlockSpec((1, tq, GATE_PAD), lambda i, h, t, sl: (i, t, h)),
            pl.BlockSpec((1, comp.shape[1], 128), lambda i, h, t, sl: (i, 0, h)),
            pl.BlockSpec((1, 1, s, 128), lambda i, h, t, sl: (i, h, 0, 0)),
            pl.BlockSpec((1, 1, s, 128), lambda i, h, t, sl: (i, h, 0, 0)),
            pl.BlockSpec(cover.shape, lambda i, h, t, sl: (0, 0)),
        ],
        out_specs=pl.BlockSpec((1, tq, half), lambda i, h, t, sl: (i, t, h)),
        scratch_shapes=[pltpu.VMEM((B_GROUP, tq, 1), F32), pltpu.VMEM((B_GROUP, tq, 1), F32),
                        pltpu.VMEM((B_GROUP, tq, 128), F32)],
    )
    return pl.pallas_call(
        functools.partial(_nsa_prompt_kernel, tq=tq, tk=tk),
        grid_spec=grid_spec,
        out_shape=jax.ShapeDtypeStruct((b, s, QB_W), BF16),
        compiler_params=_cparams(("parallel", "parallel", "arbitrary"), 48),
        name="nsa_prompt",
    )(slopes, qb, gb, comp, slc_kv, win_kv, cover)


def _cover_matrix(n_cmp, n_blk, n_pad):
    ci = jnp.arange(n_cmp)[:, None] * D_CMP
    sj = jnp.arange(n_pad)[None, :] * L_SLC
    cov = (ci < sj + L_SLC) & (ci + L_CMP > sj) & (jnp.arange(n_pad)[None, :] < n_blk)
    return cov.astype(BF16)


PAGES_PER_STEP = 16


def _cmp_pages_kernel(pt_ref, *refs, li):
    pages = refs[:PAGES_PER_STEP]
    pe_ref, w1_ref, out_ref = refs[PAGES_PER_STEP:]
    per_page = PAGE_SIZE // D_CMP
    n = PAGES_PER_STEP * per_page
    rpt = pages[0].shape[1] // PAGE_SIZE

    def load(i, kv):
        return jnp.concatenate([pg[0, pl.ds(rpt * i + 2 * li + kv, per_page, stride=rpt * D_CMP), :]
                                for pg in pages], axis=0)

    p0, p1 = _chunk_preacts(load, n, pe_ref, w1_ref)
    out_ref[0, :, 0:KV_W] = p0
    out_ref[0, :, KV_W:2 * KV_W] = p1


def _cmp_pages(page_table, cache, li, pe_b, w1bd):
    bd, n_pages = page_table.shape
    pool = cache.reshape(cache.shape[0], -1, KV_W // 2)
    steps = n_pages // PAGES_PER_STEP
    rows = PAGES_PER_STEP * PAGE_SIZE // D_CMP

    def page_spec(k):
        return pl.BlockSpec((1,) + pool.shape[1:], lambda i, j, pt: (pt[i, j * PAGES_PER_STEP + k], 0, 0))

    full = lambda a: pl.BlockSpec(a.shape, lambda i, j, pt: (0,) * a.ndim)
    grid_spec = pltpu.PrefetchScalarGridSpec(
        num_scalar_prefetch=1,
        grid=(bd, steps),
        in_specs=[page_spec(k) for k in range(PAGES_PER_STEP)] + [full(pe_b), full(w1bd)],
        out_specs=pl.BlockSpec((1, rows, 2 * KV_W), lambda i, j, pt: (i, j, 0)),
    )
    return pl.pallas_call(
        functools.partial(_cmp_pages_kernel, li=li),
        grid_spec=grid_spec,
        out_shape=jax.ShapeDtypeStruct((bd, steps * rows, 2 * KV_W), F32),
        compiler_params=_cparams(("parallel", "arbitrary"), 48),
        name="cmp_pages",
    )(page_table, *([pool] * PAGES_PER_STEP), pe_b, w1bd)


def _nsa_sample_select_kernel(sl_ref, pre_ref, new_ref, q_ref, pe_ref, w1_ref, w2_ref, cover_ref,
                              idx_ref, ocmp_ref, tail_ref, *, past):
    half = KV_W // 2

    def second_half(i, r):
        r8 = jnp.broadcast_to(r, (8, KV_W)).astype(BF16)
        return jnp.concatenate([_dot(r8[:, kv * half:(kv + 1) * half], w1_ref[D_CMP + i, kv]) for kv in range(2)],
                               axis=1)

    @pl.when(pl.program_id(0) == 0)
    def _():
        t = jnp.zeros((8, KV_W), F32)
        for i in range(D_CMP):
            t = t + second_half(i, pe_ref[D_CMP + i:D_CMP + i + 1])
        tail_ref[...] = t

    n = pre_ref.shape[1]
    p1_new = (tail_ref[...] + second_half(0, new_ref[0]))[0:1]
    p1 = pltpu.roll(pre_ref[0, :, KV_W:2 * KV_W], n - 1, axis=0)
    p1 = jnp.where(_iota((n, 1), 0) == n - 1, p1_new, p1)
    pre = pre_ref[0, :, 0:KV_W] + p1
    comp = _dot((pre * jax.nn.sigmoid(pre)).astype(BF16), w2_ref[...]).astype(BF16)

    qpos = past
    cpos = _iota((1, n), 1) * D_CMP + (L_CMP - 1)
    cmask = cpos <= qpos
    cdist = (qpos - cpos).astype(F32)
    row8 = _iota((8, 1), 0)
    imp = jnp.zeros((8, n), F32)
    for kvh in range(B_KV_HEADS):
        slope = jnp.zeros((8, 1), F32)
        for g in range(B_GROUP):
            slope = jnp.where(row8 == g, sl_ref[kvh * B_GROUP + g], slope)
        p = _masked_softmax(_dot_nt(q_ref[0, kvh], comp) - slope * cdist, cmask)
        p = jnp.where(row8 < B_GROUP, p, 0.0)
        ocmp_ref[0, kvh] = _dot(p.astype(BF16), comp)
        imp = jnp.where(row8 == kvh, jnp.sum(p, axis=0, keepdims=True), imp)

    nblk = cover_ref.shape[1]
    jb = _iota((1, nblk), 1)
    cur = qpos // L_SLC
    score = _dot3_lhs(imp, cover_ref[...])
    score = jnp.where((jb == 0) | (jb == cur) | (jb == cur - 1), SEL_FORCE, score)
    score = jnp.where(jb > cur, NEG_INF, score)
    lane = _iota((1, 128), 1)
    jbf = jb.astype(F32)

    def body(it, carry):
        sc, idx = carry
        m = jnp.max(sc, axis=-1, keepdims=True)
        j = jnp.min(jnp.where(sc == m, jbf, float(nblk)), axis=-1, keepdims=True)
        return jnp.where(jbf == j, NEG_INF, sc), jnp.where(lane == it, j, idx)

    _, idx = lax.fori_loop(0, N_SEL, body, (score, jnp.zeros((8, 128), F32)))
    idx_ref[0] = idx.astype(jnp.int32)


def _nsa_sample_select(slopes, pre, cmp_new, q8, pe_b, w1bd, w2bd, cover, past):
    bd = pre.shape[0]
    full = lambda a: pl.BlockSpec(a.shape, lambda i, sl: (0,) * a.ndim)
    grid_spec = pltpu.PrefetchScalarGridSpec(
        num_scalar_prefetch=1,
        grid=(bd,),
        in_specs=[
            pl.BlockSpec((1,) + pre.shape[1:], lambda i, sl: (i, 0, 0)),
            pl.BlockSpec((1, 1, KV_W), lambda i, sl: (i, 0, 0)),
            pl.BlockSpec((1, B_KV_HEADS, 8, KV_W), lambda i, sl: (i, 0, 0, 0)),
            full(pe_b), full(w1bd), full(w2bd), full(cover),
        ],
        out_specs=[pl.BlockSpec((1, 8, 128), lambda i, sl: (i, 0, 0)),
                   pl.BlockSpec((1, B_KV_HEADS, 8, KV_W), lambda i, sl: (i, 0, 0, 0))],
        scratch_shapes=[pltpu.VMEM((8, KV_W), F32)],
    )
    return pl.pallas_call(
        functools.partial(_nsa_sample_select_kernel, past=past),
        grid_spec=grid_spec,
        out_shape=[jax.ShapeDtypeStruct((bd, 8, 128), jnp.int32),
                   jax.ShapeDtypeStruct((bd, B_KV_HEADS, 8, KV_W), F32)],
        compiler_params=_cparams(("arbitrary",), 48),
        name="nsa_sample_select",
    )(slopes, pre, cmp_new, q8, pe_b, w1bd, w2bd, cover)


def _nsa_sample_attend_kernel(pt_ref, ix_ref, sl_ref, *refs, past, nb_past):
    n_blocks = B_KV_HEADS * N_SEL
    blocks = refs[:n_blocks]
    q_ref, slc_new_ref, win_new_ref, win_ref, g_ref, ocmp_ref, o_ref = refs[n_blocks:]
    b = pl.program_id(0)
    row8 = _iota((8, 1), 0)
    first = _iota((L_SLC, 1), 0) == 0
    new_blk = jnp.where(first, slc_new_ref[0], 0.0).astype(BF16)
    wb = win_ref.shape[1]
    win = win_ref[0].astype(BF16)
    win_new = jnp.where(_iota((8, 1), 0) == 0, win_new_ref[0], 0.0).astype(BF16)
    for kvh in range(B_KV_HEADS):
        q = q_ref[0, kvh]
        slope = jnp.zeros((8, 1), F32)
        for g in range(B_GROUP):
            slope = jnp.where(row8 == g, sl_ref[kvh * B_GROUP + g], slope)
        m = jnp.full((8, 1), NEG_INF, F32)
        l = jnp.zeros((8, 1), F32)
        acc = jnp.zeros((8, KV_W), F32)
        for n in range(N_SEL):
            j = ix_ref[b, kvh * N_SEL + n]
            blk = jnp.where(j >= nb_past, new_blk, blocks[kvh * N_SEL + n][0].astype(BF16))
            kpos = j * L_SLC + _iota((1, L_SLC), 1)
            s = _dot_nt(q, blk) - slope * (past - kpos).astype(F32)
            s = jnp.where(kpos <= past, s, NEG_INF)
            m_new = jnp.maximum(m, jnp.max(s, axis=-1, keepdims=True))
            m_safe = jnp.where(m_new > NEG_INF, m_new, 0.0)
            alpha = jnp.exp(m - m_safe)
            p = jnp.exp(s - m_safe)
            l = alpha * l + jnp.sum(p, axis=-1, keepdims=True)
            acc = alpha * acc + _dot(p.astype(BF16), blk)
            m = m_new
        o_slc = acc * (1.0 / jnp.where(l > 0, l, 1.0))
        wpos = past - wb + _iota((1, wb), 1)
        s1 = _dot_nt(q, win) - slope * (past - wpos).astype(F32)
        s1 = jnp.where(wpos > past - WINDOW, s1, NEG_INF)
        s2 = jnp.where(_iota((1, 8), 1) == 0, _dot_nt(q, win_new), NEG_INF)
        mw = jnp.maximum(jnp.max(s1, axis=-1, keepdims=True), jnp.max(s2, axis=-1, keepdims=True))
        e1 = jnp.exp(s1 - mw)
        e2 = jnp.exp(s2 - mw)
        dw = jnp.sum(e1, axis=-1, keepdims=True) + jnp.sum(e2, axis=-1, keepdims=True)
        o_win = (_dot(e1.astype(BF16), win) + _dot(e2.astype(BF16), win_new)) * (1.0 / dw)
        gate = jax.nn.sigmoid(g_ref[0, kvh])
        o_ref[0, kvh] = gate[:, 0:1] * ocmp_ref[0, kvh] + gate[:, 1:2] * o_slc + gate[:, 2:3] * o_win


def _nsa_sample_attend(page_table, idx, slopes, cache, li, q8, slc_new, win_new, win_state, gate8, o_cmp,
                       past):
    bd = q8.shape[0]
    bpp = PAGE_SIZE // L_SLC
    nb_past = past // L_SLC
    n_blocks = B_KV_HEADS * N_SEL
    pool = cache.reshape(cache.shape[0] * bpp, L_SLC, -1)

    def blk_spec(k):
        def imap(i, pt, ix, sl):
            j = jnp.minimum(ix[i, k], nb_past - 1)
            return (pt[i, j // bpp] * bpp + j % bpp, 0, li)
        return pl.BlockSpec((1, L_SLC, KV_W), imap)

    row = lambda: pl.BlockSpec((1, 1, KV_W), lambda i, pt, ix, sl: (i, 0, 0))
    per_head = lambda w: pl.BlockSpec((1, B_KV_HEADS, 8, w), lambda i, pt, ix, sl: (i, 0, 0, 0))
    grid_spec = pltpu.PrefetchScalarGridSpec(
        num_scalar_prefetch=3,
        grid=(bd,),
        in_specs=[blk_spec(k) for k in range(n_blocks)] + [
            per_head(KV_W), row(), row(),
            pl.BlockSpec((1,) + win_state.shape[1:], lambda i, pt, ix, sl: (i, 0, 0)),
            per_head(128), per_head(KV_W)],
        out_specs=per_head(KV_W),
    )
    return pl.pallas_call(
        functools.partial(_nsa_sample_attend_kernel, past=past, nb_past=nb_past),
        grid_spec=grid_spec,
        out_shape=jax.ShapeDtypeStruct((bd, B_KV_HEADS, 8, KV_W), F32),
        compiler_params=_cparams(("arbitrary",), 48),
        name="nsa_sample_attend",
    )(page_table, idx, slopes, *([pool] * n_blocks), q8, slc_new, win_new, win_state, gate8, o_cmp)


def _outproj_kernel(x_ref, a_ref, b_ref, w_ref, y_ref):
    na = a_ref.shape[1]
    y_ref[...] = x_ref[...] + _dot(a_ref[...], w_ref[0:na]) + _dot(b_ref[...], w_ref[na:])


def _outproj(x, oa, ob, w, tm):
    m = x.shape[0]
    row = lambda n: pl.BlockSpec((tm, n), lambda i: (i, 0))
    return pl.pallas_call(
        _outproj_kernel,
        grid=(m // tm,),
        in_specs=[row(D_MODEL), row(oa.shape[1]), row(ob.shape[1]), pl.BlockSpec(w.shape, lambda i: (0, 0))],
        out_specs=row(D_MODEL),
        out_shape=jax.ShapeDtypeStruct((m, D_MODEL), F32),
        compiler_params=_cparams(("parallel",), 32),
        name="outproj",
    )(x, oa, ob, w)


def _mixer_prompt_kernel(*refs, groups, convs, gated_ffn, final, tm, tf):
    x_ref, nw_ref = refs[0:2]
    up = refs[2:2 + groups]
    cw = refs[2 + groups:2 + groups + convs]
    down_ref = refs[2 + groups + convs]
    pos = 3 + groups + convs
    fw_ref = refs[pos] if final else None
    pos += int(final)
    y_ref = refs[pos]
    tails = refs[pos + 1:pos + 1 + convs]
    xn_ref, acc_ref, ubuf_ref, carry_ref = refs[pos + 1 + convs:]
    t = pl.program_id(1)
    j = pl.program_id(2)

    @pl.when(j == 0)
    def _():
        xn_ref[...] = _rms(x_ref[0], nw_ref[...]).astype(BF16)
        acc_ref[...] = jnp.zeros_like(acc_ref)

    xn = xn_ref[...]
    u = [_dot(xn, w[...]) for w in up]
    conv_in = [u[0], u[1]] if gated_ffn else [u[1] * u[0]]
    conv_out = []
    for c, z in enumerate(conv_in):
        @pl.when(t == 0)
        def _():
            carry_ref[j, c] = jnp.zeros((8, tf), F32)

        ubuf_ref[c, 0:8] = carry_ref[j, c]
        ubuf_ref[c, 8:tm + 8] = z
        w = cw[c][...]
        conv_out.append(w[0:1] * ubuf_ref[c, pl.ds(6, tm), :] + w[1:2] * ubuf_ref[c, pl.ds(7, tm), :]
                        + w[2:3] * z)
        last = z[tm - 8:tm]
        carry_ref[j, c] = last
        tails[c][0, 0] = last
    if gated_ffn:
        act = conv_out[0] * jax.nn.sigmoid(conv_out[0]) * conv_out[1]
    else:
        act = u[2] * conv_out[0]
    acc_ref[...] += _dot(act.astype(BF16), down_ref[...])

    @pl.when(j == pl.num_programs(2) - 1)
    def _():
        y = x_ref[0] + acc_ref[...]
        y_ref[0] = _rms(y, fw_ref[...]) if final else y


def _mixer_prompt(x, nw, w_up, conv_w, w_down, gated_ffn, final_w=None, tm=1024, tf=256):
    b, s, d = x.shape
    f = w_down.shape[0]
    groups = w_up.shape[1] // f
    convs = conv_w.shape[1] // f
    nf = f // tf
    final = final_w is not None
    in_specs = [pl.BlockSpec((1, tm, d), lambda i, t, j: (i, t, 0)),
                pl.BlockSpec(nw.shape, lambda i, t, j: (0, 0))]
    in_specs += [pl.BlockSpec((d, tf), lambda i, t, j, g=g: (0, g * nf + j)) for g in range(groups)]
    in_specs += [pl.BlockSpec((CONV_W, tf), lambda i, t, j, c=c: (0, c * nf + j)) for c in range(convs)]
    in_specs += [pl.BlockSpec((tf, d), lambda i, t, j: (j, 0))]
    args = [x, nw] + [w_up] * groups + [conv_w] * convs + [w_down]
    if final:
        in_specs.append(pl.BlockSpec(final_w.shape, lambda i, t, j: (0, 0)))
        args.append(final_w)
    out_specs = [pl.BlockSpec((1, tm, d), lambda i, t, j: (i, t, 0))]
    out_specs += [pl.BlockSpec((1, 1, 8, tf), lambda i, t, j: (i, t, 0, j))] * convs
    out_shape = [jax.ShapeDtypeStruct((b, s, d), F32)]
    out_shape += [jax.ShapeDtypeStruct((b, s // tm, 8, f), F32)] * convs
    res = pl.pallas_call(
        functools.partial(_mixer_prompt_kernel, groups=groups, convs=convs, gated_ffn=gated_ffn,
                          final=final, tm=tm, tf=tf),
        grid=(b, s // tm, nf),
        in_specs=in_specs,
        out_specs=out_specs,
        out_shape=out_shape,
        scratch_shapes=[pltpu.VMEM((tm, d), BF16), pltpu.VMEM((tm, d), F32),
                        pltpu.VMEM((convs, tm + 8, tf), F32), pltpu.VMEM((nf, convs, 8, tf), F32)],
        compiler_params=_cparams(("parallel", "arbitrary", "arbitrary"), 56),
        name="ffn_prompt" if gated_ffn else "sconv_prompt",
    )(*args)
    tail = jnp.concatenate([r[:, -1] for r in res[1:]], axis=-1)
    return res[0], tail[:, 8 - (CONV_W - 1):, :]


def _mixer_sample_kernel(*refs, groups, convs, gated_ffn, final):
    x_ref, nw_ref = refs[0:2]
    up = refs[2:2 + groups]
    cw = refs[2 + groups:2 + groups + convs]
    st = refs[2 + groups + convs:2 + groups + 2 * convs]
    down_ref = refs[2 + groups + 2 * convs]
    pos = 3 + groups + 2 * convs
    fw_ref = refs[pos] if final else None
    pos += int(final)
    y_ref = refs[pos]
    news = refs[pos + 1:pos + 1 + convs]
    xn_ref, acc_ref = refs[pos + 1 + convs:]
    j = pl.program_id(0)

    @pl.when(j == 0)
    def _():
        xn_ref[...] = _rms(x_ref[...], nw_ref[...]).astype(BF16)
        acc_ref[...] = jnp.zeros_like(acc_ref)

    xn = xn_ref[...]
    u = [_dot(xn, w[...]) for w in up]
    conv_in = [u[0], u[1]] if gated_ffn else [u[1] * u[0]]
    conv_out = []
    for c, z in enumerate(conv_in):
        w = cw[c][...]
        conv_out.append(w[0:1] * st[c][:, 0, :] + w[1:2] * st[c][:, 1, :] + w[2:3] * z)
        news[c][...] = z
    if gated_ffn:
        act = conv_out[0] * jax.nn.sigmoid(conv_out[0]) * conv_out[1]
    else:
        act = u[2] * conv_out[0]
    acc_ref[...] += _dot(act.astype(BF16), down_ref[...])

    @pl.when(j == pl.num_programs(0) - 1)
    def _():
        y = x_ref[...] + acc_ref[...]
        y_ref[...] = _rms(y, fw_ref[...]) if final else y


def _mixer_sample(x, nw, w_up, conv_w, w_down, state, gated_ffn, final_w=None, tf=256):
    bd, d = x.shape
    f = w_down.shape[0]
    groups = w_up.shape[1] // f
    convs = conv_w.shape[1] // f
    nf = f // tf
    final = final_w is not None
    in_specs = [pl.BlockSpec((bd, d), lambda j: (0, 0)), pl.BlockSpec(nw.shape, lambda j: (0, 0))]
    in_specs += [pl.BlockSpec((d, tf), lambda j, g=g: (0, g * nf + j)) for g in range(groups)]
    in_specs += [pl.BlockSpec((CONV_W, tf), lambda j, c=c: (0, c * nf + j)) for c in range(convs)]
    in_specs += [pl.BlockSpec((bd, CONV_W - 1, tf), lambda j, c=c: (0, 0, c * nf + j)) for c in range(convs)]
    in_specs += [pl.BlockSpec((tf, d), lambda j: (j, 0))]
    args = [x, nw] + [w_up] * groups + [conv_w] * convs + [state] * convs + [w_down]
    if final:
        in_specs.append(pl.BlockSpec(final_w.shape, lambda j: (0, 0)))
        args.append(final_w)
    out_specs = [pl.BlockSpec((bd, d), lambda j: (0, 0))]
    out_specs += [pl.BlockSpec((bd, tf), lambda j: (0, j))] * convs
    out_shape = [jax.ShapeDtypeStruct((bd, d), F32)] + [jax.ShapeDtypeStruct((bd, f), F32)] * convs
    res = pl.pallas_call(
        functools.partial(_mixer_sample_kernel, groups=groups, convs=convs, gated_ffn=gated_ffn, final=final),
        grid=(nf,),
        in_specs=in_specs,
        out_specs=out_specs,
        out_shape=out_shape,
        scratch_shapes=[pltpu.VMEM((bd, d), BF16), pltpu.VMEM((bd, d), F32)],
        compiler_params=_cparams(("arbitrary",), 32),
        name="ffn_sample" if gated_ffn else "sconv_sample",
    )(*args)
    new = jnp.concatenate(res[1:], axis=-1)
    return res[0], jnp.concatenate([state[:, 1:], new[:, None, :]], axis=1)


def _pack_kv(kv, b, s):
    return kv.reshape(b, s, 2, B_KV_HEADS, B_HD).transpose(0, 3, 1, 2, 4).reshape(b, B_KV_HEADS, s, 128).astype(BF16)


def _pad_in_weight(w):
    c = HG_W + QB_W + 3 * KV_W
    per = 3 * B_GROUP
    gates = [jnp.pad(w[:, c + h * per:c + (h + 1) * per], ((0, 0), (0, GATE_PAD - per)))
             for h in range(B_KV_HEADS)]
    return jnp.concatenate([w[:, :c]] + gates, axis=1).astype(BF16)


def _per_head_rows(z, width):
    bd = z.shape[0]
    z = z.reshape(bd, B_KV_HEADS, B_GROUP, -1)
    return jnp.pad(z, ((0, 0), (0, 0), (0, 8 - B_GROUP), (0, width - z.shape[-1])))


def kernel(x_prompt, x_sample, cache_cmp_kv, cache_slc_kv, state_win_kv, state_hgrn, state_sconv, state_ffn,
           page_table, norm_mix, norm_ffn, norm_final, ab_w_in, ab_w_out, hgrn_lb_logits, hgrn_norm,
           cmp_pe, cmp_w1, cmp_w2, c_w_in, c_conv, c_w_out, ffn_up, ffn_conv, ffn_down):
    b, s, d = x_prompt.shape
    bd, t, _ = x_sample.shape
    assert t == 1 and d == D_MODEL
    depth = norm_mix.shape[0]
    n_a = ab_w_in.shape[0]
    wb = state_win_kv.shape[2]
    n_pages = page_table.shape[1]
    past = n_pages * PAGE_SIZE
    assert wb == WINDOW and past % (PAGES_PER_STEP * PAGE_SIZE) == 0 and s % 1024 == 0
    hh = jnp.arange(1, B_HEADS + 1, dtype=F32)
    slopes = 2.0 ** (-8.0 * hh / B_HEADS)
    lb_logits = hgrn_lb_logits.astype(F32)
    row2 = lambda v: v.reshape(1, -1).astype(F32)

    xp = x_prompt.reshape(b * s, d)
    xs = x_sample.reshape(bd, d)
    outs_p = {k: [] for k in ("cmp", "slc", "win", "hg", "sc", "ff")}
    outs_s = {k: [] for k in ("cmp", "slc", "win", "hg", "sc", "ff")}
    y_prompt = y_sample = None
    for l in range(depth):
        last = l == depth - 1
        if l % 2 == 0:
            la = l // 2
            w_in = _pad_in_weight(ab_w_in[la])
            w_out = ab_w_out[la].astype(BF16)
            nw = row2(norm_mix[l])
            hnw = row2(hgrn_norm[la])
            pe_b, w1bd, w2_packed = _cmp_weights(cmp_pe[la], cmp_w1[la], cmp_w2[la], packed=True)
            _, _, w2_plain = _cmp_weights(cmp_pe[la], cmp_w1[la], cmp_w2[la], packed=False)
            hg, qb, kc, ksl, kw, gb = _inproj(xp, nw, w_in, tm=512)
            oa, hst = _hgrn_prompt(hg, lb_logits, hnw, la, b, s)
            comp = _compress_prompt(kc.reshape(b, 2 * s, KV_W // 2), pe_b, w1bd, w2_packed)
            n_cmp = s // D_CMP
            nblk = s // L_SLC
            cover = _cover_matrix(n_cmp, nblk, nblk)
            ob = _nsa_prompt(slopes, qb.reshape(b, s, QB_W), gb.reshape(b, s, -1), comp,
                             _pack_kv(ksl, b, s), _pack_kv(kw, b, s), cover)
            xp = _outproj(xp, oa, ob.reshape(b * s, QB_W), w_out, tm=512)
            kv6 = lambda z, n, tt: z.reshape(n, tt, 2, B_KV_HEADS, B_HD)
            outs_p["cmp"].append(kv6(kc, b, s))
            outs_p["slc"].append(kv6(ksl, b, s))
            kw6 = kv6(kw, b, s)
            outs_p["win"].append(jnp.pad(kw6, ((0, 0), (max(wb - s, 0), 0), (0, 0), (0, 0), (0, 0)))[:, -wb:])
            outs_p["hg"].append(hst)
            hg, qb, kc, ksl, kw, gb = _inproj(xs, nw, w_in, tm=bd)
            oa, hst = _hgrn_sample(hg, state_hgrn[la], lb_logits, hnw, la)
            pre = _cmp_pages(page_table, cache_cmp_kv, la, pe_b, w1bd)
            n_cmp = pre.shape[1]
            nblk = past // L_SLC + 1
            cover = _cover_matrix(n_cmp, nblk, -(-nblk // 128) * 128)
            q8 = jnp.stack([jnp.pad(_per_head_rows(qb, B_HD)[:, h], ((0, 0), (0, 0), (h * B_HD, KV_W - (h + 1) * B_HD)))
                            for h in range(B_KV_HEADS)], axis=1)
            idx, o_cmp = _nsa_sample_select(slopes, pre, kc.reshape(bd, 1, KV_W), q8, pe_b, w1bd, w2_plain,
                                            cover, past)
            idx = idx[:, :B_KV_HEADS, :N_SEL].reshape(bd, B_KV_HEADS * N_SEL)
            gate8 = _per_head_rows(jnp.concatenate([gb[:, h * GATE_PAD:h * GATE_PAD + 3 * B_GROUP]
                                                    for h in range(B_KV_HEADS)], axis=1), 128)
            win_state = state_win_kv[la].reshape(bd, wb, KV_W)
            o8 = _nsa_sample_attend(page_table, idx, slopes, cache_slc_kv, la, q8, ksl.reshape(bd, 1, KV_W),
                                    kw.reshape(bd, 1, KV_W), win_state, gate8, o_cmp, past)
            ob = jnp.concatenate([o8[:, h, :B_GROUP, 2 * B_HD + h * B_HD:2 * B_HD + (h + 1) * B_HD]
                                  .reshape(bd, B_GROUP * B_HD) for h in range(B_KV_HEADS)], axis=1)
            xs = _outproj(xs, oa, ob.astype(BF16), w_out, tm=bd)
            outs_s["cmp"].append(kv6(kc, bd, t))
            outs_s["slc"].append(kv6(ksl, bd, t))
            outs_s["win"].append(jnp.concatenate([state_win_kv[la], kv6(kw, bd, t)], axis=1)[:, t:])
            outs_s["hg"].append(hst)
        else:
            lc = l // 2
            xp3, tail = _mixer_prompt(xp.reshape(b, s, d), row2(norm_mix[l]), c_w_in[lc].astype(BF16),
                                      c_conv[lc], c_w_out[lc].astype(BF16), gated_ffn=False)
            xp = xp3.reshape(b * s, d)
            outs_p["sc"].append(tail)
            xs, new = _mixer_sample(xs, row2(norm_mix[l]), c_w_in[lc].astype(BF16), c_conv[lc],
                                    c_w_out[lc].astype(BF16), state_sconv[lc], gated_ffn=False)
            outs_s["sc"].append(new)
        fw = row2(norm_final) if last else None
        xp3, tail = _mixer_prompt(xp.reshape(b, s, d), row2(norm_ffn[l]), ffn_up[l].astype(BF16), ffn_conv[l],
                                  ffn_down[l].astype(BF16), gated_ffn=True, final_w=fw)
        xp = xp3.reshape(b * s, d)
        outs_p["ff"].append(tail)
        xs, new = _mixer_sample(xs, row2(norm_ffn[l]), ffn_up[l].astype(BF16), ffn_conv[l],
                                ffn_down[l].astype(BF16), state_ffn[l], gated_ffn=True, final_w=fw)
        outs_s["ff"].append(new)
    y_prompt = xp.reshape(b, s, d)
    y_sample = xs.reshape(bd, t, d)
    return (y_prompt, y_sample,
            jnp.stack(outs_p["cmp"], axis=2), jnp.stack(outs_p["slc"], axis=2), jnp.stack(outs_p["win"], axis=0),
            jnp.stack(outs_p["hg"], axis=0), jnp.stack(outs_p["sc"], axis=0), jnp.stack(outs_p["ff"], axis=0),
            jnp.stack(outs_s["cmp"], axis=2), jnp.stack(outs_s["slc"], axis=2), jnp.stack(outs_s["win"], axis=0),
            jnp.stack(outs_s["hg"], axis=0), jnp.stack(outs_s["sc"], axis=0), jnp.stack(outs_s["ff"], axis=0))
```

```python
import functools

import jax
import jax.numpy as jnp
from jax import lax
from jax.experimental import pallas as pl
from jax.experimental.pallas import tpu as pltpu

F32 = jnp.float32
BF16 = jnp.bfloat16
NEG_INF = float("-inf")

D_MODEL = 1024
PAGE_SIZE = 128
A_HEADS = 4
A_DK = 128
A_DV = 128
B_HEADS = 8
B_KV_HEADS = 2
B_GROUP = B_HEADS // B_KV_HEADS
B_HD = 64
L_CMP = 32
D_CMP = 16
CMP_HID = 64
L_SLC = 64
SLC_SHIFT = 6
N_SEL = 16
WINDOW = 512
SEL_FORCE = 1e9
CONV_W = 3
EPS = 1e-6
KV_W = 2 * B_KV_HEADS * B_HD
HG_W = 4 * A_HEADS * A_DK
QB_W = B_HEADS * B_HD
GATE_PAD = 128
IN_PAD = HG_W + QB_W + 3 * KV_W + B_KV_HEADS * GATE_PAD

V7X_VMEM_BYTES = 64 * 2**20


def _cparams(sem, vmem_mb):
    assert vmem_mb * 2**20 < V7X_VMEM_BYTES
    return pltpu.CompilerParams(dimension_semantics=sem, vmem_limit_bytes=vmem_mb * 2**20)


def _dot(a, b):
    return jnp.dot(a, b, preferred_element_type=F32)


def _dot_nt(a, b):
    return lax.dot_general(a, b, (((1,), (1,)), ((), ())), preferred_element_type=F32)


def _dot_tn(a, b):
    return lax.dot_general(a, b, (((0,), (0,)), ((), ())), preferred_element_type=F32)


def _split3(x):
    hi = x.astype(BF16)
    r1 = x - hi.astype(F32)
    mid = r1.astype(BF16)
    lo = (r1 - mid.astype(F32)).astype(BF16)
    return hi, mid, lo


def _dot3_lhs(x, w):
    hi, mid, lo = _split3(x)
    return _dot(hi, w) + _dot(mid, w) + _dot(lo, w)


def _dot3_rhs(w, x):
    hi, mid, lo = _split3(x)
    return _dot(w, hi) + _dot(w, mid) + _dot(w, lo)


def _rms(x, w):
    return x * lax.rsqrt(jnp.mean(x * x, axis=-1, keepdims=True) + EPS) * w


def _masked_softmax(s, mask):
    s = jnp.where(mask, s, NEG_INF)
    m = jnp.max(s, axis=-1, keepdims=True)
    m = jnp.where(m > NEG_INF, m, 0.0)
    e = jnp.exp(s - m)
    d = jnp.sum(e, axis=-1, keepdims=True)
    return e * (1.0 / jnp.where(d > 0, d, 1.0))


def _iota(shape, dim):
    return lax.broadcasted_iota(jnp.int32, shape, dim)


def _top_blocks(score, jb, n_sel):
    big = float(score.shape[-1])
    jbf = jb.astype(F32)

    def body(_, carry):
        sc, sel = carry
        m = jnp.max(sc, axis=-1, keepdims=True)
        idx = jnp.min(jnp.where(sc == m, jbf, big), axis=-1, keepdims=True)
        pick = jbf == idx
        return jnp.where(pick, NEG_INF, sc), jnp.where(pick, 1.0, sel)

    _, sel = lax.fori_loop(0, n_sel, body, (score, jnp.zeros_like(score)))
    return sel


def _inproj_kernel(x_ref, nw_ref, w_ref, hg_ref, qb_ref, kc_ref, ksl_ref, kw_ref, gb_ref):
    xn = _rms(x_ref[...], nw_ref[...]).astype(BF16)

    def proj(c0, n):
        return _dot(xn, w_ref[:, c0:c0 + n])

    hg_ref[...] = proj(0, HG_W)
    qb_ref[...] = (proj(HG_W, QB_W) * (B_HD ** -0.5)).astype(BF16)
    c = HG_W + QB_W
    kc_ref[...] = proj(c, KV_W)
    ksl_ref[...] = proj(c + KV_W, KV_W)
    kw_ref[...] = proj(c + 2 * KV_W, KV_W)
    gb_ref[...] = proj(c + 3 * KV_W, B_KV_HEADS * GATE_PAD)


def _inproj(x, nw, w_pad, tm):
    m = x.shape[0]
    row = lambda n: pl.BlockSpec((tm, n), lambda i: (i, 0))
    full = lambda a: pl.BlockSpec(a.shape, lambda i: (0,) * a.ndim)
    widths = (HG_W, QB_W, KV_W, KV_W, KV_W, B_KV_HEADS * GATE_PAD)
    dtypes = (F32, BF16, F32, F32, F32, F32)
    return pl.pallas_call(
        _inproj_kernel,
        grid=(m // tm,),
        in_specs=[row(D_MODEL), full(nw), full(w_pad)],
        out_specs=[row(n) for n in widths],
        out_shape=[jax.ShapeDtypeStruct((m, n), d) for n, d in zip(widths, dtypes)],
        compiler_params=_cparams(("parallel",), 48),
        name="inproj",
    )(x, nw, w_pad)


def _lower_bound(logit_ref, la):
    lg = logit_ref[...]
    e = jnp.exp(lg - jnp.max(lg, axis=0, keepdims=True))
    return jnp.sum(e[:la + 1], axis=0, keepdims=True) / jnp.sum(e, axis=0, keepdims=True)


def _hgrn_post(o, nw, ga):
    o = o * lax.rsqrt(jnp.mean(o * o, axis=-1, keepdims=True) + EPS)
    return o * nw * jax.nn.sigmoid(ga)


def _hgrn_prompt_kernel(hg_ref, lg_ref, nw_ref, oa_ref, st_ref, s_ref, *, la, chunk, sub):
    ci = pl.program_id(1)

    @pl.when(ci == 0)
    def _():
        s_ref[...] = jnp.zeros_like(s_ref)

    lb = _lower_bound(lg_ref, la)
    tri = (_iota((chunk, chunk), 0) >= _iota((chunk, chunk), 1)).astype(BF16)
    trow = _iota((chunk, 1), 0)
    srow = _iota((sub, 1), 0)
    lane = _iota((sub, chunk), 1)
    for h in range(A_HEADS):
        hs = slice(h * A_DK, (h + 1) * A_DK)
        q = hg_ref[:, h * A_DK:(h + 1) * A_DK]
        fl = hg_ref[:, 512 + h * A_DK:512 + (h + 1) * A_DK]
        v = hg_ref[:, 1024 + h * A_DV:1024 + (h + 1) * A_DV]
        ga = hg_ref[:, 1536 + h * A_DV:1536 + (h + 1) * A_DV]
        lbh = lb[:, hs]
        f = lbh + (1.0 - lbh) * jax.nn.sigmoid(fl)
        a = _dot3_rhs(tri, jnp.log(f))
        k = 1.0 - f
        st = s_ref[h]
        vb = v.astype(BF16)
        o = _dot_nt((q * jnp.exp(a)).astype(BF16), st.astype(BF16))
        rows = []
        for i in range(chunk // sub):
            r0 = i * sub
            qi, ai = q[r0:r0 + sub], a[r0:r0 + sub]
            att = jnp.zeros((sub, chunk), F32)
            for s in range(sub):
                dec = jnp.exp(jnp.where(srow >= s, ai - a[r0 + s:r0 + s + 1], NEG_INF))
                col = jnp.sum(qi * dec * k[r0 + s:r0 + s + 1], axis=-1, keepdims=True)
                att = jnp.where(lane == r0 + s, col, att)
            if i > 0:
                ref_a = a[r0 - 1:r0]
                qt = qi * jnp.exp(ai - ref_a)
                kt = k * jnp.exp(jnp.where(trow < r0, ref_a - a, NEG_INF))
                att = att + _dot_nt(qt.astype(BF16), kt.astype(BF16))
            rows.append(att)
        att = jnp.concatenate(rows, axis=0)
        o = o + _dot(att.astype(BF16), vb)
        a_end = a[chunk - 1:chunk]
        khat = k * jnp.exp(a_end - a)
        s_ref[h] = st * jnp.exp(a_end) + _dot_tn(vb, khat.astype(BF16))
        oa_ref[:, hs] = _hgrn_post(o, nw_ref[:, hs], ga).astype(BF16)

    @pl.when(ci == pl.num_programs(1) - 1)
    def _():
        for h in range(A_HEADS):
            st_ref[0, h] = s_ref[h].T


def _hgrn_prompt(hg, lb_logits, nw, la, b, s, chunk=64, sub=16):
    n = s // chunk
    full = lambda a: pl.BlockSpec(a.shape, lambda i, j: (0,) * a.ndim)
    return pl.pallas_call(
        functools.partial(_hgrn_prompt_kernel, la=la, chunk=chunk, sub=sub),
        grid=(b, n),
        in_specs=[pl.BlockSpec((chunk, HG_W), lambda i, j: (i * n + j, 0)), full(lb_logits), full(nw)],
        out_specs=[pl.BlockSpec((chunk, A_HEADS * A_DV), lambda i, j: (i * n + j, 0)),
                   pl.BlockSpec((1, A_HEADS, A_DK, A_DV), lambda i, j: (i, 0, 0, 0))],
        out_shape=[jax.ShapeDtypeStruct((b * s, A_HEADS * A_DV), BF16),
                   jax.ShapeDtypeStruct((b, A_HEADS, A_DK, A_DV), F32)],
        scratch_shapes=[pltpu.VMEM((A_HEADS, A_DV, A_DK), F32)],
        compiler_params=_cparams(("parallel", "arbitrary"), 32),
        name="hgrn_prompt",
    )(hg, lb_logits, nw)


def _hgrn_sample_kernel(hg_ref, s0_ref, lg_ref, nw_ref, oa_ref, st_ref, *, la):
    lb = _lower_bound(lg_ref, la)
    eye = (_iota((A_DK, A_DK), 0) == _iota((A_DK, A_DK), 1)).astype(F32)

    def col(r):
        return jnp.sum(eye * r, axis=1, keepdims=True)

    for h in range(A_HEADS):
        hs = slice(h * A_DK, (h + 1) * A_DK)
        q = hg_ref[0, :, h * A_DK:(h + 1) * A_DK]
        fl = hg_ref[0, :, 512 + h * A_DK:512 + (h + 1) * A_DK]
        v = hg_ref[0, :, 1024 + h * A_DV:1024 + (h + 1) * A_DV]
        ga = hg_ref[0, :, 1536 + h * A_DV:1536 + (h + 1) * A_DV]
        lbh = lb[:, hs]
        f = lbh + (1.0 - lbh) * jax.nn.sigmoid(fl)
        sn = col(f) * s0_ref[0, h] + col(1.0 - f) * v
        st_ref[0, h] = sn
        o = jnp.sum(col(q) * sn, axis=0, keepdims=True)
        oa_ref[0, :, hs] = _hgrn_post(o, nw_ref[:, hs], ga).astype(BF16)


def _hgrn_sample(hg, s0, lb_logits, nw, la):
    bd = hg.shape[0]
    full = lambda a: pl.BlockSpec(a.shape, lambda i: (0,) * a.ndim)
    st_spec = pl.BlockSpec((1, A_HEADS, A_DK, A_DV), lambda i: (i, 0, 0, 0))
    oa, st = pl.pallas_call(
        functools.partial(_hgrn_sample_kernel, la=la),
        grid=(bd,),
        in_specs=[pl.BlockSpec((1, 1, HG_W), lambda i: (i, 0, 0)), st_spec, full(lb_logits), full(nw)],
        out_specs=[pl.BlockSpec((1, 1, A_HEADS * A_DV), lambda i: (i, 0, 0)), st_spec],
        out_shape=[jax.ShapeDtypeStruct((bd, 1, A_HEADS * A_DV), BF16),
                   jax.ShapeDtypeStruct(s0.shape, F32)],
        compiler_params=_cparams(("parallel",), 16),
        name="hgrn_sample",
    )(hg.reshape(bd, 1, HG_W), s0, lb_logits, nw)
    return oa.reshape(bd, A_HEADS * A_DV), st


def _cmp_weights(pe, w1, w2, packed):
    eye = jnp.eye(B_KV_HEADS, dtype=F32)
    w1r = w1.reshape(2, L_CMP, B_HD, CMP_HID)
    w1bd = jnp.einsum("kidm,hH->ikhdHm", w1r, eye).reshape(L_CMP, 2, KV_W // 2, KV_W // 2)
    pe_b = jnp.broadcast_to(pe.transpose(1, 0, 2)[:, :, None, :], (L_CMP, 2, B_KV_HEADS, B_HD))
    if not packed:
        w2bd = jnp.einsum("kmd,kK,hH->khmKHd", w2, eye, eye).reshape(KV_W, KV_W)
        return pe_b.reshape(L_CMP, KV_W), w1bd.astype(BF16), w2bd.astype(BF16)
    half = KV_W // 2
    zero = jnp.zeros((half, KV_W), F32)
    k_blk = jnp.concatenate([w2[0], jnp.zeros_like(w2[0])], axis=1)
    v_blk = jnp.concatenate([w2[1], w2[1]], axis=1)
    w2k = jnp.concatenate([jnp.einsum("ml,hH->hmHl", k_blk, eye).reshape(half, KV_W), zero], axis=0)
    w2v = jnp.concatenate([zero, jnp.einsum("ml,hH->hmHl", v_blk, eye).reshape(half, KV_W)], axis=0)
    return pe_b.reshape(L_CMP, KV_W), w1bd.astype(BF16), (w2k.astype(BF16), w2v.astype(BF16))


def _chunk_preacts(load, n, pe_ref, w1_ref):
    half = KV_W // 2
    p0 = [jnp.zeros((n, half), F32) for _ in range(2)]
    p1 = [jnp.zeros((n, half), F32) for _ in range(2)]
    for i in range(D_CMP):
        for kv in range(2):
            x = load(i, kv)
            ls = slice(kv * half, (kv + 1) * half)
            p0[kv] = p0[kv] + _dot((x + pe_ref[i:i + 1, ls]).astype(BF16), w1_ref[i, kv])
            p1[kv] = p1[kv] + _dot((x + pe_ref[D_CMP + i:D_CMP + i + 1, ls]).astype(BF16), w1_ref[D_CMP + i, kv])
    return jnp.concatenate(p0, axis=1), jnp.concatenate(p1, axis=1)


def _with_position(k, pos, lane):
    col = lane & 127
    return jnp.where(col == B_HD, (pos >> 7).astype(F32), jnp.where(col == B_HD + 1, (pos & 127).astype(F32), k))


def _compress_kernel(kc_ref, pe_ref, w1_ref, w2k_ref, w2v_ref, ck_ref, cv_ref):
    n = ck_ref.shape[1]
    p0, p1 = _chunk_preacts(lambda i, kv: kc_ref[0, pl.ds(2 * i + kv, n, stride=2 * D_CMP), :], n, pe_ref, w1_ref)
    pre = p0 + pltpu.roll(p1, n - 1, axis=0)
    hid = (pre * jax.nn.sigmoid(pre)).astype(BF16)
    cpos = _iota((n, 1), 0) * D_CMP + (L_CMP - 1)
    ck_ref[0] = _with_position(_dot(hid, w2k_ref[...]), cpos, _iota((n, KV_W), 1)).astype(BF16)
    cv_ref[0] = _dot(hid, w2v_ref[...]).astype(BF16)


def _compress_prompt(kc, pe_b, w1bd, w2kv):
    b, s2, _ = kc.shape
    n = s2 // (2 * D_CMP)
    full = lambda a: pl.BlockSpec(a.shape, lambda i: (0,) * a.ndim)
    out = pl.BlockSpec((1, n, KV_W), lambda i: (i, 0, 0))
    return pl.pallas_call(
        _compress_kernel,
        grid=(b,),
        in_specs=[pl.BlockSpec((1, s2, KV_W // 2), lambda i: (i, 0, 0)), full(pe_b), full(w1bd),
                  full(w2kv[0]), full(w2kv[1])],
        out_specs=[out, out],
        out_shape=[jax.ShapeDtypeStruct((b, n, KV_W), BF16)] * 2,
        compiler_params=_cparams(("parallel",), 48),
        name="compress_prompt",
    )(kc, pe_b, w1bd, *w2kv)


MASK_BIAS = 2.0 ** 100

def _top_blocks_t(score_t, n_sel):
    nblk = score_t.shape[0]
    blk = _iota(score_t.shape, 0).astype(F32)

    def body(_, carry):
        sc, sel = carry
        m = jnp.max(sc, axis=0, keepdims=True)
        idx = jnp.min(jnp.where(sc == m, blk, float(nblk)), axis=0, keepdims=True)
        pick = blk == idx
        return jnp.where(pick, NEG_INF, sc), jnp.where(pick, 1.0, sel)

    _, sel = lax.fori_loop(0, n_sel, body, (score_t, jnp.zeros_like(score_t)), unroll=True)
    return sel


def _nsa_prompt_kernel(sl_ref, q_ref, g_ref, ck_ref, cv_ref, sk_ref, sv_ref, wk_ref, wv_ref, cover_ref, o_ref,
                       m_ref, l_ref, acc_ref, *, tq, tk):
    kvh = pl.program_id(1)
    q0 = pl.program_id(2) * tq
    rows = B_GROUP * tq
    lane = _iota((tq, 128), 1)
    lo = lane < B_HD
    qf = q_ref[0].astype(F32)
    left = []
    for g in range(B_GROUP):
        pair = qf[:, 128 * (g // 2):128 * (g // 2) + 128]
        if g % 2:
            pair = pltpu.roll(pair, B_HD, axis=1)
        slope = sl_ref[kvh * B_GROUP + g]
        pos_cols = jnp.where(lane == B_HD, slope * 128.0, jnp.where(lane == B_HD + 1, slope, 0.0))
        left.append(jnp.where(lo, pair, pos_cols).astype(BF16))
    ql = jnp.concatenate(left, axis=0)
    qpos = q0 + (_iota((rows, 1), 0) & (tq - 1))

    nc = ck_ref.shape[1]
    cpos = _iota((1, nc), 1) * D_CMP + (L_CMP - 1)
    span = WINDOW + tq
    start = pl.multiple_of(jnp.maximum(q0 - WINDOW, 0), 128)
    wpos = start + _iota((1, span), 1)
    s_cmp = _dot_nt(ql, ck_ref[0])
    s_win = _dot_nt(ql, wk_ref[0, 0, pl.ds(start, span), :])
    p = _masked_softmax(s_cmp, cpos <= qpos)
    pw = _masked_softmax(s_win, (wpos <= qpos) & (wpos > qpos - WINDOW))
    o_cmp = _dot(p.astype(BF16), cv_ref[0])
    o_win = _dot(pw.astype(BF16), wv_ref[0, 0, pl.ds(start, span), :])
    imp = p[0:tq]
    for g in range(1, B_GROUP):
        imp = imp + p[g * tq:(g + 1) * tq]

    nblk = cover_ref.shape[1]
    jb = _iota((1, nblk), 1)
    cur = (q0 + _iota((tq, 1), 0)) >> SLC_SHIFT
    score = _dot3_lhs(imp, cover_ref[...])
    score = jnp.where((jb == 0) | (jb == cur) | (jb == cur - 1), SEL_FORCE, score)
    score = jnp.where(jb > cur, NEG_INF, score)
    sel = _top_blocks_t(score.T, min(N_SEL, nblk)).T

    bias = ((sel - 1.0) * MASK_BIAS).astype(BF16)
    qs = jnp.concatenate([ql, jnp.concatenate([bias] * B_GROUP, axis=0)], axis=1)
    m_ref[...] = jnp.full_like(m_ref, NEG_INF)
    l_ref[...] = jnp.zeros_like(l_ref)
    acc_ref[...] = jnp.zeros_like(acc_ref)

    def scores(kt):
        return _dot_nt(qs, sk_ref[0, 0, pl.ds(pl.multiple_of(kt * tk, tk), tk), :])

    def update(s, kt, causal):
        k0 = pl.multiple_of(kt * tk, tk)
        if causal:
            s = jnp.where(k0 + _iota((1, tk), 1) <= qpos, s, NEG_INF)
        m_old = m_ref[...]
        m_new = jnp.maximum(m_old, jnp.max(s, axis=-1, keepdims=True))
        alpha = jnp.exp(m_old - m_new)
        e = jnp.exp(s - m_new)
        l_ref[...] = alpha * l_ref[...] + jnp.sum(e, axis=-1, keepdims=True)
        acc_ref[...] = alpha * acc_ref[...] + _dot(e.astype(BF16), sv_ref[0, 0, pl.ds(k0, tk), :])
        m_ref[...] = m_new

    n_full = q0 // tk

    def full_tile(kt, s):
        s_next = scores(kt + 1)
        update(s, kt, False)
        return s_next

    update(lax.fori_loop(0, n_full, full_tile, scores(0)), n_full, True)
    o_slc = acc_ref[...] * (1.0 / l_ref[...])

    gate = jax.nn.sigmoid(g_ref[0])
    outs = []
    for g in range(B_GROUP):
        rs = slice(g * tq, (g + 1) * tq)
        outs.append(gate[:, 3 * g:3 * g + 1] * o_cmp[rs] + gate[:, 3 * g + 1:3 * g + 2] * o_slc[rs]
                    + gate[:, 3 * g + 2:3 * g + 3] * o_win[rs])
    for j in range(B_GROUP // 2):
        o_ref[0, :, 128 * j:128 * j + 128] = jnp.where(lo, outs[2 * j], outs[2 * j + 1]).astype(BF16)


def _nsa_prompt(slopes, qb, gb, ck, cv, sk, sv, wk, wv, cover, tq=128, tk=512):
    b, s, _ = qb.shape
    half = QB_W // B_KV_HEADS
    rows = B_GROUP * tq
    assert cover.shape[1] == 128 and s % tk == 0 and tk % tq == 0
    seq = lambda a: pl.BlockSpec((1, 1) + a.shape[2:], lambda i, h, t, sl: (i, h, 0, 0))
    grid_spec = pltpu.PrefetchScalarGridSpec(
        num_scalar_prefetch=1,
        grid=(b, B_KV_HEADS, s // tq),
        in_specs=[
            pl.BlockSpec((1, tq, half), lambda i, h, t, sl: (i, t, h)),
            pl.BlockSpec((1, tq, GATE_PAD), lambda i, h, t, sl: (i, t, h)),
            pl.BlockSpec((1, ck.shape[1], 128), lambda i, h, t, sl: (i, 0, h)),
            pl.BlockSpec((1, cv.shape[1], 128), lambda i, h, t, sl: (i, 0, h)),
            seq(sk), seq(sv), seq(wk), seq(wv),
            pl.BlockSpec(cover.shape, lambda i, h, t, sl: (0, 0)),
        ],
        out_specs=pl.BlockSpec((1, tq, half), lambda i, h, t, sl: (i, t, h)),
        scratch_shapes=[pltpu.VMEM((rows, 1), F32), pltpu.VMEM((rows, 1), F32), pltpu.VMEM((rows, 128), F32)],
    )
    return pl.pallas_call(
        functools.partial(_nsa_prompt_kernel, tq=tq, tk=tk),
        grid_spec=grid_spec,
        out_shape=jax.ShapeDtypeStruct((b, s, QB_W), BF16),
        compiler_params=_cparams(("parallel", "parallel", "arbitrary"), 56),
        name="nsa_prompt",
    )(slopes, qb, gb, ck, cv, sk, sv, wk, wv, cover)


def _key_value_rows(kv, b, s, with_blocks):
    z = kv.reshape(b, s, 2, B_KV_HEADS, B_HD).transpose(2, 0, 3, 1, 4)
    pos = jnp.arange(s)
    shape = (b, B_KV_HEADS, s, 1)
    cols = [z[0], jnp.broadcast_to((pos >> 7).astype(F32)[:, None], shape),
            jnp.broadcast_to((pos & 127).astype(F32)[:, None], shape),
            jnp.zeros((b, B_KV_HEADS, s, 128 - B_HD - 2), F32)]
    if with_blocks:
        onehot = (pos[:, None] >> SLC_SHIFT) == jnp.arange(s // L_SLC)[None, :]
        cols.append(jnp.broadcast_to(onehot.astype(F32), (b, B_KV_HEADS, s, s // L_SLC)))
    return jnp.concatenate(cols, axis=-1).astype(BF16), jnp.concatenate([z[1], z[1]], axis=-1).astype(BF16)


def _cover_matrix(n_cmp, n_blk, n_pad):
    ci = jnp.arange(n_cmp)[:, None] * D_CMP
    sj = jnp.arange(n_pad)[None, :] * L_SLC
    cov = (ci < sj + L_SLC) & (ci + L_CMP > sj) & (jnp.arange(n_pad)[None, :] < n_blk)
    return cov.astype(BF16)


PAGES_PER_STEP = 16


def _cmp_pages_kernel(pt_ref, *refs):
    pages = refs[:PAGES_PER_STEP]
    pe_ref, w1_ref, out_ref, rows_ref = refs[PAGES_PER_STEP:]
    per_page = PAGE_SIZE // D_CMP
    n = PAGES_PER_STEP * per_page
    half = KV_W // 2
    for k, pg in enumerate(pages):
        for kv in range(2):
            rows_ref[kv, k * PAGE_SIZE:(k + 1) * PAGE_SIZE, :] = pg[0, kv * half:(kv + 1) * half, :].T

    p0, p1 = _chunk_preacts(lambda i, kv: rows_ref[kv, pl.ds(i, n, stride=D_CMP), :], n, pe_ref, w1_ref)
    out_ref[0, :, 0:KV_W] = p0
    out_ref[0, :, KV_W:2 * KV_W] = p1


def _feature_major_pages(cache):
    return cache.transpose(0, 2, 3, 4, 5, 1).reshape(cache.shape[0], -1, cache.shape[1])


def _cmp_pages(page_table, cache, li, pe_b, w1bd):
    bd, n_pages = page_table.shape
    pool = _feature_major_pages(cache)
    steps = n_pages // PAGES_PER_STEP
    rows = PAGES_PER_STEP * PAGE_SIZE // D_CMP

    def page_spec(k):
        return pl.BlockSpec((1, KV_W, PAGE_SIZE), lambda i, j, pt: (pt[i, j * PAGES_PER_STEP + k], li, 0))

    full = lambda a: pl.BlockSpec(a.shape, lambda i, j, pt: (0,) * a.ndim)
    grid_spec = pltpu.PrefetchScalarGridSpec(
        num_scalar_prefetch=1,
        grid=(bd, steps),
        in_specs=[page_spec(k) for k in range(PAGES_PER_STEP)] + [full(pe_b), full(w1bd)],
        out_specs=pl.BlockSpec((1, rows, 2 * KV_W), lambda i, j, pt: (i, j, 0)),
        scratch_shapes=[pltpu.VMEM((2, PAGES_PER_STEP * PAGE_SIZE, KV_W // 2), F32)],
    )
    return pl.pallas_call(
        _cmp_pages_kernel,
        grid_spec=grid_spec,
        out_shape=jax.ShapeDtypeStruct((bd, steps * rows, 2 * KV_W), F32),
        compiler_params=_cparams(("parallel", "arbitrary"), 48),
        name="cmp_pages",
    )(page_table, *([pool] * PAGES_PER_STEP), pe_b, w1bd)


def _nsa_sample_select_kernel(sl_ref, pre_ref, new_ref, q_ref, pt_ref, pe_ref, w1_ref, w2_ref, cover_ref,
                              idx_ref, phys_ref, ocmp_ref, tail_ref, *, past):
    half = KV_W // 2

    def second_half(i, r):
        r8 = jnp.broadcast_to(r, (8, KV_W)).astype(BF16)
        return jnp.concatenate([_dot(r8[:, kv * half:(kv + 1) * half], w1_ref[D_CMP + i, kv]) for kv in range(2)],
                               axis=1)

    @pl.when(pl.program_id(0) == 0)
    def _():
        t = jnp.zeros((8, KV_W), F32)
        for i in range(D_CMP):
            t = t + second_half(i, pe_ref[D_CMP + i:D_CMP + i + 1])
        tail_ref[...] = t

    n = pre_ref.shape[1]
    p1_new = (tail_ref[...] + second_half(0, new_ref[0]))[0:1]
    p1 = pltpu.roll(pre_ref[0, :, KV_W:2 * KV_W], n - 1, axis=0)
    p1 = jnp.where(_iota((n, 1), 0) == n - 1, p1_new, p1)
    pre = pre_ref[0, :, 0:KV_W] + p1
    comp = _dot((pre * jax.nn.sigmoid(pre)).astype(BF16), w2_ref[...]).astype(BF16)

    qpos = past
    cpos = _iota((1, n), 1) * D_CMP + (L_CMP - 1)
    cmask = cpos <= qpos
    cdist = (qpos - cpos).astype(F32)
    row8 = _iota((8, 1), 0)
    imp = jnp.zeros((8, n), F32)
    for kvh in range(B_KV_HEADS):
        slope = jnp.zeros((8, 1), F32)
        for g in range(B_GROUP):
            slope = jnp.where(row8 == g, sl_ref[kvh * B_GROUP + g], slope)
        p = _masked_softmax(_dot_nt(q_ref[0, kvh], comp) - slope * cdist, cmask)
        p = jnp.where(row8 < B_GROUP, p, 0.0)
        ocmp_ref[0, kvh] = _dot(p.astype(BF16), comp)
        imp = jnp.where(row8 == kvh, jnp.sum(p, axis=0, keepdims=True), imp)

    nblk = cover_ref.shape[1]
    jb = _iota((1, nblk), 1)
    cur = qpos // L_SLC
    score = _dot3_lhs(imp, cover_ref[...])
    score = jnp.where((jb == 0) | (jb == cur) | (jb == cur - 1), SEL_FORCE, score)
    score = jnp.where(jb > cur, NEG_INF, score)
    lane = _iota((1, 128), 1)
    jbf = jb.astype(F32)
    pages = pt_ref[0].astype(F32)
    bpp_shift = (PAGE_SIZE // L_SLC).bit_length() - 1

    def body(it, carry):
        sc, idx, phys = carry
        m = jnp.max(sc, axis=-1, keepdims=True)
        j = jnp.min(jnp.where(sc == m, jbf, float(nblk)), axis=-1, keepdims=True)
        logical = jnp.minimum(j.astype(jnp.int32), past // L_SLC - 1) >> bpp_shift
        page = jnp.sum(jnp.where(lane == logical, pages, 0.0), axis=-1, keepdims=True)
        return jnp.where(jbf == j, NEG_INF, sc), jnp.where(lane == it, j, idx), jnp.where(lane == it, page, phys)

    zero = jnp.zeros((8, 128), F32)
    _, idx, phys = lax.fori_loop(0, N_SEL, body, (score, zero, zero))
    idx_ref[0] = idx.astype(jnp.int32)
    phys_ref[0] = phys.astype(jnp.int32)


def _nsa_sample_select(slopes, pre, cmp_new, q8, page_table, pe_b, w1bd, w2bd, cover, past):
    bd = pre.shape[0]
    assert page_table.shape[1] == 128
    full = lambda a: pl.BlockSpec(a.shape, lambda i, sl: (0,) * a.ndim)
    picks = pl.BlockSpec((1, 8, 128), lambda i, sl: (i, 0, 0))
    grid_spec = pltpu.PrefetchScalarGridSpec(
        num_scalar_prefetch=1,
        grid=(bd,),
        in_specs=[
            pl.BlockSpec((1,) + pre.shape[1:], lambda i, sl: (i, 0, 0)),
            pl.BlockSpec((1, 1, KV_W), lambda i, sl: (i, 0, 0)),
            pl.BlockSpec((1, B_KV_HEADS, 8, KV_W), lambda i, sl: (i, 0, 0, 0)),
            pl.BlockSpec((1, 1, 128), lambda i, sl: (i, 0, 0)),
            full(pe_b), full(w1bd), full(w2bd), full(cover),
        ],
        out_specs=[picks, picks, pl.BlockSpec((1, B_KV_HEADS, 8, KV_W), lambda i, sl: (i, 0, 0, 0))],
        scratch_shapes=[pltpu.VMEM((8, KV_W), F32)],
    )
    return pl.pallas_call(
        functools.partial(_nsa_sample_select_kernel, past=past),
        grid_spec=grid_spec,
        out_shape=[jax.ShapeDtypeStruct((bd, 8, 128), jnp.int32)] * 2
        + [jax.ShapeDtypeStruct((bd, B_KV_HEADS, 8, KV_W), F32)],
        compiler_params=_cparams(("arbitrary",), 48),
        name="nsa_sample_select",
    )(slopes, pre, cmp_new, q8, page_table.reshape(bd, 1, 128), pe_b, w1bd, w2bd, cover)


def _nsa_sample_attend_kernel(pt_ref, ix_ref, sl_ref, *refs, past, nb_past):
    n_blocks = B_KV_HEADS * N_SEL
    pages = refs[:n_blocks]
    q_ref, slc_new_ref, win_new_ref, win_ref, g_ref, ocmp_ref, o_ref = refs[n_blocks:]
    b = pl.program_id(0)
    bpp = PAGE_SIZE // L_SLC
    half = KV_W // 2

    def token_rows(ref, c0):
        return jnp.concatenate([ref[0, kv * half:(kv + 1) * half, c0:c0 + 128].T for kv in range(2)], axis=1)

    row8 = _iota((8, 1), 0)
    first = _iota((L_SLC, 1), 0) == 0
    new_blk = jnp.where(first, slc_new_ref[0], 0.0).astype(BF16)
    wb = win_ref.shape[2]
    win = jnp.concatenate([token_rows(win_ref, c0) for c0 in range(0, wb, 128)], axis=0).astype(BF16)
    win_new = jnp.where(_iota((8, 1), 0) == 0, win_new_ref[0], 0.0).astype(BF16)
    for kvh in range(B_KV_HEADS):
        q = q_ref[0, kvh]
        slope = jnp.zeros((8, 1), F32)
        for g in range(B_GROUP):
            slope = jnp.where(row8 == g, sl_ref[kvh * B_GROUP + g], slope)
        m = jnp.full((8, 1), NEG_INF, F32)
        l = jnp.zeros((8, 1), F32)
        acc = jnp.zeros((8, KV_W), F32)
        for n in range(N_SEL):
            j = ix_ref[b, kvh * N_SEL + n]
            pg = token_rows(pages[kvh * N_SEL + n], 0).astype(BF16)
            blk = pg[0:L_SLC]
            for sub in range(1, bpp):
                blk = jnp.where(jnp.minimum(j, nb_past - 1) % bpp == sub, pg[sub * L_SLC:(sub + 1) * L_SLC], blk)
            blk = jnp.where(j >= nb_past, new_blk, blk)
            kpos = j * L_SLC + _iota((1, L_SLC), 1)
            s = _dot_nt(q, blk) - slope * (past - kpos).astype(F32)
            s = jnp.where(kpos <= past, s, NEG_INF)
            m_new = jnp.maximum(m, jnp.max(s, axis=-1, keepdims=True))
            m_safe = jnp.where(m_new > NEG_INF, m_new, 0.0)
            alpha = jnp.exp(m - m_safe)
            p = jnp.exp(s - m_safe)
            l = alpha * l + jnp.sum(p, axis=-1, keepdims=True)
            acc = alpha * acc + _dot(p.astype(BF16), blk)
            m = m_new
        o_slc = acc * (1.0 / jnp.where(l > 0, l, 1.0))
        wpos = past - wb + _iota((1, wb), 1)
        s1 = _dot_nt(q, win) - slope * (past - wpos).astype(F32)
        s1 = jnp.where(wpos > past - WINDOW, s1, NEG_INF)
        s2 = jnp.where(_iota((1, 8), 1) == 0, _dot_nt(q, win_new), NEG_INF)
        mw = jnp.maximum(jnp.max(s1, axis=-1, keepdims=True), jnp.max(s2, axis=-1, keepdims=True))
        e1 = jnp.exp(s1 - mw)
        e2 = jnp.exp(s2 - mw)
        dw = jnp.sum(e1, axis=-1, keepdims=True) + jnp.sum(e2, axis=-1, keepdims=True)
        o_win = (_dot(e1.astype(BF16), win) + _dot(e2.astype(BF16), win_new)) * (1.0 / dw)
        gate = jax.nn.sigmoid(g_ref[0, kvh])
        o_ref[0, kvh] = gate[:, 0:1] * ocmp_ref[0, kvh] + gate[:, 1:2] * o_slc + gate[:, 2:3] * o_win


def _nsa_sample_attend(phys, idx, slopes, cache, li, q8, slc_new, win_new, win_state, gate8, o_cmp, past):
    bd = q8.shape[0]
    bpp = PAGE_SIZE // L_SLC
    nb_past = past // L_SLC
    n_blocks = B_KV_HEADS * N_SEL
    pool = _feature_major_pages(cache)

    def blk_spec(k):
        return pl.BlockSpec((1, KV_W, PAGE_SIZE), lambda i, ph, ix, sl: (ph[i, k], li, 0))

    row = lambda: pl.BlockSpec((1, 1, KV_W), lambda i, pt, ix, sl: (i, 0, 0))
    per_head = lambda w: pl.BlockSpec((1, B_KV_HEADS, 8, w), lambda i, pt, ix, sl: (i, 0, 0, 0))
    grid_spec = pltpu.PrefetchScalarGridSpec(
        num_scalar_prefetch=3,
        grid=(bd,),
        in_specs=[blk_spec(k) for k in range(n_blocks)] + [
            per_head(KV_W), row(), row(),
            pl.BlockSpec((1,) + win_state.shape[1:], lambda i, pt, ix, sl: (i, 0, 0)),
            per_head(128), per_head(KV_W)],
        out_specs=per_head(KV_W),
    )
    return pl.pallas_call(
        functools.partial(_nsa_sample_attend_kernel, past=past, nb_past=nb_past),
        grid_spec=grid_spec,
        out_shape=jax.ShapeDtypeStruct((bd, B_KV_HEADS, 8, KV_W), F32),
        compiler_params=_cparams(("arbitrary",), 48),
        name="nsa_sample_attend",
    )(phys, idx, slopes, *([pool] * n_blocks), q8, slc_new, win_new, win_state, gate8, o_cmp)


def _outproj_kernel(x_ref, a_ref, b_ref, w_ref, y_ref):
    na = a_ref.shape[1]
    y_ref[...] = x_ref[...] + _dot(a_ref[...], w_ref[0:na]) + _dot(b_ref[...], w_ref[na:])


def _outproj(x, oa, ob, w, tm):
    m = x.shape[0]
    row = lambda n: pl.BlockSpec((tm, n), lambda i: (i, 0))
    return pl.pallas_call(
        _outproj_kernel,
        grid=(m // tm,),
        in_specs=[row(D_MODEL), row(oa.shape[1]), row(ob.shape[1]), pl.BlockSpec(w.shape, lambda i: (0, 0))],
        out_specs=row(D_MODEL),
        out_shape=jax.ShapeDtypeStruct((m, D_MODEL), F32),
        compiler_params=_cparams(("parallel",), 32),
        name="outproj",
    )(x, oa, ob, w)


def _mixer_prompt_kernel(*refs, groups, convs, gated_ffn, final, tm, tf):
    x_ref, nw_ref = refs[0:2]
    up = refs[2:2 + groups]
    cw = refs[2 + groups:2 + groups + convs]
    down_ref = refs[2 + groups + convs]
    pos = 3 + groups + convs
    fw_ref = refs[pos] if final else None
    pos += int(final)
    y_ref = refs[pos]
    tails = refs[pos + 1:pos + 1 + convs]
    xn_ref, acc_ref, ubuf_ref, carry_ref = refs[pos + 1 + convs:]
    t = pl.program_id(1)
    j = pl.program_id(2)

    @pl.when(j == 0)
    def _():
        xn_ref[...] = _rms(x_ref[0], nw_ref[...]).astype(BF16)
        acc_ref[...] = jnp.zeros_like(acc_ref)

    xn = xn_ref[...]
    u = [_dot(xn, w[...]) for w in up]
    conv_in = [u[0], u[1]] if gated_ffn else [u[1] * u[0]]
    conv_out = []
    for c, z in enumerate(conv_in):
        @pl.when(t == 0)
        def _():
            carry_ref[j, c] = jnp.zeros((8, tf), F32)

        ubuf_ref[c, 0:8] = carry_ref[j, c]
        ubuf_ref[c, 8:tm + 8] = z
        w = cw[c][...]
        conv_out.append(w[0:1] * ubuf_ref[c, pl.ds(6, tm), :] + w[1:2] * ubuf_ref[c, pl.ds(7, tm), :]
                        + w[2:3] * z)
        last = z[tm - 8:tm]
        carry_ref[j, c] = last
        tails[c][0, 0] = last
    if gated_ffn:
        act = conv_out[0] * jax.nn.sigmoid(conv_out[0]) * conv_out[1]
    else:
        act = u[2] * conv_out[0]
    acc_ref[...] += _dot(act.astype(BF16), down_ref[...])

    @pl.when(j == pl.num_programs(2) - 1)
    def _():
        y = x_ref[0] + acc_ref[...]
        y_ref[0] = _rms(y, fw_ref[...]) if final else y


def _mixer_prompt(x, nw, w_up, conv_w, w_down, gated_ffn, final_w=None, tm=1024, tf=256):
    b, s, d = x.shape
    f = w_down.shape[0]
    groups = w_up.shape[1] // f
    convs = conv_w.shape[1] // f
    nf = f // tf
    final = final_w is not None
    in_specs = [pl.BlockSpec((1, tm, d), lambda i, t, j: (i, t, 0)),
                pl.BlockSpec(nw.shape, lambda i, t, j: (0, 0))]
    in_specs += [pl.BlockSpec((d, tf), lambda i, t, j, g=g: (0, g * nf + j)) for g in range(groups)]
    in_specs += [pl.BlockSpec((CONV_W, tf), lambda i, t, j, c=c: (0, c * nf + j)) for c in range(convs)]
    in_specs += [pl.BlockSpec((tf, d), lambda i, t, j: (j, 0))]
    args = [x, nw] + [w_up] * groups + [conv_w] * convs + [w_down]
    if final:
        in_specs.append(pl.BlockSpec(final_w.shape, lambda i, t, j: (0, 0)))
        args.append(final_w)
    out_specs = [pl.BlockSpec((1, tm, d), lambda i, t, j: (i, t, 0))]
    out_specs += [pl.BlockSpec((1, 1, 8, tf), lambda i, t, j: (i, t, 0, j))] * convs
    out_shape = [jax.ShapeDtypeStruct((b, s, d), F32)]
    out_shape += [jax.ShapeDtypeStruct((b, s // tm, 8, f), F32)] * convs
    res = pl.pallas_call(
        functools.partial(_mixer_prompt_kernel, groups=groups, convs=convs, gated_ffn=gated_ffn,
                          final=final, tm=tm, tf=tf),
        grid=(b, s // tm, nf),
        in_specs=in_specs,
        out_specs=out_specs,
        out_shape=out_shape,
        scratch_shapes=[pltpu.VMEM((tm, d), BF16), pltpu.VMEM((tm, d), F32),
                        pltpu.VMEM((convs, tm + 8, tf), F32), pltpu.VMEM((nf, convs, 8, tf), F32)],
        compiler_params=_cparams(("parallel", "arbitrary", "arbitrary"), 56),
        name="ffn_prompt" if gated_ffn else "sconv_prompt",
    )(*args)
    tail = jnp.concatenate([r[:, -1] for r in res[1:]], axis=-1)
    return res[0], tail[:, 8 - (CONV_W - 1):, :]


def _mixer_sample_kernel(*refs, groups, convs, gated_ffn, final):
    x_ref, nw_ref = refs[0:2]
    up = refs[2:2 + groups]
    cw = refs[2 + groups:2 + groups + convs]
    st = refs[2 + groups + convs:2 + groups + 2 * convs]
    down_ref = refs[2 + groups + 2 * convs]
    pos = 3 + groups + 2 * convs
    fw_ref = refs[pos] if final else None
    pos += int(final)
    y_ref = refs[pos]
    news = refs[pos + 1:pos + 1 + convs]
    xn_ref, acc_ref = refs[pos + 1 + convs:]
    j = pl.program_id(0)

    @pl.when(j == 0)
    def _():
        xn_ref[...] = _rms(x_ref[...], nw_ref[...]).astype(BF16)
        acc_ref[...] = jnp.zeros_like(acc_ref)

    xn = xn_ref[...]
    u = [_dot(xn, w[...]) for w in up]
    conv_in = [u[0], u[1]] if gated_ffn else [u[1] * u[0]]
    conv_out = []
    for c, z in enumerate(conv_in):
        w = cw[c][...]
        conv_out.append(w[0:1] * st[c][:, 0, :] + w[1:2] * st[c][:, 1, :] + w[2:3] * z)
        news[c][...] = z
    if gated_ffn:
        act = conv_out[0] * jax.nn.sigmoid(conv_out[0]) * conv_out[1]
    else:
        act = u[2] * conv_out[0]
    acc_ref[...] += _dot(act.astype(BF16), down_ref[...])

    @pl.when(j == pl.num_programs(0) - 1)
    def _():
        y = x_ref[...] + acc_ref[...]
        y_ref[...] = _rms(y, fw_ref[...]) if final else y


def _mixer_sample(x, nw, w_up, conv_w, w_down, state, gated_ffn, final_w=None, tf=256):
    bd, d = x.shape
    f = w_down.shape[0]
    groups = w_up.shape[1] // f
    convs = conv_w.shape[1] // f
    nf = f // tf
    final = final_w is not None
    in_specs = [pl.BlockSpec((bd, d), lambda j: (0, 0)), pl.BlockSpec(nw.shape, lambda j: (0, 0))]
    in_specs += [pl.BlockSpec((d, tf), lambda j, g=g: (0, g * nf + j)) for g in range(groups)]
    in_specs += [pl.BlockSpec((CONV_W, tf), lambda j, c=c: (0, c * nf + j)) for c in range(convs)]
    in_specs += [pl.BlockSpec((bd, CONV_W - 1, tf), lambda j, c=c: (0, 0, c * nf + j)) for c in range(convs)]
    in_specs += [pl.BlockSpec((tf, d), lambda j: (j, 0))]
    args = [x, nw] + [w_up] * groups + [conv_w] * convs + [state] * convs + [w_down]
    if final:
        in_specs.append(pl.BlockSpec(final_w.shape, lambda j: (0, 0)))
        args.append(final_w)
    out_specs = [pl.BlockSpec((bd, d), lambda j: (0, 0))]
    out_specs += [pl.BlockSpec((bd, tf), lambda j: (0, j))] * convs
    out_shape = [jax.ShapeDtypeStruct((bd, d), F32)] + [jax.ShapeDtypeStruct((bd, f), F32)] * convs
    res = pl.pallas_call(
        functools.partial(_mixer_sample_kernel, groups=groups, convs=convs, gated_ffn=gated_ffn, final=final),
        grid=(nf,),
        in_specs=in_specs,
        out_specs=out_specs,
        out_shape=out_shape,
        scratch_shapes=[pltpu.VMEM((bd, d), BF16), pltpu.VMEM((bd, d), F32)],
        compiler_params=_cparams(("arbitrary",), 32),
        name="ffn_sample" if gated_ffn else "sconv_sample",
    )(*args)
    new = jnp.concatenate(res[1:], axis=-1)
    return res[0], jnp.concatenate([state[:, 1:], new[:, None, :]], axis=1)


def _pad_in_weight(w):
    c = HG_W + QB_W + 3 * KV_W
    per = 3 * B_GROUP
    gates = [jnp.pad(w[:, c + h * per:c + (h + 1) * per], ((0, 0), (0, GATE_PAD - per)))
             for h in range(B_KV_HEADS)]
    return jnp.concatenate([w[:, :c]] + gates, axis=1).astype(BF16)


def _per_head_rows(z, width):
    bd = z.shape[0]
    z = z.reshape(bd, B_KV_HEADS, B_GROUP, -1)
    return jnp.pad(z, ((0, 0), (0, 0), (0, 8 - B_GROUP), (0, width - z.shape[-1])))


def kernel(x_prompt, x_sample, cache_cmp_kv, cache_slc_kv, state_win_kv, state_hgrn, state_sconv, state_ffn,
           page_table, norm_mix, norm_ffn, norm_final, ab_w_in, ab_w_out, hgrn_lb_logits, hgrn_norm,
           cmp_pe, cmp_w1, cmp_w2, c_w_in, c_conv, c_w_out, ffn_up, ffn_conv, ffn_down):
    b, s, d = x_prompt.shape
    bd, t, _ = x_sample.shape
    assert t == 1 and d == D_MODEL
    depth = norm_mix.shape[0]
    n_a = ab_w_in.shape[0]
    wb = state_win_kv.shape[2]
    n_pages = page_table.shape[1]
    past = n_pages * PAGE_SIZE
    assert wb == WINDOW and past % (PAGES_PER_STEP * PAGE_SIZE) == 0 and s % 1024 == 0
    hh = jnp.arange(1, B_HEADS + 1, dtype=F32)
    slopes = 2.0 ** (-8.0 * hh / B_HEADS)
    lb_logits = hgrn_lb_logits.astype(F32)
    row2 = lambda v: v.reshape(1, -1).astype(F32)

    xp = x_prompt.reshape(b * s, d)
    xs = x_sample.reshape(bd, d)
    outs_p = {k: [] for k in ("cmp", "slc", "win", "hg", "sc", "ff")}
    outs_s = {k: [] for k in ("cmp", "slc", "win", "hg", "sc", "ff")}
    y_prompt = y_sample = None
    for l in range(depth):
        last = l == depth - 1
        if l % 2 == 0:
            la = l // 2
            w_in = _pad_in_weight(ab_w_in[la])
            w_out = ab_w_out[la].astype(BF16)
            nw = row2(norm_mix[l])
            hnw = row2(hgrn_norm[la])
            pe_b, w1bd, w2_packed = _cmp_weights(cmp_pe[la], cmp_w1[la], cmp_w2[la], packed=True)
            _, _, w2_plain = _cmp_weights(cmp_pe[la], cmp_w1[la], cmp_w2[la], packed=False)
            hg, qb, kc, ksl, kw, gb = _inproj(xp, nw, w_in, tm=512)
            oa, hst = _hgrn_prompt(hg, lb_logits, hnw, la, b, s)
            ck, cv = _compress_prompt(kc.reshape(b, 2 * s, KV_W // 2), pe_b, w1bd, w2_packed)
            n_cmp = s // D_CMP
            nblk = s // L_SLC
            cover = _cover_matrix(n_cmp, nblk, nblk)
            sk, sv = _key_value_rows(ksl, b, s, with_blocks=True)
            wk, wv = _key_value_rows(kw, b, s, with_blocks=False)
            ob = _nsa_prompt(slopes, qb.reshape(b, s, QB_W), gb.reshape(b, s, -1), ck, cv, sk, sv, wk, wv, cover)
            xp = _outproj(xp, oa, ob.reshape(b * s, QB_W), w_out, tm=512)
            kv6 = lambda z, n, tt: z.reshape(n, tt, 2, B_KV_HEADS, B_HD)
            outs_p["cmp"].append(kv6(kc, b, s))
            outs_p["slc"].append(kv6(ksl, b, s))
            kw6 = kv6(kw, b, s)
            outs_p["win"].append(jnp.pad(kw6, ((0, 0), (max(wb - s, 0), 0), (0, 0), (0, 0), (0, 0)))[:, -wb:])
            outs_p["hg"].append(hst)
            hg, qb, kc, ksl, kw, gb = _inproj(xs, nw, w_in, tm=bd)
            oa, hst = _hgrn_sample(hg, state_hgrn[la], lb_logits, hnw, la)
            pre = _cmp_pages(page_table, cache_cmp_kv, la, pe_b, w1bd)
            n_cmp = pre.shape[1]
            nblk = past // L_SLC + 1
            cover = _cover_matrix(n_cmp, nblk, -(-nblk // 128) * 128)
            q8 = jnp.stack([jnp.pad(_per_head_rows(qb, B_HD)[:, h], ((0, 0), (0, 0), (h * B_HD, KV_W - (h + 1) * B_HD)))
                            for h in range(B_KV_HEADS)], axis=1)
            idx, phys, o_cmp = _nsa_sample_select(slopes, pre, kc.reshape(bd, 1, KV_W), q8, page_table, pe_b, w1bd,
                                                  w2_plain, cover, past)
            picks = lambda z: z[:, :B_KV_HEADS, :N_SEL].reshape(bd, B_KV_HEADS * N_SEL)
            idx, phys = picks(idx), picks(phys)
            gate8 = _per_head_rows(jnp.concatenate([gb[:, h * GATE_PAD:h * GATE_PAD + 3 * B_GROUP]
                                                    for h in range(B_KV_HEADS)], axis=1), 128)
            win_state = state_win_kv[la].transpose(0, 2, 3, 4, 1).reshape(bd, KV_W, wb)
            o8 = _nsa_sample_attend(phys, idx, slopes, cache_slc_kv, la, q8, ksl.reshape(bd, 1, KV_W),
                                    kw.reshape(bd, 1, KV_W), win_state, gate8, o_cmp, past)
            ob = jnp.concatenate([o8[:, h, :B_GROUP, 2 * B_HD + h * B_HD:2 * B_HD + (h + 1) * B_HD]
                                  .reshape(bd, B_GROUP * B_HD) for h in range(B_KV_HEADS)], axis=1)
            xs = _outproj(xs, oa, ob.astype(BF16), w_out, tm=bd)
            outs_s["cmp"].append(kv6(kc, bd, t))
            outs_s["slc"].append(kv6(ksl, bd, t))
            outs_s["win"].append(jnp.concatenate([state_win_kv[la], kv6(kw, bd, t)], axis=1)[:, t:])
            outs_s["hg"].append(hst)
        else:
            lc = l // 2
            xp3, tail = _mixer_prompt(xp.reshape(b, s, d), row2(norm_mix[l]), c_w_in[lc].astype(BF16),
                                      c_conv[lc], c_w_out[lc].astype(BF16), gated_ffn=False)
            xp = xp3.reshape(b * s, d)
            outs_p["sc"].append(tail)
            xs, new = _mixer_sample(xs, row2(norm_mix[l]), c_w_in[lc].astype(BF16), c_conv[lc],
                                    c_w_out[lc].astype(BF16), state_sconv[lc], gated_ffn=False)
            outs_s["sc"].append(new)
        fw = row2(norm_final) if last else None
        xp3, tail = _mixer_prompt(xp.reshape(b, s, d), row2(norm_ffn[l]), ffn_up[l].astype(BF16), ffn_conv[l],
                                  ffn_down[l].astype(BF16), gated_ffn=True, final_w=fw)
        xp = xp3.reshape(b * s, d)
        outs_p["ff"].append(tail)
        xs, new = _mixer_sample(xs, row2(norm_ffn[l]), ffn_up[l].astype(BF16), ffn_conv[l],
                                ffn_down[l].astype(BF16), state_ffn[l], gated_ffn=True, final_w=fw)
        outs_s["ff"].append(new)
    y_prompt = xp.reshape(b, s, d)
    y_sample = xs.reshape(bd, t, d)
    return (y_prompt, y_sample,
            jnp.stack(outs_p["cmp"], axis=2), jnp.stack(outs_p["slc"], axis=2), jnp.stack(outs_p["win"], axis=0),
            jnp.stack(outs_p["hg"], axis=0), jnp.stack(outs_p["sc"], axis=0), jnp.stack(outs_p["ff"], axis=0),
            jnp.stack(outs_s["cmp"], axis=2), jnp.stack(outs_s["slc"], axis=2), jnp.stack(outs_s["win"], axis=0),
            jnp.stack(outs_s["hg"], axis=0), jnp.stack(outs_s["sc"], axis=0), jnp.stack(outs_s["ff"], axis=0))
```

```python
import functools

import jax
import jax.numpy as jnp
from jax import lax
from jax.experimental import pallas as pl
from jax.experimental.pallas import tpu as pltpu

F32 = jnp.float32
BF16 = jnp.bfloat16
NEG_INF = float("-inf")

D_MODEL = 1024
PAGE_SIZE = 128
A_HEADS = 4
A_DK = 128
A_DV = 128
B_HEADS = 8
B_KV_HEADS = 2
B_GROUP = B_HEADS // B_KV_HEADS
B_HD = 64
L_CMP = 32
D_CMP = 16
CMP_HID = 64
L_SLC = 64
SLC_SHIFT = 6
N_SEL = 16
WINDOW = 512
SEL_FORCE = 1e9
CONV_W = 3
EPS = 1e-6
KV_W = 2 * B_KV_HEADS * B_HD
HG_W = 4 * A_HEADS * A_DK
QB_W = B_HEADS * B_HD
GATE_PAD = 128
IN_PAD = HG_W + QB_W + 3 * KV_W + B_KV_HEADS * GATE_PAD

V7X_VMEM_BYTES = 64 * 2**20


def _cparams(sem, vmem_mb):
    assert vmem_mb * 2**20 < V7X_VMEM_BYTES
    return pltpu.CompilerParams(dimension_semantics=sem, vmem_limit_bytes=vmem_mb * 2**20)


def _dot(a, b):
    return jnp.dot(a, b, preferred_element_type=F32)


def _dot_nt(a, b):
    return lax.dot_general(a, b, (((1,), (1,)), ((), ())), preferred_element_type=F32)


def _dot_tn(a, b):
    return lax.dot_general(a, b, (((0,), (0,)), ((), ())), preferred_element_type=F32)


def _split3(x):
    hi = x.astype(BF16)
    r1 = x - hi.astype(F32)
    mid = r1.astype(BF16)
    lo = (r1 - mid.astype(F32)).astype(BF16)
    return hi, mid, lo


def _dot3_lhs(x, w):
    hi, mid, lo = _split3(x)
    return _dot(hi, w) + _dot(mid, w) + _dot(lo, w)


def _dot3_rhs(w, x):
    hi, mid, lo = _split3(x)
    return _dot(w, hi) + _dot(w, mid) + _dot(w, lo)


def _rms(x, w):
    return x * lax.rsqrt(jnp.mean(x * x, axis=-1, keepdims=True) + EPS) * w


def _masked_softmax(s, mask):
    s = jnp.where(mask, s, NEG_INF)
    m = jnp.max(s, axis=-1, keepdims=True)
    m = jnp.where(m > NEG_INF, m, 0.0)
    e = jnp.exp(s - m)
    d = jnp.sum(e, axis=-1, keepdims=True)
    return e * (1.0 / jnp.where(d > 0, d, 1.0))


def _iota(shape, dim):
    return lax.broadcasted_iota(jnp.int32, shape, dim)


def _top_blocks(score, jb, n_sel):
    big = float(score.shape[-1])
    jbf = jb.astype(F32)

    def body(_, carry):
        sc, sel = carry
        m = jnp.max(sc, axis=-1, keepdims=True)
        idx = jnp.min(jnp.where(sc == m, jbf, big), axis=-1, keepdims=True)
        pick = jbf == idx
        return jnp.where(pick, NEG_INF, sc), jnp.where(pick, 1.0, sel)

    _, sel = lax.fori_loop(0, n_sel, body, (score, jnp.zeros_like(score)))
    return sel


def _inproj_kernel(x_ref, nw_ref, w_ref, hg_ref, qb_ref, kc_ref, ksl_ref, kw_ref, gb_ref):
    xn = _rms(x_ref[...], nw_ref[...]).astype(BF16)

    def proj(c0, n):
        return _dot(xn, w_ref[:, c0:c0 + n])

    hg_ref[...] = proj(0, HG_W)
    qb_ref[...] = (proj(HG_W, QB_W) * (B_HD ** -0.5)).astype(BF16)
    c = HG_W + QB_W
    kc_ref[...] = proj(c, KV_W)
    ksl_ref[...] = proj(c + KV_W, KV_W)
    kw_ref[...] = proj(c + 2 * KV_W, KV_W)
    gb_ref[...] = proj(c + 3 * KV_W, B_KV_HEADS * GATE_PAD)


def _inproj(x, nw, w_pad, tm):
    m = x.shape[0]
    row = lambda n: pl.BlockSpec((tm, n), lambda i: (i, 0))
    full = lambda a: pl.BlockSpec(a.shape, lambda i: (0,) * a.ndim)
    widths = (HG_W, QB_W, KV_W, KV_W, KV_W, B_KV_HEADS * GATE_PAD)
    dtypes = (F32, BF16, F32, F32, F32, F32)
    return pl.pallas_call(
        _inproj_kernel,
        grid=(m // tm,),
        in_specs=[row(D_MODEL), full(nw), full(w_pad)],
        out_specs=[row(n) for n in widths],
        out_shape=[jax.ShapeDtypeStruct((m, n), d) for n, d in zip(widths, dtypes)],
        compiler_params=_cparams(("parallel",), 48),
        name="inproj",
    )(x, nw, w_pad)


def _lower_bound(logit_ref, la):
    lg = logit_ref[...]
    e = jnp.exp(lg - jnp.max(lg, axis=0, keepdims=True))
    return jnp.sum(e[:la + 1], axis=0, keepdims=True) / jnp.sum(e, axis=0, keepdims=True)


def _hgrn_post(o, nw, ga):
    o = o * lax.rsqrt(jnp.mean(o * o, axis=-1, keepdims=True) + EPS)
    return o * nw * jax.nn.sigmoid(ga)


def _hgrn_prompt_kernel(hg_ref, lg_ref, nw_ref, oa_ref, st_ref, s_ref, *, la, chunk, sub):
    ci = pl.program_id(1)

    @pl.when(ci == 0)
    def _():
        s_ref[...] = jnp.zeros_like(s_ref)

    lb = _lower_bound(lg_ref, la)
    tri = (_iota((chunk, chunk), 0) >= _iota((chunk, chunk), 1)).astype(BF16)
    trow = _iota((chunk, 1), 0)
    srow = _iota((sub, 1), 0)
    lane = _iota((sub, chunk), 1)
    for h in range(A_HEADS):
        hs = slice(h * A_DK, (h + 1) * A_DK)
        q = hg_ref[:, h * A_DK:(h + 1) * A_DK]
        fl = hg_ref[:, 512 + h * A_DK:512 + (h + 1) * A_DK]
        v = hg_ref[:, 1024 + h * A_DV:1024 + (h + 1) * A_DV]
        ga = hg_ref[:, 1536 + h * A_DV:1536 + (h + 1) * A_DV]
        lbh = lb[:, hs]
        f = lbh + (1.0 - lbh) * jax.nn.sigmoid(fl)
        a = _dot3_rhs(tri, jnp.log(f))
        k = 1.0 - f
        st = s_ref[h]
        vb = v.astype(BF16)
        o = _dot_nt((q * jnp.exp(a)).astype(BF16), st.astype(BF16))
        rows = []
        for i in range(chunk // sub):
            r0 = i * sub
            qi, ai = q[r0:r0 + sub], a[r0:r0 + sub]
            att = jnp.zeros((sub, chunk), F32)
            for s in range(sub):
                dec = jnp.exp(jnp.where(srow >= s, ai - a[r0 + s:r0 + s + 1], NEG_INF))
                col = jnp.sum(qi * dec * k[r0 + s:r0 + s + 1], axis=-1, keepdims=True)
                att = jnp.where(lane == r0 + s, col, att)
            if i > 0:
                ref_a = a[r0 - 1:r0]
                qt = qi * jnp.exp(ai - ref_a)
                kt = k * jnp.exp(jnp.where(trow < r0, ref_a - a, NEG_INF))
                att = att + _dot_nt(qt.astype(BF16), kt.astype(BF16))
            rows.append(att)
        att = jnp.concatenate(rows, axis=0)
        o = o + _dot(att.astype(BF16), vb)
        a_end = a[chunk - 1:chunk]
        khat = k * jnp.exp(a_end - a)
        s_ref[h] = st * jnp.exp(a_end) + _dot_tn(vb, khat.astype(BF16))
        oa_ref[:, hs] = _hgrn_post(o, nw_ref[:, hs], ga).astype(BF16)

    @pl.when(ci == pl.num_programs(1) - 1)
    def _():
        for h in range(A_HEADS):
            st_ref[0, h] = s_ref[h].T


def _hgrn_prompt(hg, lb_logits, nw, la, b, s, chunk=64, sub=16):
    n = s // chunk
    full = lambda a: pl.BlockSpec(a.shape, lambda i, j: (0,) * a.ndim)
    return pl.pallas_call(
        functools.partial(_hgrn_prompt_kernel, la=la, chunk=chunk, sub=sub),
        grid=(b, n),
        in_specs=[pl.BlockSpec((chunk, HG_W), lambda i, j: (i * n + j, 0)), full(lb_logits), full(nw)],
        out_specs=[pl.BlockSpec((chunk, A_HEADS * A_DV), lambda i, j: (i * n + j, 0)),
                   pl.BlockSpec((1, A_HEADS, A_DK, A_DV), lambda i, j: (i, 0, 0, 0))],
        out_shape=[jax.ShapeDtypeStruct((b * s, A_HEADS * A_DV), BF16),
                   jax.ShapeDtypeStruct((b, A_HEADS, A_DK, A_DV), F32)],
        scratch_shapes=[pltpu.VMEM((A_HEADS, A_DV, A_DK), F32)],
        compiler_params=_cparams(("parallel", "arbitrary"), 32),
        name="hgrn_prompt",
    )(hg, lb_logits, nw)


def _hgrn_sample_kernel(hg_ref, s0_ref, lg_ref, nw_ref, oa_ref, st_ref, *, la):
    lb = _lower_bound(lg_ref, la)
    eye = (_iota((A_DK, A_DK), 0) == _iota((A_DK, A_DK), 1)).astype(F32)

    def col(r):
        return jnp.sum(eye * r, axis=1, keepdims=True)

    for h in range(A_HEADS):
        hs = slice(h * A_DK, (h + 1) * A_DK)
        q = hg_ref[0, :, h * A_DK:(h + 1) * A_DK]
        fl = hg_ref[0, :, 512 + h * A_DK:512 + (h + 1) * A_DK]
        v = hg_ref[0, :, 1024 + h * A_DV:1024 + (h + 1) * A_DV]
        ga = hg_ref[0, :, 1536 + h * A_DV:1536 + (h + 1) * A_DV]
        lbh = lb[:, hs]
        f = lbh + (1.0 - lbh) * jax.nn.sigmoid(fl)
        sn = col(f) * s0_ref[0, h] + col(1.0 - f) * v
        st_ref[0, h] = sn
        o = jnp.sum(col(q) * sn, axis=0, keepdims=True)
        oa_ref[0, :, hs] = _hgrn_post(o, nw_ref[:, hs], ga).astype(BF16)


def _hgrn_sample(hg, s0, lb_logits, nw, la):
    bd = hg.shape[0]
    full = lambda a: pl.BlockSpec(a.shape, lambda i: (0,) * a.ndim)
    st_spec = pl.BlockSpec((1, A_HEADS, A_DK, A_DV), lambda i: (i, 0, 0, 0))
    oa, st = pl.pallas_call(
        functools.partial(_hgrn_sample_kernel, la=la),
        grid=(bd,),
        in_specs=[pl.BlockSpec((1, 1, HG_W), lambda i: (i, 0, 0)), st_spec, full(lb_logits), full(nw)],
        out_specs=[pl.BlockSpec((1, 1, A_HEADS * A_DV), lambda i: (i, 0, 0)), st_spec],
        out_shape=[jax.ShapeDtypeStruct((bd, 1, A_HEADS * A_DV), BF16),
                   jax.ShapeDtypeStruct(s0.shape, F32)],
        compiler_params=_cparams(("parallel",), 16),
        name="hgrn_sample",
    )(hg.reshape(bd, 1, HG_W), s0, lb_logits, nw)
    return oa.reshape(bd, A_HEADS * A_DV), st


def _cmp_weights(pe, w1, w2, packed):
    eye = jnp.eye(B_KV_HEADS, dtype=F32)
    w1r = w1.reshape(2, L_CMP, B_HD, CMP_HID)
    w1bd = jnp.einsum("kidm,hH->ikhdHm", w1r, eye).reshape(L_CMP, 2, KV_W // 2, KV_W // 2)
    pe_b = jnp.broadcast_to(pe.transpose(1, 0, 2)[:, :, None, :], (L_CMP, 2, B_KV_HEADS, B_HD))
    if not packed:
        w2bd = jnp.einsum("kmd,kK,hH->khmKHd", w2, eye, eye).reshape(KV_W, KV_W)
        return pe_b.reshape(L_CMP, KV_W), w1bd.astype(BF16), w2bd.astype(BF16)
    half = KV_W // 2
    zero = jnp.zeros((half, KV_W), F32)
    k_blk = jnp.concatenate([w2[0], jnp.zeros_like(w2[0])], axis=1)
    v_blk = jnp.concatenate([w2[1], w2[1]], axis=1)
    w2k = jnp.concatenate([jnp.einsum("ml,hH->hmHl", k_blk, eye).reshape(half, KV_W), zero], axis=0)
    w2v = jnp.concatenate([zero, jnp.einsum("ml,hH->hmHl", v_blk, eye).reshape(half, KV_W)], axis=0)
    return pe_b.reshape(L_CMP, KV_W), w1bd.astype(BF16), (w2k.astype(BF16), w2v.astype(BF16))


def _chunk_preacts(load, n, pe_ref, w1_ref):
    half = KV_W // 2
    p0 = [jnp.zeros((n, half), F32) for _ in range(2)]
    p1 = [jnp.zeros((n, half), F32) for _ in range(2)]
    for i in range(D_CMP):
        for kv in range(2):
            x = load(i, kv)
            ls = slice(kv * half, (kv + 1) * half)
            p0[kv] = p0[kv] + _dot((x + pe_ref[i:i + 1, ls]).astype(BF16), w1_ref[i, kv])
            p1[kv] = p1[kv] + _dot((x + pe_ref[D_CMP + i:D_CMP + i + 1, ls]).astype(BF16), w1_ref[D_CMP + i, kv])
    return jnp.concatenate(p0, axis=1), jnp.concatenate(p1, axis=1)


def _with_position(k, pos, lane):
    col = lane & 127
    return jnp.where(col == B_HD, (pos >> 7).astype(F32), jnp.where(col == B_HD + 1, (pos & 127).astype(F32), k))


def _compress_kernel(kc_ref, pe_ref, w1_ref, w2k_ref, w2v_ref, ck_ref, cv_ref):
    n = ck_ref.shape[1]
    p0, p1 = _chunk_preacts(lambda i, kv: kc_ref[0, pl.ds(2 * i + kv, n, stride=2 * D_CMP), :], n, pe_ref, w1_ref)
    pre = p0 + pltpu.roll(p1, n - 1, axis=0)
    hid = (pre * jax.nn.sigmoid(pre)).astype(BF16)
    cpos = _iota((n, 1), 0) * D_CMP + (L_CMP - 1)
    ck_ref[0] = _with_position(_dot(hid, w2k_ref[...]), cpos, _iota((n, KV_W), 1)).astype(BF16)
    cv_ref[0] = _dot(hid, w2v_ref[...]).astype(BF16)


def _compress_prompt(kc, pe_b, w1bd, w2kv):
    b, s2, _ = kc.shape
    n = s2 // (2 * D_CMP)
    full = lambda a: pl.BlockSpec(a.shape, lambda i: (0,) * a.ndim)
    out = pl.BlockSpec((1, n, KV_W), lambda i: (i, 0, 0))
    return pl.pallas_call(
        _compress_kernel,
        grid=(b,),
        in_specs=[pl.BlockSpec((1, s2, KV_W // 2), lambda i: (i, 0, 0)), full(pe_b), full(w1bd),
                  full(w2kv[0]), full(w2kv[1])],
        out_specs=[out, out],
        out_shape=[jax.ShapeDtypeStruct((b, n, KV_W), BF16)] * 2,
        compiler_params=_cparams(("parallel",), 48),
        name="compress_prompt",
    )(kc, pe_b, w1bd, *w2kv)


MASK_BIAS = 2.0 ** 100

def _top_blocks_t(score_t, n_sel):
    nblk = score_t.shape[0]
    blk = _iota(score_t.shape, 0).astype(F32)

    def body(_, carry):
        sc, sel = carry
        m = jnp.max(sc, axis=0, keepdims=True)
        idx = jnp.min(jnp.where(sc == m, blk, float(nblk)), axis=0, keepdims=True)
        pick = blk == idx
        return jnp.where(pick, NEG_INF, sc), jnp.where(pick, 1.0, sel)

    _, sel = lax.fori_loop(0, n_sel, body, (score_t, jnp.zeros_like(score_t)), unroll=True)
    return sel


def _nsa_prompt_kernel(sl_ref, q_ref, g_ref, ck_ref, cv_ref, sk_ref, sv_ref, wk_ref, wv_ref, cover_ref, o_ref,
                       m_ref, l_ref, acc_ref, *, tq, tk):
    kvh = pl.program_id(1)
    q0 = pl.program_id(2) * tq
    rows = B_GROUP * tq
    lane = _iota((tq, 128), 1)
    lo = lane < B_HD
    qf = q_ref[0].astype(F32)
    left = []
    for g in range(B_GROUP):
        pair = qf[:, 128 * (g // 2):128 * (g // 2) + 128]
        if g % 2:
            pair = pltpu.roll(pair, B_HD, axis=1)
        slope = sl_ref[kvh * B_GROUP + g]
        pos_cols = jnp.where(lane == B_HD, slope * 128.0, jnp.where(lane == B_HD + 1, slope, 0.0))
        left.append(jnp.where(lo, pair, pos_cols).astype(BF16))
    ql = jnp.concatenate(left, axis=0)
    qpos = q0 + (_iota((rows, 1), 0) & (tq - 1))

    nc = ck_ref.shape[1]
    cpos = _iota((1, nc), 1) * D_CMP + (L_CMP - 1)
    span = WINDOW + tq
    start = pl.multiple_of(jnp.maximum(q0 - WINDOW, 0), 128)
    wpos = start + _iota((1, span), 1)
    s_cmp = _dot_nt(ql, ck_ref[0])
    s_win = _dot_nt(ql, wk_ref[0, 0, pl.ds(start, span), :])
    p = _masked_softmax(s_cmp, cpos <= qpos)
    pw = _masked_softmax(s_win, (wpos <= qpos) & (wpos > qpos - WINDOW))
    o_cmp = _dot(p.astype(BF16), cv_ref[0])
    o_win = _dot(pw.astype(BF16), wv_ref[0, 0, pl.ds(start, span), :])
    imp = p[0:tq]
    for g in range(1, B_GROUP):
        imp = imp + p[g * tq:(g + 1) * tq]

    nblk = cover_ref.shape[1]
    jb = _iota((1, nblk), 1)
    cur = (q0 + _iota((tq, 1), 0)) >> SLC_SHIFT
    score = _dot3_lhs(imp, cover_ref[...])
    score = jnp.where((jb == 0) | (jb == cur) | (jb == cur - 1), SEL_FORCE, score)
    score = jnp.where(jb > cur, NEG_INF, score)
    sel = _top_blocks_t(score.T, min(N_SEL, nblk)).T

    bias = ((sel - 1.0) * MASK_BIAS).astype(BF16)
    qs = jnp.concatenate([ql, jnp.concatenate([bias] * B_GROUP, axis=0)], axis=1)
    m_ref[...] = jnp.full_like(m_ref, NEG_INF)
    l_ref[...] = jnp.zeros_like(l_ref)
    acc_ref[...] = jnp.zeros_like(acc_ref)

    def scores(kt):
        return _dot_nt(qs, sk_ref[0, 0, pl.ds(pl.multiple_of(kt * tk, tk), tk), :])

    def update(s, kt, causal):
        k0 = pl.multiple_of(kt * tk, tk)
        if causal:
            s = jnp.where(k0 + _iota((1, tk), 1) <= qpos, s, NEG_INF)
        m_old = m_ref[...]
        m_new = jnp.maximum(m_old, jnp.max(s, axis=-1, keepdims=True))
        alpha = jnp.exp(m_old - m_new)
        e = jnp.exp(s - m_new)
        l_ref[...] = alpha * l_ref[...] + jnp.sum(e, axis=-1, keepdims=True)
        acc_ref[...] = alpha * acc_ref[...] + _dot(e.astype(BF16), sv_ref[0, 0, pl.ds(k0, tk), :])
        m_ref[...] = m_new

    n_full = q0 // tk

    def full_tile(kt, s):
        s_next = scores(kt + 1)
        update(s, kt, False)
        return s_next

    update(lax.fori_loop(0, n_full, full_tile, scores(0)), n_full, True)
    o_slc = acc_ref[...] * (1.0 / l_ref[...])

    gate = jax.nn.sigmoid(g_ref[0])
    outs = []
    for g in range(B_GROUP):
        rs = slice(g * tq, (g + 1) * tq)
        outs.append(gate[:, 3 * g:3 * g + 1] * o_cmp[rs] + gate[:, 3 * g + 1:3 * g + 2] * o_slc[rs]
                    + gate[:, 3 * g + 2:3 * g + 3] * o_win[rs])
    for j in range(B_GROUP // 2):
        o_ref[0, :, 128 * j:128 * j + 128] = jnp.where(lo, outs[2 * j], outs[2 * j + 1]).astype(BF16)


def _nsa_prompt(slopes, qb, gb, ck, cv, sk, sv, wk, wv, cover, tq=128, tk=512):
    b, s, _ = qb.shape
    half = QB_W // B_KV_HEADS
    rows = B_GROUP * tq
    assert cover.shape[1] == 128 and s % tk == 0 and tk % tq == 0
    seq = lambda a: pl.BlockSpec((1, 1) + a.shape[2:], lambda i, h, t, sl: (i, h, 0, 0))
    grid_spec = pltpu.PrefetchScalarGridSpec(
        num_scalar_prefetch=1,
        grid=(b, B_KV_HEADS, s // tq),
        in_specs=[
            pl.BlockSpec((1, tq, half), lambda i, h, t, sl: (i, t, h)),
            pl.BlockSpec((1, tq, GATE_PAD), lambda i, h, t, sl: (i, t, h)),
            pl.BlockSpec((1, ck.shape[1], 128), lambda i, h, t, sl: (i, 0, h)),
            pl.BlockSpec((1, cv.shape[1], 128), lambda i, h, t, sl: (i, 0, h)),
            seq(sk), seq(sv), seq(wk), seq(wv),
            pl.BlockSpec(cover.shape, lambda i, h, t, sl: (0, 0)),
        ],
        out_specs=pl.BlockSpec((1, tq, half), lambda i, h, t, sl: (i, t, h)),
        scratch_shapes=[pltpu.VMEM((rows, 1), F32), pltpu.VMEM((rows, 1), F32), pltpu.VMEM((rows, 128), F32)],
    )
    return pl.pallas_call(
        functools.partial(_nsa_prompt_kernel, tq=tq, tk=tk),
        grid_spec=grid_spec,
        out_shape=jax.ShapeDtypeStruct((b, s, QB_W), BF16),
        compiler_params=_cparams(("parallel", "parallel", "arbitrary"), 56),
        name="nsa_prompt",
    )(slopes, qb, gb, ck, cv, sk, sv, wk, wv, cover)


def _key_value_rows(kv, b, s, with_blocks):
    z = kv.reshape(b, s, 2, B_KV_HEADS, B_HD).transpose(2, 0, 3, 1, 4)
    pos = jnp.arange(s)
    shape = (b, B_KV_HEADS, s, 1)
    cols = [z[0], jnp.broadcast_to((pos >> 7).astype(F32)[:, None], shape),
            jnp.broadcast_to((pos & 127).astype(F32)[:, None], shape),
            jnp.zeros((b, B_KV_HEADS, s, 128 - B_HD - 2), F32)]
    if with_blocks:
        onehot = (pos[:, None] >> SLC_SHIFT) == jnp.arange(s // L_SLC)[None, :]
        cols.append(jnp.broadcast_to(onehot.astype(F32), (b, B_KV_HEADS, s, s // L_SLC)))
    return jnp.concatenate(cols, axis=-1).astype(BF16), jnp.concatenate([z[1], z[1]], axis=-1).astype(BF16)


def _cover_matrix(n_cmp, n_blk, n_pad):
    ci = jnp.arange(n_cmp)[:, None] * D_CMP
    sj = jnp.arange(n_pad)[None, :] * L_SLC
    cov = (ci < sj + L_SLC) & (ci + L_CMP > sj) & (jnp.arange(n_pad)[None, :] < n_blk)
    return cov.astype(BF16)


PAGES_PER_STEP = 64
PAGE_GROUPS = 2


def _cmp_pages_kernel(pt_ref, *refs):
    pages = refs[:PAGES_PER_STEP]
    w_ref, out_ref = refs[PAGES_PER_STEP:PAGES_PER_STEP + 2]
    row_refs = refs[PAGES_PER_STEP + 2:]
    per_group = PAGES_PER_STEP // PAGE_GROUPS
    n = per_group * PAGE_SIZE // D_CMP
    half = KV_W // 2
    for grp, rows_ref in enumerate(row_refs):
        for k, pg in enumerate(pages[grp * per_group:(grp + 1) * per_group]):
            for kv in range(2):
                rows_ref[kv, k * PAGE_SIZE:(k + 1) * PAGE_SIZE, :] = pg[0, kv * half:(kv + 1) * half, :].T
        x = jnp.concatenate([rows_ref[kv, pl.ds(i, n, stride=D_CMP), :].astype(BF16)
                             for i in range(D_CMP) for kv in range(2)], axis=1)
        out_ref[0, grp * n:(grp + 1) * n, :] = _dot(x, w_ref[...])


def _pages_weight(w1bd):
    zero = jnp.zeros((D_CMP,) + w1bd.shape[2:], w1bd.dtype)

    def diag(a):
        return jnp.concatenate([jnp.concatenate([a[:, 0], zero], axis=2),
                                jnp.concatenate([zero, a[:, 1]], axis=2)], axis=1)

    w = jnp.concatenate([diag(w1bd[:D_CMP]), diag(w1bd[D_CMP:])], axis=2)
    return w.reshape(D_CMP * KV_W, 2 * KV_W)


def _feature_major_pages(cache):
    return cache.transpose(0, 2, 3, 4, 5, 1).reshape(cache.shape[0], -1, cache.shape[1])


def _cmp_pages(page_table, cache, li, w1bd):
    bd, n_pages = page_table.shape
    pool = _feature_major_pages(cache)
    steps = n_pages // PAGES_PER_STEP
    rows = PAGES_PER_STEP * PAGE_SIZE // D_CMP
    w = _pages_weight(w1bd)

    def page_spec(k):
        return pl.BlockSpec((1, KV_W, PAGE_SIZE), lambda i, j, pt: (pt[i, j * PAGES_PER_STEP + k], li, 0))

    grid_spec = pltpu.PrefetchScalarGridSpec(
        num_scalar_prefetch=1,
        grid=(bd, steps),
        in_specs=[page_spec(k) for k in range(PAGES_PER_STEP)] + [pl.BlockSpec(w.shape, lambda i, j, pt: (0, 0))],
        out_specs=pl.BlockSpec((1, rows, 2 * KV_W), lambda i, j, pt: (i, j, 0)),
        scratch_shapes=[pltpu.VMEM((2, PAGES_PER_STEP // PAGE_GROUPS * PAGE_SIZE, KV_W // 2), F32)] * PAGE_GROUPS,
    )
    return pl.pallas_call(
        _cmp_pages_kernel,
        grid_spec=grid_spec,
        out_shape=jax.ShapeDtypeStruct((bd, steps * rows, 2 * KV_W), F32),
        compiler_params=_cparams(("parallel", "arbitrary"), 48),
        name="cmp_pages",
    )(page_table, *([pool] * PAGES_PER_STEP), w)


def _nsa_sample_select_kernel(sl_ref, pre_ref, new_ref, q_ref, pt_ref, pe_ref, w1_ref, w2_ref, cover_ref,
                              idx_ref, phys_ref, ocmp_ref, tail_ref, *, past):
    half = KV_W // 2

    def row_act(i, r):
        r8 = jnp.broadcast_to(r, (8, KV_W)).astype(BF16)
        return jnp.concatenate([_dot(r8[:, kv * half:(kv + 1) * half], w1_ref[i, kv]) for kv in range(2)], axis=1)

    @pl.when(pl.program_id(0) == 0)
    def _():
        t = jnp.zeros((8, KV_W), F32)
        for i in range(L_CMP):
            t = t + row_act(i, pe_ref[i:i + 1])
        tail_ref[...] = t

    n = pre_ref.shape[1]
    p1 = pltpu.roll(pre_ref[0, :, KV_W:2 * KV_W], n - 1, axis=0)
    p1 = jnp.where(_iota((n, 1), 0) == n - 1, row_act(D_CMP, new_ref[0])[0:1], p1)
    pre = pre_ref[0, :, 0:KV_W] + p1 + tail_ref[0:1]
    comp = _dot((pre * jax.nn.sigmoid(pre)).astype(BF16), w2_ref[...]).astype(BF16)

    qpos = past
    cpos = _iota((1, n), 1) * D_CMP + (L_CMP - 1)
    cmask = cpos <= qpos
    cdist = (qpos - cpos).astype(F32)
    row8 = _iota((8, 1), 0)
    imp = jnp.zeros((8, n), F32)
    for kvh in range(B_KV_HEADS):
        slope = jnp.zeros((8, 1), F32)
        for g in range(B_GROUP):
            slope = jnp.where(row8 == g, sl_ref[kvh * B_GROUP + g], slope)
        p = _masked_softmax(_dot_nt(q_ref[0, kvh], comp) - slope * cdist, cmask)
        p = jnp.where(row8 < B_GROUP, p, 0.0)
        ocmp_ref[0, kvh] = _dot(p.astype(BF16), comp)
        imp = jnp.where(row8 == kvh, jnp.sum(p, axis=0, keepdims=True), imp)

    nblk = cover_ref.shape[1]
    jb = _iota((1, nblk), 1)
    cur = qpos // L_SLC
    score = _dot3_lhs(imp, cover_ref[...])
    score = jnp.where((jb == 0) | (jb == cur) | (jb == cur - 1), SEL_FORCE, score)
    score = jnp.where(jb > cur, NEG_INF, score)
    lane = _iota((1, 128), 1)
    jbf = jb.astype(F32)
    pages = pt_ref[0].astype(F32)
    bpp_shift = (PAGE_SIZE // L_SLC).bit_length() - 1

    def body(it, carry):
        sc, idx, phys = carry
        m = jnp.max(sc, axis=-1, keepdims=True)
        j = jnp.min(jnp.where(sc == m, jbf, float(nblk)), axis=-1, keepdims=True)
        logical = jnp.minimum(j.astype(jnp.int32), past // L_SLC - 1) >> bpp_shift
        page = jnp.sum(jnp.where(lane == logical, pages, 0.0), axis=-1, keepdims=True)
        return jnp.where(jbf == j, NEG_INF, sc), jnp.where(lane == it, j, idx), jnp.where(lane == it, page, phys)

    zero = jnp.zeros((8, 128), F32)
    _, idx, phys = lax.fori_loop(0, N_SEL, body, (score, zero, zero))
    idx_ref[0] = idx.astype(jnp.int32)
    phys_ref[0] = phys.astype(jnp.int32)


def _nsa_sample_select(slopes, pre, cmp_new, q8, page_table, pe_b, w1bd, w2bd, cover, past):
    bd = pre.shape[0]
    assert page_table.shape[1] == 128
    full = lambda a: pl.BlockSpec(a.shape, lambda i, sl: (0,) * a.ndim)
    picks = pl.BlockSpec((1, 8, 128), lambda i, sl: (i, 0, 0))
    grid_spec = pltpu.PrefetchScalarGridSpec(
        num_scalar_prefetch=1,
        grid=(bd,),
        in_specs=[
            pl.BlockSpec((1,) + pre.shape[1:], lambda i, sl: (i, 0, 0)),
            pl.BlockSpec((1, 1, KV_W), lambda i, sl: (i, 0, 0)),
            pl.BlockSpec((1, B_KV_HEADS, 8, KV_W), lambda i, sl: (i, 0, 0, 0)),
            pl.BlockSpec((1, 1, 128), lambda i, sl: (i, 0, 0)),
            full(pe_b), full(w1bd), full(w2bd), full(cover),
        ],
        out_specs=[picks, picks, pl.BlockSpec((1, B_KV_HEADS, 8, KV_W), lambda i, sl: (i, 0, 0, 0))],
        scratch_shapes=[pltpu.VMEM((8, KV_W), F32)],
    )
    return pl.pallas_call(
        functools.partial(_nsa_sample_select_kernel, past=past),
        grid_spec=grid_spec,
        out_shape=[jax.ShapeDtypeStruct((bd, 8, 128), jnp.int32)] * 2
        + [jax.ShapeDtypeStruct((bd, B_KV_HEADS, 8, KV_W), F32)],
        compiler_params=_cparams(("arbitrary",), 48),
        name="nsa_sample_select",
    )(slopes, pre, cmp_new, q8, page_table.reshape(bd, 1, 128), pe_b, w1bd, w2bd, cover)


def _nsa_sample_attend_kernel(pt_ref, ix_ref, sl_ref, *refs, past, nb_past):
    n_blocks = B_KV_HEADS * N_SEL
    pages = refs[:n_blocks]
    q_ref, slc_new_ref, win_new_ref, win_ref, g_ref, ocmp_ref, o_ref = refs[n_blocks:]
    b = pl.program_id(0)
    bpp = PAGE_SIZE // L_SLC
    half = KV_W // 2

    def token_rows(ref, c0):
        return jnp.concatenate([ref[0, kv * half:(kv + 1) * half, c0:c0 + 128].T for kv in range(2)], axis=1)

    row8 = _iota((8, 1), 0)
    first = _iota((L_SLC, 1), 0) == 0
    new_blk = jnp.where(first, slc_new_ref[0], 0.0).astype(BF16)
    wb = win_ref.shape[2]
    win = jnp.concatenate([token_rows(win_ref, c0) for c0 in range(0, wb, 128)], axis=0).astype(BF16)
    win_new = jnp.where(_iota((8, 1), 0) == 0, win_new_ref[0], 0.0).astype(BF16)
    for kvh in range(B_KV_HEADS):
        q = q_ref[0, kvh]
        slope = jnp.zeros((8, 1), F32)
        for g in range(B_GROUP):
            slope = jnp.where(row8 == g, sl_ref[kvh * B_GROUP + g], slope)
        m = jnp.full((8, 1), NEG_INF, F32)
        l = jnp.zeros((8, 1), F32)
        acc = jnp.zeros((8, KV_W), F32)
        for n in range(N_SEL):
            j = ix_ref[b, kvh * N_SEL + n]
            pg = token_rows(pages[kvh * N_SEL + n], 0).astype(BF16)
            blk = pg[0:L_SLC]
            for sub in range(1, bpp):
                blk = jnp.where(jnp.minimum(j, nb_past - 1) % bpp == sub, pg[sub * L_SLC:(sub + 1) * L_SLC], blk)
            blk = jnp.where(j >= nb_past, new_blk, blk)
            kpos = j * L_SLC + _iota((1, L_SLC), 1)
            s = _dot_nt(q, blk) - slope * (past - kpos).astype(F32)
            s = jnp.where(kpos <= past, s, NEG_INF)
            m_new = jnp.maximum(m, jnp.max(s, axis=-1, keepdims=True))
            m_safe = jnp.where(m_new > NEG_INF, m_new, 0.0)
            alpha = jnp.exp(m - m_safe)
            p = jnp.exp(s - m_safe)
            l = alpha * l + jnp.sum(p, axis=-1, keepdims=True)
            acc = alpha * acc + _dot(p.astype(BF16), blk)
            m = m_new
        o_slc = acc * (1.0 / jnp.where(l > 0, l, 1.0))
        wpos = past - wb + _iota((1, wb), 1)
        s1 = _dot_nt(q, win) - slope * (past - wpos).astype(F32)
        s1 = jnp.where(wpos > past - WINDOW, s1, NEG_INF)
        s2 = jnp.where(_iota((1, 8), 1) == 0, _dot_nt(q, win_new), NEG_INF)
        mw = jnp.maximum(jnp.max(s1, axis=-1, keepdims=True), jnp.max(s2, axis=-1, keepdims=True))
        e1 = jnp.exp(s1 - mw)
        e2 = jnp.exp(s2 - mw)
        dw = jnp.sum(e1, axis=-1, keepdims=True) + jnp.sum(e2, axis=-1, keepdims=True)
        o_win = (_dot(e1.astype(BF16), win) + _dot(e2.astype(BF16), win_new)) * (1.0 / dw)
        gate = jax.nn.sigmoid(g_ref[0, kvh])
        o_ref[0, kvh] = gate[:, 0:1] * ocmp_ref[0, kvh] + gate[:, 1:2] * o_slc + gate[:, 2:3] * o_win


def _nsa_sample_attend(phys, idx, slopes, cache, li, q8, slc_new, win_new, win_state, gate8, o_cmp, past):
    bd = q8.shape[0]
    bpp = PAGE_SIZE // L_SLC
    nb_past = past // L_SLC
    n_blocks = B_KV_HEADS * N_SEL
    pool = _feature_major_pages(cache)

    def blk_spec(k):
        return pl.BlockSpec((1, KV_W, PAGE_SIZE), lambda i, ph, ix, sl: (ph[i, k], li, 0))

    row = lambda: pl.BlockSpec((1, 1, KV_W), lambda i, pt, ix, sl: (i, 0, 0))
    per_head = lambda w: pl.BlockSpec((1, B_KV_HEADS, 8, w), lambda i, pt, ix, sl: (i, 0, 0, 0))
    grid_spec = pltpu.PrefetchScalarGridSpec(
        num_scalar_prefetch=3,
        grid=(bd,),
        in_specs=[blk_spec(k) for k in range(n_blocks)] + [
            per_head(KV_W), row(), row(),
            pl.BlockSpec((1,) + win_state.shape[1:], lambda i, pt, ix, sl: (i, 0, 0)),
            per_head(128), per_head(KV_W)],
        out_specs=per_head(KV_W),
    )
    return pl.pallas_call(
        functools.partial(_nsa_sample_attend_kernel, past=past, nb_past=nb_past),
        grid_spec=grid_spec,
        out_shape=jax.ShapeDtypeStruct((bd, B_KV_HEADS, 8, KV_W), F32),
        compiler_params=_cparams(("arbitrary",), 48),
        name="nsa_sample_attend",
    )(phys, idx, slopes, *([pool] * n_blocks), q8, slc_new, win_new, win_state, gate8, o_cmp)


def _outproj_kernel(x_ref, a_ref, b_ref, w_ref, y_ref):
    na = a_ref.shape[1]
    y_ref[...] = x_ref[...] + _dot(a_ref[...], w_ref[0:na]) + _dot(b_ref[...], w_ref[na:])


def _outproj(x, oa, ob, w, tm):
    m = x.shape[0]
    row = lambda n: pl.BlockSpec((tm, n), lambda i: (i, 0))
    return pl.pallas_call(
        _outproj_kernel,
        grid=(m // tm,),
        in_specs=[row(D_MODEL), row(oa.shape[1]), row(ob.shape[1]), pl.BlockSpec(w.shape, lambda i: (0, 0))],
        out_specs=row(D_MODEL),
        out_shape=jax.ShapeDtypeStruct((m, D_MODEL), F32),
        compiler_params=_cparams(("parallel",), 32),
        name="outproj",
    )(x, oa, ob, w)


def _mixer_prompt_kernel(*refs, groups, convs, gated_ffn, final, tm, tf, f):
    x_ref, nw_ref, up_ref, cw_ref, down_ref = refs[0:5]
    pos = 5
    fw_ref = refs[pos] if final else None
    pos += int(final)
    y_ref, tail_ref = refs[pos], refs[pos + 1]
    act_ref, ubuf_ref, carry_ref = refs[pos + 2:]

    @pl.when(pl.program_id(1) == 0)
    def _():
        carry_ref[...] = jnp.zeros_like(carry_ref)

    x = x_ref[0]
    xn = _rms(x, nw_ref[...]).astype(BF16)
    for j in range(f // tf):
        u = [_dot(xn, up_ref[:, g * f + j * tf:g * f + (j + 1) * tf]) for g in range(groups)]
        conv_in = [u[0], u[1]] if gated_ffn else [u[1] * u[0]]
        conv_out = []
        for c, z in enumerate(conv_in):
            cols = slice(c * f + j * tf, c * f + (j + 1) * tf)
            slot = (j % 2) * convs + c
            ubuf_ref[slot, 0:8] = carry_ref[:, cols]
            ubuf_ref[slot, 8:tm + 8] = z
            w = cw_ref[:, cols]
            conv_out.append(w[0:1] * ubuf_ref[slot, pl.ds(6, tm), :] + w[1:2] * ubuf_ref[slot, pl.ds(7, tm), :]
                            + w[2:3] * z)
            last = z[tm - 8:tm]
            carry_ref[:, cols] = last
            tail_ref[0, 0, :, cols] = last
        if gated_ffn:
            act = conv_out[0] * jax.nn.sigmoid(conv_out[0]) * conv_out[1]
        else:
            act = u[2] * conv_out[0]
        act_ref[:, j * tf:(j + 1) * tf] = act.astype(BF16)
    y = x + _dot(act_ref[...], down_ref[...])
    y_ref[0] = _rms(y, fw_ref[...]) if final else y


def _mixer_prompt(x, nw, w_up, conv_w, w_down, gated_ffn, final_w=None, tm=512, tf=256):
    b, s, d = x.shape
    f = w_down.shape[0]
    groups = w_up.shape[1] // f
    convs = conv_w.shape[1] // f
    final = final_w is not None
    const = lambda a: pl.BlockSpec(a.shape, lambda i, t: (0,) * a.ndim, pipeline_mode=pl.Buffered(1))
    in_specs = [pl.BlockSpec((1, tm, d), lambda i, t: (i, t, 0)), const(nw), const(w_up), const(conv_w),
                const(w_down)]
    args = [x, nw, w_up, conv_w, w_down]
    if final:
        in_specs.append(const(final_w))
        args.append(final_w)
    y, tail = pl.pallas_call(
        functools.partial(_mixer_prompt_kernel, groups=groups, convs=convs, gated_ffn=gated_ffn,
                          final=final, tm=tm, tf=tf, f=f),
        grid=(b, s // tm),
        in_specs=in_specs,
        out_specs=[pl.BlockSpec((1, tm, d), lambda i, t: (i, t, 0)),
                   pl.BlockSpec((1, 1, 8, convs * f), lambda i, t: (i, t, 0, 0))],
        out_shape=[jax.ShapeDtypeStruct((b, s, d), F32), jax.ShapeDtypeStruct((b, s // tm, 8, convs * f), F32)],
        scratch_shapes=[pltpu.VMEM((tm, f), BF16), pltpu.VMEM((2 * convs, tm + 8, tf), F32),
                        pltpu.VMEM((8, convs * f), F32)],
        compiler_params=_cparams(("parallel", "arbitrary"), 56),
        name="ffn_prompt" if gated_ffn else "sconv_prompt",
    )(*args)
    return y, tail[:, -1, 8 - (CONV_W - 1):, :]


def _mixer_sample_kernel(*refs, groups, convs, gated_ffn, final):
    x_ref, nw_ref = refs[0:2]
    up = refs[2:2 + groups]
    cw = refs[2 + groups:2 + groups + convs]
    st = refs[2 + groups + convs:2 + groups + 2 * convs]
    down_ref = refs[2 + groups + 2 * convs]
    pos = 3 + groups + 2 * convs
    fw_ref = refs[pos] if final else None
    pos += int(final)
    y_ref = refs[pos]
    news = refs[pos + 1:pos + 1 + convs]
    xn_ref, acc_ref = refs[pos + 1 + convs:]
    j = pl.program_id(0)

    @pl.when(j == 0)
    def _():
        xn_ref[...] = _rms(x_ref[...], nw_ref[...]).astype(BF16)
        acc_ref[...] = jnp.zeros_like(acc_ref)

    xn = xn_ref[...]
    u = [_dot(xn, w[...]) for w in up]
    conv_in = [u[0], u[1]] if gated_ffn else [u[1] * u[0]]
    conv_out = []
    for c, z in enumerate(conv_in):
        w = cw[c][...]
        conv_out.append(w[0:1] * st[c][:, 0, :] + w[1:2] * st[c][:, 1, :] + w[2:3] * z)
        news[c][...] = z
    if gated_ffn:
        act = conv_out[0] * jax.nn.sigmoid(conv_out[0]) * conv_out[1]
    else:
        act = u[2] * conv_out[0]
    acc_ref[...] += _dot(act.astype(BF16), down_ref[...])

    @pl.when(j == pl.num_programs(0) - 1)
    def _():
        y = x_ref[...] + acc_ref[...]
        y_ref[...] = _rms(y, fw_ref[...]) if final else y


def _mixer_sample(x, nw, w_up, conv_w, w_down, state, gated_ffn, final_w=None, tf=256):
    bd, d = x.shape
    f = w_down.shape[0]
    groups = w_up.shape[1] // f
    convs = conv_w.shape[1] // f
    nf = f // tf
    final = final_w is not None
    in_specs = [pl.BlockSpec((bd, d), lambda j: (0, 0)), pl.BlockSpec(nw.shape, lambda j: (0, 0))]
    in_specs += [pl.BlockSpec((d, tf), lambda j, g=g: (0, g * nf + j)) for g in range(groups)]
    in_specs += [pl.BlockSpec((CONV_W, tf), lambda j, c=c: (0, c * nf + j)) for c in range(convs)]
    in_specs += [pl.BlockSpec((bd, CONV_W - 1, tf), lambda j, c=c: (0, 0, c * nf + j)) for c in range(convs)]
    in_specs += [pl.BlockSpec((tf, d), lambda j: (j, 0))]
    args = [x, nw] + [w_up] * groups + [conv_w] * convs + [state] * convs + [w_down]
    if final:
        in_specs.append(pl.BlockSpec(final_w.shape, lambda j: (0, 0)))
        args.append(final_w)
    out_specs = [pl.BlockSpec((bd, d), lambda j: (0, 0))]
    out_specs += [pl.BlockSpec((bd, tf), lambda j: (0, j))] * convs
    out_shape = [jax.ShapeDtypeStruct((bd, d), F32)] + [jax.ShapeDtypeStruct((bd, f), F32)] * convs
    res = pl.pallas_call(
        functools.partial(_mixer_sample_kernel, groups=groups, convs=convs, gated_ffn=gated_ffn, final=final),
        grid=(nf,),
        in_specs=in_specs,
        out_specs=out_specs,
        out_shape=out_shape,
        scratch_shapes=[pltpu.VMEM((bd, d), BF16), pltpu.VMEM((bd, d), F32)],
        compiler_params=_cparams(("arbitrary",), 32),
        name="ffn_sample" if gated_ffn else "sconv_sample",
    )(*args)
    new = jnp.concatenate(res[1:], axis=-1)
    return res[0], jnp.concatenate([state[:, 1:], new[:, None, :]], axis=1)


def _pad_in_weight(w):
    c = HG_W + QB_W + 3 * KV_W
    per = 3 * B_GROUP
    gates = [jnp.pad(w[:, c + h * per:c + (h + 1) * per], ((0, 0), (0, GATE_PAD - per)))
             for h in range(B_KV_HEADS)]
    return jnp.concatenate([w[:, :c]] + gates, axis=1).astype(BF16)


def _per_head_rows(z, width):
    bd = z.shape[0]
    z = z.reshape(bd, B_KV_HEADS, B_GROUP, -1)
    return jnp.pad(z, ((0, 0), (0, 0), (0, 8 - B_GROUP), (0, width - z.shape[-1])))


def kernel(x_prompt, x_sample, cache_cmp_kv, cache_slc_kv, state_win_kv, state_hgrn, state_sconv, state_ffn,
           page_table, norm_mix, norm_ffn, norm_final, ab_w_in, ab_w_out, hgrn_lb_logits, hgrn_norm,
           cmp_pe, cmp_w1, cmp_w2, c_w_in, c_conv, c_w_out, ffn_up, ffn_conv, ffn_down):
    b, s, d = x_prompt.shape
    bd, t, _ = x_sample.shape
    assert t == 1 and d == D_MODEL
    depth = norm_mix.shape[0]
    n_a = ab_w_in.shape[0]
    wb = state_win_kv.shape[2]
    n_pages = page_table.shape[1]
    past = n_pages * PAGE_SIZE
    assert wb == WINDOW and past % (PAGES_PER_STEP * PAGE_SIZE) == 0 and s % 1024 == 0
    hh = jnp.arange(1, B_HEADS + 1, dtype=F32)
    slopes = 2.0 ** (-8.0 * hh / B_HEADS)
    lb_logits = hgrn_lb_logits.astype(F32)
    row2 = lambda v: v.reshape(1, -1).astype(F32)

    xp = x_prompt.reshape(b * s, d)
    xs = x_sample.reshape(bd, d)
    outs_p = {k: [] for k in ("cmp", "slc", "win", "hg", "sc", "ff")}
    outs_s = {k: [] for k in ("cmp", "slc", "win", "hg", "sc", "ff")}
    y_prompt = y_sample = None
    for l in range(depth):
        last = l == depth - 1
        if l % 2 == 0:
            la = l // 2
            w_in = _pad_in_weight(ab_w_in[la])
            w_out = ab_w_out[la].astype(BF16)
            nw = row2(norm_mix[l])
            hnw = row2(hgrn_norm[la])
            pe_b, w1bd, w2_packed = _cmp_weights(cmp_pe[la], cmp_w1[la], cmp_w2[la], packed=True)
            _, _, w2_plain = _cmp_weights(cmp_pe[la], cmp_w1[la], cmp_w2[la], packed=False)
            hg, qb, kc, ksl, kw, gb = _inproj(xp, nw, w_in, tm=512)
            oa, hst = _hgrn_prompt(hg, lb_logits, hnw, la, b, s)
            ck, cv = _compress_prompt(kc.reshape(b, 2 * s, KV_W // 2), pe_b, w1bd, w2_packed)
            n_cmp = s // D_CMP
            nblk = s // L_SLC
            cover = _cover_matrix(n_cmp, nblk, nblk)
            sk, sv = _key_value_rows(ksl, b, s, with_blocks=True)
            wk, wv = _key_value_rows(kw, b, s, with_blocks=False)
            ob = _nsa_prompt(slopes, qb.reshape(b, s, QB_W), gb.reshape(b, s, -1), ck, cv, sk, sv, wk, wv, cover)
            xp = _outproj(xp, oa, ob.reshape(b * s, QB_W), w_out, tm=512)
            kv6 = lambda z, n, tt: z.reshape(n, tt, 2, B_KV_HEADS, B_HD)
            outs_p["cmp"].append(kv6(kc, b, s))
            outs_p["slc"].append(kv6(ksl, b, s))
            kw6 = kv6(kw, b, s)
            outs_p["win"].append(jnp.pad(kw6, ((0, 0), (max(wb - s, 0), 0), (0, 0), (0, 0), (0, 0)))[:, -wb:])
            outs_p["hg"].append(hst)
            hg, qb, kc, ksl, kw, gb = _inproj(xs, nw, w_in, tm=bd)
            oa, hst = _hgrn_sample(hg, state_hgrn[la], lb_logits, hnw, la)
            pre = _cmp_pages(page_table, cache_cmp_kv, la, w1bd)
            n_cmp = pre.shape[1]
            nblk = past // L_SLC + 1
            cover = _cover_matrix(n_cmp, nblk, -(-nblk // 128) * 128)
            q8 = jnp.stack([jnp.pad(_per_head_rows(qb, B_HD)[:, h], ((0, 0), (0, 0), (h * B_HD, KV_W - (h + 1) * B_HD)))
                            for h in range(B_KV_HEADS)], axis=1)
            idx, phys, o_cmp = _nsa_sample_select(slopes, pre, kc.reshape(bd, 1, KV_W), q8, page_table, pe_b, w1bd,
                                                  w2_plain, cover, past)
            picks = lambda z: z[:, :B_KV_HEADS, :N_SEL].reshape(bd, B_KV_HEADS * N_SEL)
            idx, phys = picks(idx), picks(phys)
            gate8 = _per_head_rows(jnp.concatenate([gb[:, h * GATE_PAD:h * GATE_PAD + 3 * B_GROUP]
                                                    for h in range(B_KV_HEADS)], axis=1), 128)
            win_state = state_win_kv[la].transpose(0, 2, 3, 4, 1).reshape(bd, KV_W, wb)
            o8 = _nsa_sample_attend(phys, idx, slopes, cache_slc_kv, la, q8, ksl.reshape(bd, 1, KV_W),
                                    kw.reshape(bd, 1, KV_W), win_state, gate8, o_cmp, past)
            ob = jnp.concatenate([o8[:, h, :B_GROUP, 2 * B_HD + h * B_HD:2 * B_HD + (h + 1) * B_HD]
                                  .reshape(bd, B_GROUP * B_HD) for h in range(B_KV_HEADS)], axis=1)
            xs = _outproj(xs, oa, ob.astype(BF16), w_out, tm=bd)
            outs_s["cmp"].append(kv6(kc, bd, t))
            outs_s["slc"].append(kv6(ksl, bd, t))
            outs_s["win"].append(jnp.concatenate([state_win_kv[la], kv6(kw, bd, t)], axis=1)[:, t:])
            outs_s["hg"].append(hst)
        else:
            lc = l // 2
            xp3, tail = _mixer_prompt(xp.reshape(b, s, d), row2(norm_mix[l]), c_w_in[lc].astype(BF16),
                                      c_conv[lc], c_w_out[lc].astype(BF16), gated_ffn=False)
            xp = xp3.reshape(b * s, d)
            outs_p["sc"].append(tail)
            xs, new = _mixer_sample(xs, row2(norm_mix[l]), c_w_in[lc].astype(BF16), c_conv[lc],
                                    c_w_out[lc].astype(BF16), state_sconv[lc], gated_ffn=False)
            outs_s["sc"].append(new)
        fw = row2(norm_final) if last else None
        xp3, tail = _mixer_prompt(xp.reshape(b, s, d), row2(norm_ffn[l]), ffn_up[l].astype(BF16), ffn_conv[l],
                                  ffn_down[l].astype(BF16), gated_ffn=True, final_w=fw)
        xp = xp3.reshape(b * s, d)
        outs_p["ff"].append(tail)
        xs, new = _mixer_sample(xs, row2(norm_ffn[l]), ffn_up[l].astype(BF16), ffn_conv[l],
                                ffn_down[l].astype(BF16), state_ffn[l], gated_ffn=True, final_w=fw)
        outs_s["ff"].append(new)
    y_prompt = xp.reshape(b, s, d)
    y_sample = xs.reshape(bd, t, d)
    return (y_prompt, y_sample,
            jnp.stack(outs_p["cmp"], axis=2), jnp.stack(outs_p["slc"], axis=2), jnp.stack(outs_p["win"], axis=0),
            jnp.stack(outs_p["hg"], axis=0), jnp.stack(outs_p["sc"], axis=0), jnp.stack(outs_p["ff"], axis=0),
            jnp.stack(outs_s["cmp"], axis=2), jnp.stack(outs_s["slc"], axis=2), jnp.stack(outs_s["win"], axis=0),
            jnp.stack(outs_s["hg"], axis=0), jnp.stack(outs_s["sc"], axis=0), jnp.stack(outs_s["ff"], axis=0))
```

```python
import functools

import jax
import jax.numpy as jnp
from jax import lax
from jax.experimental import pallas as pl
from jax.experimental.pallas import tpu as pltpu

F32 = jnp.float32
BF16 = jnp.bfloat16
NEG_INF = float("-inf")

D_MODEL = 1024
PAGE_SIZE = 128
A_HEADS = 4
A_DK = 128
A_DV = 128
B_HEADS = 8
B_KV_HEADS = 2
B_GROUP = B_HEADS // B_KV_HEADS
B_HD = 64
L_CMP = 32
D_CMP = 16
CMP_HID = 64
L_SLC = 64
SLC_SHIFT = 6
N_SEL = 16
WINDOW = 512
SEL_FORCE = 1e9
CONV_W = 3
EPS = 1e-6
KV_W = 2 * B_KV_HEADS * B_HD
HG_W = 4 * A_HEADS * A_DK
QB_W = B_HEADS * B_HD
GATE_PAD = 128
IN_PAD = HG_W + QB_W + 3 * KV_W + B_KV_HEADS * GATE_PAD

V7X_VMEM_BYTES = 64 * 2**20


def _cparams(sem, vmem_mb):
    assert vmem_mb * 2**20 < V7X_VMEM_BYTES
    return pltpu.CompilerParams(dimension_semantics=sem, vmem_limit_bytes=vmem_mb * 2**20)


def _dot(a, b):
    return jnp.dot(a, b, preferred_element_type=F32)


def _dot_nt(a, b):
    return lax.dot_general(a, b, (((1,), (1,)), ((), ())), preferred_element_type=F32)


def _dot_tn(a, b):
    return lax.dot_general(a, b, (((0,), (0,)), ((), ())), preferred_element_type=F32)


def _split3(x):
    hi = x.astype(BF16)
    r1 = x - hi.astype(F32)
    mid = r1.astype(BF16)
    lo = (r1 - mid.astype(F32)).astype(BF16)
    return hi, mid, lo


def _dot3_lhs(x, w):
    hi, mid, lo = _split3(x)
    return _dot(hi, w) + _dot(mid, w) + _dot(lo, w)


def _dot3_rhs(w, x):
    hi, mid, lo = _split3(x)
    return _dot(w, hi) + _dot(w, mid) + _dot(w, lo)


def _rms(x, w):
    return x * lax.rsqrt(jnp.mean(x * x, axis=-1, keepdims=True) + EPS) * w


def _masked_softmax(s, mask):
    s = jnp.where(mask, s, NEG_INF)
    m = jnp.max(s, axis=-1, keepdims=True)
    m = jnp.where(m > NEG_INF, m, 0.0)
    e = jnp.exp(s - m)
    d = jnp.sum(e, axis=-1, keepdims=True)
    return e * (1.0 / jnp.where(d > 0, d, 1.0))


def _iota(shape, dim):
    return lax.broadcasted_iota(jnp.int32, shape, dim)


def _top_blocks(score, jb, n_sel):
    big = float(score.shape[-1])
    jbf = jb.astype(F32)

    def body(_, carry):
        sc, sel = carry
        m = jnp.max(sc, axis=-1, keepdims=True)
        idx = jnp.min(jnp.where(sc == m, jbf, big), axis=-1, keepdims=True)
        pick = jbf == idx
        return jnp.where(pick, NEG_INF, sc), jnp.where(pick, 1.0, sel)

    _, sel = lax.fori_loop(0, n_sel, body, (score, jnp.zeros_like(score)))
    return sel


def _attention_rows(kv, pos, with_blocks):
    tm = kv.shape[0]
    lane = _iota((tm, 128), 1)
    lo = lane < B_HD
    pos_cols = jnp.where(lane == B_HD, (pos >> 7).astype(F32),
                         jnp.where(lane == B_HD + 1, (pos & 127).astype(F32), 0.0))
    k_pair, v_pair = kv[:, 0:128], kv[:, 128:256]
    k_swap, v_swap = pltpu.roll(k_pair, B_HD, axis=1), pltpu.roll(v_pair, B_HD, axis=1)
    onehot = ((pos >> SLC_SHIFT) == lane).astype(BF16)
    keys, values = [], []
    for h, (k, v_lo, v_hi) in enumerate(((k_pair, v_pair, v_swap), (k_swap, v_swap, v_pair))):
        key = jnp.where(lo, k, pos_cols).astype(BF16)
        keys.append(jnp.concatenate([key, onehot], axis=1) if with_blocks else key)
        values.append(jnp.where(lo, v_lo, v_hi).astype(BF16))
    return keys, values


def _inproj_kernel(x_ref, nw_ref, w_ref, hg_ref, qb_ref, kc_ref, ksl_ref, kw_ref, gb_ref, *row_refs, seq):
    xn = _rms(x_ref[...], nw_ref[...]).astype(BF16)

    def proj(c0, n):
        return _dot(xn, w_ref[:, c0:c0 + n])

    hg_ref[...] = proj(0, HG_W)
    qb_ref[...] = (proj(HG_W, QB_W) * (B_HD ** -0.5)).astype(BF16)
    c = HG_W + QB_W
    kc_ref[...] = proj(c, KV_W)
    ksl = proj(c + KV_W, KV_W)
    kw = proj(c + 2 * KV_W, KV_W)
    ksl_ref[...] = ksl
    kw_ref[...] = kw
    gb_ref[...] = proj(c + 3 * KV_W, B_KV_HEADS * GATE_PAD)
    if row_refs:
        sk_ref, sv_ref, wk_ref, wv_ref = row_refs
        tm = x_ref.shape[0]
        pos = (pl.program_id(0) * tm + _iota((tm, 1), 0)) & (seq - 1)
        for kv, k_ref, v_ref, with_blocks in ((ksl, sk_ref, sv_ref, True), (kw, wk_ref, wv_ref, False)):
            keys, values = _attention_rows(kv, pos, with_blocks)
            for h in range(B_KV_HEADS):
                k_ref[h] = keys[h]
                v_ref[h] = values[h]


def _inproj(x, nw, w_pad, tm, seq=None):
    m = x.shape[0]
    row = lambda n: pl.BlockSpec((tm, n), lambda i: (i, 0))
    full = lambda a: pl.BlockSpec(a.shape, lambda i: (0,) * a.ndim)
    widths = (HG_W, QB_W, KV_W, KV_W, KV_W, B_KV_HEADS * GATE_PAD)
    dtypes = (F32, BF16, F32, F32, F32, F32)
    out_specs = [row(n) for n in widths]
    out_shape = [jax.ShapeDtypeStruct((m, n), d) for n, d in zip(widths, dtypes)]
    if seq is not None:
        assert seq // L_SLC == 128 and seq % tm == 0 and seq & (seq - 1) == 0
        for n in (256, 128, 128, 128):
            out_specs.append(pl.BlockSpec((B_KV_HEADS, tm, n), lambda i: (0, i, 0)))
            out_shape.append(jax.ShapeDtypeStruct((B_KV_HEADS, m, n), BF16))
    return pl.pallas_call(
        functools.partial(_inproj_kernel, seq=seq),
        grid=(m // tm,),
        in_specs=[row(D_MODEL), full(nw), full(w_pad)],
        out_specs=out_specs,
        out_shape=out_shape,
        compiler_params=_cparams(("parallel",), 48),
        name="inproj",
    )(x, nw, w_pad)


def _lower_bound(logit_ref, la):
    lg = logit_ref[...]
    e = jnp.exp(lg - jnp.max(lg, axis=0, keepdims=True))
    return jnp.sum(e[:la + 1], axis=0, keepdims=True) / jnp.sum(e, axis=0, keepdims=True)


def _hgrn_post(o, nw, ga):
    o = o * lax.rsqrt(jnp.mean(o * o, axis=-1, keepdims=True) + EPS)
    return o * nw * jax.nn.sigmoid(ga)


def _hgrn_prompt_kernel(hg_ref, lg_ref, nw_ref, oa_ref, st_ref, s_ref, *, la, chunk, sub):
    ci = pl.program_id(1)

    @pl.when(ci == 0)
    def _():
        s_ref[...] = jnp.zeros_like(s_ref)

    lb = _lower_bound(lg_ref, la)
    tri = (_iota((chunk, chunk), 0) >= _iota((chunk, chunk), 1)).astype(BF16)
    trow = _iota((chunk, 1), 0)
    srow = _iota((sub, 1), 0)
    lane = _iota((sub, chunk), 1)
    for h in range(A_HEADS):
        hs = slice(h * A_DK, (h + 1) * A_DK)
        q = hg_ref[:, h * A_DK:(h + 1) * A_DK]
        fl = hg_ref[:, 512 + h * A_DK:512 + (h + 1) * A_DK]
        v = hg_ref[:, 1024 + h * A_DV:1024 + (h + 1) * A_DV]
        ga = hg_ref[:, 1536 + h * A_DV:1536 + (h + 1) * A_DV]
        lbh = lb[:, hs]
        f = lbh + (1.0 - lbh) * jax.nn.sigmoid(fl)
        a = _dot3_rhs(tri, jnp.log(f))
        k = 1.0 - f
        st = s_ref[h]
        vb = v.astype(BF16)
        o = _dot_nt((q * jnp.exp(a)).astype(BF16), st.astype(BF16))
        rows = []
        for i in range(chunk // sub):
            r0 = i * sub
            qi, ai = q[r0:r0 + sub], a[r0:r0 + sub]
            att = jnp.zeros((sub, chunk), F32)
            for s in range(sub):
                dec = jnp.exp(jnp.where(srow >= s, ai - a[r0 + s:r0 + s + 1], NEG_INF))
                col = jnp.sum(qi * dec * k[r0 + s:r0 + s + 1], axis=-1, keepdims=True)
                att = jnp.where(lane == r0 + s, col, att)
            if i > 0:
                ref_a = a[r0 - 1:r0]
                qt = qi * jnp.exp(ai - ref_a)
                kt = k * jnp.exp(jnp.where(trow < r0, ref_a - a, NEG_INF))
                att = att + _dot_nt(qt.astype(BF16), kt.astype(BF16))
            rows.append(att)
        att = jnp.concatenate(rows, axis=0)
        o = o + _dot(att.astype(BF16), vb)
        a_end = a[chunk - 1:chunk]
        khat = k * jnp.exp(a_end - a)
        s_ref[h] = st * jnp.exp(a_end) + _dot_tn(vb, khat.astype(BF16))
        oa_ref[:, hs] = _hgrn_post(o, nw_ref[:, hs], ga).astype(BF16)

    @pl.when(ci == pl.num_programs(1) - 1)
    def _():
        for h in range(A_HEADS):
            st_ref[0, h] = s_ref[h].T


def _hgrn_prompt(hg, lb_logits, nw, la, b, s, chunk=64, sub=16):
    n = s // chunk
    full = lambda a: pl.BlockSpec(a.shape, lambda i, j: (0,) * a.ndim)
    return pl.pallas_call(
        functools.partial(_hgrn_prompt_kernel, la=la, chunk=chunk, sub=sub),
        grid=(b, n),
        in_specs=[pl.BlockSpec((chunk, HG_W), lambda i, j: (i * n + j, 0)), full(lb_logits), full(nw)],
        out_specs=[pl.BlockSpec((chunk, A_HEADS * A_DV), lambda i, j: (i * n + j, 0)),
                   pl.BlockSpec((1, A_HEADS, A_DK, A_DV), lambda i, j: (i, 0, 0, 0))],
        out_shape=[jax.ShapeDtypeStruct((b * s, A_HEADS * A_DV), BF16),
                   jax.ShapeDtypeStruct((b, A_HEADS, A_DK, A_DV), F32)],
        scratch_shapes=[pltpu.VMEM((A_HEADS, A_DV, A_DK), F32)],
        compiler_params=_cparams(("parallel", "arbitrary"), 32),
        name="hgrn_prompt",
    )(hg, lb_logits, nw)


def _hgrn_sample_kernel(hg_ref, s0_ref, lg_ref, nw_ref, oa_ref, st_ref, *, la):
    lb = _lower_bound(lg_ref, la)
    eye = (_iota((A_DK, A_DK), 0) == _iota((A_DK, A_DK), 1)).astype(F32)

    def col(r):
        return jnp.sum(eye * r, axis=1, keepdims=True)

    for h in range(A_HEADS):
        hs = slice(h * A_DK, (h + 1) * A_DK)
        q = hg_ref[0, :, h * A_DK:(h + 1) * A_DK]
        fl = hg_ref[0, :, 512 + h * A_DK:512 + (h + 1) * A_DK]
        v = hg_ref[0, :, 1024 + h * A_DV:1024 + (h + 1) * A_DV]
        ga = hg_ref[0, :, 1536 + h * A_DV:1536 + (h + 1) * A_DV]
        lbh = lb[:, hs]
        f = lbh + (1.0 - lbh) * jax.nn.sigmoid(fl)
        sn = col(f) * s0_ref[0, h] + col(1.0 - f) * v
        st_ref[0, h] = sn
        o = jnp.sum(col(q) * sn, axis=0, keepdims=True)
        oa_ref[0, :, hs] = _hgrn_post(o, nw_ref[:, hs], ga).astype(BF16)


def _hgrn_sample(hg, s0, lb_logits, nw, la):
    bd = hg.shape[0]
    full = lambda a: pl.BlockSpec(a.shape, lambda i: (0,) * a.ndim)
    st_spec = pl.BlockSpec((1, A_HEADS, A_DK, A_DV), lambda i: (i, 0, 0, 0))
    oa, st = pl.pallas_call(
        functools.partial(_hgrn_sample_kernel, la=la),
        grid=(bd,),
        in_specs=[pl.BlockSpec((1, 1, HG_W), lambda i: (i, 0, 0)), st_spec, full(lb_logits), full(nw)],
        out_specs=[pl.BlockSpec((1, 1, A_HEADS * A_DV), lambda i: (i, 0, 0)), st_spec],
        out_shape=[jax.ShapeDtypeStruct((bd, 1, A_HEADS * A_DV), BF16),
                   jax.ShapeDtypeStruct(s0.shape, F32)],
        compiler_params=_cparams(("parallel",), 16),
        name="hgrn_sample",
    )(hg.reshape(bd, 1, HG_W), s0, lb_logits, nw)
    return oa.reshape(bd, A_HEADS * A_DV), st


def _cmp_weights(pe, w1, w2, packed):
    eye = jnp.eye(B_KV_HEADS, dtype=F32)
    w1r = w1.reshape(2, L_CMP, B_HD, CMP_HID)
    w1bd = jnp.einsum("kidm,hH->ikhdHm", w1r, eye).reshape(L_CMP, 2, KV_W // 2, KV_W // 2)
    pe_b = jnp.broadcast_to(pe.transpose(1, 0, 2)[:, :, None, :], (L_CMP, 2, B_KV_HEADS, B_HD))
    if not packed:
        w2bd = jnp.einsum("kmd,kK,hH->khmKHd", w2, eye, eye).reshape(KV_W, KV_W)
        return pe_b.reshape(L_CMP, KV_W), w1bd.astype(BF16), w2bd.astype(BF16)
    half = KV_W // 2
    zero = jnp.zeros((half, KV_W), F32)
    k_blk = jnp.concatenate([w2[0], jnp.zeros_like(w2[0])], axis=1)
    v_blk = jnp.concatenate([w2[1], w2[1]], axis=1)
    w2k = jnp.concatenate([jnp.einsum("ml,hH->hmHl", k_blk, eye).reshape(half, KV_W), zero], axis=0)
    w2v = jnp.concatenate([zero, jnp.einsum("ml,hH->hmHl", v_blk, eye).reshape(half, KV_W)], axis=0)
    return pe_b.reshape(L_CMP, KV_W), w1bd.astype(BF16), (w2k.astype(BF16), w2v.astype(BF16))


def _chunk_preacts(load, n, pe_ref, w1_ref):
    half = KV_W // 2
    p0 = [jnp.zeros((n, half), F32) for _ in range(2)]
    p1 = [jnp.zeros((n, half), F32) for _ in range(2)]
    for i in range(D_CMP):
        for kv in range(2):
            x = load(i, kv)
            ls = slice(kv * half, (kv + 1) * half)
            p0[kv] = p0[kv] + _dot((x + pe_ref[i:i + 1, ls]).astype(BF16), w1_ref[i, kv])
            p1[kv] = p1[kv] + _dot((x + pe_ref[D_CMP + i:D_CMP + i + 1, ls]).astype(BF16), w1_ref[D_CMP + i, kv])
    return jnp.concatenate(p0, axis=1), jnp.concatenate(p1, axis=1)


def _with_position(k, pos, lane):
    col = lane & 127
    return jnp.where(col == B_HD, (pos >> 7).astype(F32), jnp.where(col == B_HD + 1, (pos & 127).astype(F32), k))


def _compress_kernel(kc_ref, pe_ref, w1_ref, w2k_ref, w2v_ref, ck_ref, cv_ref):
    n = ck_ref.shape[1]
    p0, p1 = _chunk_preacts(lambda i, kv: kc_ref[0, pl.ds(2 * i + kv, n, stride=2 * D_CMP), :], n, pe_ref, w1_ref)
    pre = p0 + pltpu.roll(p1, n - 1, axis=0)
    hid = (pre * jax.nn.sigmoid(pre)).astype(BF16)
    cpos = _iota((n, 1), 0) * D_CMP + (L_CMP - 1)
    ck_ref[0] = _with_position(_dot(hid, w2k_ref[...]), cpos, _iota((n, KV_W), 1)).astype(BF16)
    cv_ref[0] = _dot(hid, w2v_ref[...]).astype(BF16)


def _compress_prompt(kc, pe_b, w1bd, w2kv):
    b, s2, _ = kc.shape
    n = s2 // (2 * D_CMP)
    full = lambda a: pl.BlockSpec(a.shape, lambda i: (0,) * a.ndim)
    out = pl.BlockSpec((1, n, KV_W), lambda i: (i, 0, 0))
    return pl.pallas_call(
        _compress_kernel,
        grid=(b,),
        in_specs=[pl.BlockSpec((1, s2, KV_W // 2), lambda i: (i, 0, 0)), full(pe_b), full(w1bd),
                  full(w2kv[0]), full(w2kv[1])],
        out_specs=[out, out],
        out_shape=[jax.ShapeDtypeStruct((b, n, KV_W), BF16)] * 2,
        compiler_params=_cparams(("parallel",), 48),
        name="compress_prompt",
    )(kc, pe_b, w1bd, *w2kv)


MASK_BIAS = 2.0 ** 100

def _top_blocks_t(score_t, n_sel):
    nblk = score_t.shape[0]
    blk = _iota(score_t.shape, 0).astype(F32)

    def body(_, carry):
        sc, sel = carry
        m = jnp.max(sc, axis=0, keepdims=True)
        idx = jnp.min(jnp.where(sc == m, blk, float(nblk)), axis=0, keepdims=True)
        pick = blk == idx
        return jnp.where(pick, NEG_INF, sc), jnp.where(pick, 1.0, sel)

    _, sel = lax.fori_loop(0, n_sel, body, (score_t, jnp.zeros_like(score_t)), unroll=True)
    return sel


def _nsa_prompt_kernel(sl_ref, q_ref, g_ref, ck_ref, cv_ref, sk_ref, sv_ref, wk_ref, wv_ref, cover_ref, o_ref,
                       m_ref, l_ref, acc_ref, sa_ref, sb_ref, *, tq, tk):
    kvh = pl.program_id(1)
    q0 = pl.program_id(2) * tq
    rows = B_GROUP * tq
    lane = _iota((tq, 128), 1)
    lo = lane < B_HD
    qf = q_ref[0].astype(F32)
    left = []
    for g in range(B_GROUP):
        pair = qf[:, 128 * (g // 2):128 * (g // 2) + 128]
        if g % 2:
            pair = pltpu.roll(pair, B_HD, axis=1)
        slope = sl_ref[kvh * B_GROUP + g]
        pos_cols = jnp.where(lane == B_HD, slope * 128.0, jnp.where(lane == B_HD + 1, slope, 0.0))
        left.append(jnp.where(lo, pair, pos_cols).astype(BF16))
    ql = jnp.concatenate(left, axis=0)
    qpos = q0 + (_iota((rows, 1), 0) & (tq - 1))

    nc = ck_ref.shape[1]
    cpos = _iota((1, nc), 1) * D_CMP + (L_CMP - 1)
    span = WINDOW + tq
    start = pl.multiple_of(jnp.maximum(q0 - WINDOW, 0), 128)
    wpos = start + _iota((1, span), 1)
    s_cmp = _dot_nt(ql, ck_ref[0])
    s_win = _dot_nt(ql, wk_ref[0, 0, pl.ds(start, span), :])
    p = _masked_softmax(s_cmp, cpos <= qpos)
    pw = _masked_softmax(s_win, (wpos <= qpos) & (wpos > qpos - WINDOW))
    o_cmp = _dot(p.astype(BF16), cv_ref[0])
    o_win = _dot(pw.astype(BF16), wv_ref[0, 0, pl.ds(start, span), :])
    imp = p[0:tq]
    for g in range(1, B_GROUP):
        imp = imp + p[g * tq:(g + 1) * tq]

    nblk = cover_ref.shape[1]
    jb = _iota((1, nblk), 1)
    cur = (q0 + _iota((tq, 1), 0)) >> SLC_SHIFT
    score = _dot3_lhs(imp, cover_ref[...])
    score = jnp.where((jb == 0) | (jb == cur) | (jb == cur - 1), SEL_FORCE, score)
    score = jnp.where(jb > cur, NEG_INF, score)
    sel = _top_blocks_t(score.T, min(N_SEL, nblk)).T

    bias = ((sel - 1.0) * MASK_BIAS).astype(BF16)
    qs = jnp.concatenate([ql, jnp.concatenate([bias] * B_GROUP, axis=0)], axis=1)
    m_ref[...] = jnp.full_like(m_ref, NEG_INF)
    l_ref[...] = jnp.zeros_like(l_ref)
    acc_ref[...] = jnp.zeros_like(acc_ref)

    def scores(kt):
        return _dot_nt(qs, sk_ref[0, 0, pl.ds(pl.multiple_of(kt * tk, tk), tk), :])

    def update(s, kt, causal):
        k0 = pl.multiple_of(kt * tk, tk)
        if causal:
            s = jnp.where(k0 + _iota((1, tk), 1) <= qpos, s, NEG_INF)
        m_old = m_ref[...]
        m_new = jnp.maximum(m_old, jnp.max(s, axis=-1, keepdims=True))
        alpha = jnp.exp(m_old - m_new)
        e = jnp.exp(s - m_new)
        l_ref[...] = alpha * l_ref[...] + jnp.sum(e, axis=-1, keepdims=True)
        acc_ref[...] = alpha * acc_ref[...] + _dot(e.astype(BF16), sv_ref[0, 0, pl.ds(k0, tk), :])
        m_ref[...] = m_new

    n_full = q0 // tk

    sa_ref[...] = scores(0)

    def tile_pair(i, carry):
        sb_ref[...] = scores(2 * i + 1)
        update(sa_ref[...], 2 * i, False)
        sa_ref[...] = scores(2 * i + 2)
        update(sb_ref[...], 2 * i + 1, False)
        return carry

    lax.fori_loop(0, n_full // 2, tile_pair, 0)

    @pl.when(n_full % 2 == 1)
    def _():
        sb_ref[...] = scores(n_full)
        update(sa_ref[...], n_full - 1, False)
        update(sb_ref[...], n_full, True)

    @pl.when(n_full % 2 == 0)
    def _():
        update(sa_ref[...], n_full, True)

    o_slc = acc_ref[...] * (1.0 / l_ref[...])

    gate = jax.nn.sigmoid(g_ref[0])
    outs = []
    for g in range(B_GROUP):
        rs = slice(g * tq, (g + 1) * tq)
        outs.append(gate[:, 3 * g:3 * g + 1] * o_cmp[rs] + gate[:, 3 * g + 1:3 * g + 2] * o_slc[rs]
                    + gate[:, 3 * g + 2:3 * g + 3] * o_win[rs])
    for j in range(B_GROUP // 2):
        o_ref[0, :, 128 * j:128 * j + 128] = jnp.where(lo, outs[2 * j], outs[2 * j + 1]).astype(BF16)


def _nsa_prompt(slopes, qb, gb, ck, cv, sk, sv, wk, wv, cover, tq=128, tk=512):
    b, s, _ = qb.shape
    half = QB_W // B_KV_HEADS
    rows = B_GROUP * tq
    assert cover.shape[1] == 128 and s % tk == 0 and tk % tq == 0
    seq = lambda a: pl.BlockSpec((1, 1) + a.shape[2:], lambda i, h, t, sl: (h, i, 0, 0))
    grid_spec = pltpu.PrefetchScalarGridSpec(
        num_scalar_prefetch=1,
        grid=(b, B_KV_HEADS, s // tq),
        in_specs=[
            pl.BlockSpec((1, tq, half), lambda i, h, t, sl: (i, t, h)),
            pl.BlockSpec((1, tq, GATE_PAD), lambda i, h, t, sl: (i, t, h)),
            pl.BlockSpec((1, ck.shape[1], 128), lambda i, h, t, sl: (i, 0, h)),
            pl.BlockSpec((1, cv.shape[1], 128), lambda i, h, t, sl: (i, 0, h)),
            seq(sk), seq(sv), seq(wk), seq(wv),
            pl.BlockSpec(cover.shape, lambda i, h, t, sl: (0, 0)),
        ],
        out_specs=pl.BlockSpec((1, tq, half), lambda i, h, t, sl: (i, t, h)),
        scratch_shapes=[pltpu.VMEM((rows, 1), F32), pltpu.VMEM((rows, 1), F32), pltpu.VMEM((rows, 128), F32),
                        pltpu.VMEM((rows, tk), F32), pltpu.VMEM((rows, tk), F32)],
    )
    return pl.pallas_call(
        functools.partial(_nsa_prompt_kernel, tq=tq, tk=tk),
        grid_spec=grid_spec,
        out_shape=jax.ShapeDtypeStruct((b, s, QB_W), BF16),
        compiler_params=_cparams(("parallel", "parallel", "arbitrary"), 56),
        name="nsa_prompt",
    )(slopes, qb, gb, ck, cv, sk, sv, wk, wv, cover)


def _cover_matrix(n_cmp, n_blk, n_pad):
    ci = jnp.arange(n_cmp)[:, None] * D_CMP
    sj = jnp.arange(n_pad)[None, :] * L_SLC
    cov = (ci < sj + L_SLC) & (ci + L_CMP > sj) & (jnp.arange(n_pad)[None, :] < n_blk)
    return cov.astype(BF16)


PAGES_PER_STEP = 64
PAGE_GROUPS = 2


def _cmp_pages_kernel(pt_ref, *refs):
    pages = refs[:PAGES_PER_STEP]
    w_ref, out_ref = refs[PAGES_PER_STEP:PAGES_PER_STEP + 2]
    row_refs = refs[PAGES_PER_STEP + 2:]
    per_group = PAGES_PER_STEP // PAGE_GROUPS
    n = per_group * PAGE_SIZE // D_CMP
    half = KV_W // 2
    for grp, rows_ref in enumerate(row_refs):
        for k, pg in enumerate(pages[grp * per_group:(grp + 1) * per_group]):
            for kv in range(2):
                rows_ref[kv, k * PAGE_SIZE:(k + 1) * PAGE_SIZE, :] = pg[0, kv * half:(kv + 1) * half, :].T
        x = jnp.concatenate([rows_ref[kv, pl.ds(i, n, stride=D_CMP), :].astype(BF16)
                             for i in range(D_CMP) for kv in range(2)], axis=1)
        out_ref[0, grp * n:(grp + 1) * n, :] = _dot(x, w_ref[...])


def _pages_weight(w1bd):
    zero = jnp.zeros((D_CMP,) + w1bd.shape[2:], w1bd.dtype)

    def diag(a):
        return jnp.concatenate([jnp.concatenate([a[:, 0], zero], axis=2),
                                jnp.concatenate([zero, a[:, 1]], axis=2)], axis=1)

    w = jnp.concatenate([diag(w1bd[:D_CMP]), diag(w1bd[D_CMP:])], axis=2)
    return w.reshape(D_CMP * KV_W, 2 * KV_W)


def _feature_major_pages(cache):
    return cache.transpose(0, 2, 3, 4, 5, 1).reshape(cache.shape[0], -1, cache.shape[1])


def _cmp_pages(page_table, cache, li, w1bd):
    bd, n_pages = page_table.shape
    pool = _feature_major_pages(cache)
    steps = n_pages // PAGES_PER_STEP
    rows = PAGES_PER_STEP * PAGE_SIZE // D_CMP
    w = _pages_weight(w1bd)

    def page_spec(k):
        return pl.BlockSpec((1, KV_W, PAGE_SIZE), lambda i, j, pt: (pt[i, j * PAGES_PER_STEP + k], li, 0))

    grid_spec = pltpu.PrefetchScalarGridSpec(
        num_scalar_prefetch=1,
        grid=(bd, steps),
        in_specs=[page_spec(k) for k in range(PAGES_PER_STEP)] + [pl.BlockSpec(w.shape, lambda i, j, pt: (0, 0))],
        out_specs=pl.BlockSpec((1, rows, 2 * KV_W), lambda i, j, pt: (i, j, 0)),
        scratch_shapes=[pltpu.VMEM((2, PAGES_PER_STEP // PAGE_GROUPS * PAGE_SIZE, KV_W // 2), F32)] * PAGE_GROUPS,
    )
    return pl.pallas_call(
        _cmp_pages_kernel,
        grid_spec=grid_spec,
        out_shape=jax.ShapeDtypeStruct((bd, steps * rows, 2 * KV_W), F32),
        compiler_params=_cparams(("parallel", "arbitrary"), 48),
        name="cmp_pages",
    )(page_table, *([pool] * PAGES_PER_STEP), w)


def _nsa_sample_select_kernel(sl_ref, pre_ref, new_ref, q_ref, pt_ref, pe_ref, w1_ref, w2_ref, cover_ref,
                              idx_ref, phys_ref, ocmp_ref, tail_ref, *, past):
    half = KV_W // 2

    def row_act(i, r):
        r8 = jnp.broadcast_to(r, (8, KV_W)).astype(BF16)
        return jnp.concatenate([_dot(r8[:, kv * half:(kv + 1) * half], w1_ref[i, kv]) for kv in range(2)], axis=1)

    @pl.when(pl.program_id(0) == 0)
    def _():
        t = jnp.zeros((8, KV_W), F32)
        for i in range(L_CMP):
            t = t + row_act(i, pe_ref[i:i + 1])
        tail_ref[...] = t

    n = pre_ref.shape[1]
    p1 = pltpu.roll(pre_ref[0, :, KV_W:2 * KV_W], n - 1, axis=0)
    p1 = jnp.where(_iota((n, 1), 0) == n - 1, row_act(D_CMP, new_ref[0])[0:1], p1)
    pre = pre_ref[0, :, 0:KV_W] + p1 + tail_ref[0:1]
    comp = _dot((pre * jax.nn.sigmoid(pre)).astype(BF16), w2_ref[...]).astype(BF16)

    qpos = past
    cpos = _iota((1, n), 1) * D_CMP + (L_CMP - 1)
    cmask = cpos <= qpos
    cdist = (qpos - cpos).astype(F32)
    row8 = _iota((8, 1), 0)
    imp = jnp.zeros((8, n), F32)
    for kvh in range(B_KV_HEADS):
        slope = jnp.zeros((8, 1), F32)
        for g in range(B_GROUP):
            slope = jnp.where(row8 == g, sl_ref[kvh * B_GROUP + g], slope)
        p = _masked_softmax(_dot_nt(q_ref[0, kvh], comp) - slope * cdist, cmask)
        p = jnp.where(row8 < B_GROUP, p, 0.0)
        ocmp_ref[0, kvh] = _dot(p.astype(BF16), comp)
        imp = jnp.where(row8 == kvh, jnp.sum(p, axis=0, keepdims=True), imp)

    nblk = cover_ref.shape[1]
    jb = _iota((1, nblk), 1)
    cur = qpos // L_SLC
    score = _dot3_lhs(imp, cover_ref[...])
    score = jnp.where((jb == 0) | (jb == cur) | (jb == cur - 1), SEL_FORCE, score)
    score = jnp.where(jb > cur, NEG_INF, score)
    lane = _iota((1, 128), 1)
    jbf = jb.astype(F32)
    pages = pt_ref[0].astype(F32)
    bpp_shift = (PAGE_SIZE // L_SLC).bit_length() - 1

    def body(it, carry):
        sc, idx, phys = carry
        m = jnp.max(sc, axis=-1, keepdims=True)
        j = jnp.min(jnp.where(sc == m, jbf, float(nblk)), axis=-1, keepdims=True)
        logical = jnp.minimum(j.astype(jnp.int32), past // L_SLC - 1) >> bpp_shift
        page = jnp.sum(jnp.where(lane == logical, pages, 0.0), axis=-1, keepdims=True)
        return jnp.where(jbf == j, NEG_INF, sc), jnp.where(lane == it, j, idx), jnp.where(lane == it, page, phys)

    zero = jnp.zeros((8, 128), F32)
    _, idx, phys = lax.fori_loop(0, N_SEL, body, (score, zero, zero))
    idx_ref[0] = idx.astype(jnp.int32)
    phys_ref[0] = phys.astype(jnp.int32)


def _nsa_sample_select(slopes, pre, cmp_new, q8, page_table, pe_b, w1bd, w2bd, cover, past):
    bd = pre.shape[0]
    assert page_table.shape[1] == 128
    full = lambda a: pl.BlockSpec(a.shape, lambda i, sl: (0,) * a.ndim)
    picks = pl.BlockSpec((1, 8, 128), lambda i, sl: (i, 0, 0))
    grid_spec = pltpu.PrefetchScalarGridSpec(
        num_scalar_prefetch=1,
        grid=(bd,),
        in_specs=[
            pl.BlockSpec((1,) + pre.shape[1:], lambda i, sl: (i, 0, 0)),
            pl.BlockSpec((1, 1, KV_W), lambda i, sl: (i, 0, 0)),
            pl.BlockSpec((1, B_KV_HEADS, 8, KV_W), lambda i, sl: (i, 0, 0, 0)),
            pl.BlockSpec((1, 1, 128), lambda i, sl: (i, 0, 0)),
            full(pe_b), full(w1bd), full(w2bd), full(cover),
        ],
        out_specs=[picks, picks, pl.BlockSpec((1, B_KV_HEADS, 8, KV_W), lambda i, sl: (i, 0, 0, 0))],
        scratch_shapes=[pltpu.VMEM((8, KV_W), F32)],
    )
    return pl.pallas_call(
        functools.partial(_nsa_sample_select_kernel, past=past),
        grid_spec=grid_spec,
        out_shape=[jax.ShapeDtypeStruct((bd, 8, 128), jnp.int32)] * 2
        + [jax.ShapeDtypeStruct((bd, B_KV_HEADS, 8, KV_W), F32)],
        compiler_params=_cparams(("arbitrary",), 48),
        name="nsa_sample_select",
    )(slopes, pre, cmp_new, q8, page_table.reshape(bd, 1, 128), pe_b, w1bd, w2bd, cover)


def _nsa_sample_attend_kernel(pt_ref, ix_ref, sl_ref, *refs, past, nb_past):
    n_blocks = B_KV_HEADS * N_SEL
    pages = refs[:n_blocks]
    q_ref, slc_new_ref, win_new_ref, win_ref, g_ref, ocmp_ref, o_ref = refs[n_blocks:]
    b = pl.program_id(0)
    bpp = PAGE_SIZE // L_SLC
    half = KV_W // 2

    def token_rows(ref, c0):
        return jnp.concatenate([ref[0, kv * half:(kv + 1) * half, c0:c0 + 128].T for kv in range(2)], axis=1)

    row8 = _iota((8, 1), 0)
    first = _iota((L_SLC, 1), 0) == 0
    new_blk = jnp.where(first, slc_new_ref[0], 0.0).astype(BF16)
    wb = win_ref.shape[2]
    win = jnp.concatenate([token_rows(win_ref, c0) for c0 in range(0, wb, 128)], axis=0).astype(BF16)
    win_new = jnp.where(_iota((8, 1), 0) == 0, win_new_ref[0], 0.0).astype(BF16)
    for kvh in range(B_KV_HEADS):
        q = q_ref[0, kvh]
        slope = jnp.zeros((8, 1), F32)
        for g in range(B_GROUP):
            slope = jnp.where(row8 == g, sl_ref[kvh * B_GROUP + g], slope)
        m = jnp.full((8, 1), NEG_INF, F32)
        l = jnp.zeros((8, 1), F32)
        acc = jnp.zeros((8, KV_W), F32)
        for n in range(N_SEL):
            j = ix_ref[b, kvh * N_SEL + n]
            pg = token_rows(pages[kvh * N_SEL + n], 0).astype(BF16)
            blk = pg[0:L_SLC]
            for sub in range(1, bpp):
                blk = jnp.where(jnp.minimum(j, nb_past - 1) % bpp == sub, pg[sub * L_SLC:(sub + 1) * L_SLC], blk)
            blk = jnp.where(j >= nb_past, new_blk, blk)
            kpos = j * L_SLC + _iota((1, L_SLC), 1)
            s = _dot_nt(q, blk) - slope * (past - kpos).astype(F32)
            s = jnp.where(kpos <= past, s, NEG_INF)
            m_new = jnp.maximum(m, jnp.max(s, axis=-1, keepdims=True))
            m_safe = jnp.where(m_new > NEG_INF, m_new, 0.0)
            alpha = jnp.exp(m - m_safe)
            p = jnp.exp(s - m_safe)
            l = alpha * l + jnp.sum(p, axis=-1, keepdims=True)
            acc = alpha * acc + _dot(p.astype(BF16), blk)
            m = m_new
        o_slc = acc * (1.0 / jnp.where(l > 0, l, 1.0))
        wpos = past - wb + _iota((1, wb), 1)
        s1 = _dot_nt(q, win) - slope * (past - wpos).astype(F32)
        s1 = jnp.where(wpos > past - WINDOW, s1, NEG_INF)
        s2 = jnp.where(_iota((1, 8), 1) == 0, _dot_nt(q, win_new), NEG_INF)
        mw = jnp.maximum(jnp.max(s1, axis=-1, keepdims=True), jnp.max(s2, axis=-1, keepdims=True))
        e1 = jnp.exp(s1 - mw)
        e2 = jnp.exp(s2 - mw)
        dw = jnp.sum(e1, axis=-1, keepdims=True) + jnp.sum(e2, axis=-1, keepdims=True)
        o_win = (_dot(e1.astype(BF16), win) + _dot(e2.astype(BF16), win_new)) * (1.0 / dw)
        gate = jax.nn.sigmoid(g_ref[0, kvh])
        o_ref[0, kvh] = gate[:, 0:1] * ocmp_ref[0, kvh] + gate[:, 1:2] * o_slc + gate[:, 2:3] * o_win


def _nsa_sample_attend(phys, idx, slopes, cache, li, q8, slc_new, win_new, win_state, gate8, o_cmp, past):
    bd = q8.shape[0]
    bpp = PAGE_SIZE // L_SLC
    nb_past = past // L_SLC
    n_blocks = B_KV_HEADS * N_SEL
    pool = _feature_major_pages(cache)

    def blk_spec(k):
        return pl.BlockSpec((1, KV_W, PAGE_SIZE), lambda i, ph, ix, sl: (ph[i, k], li, 0))

    row = lambda: pl.BlockSpec((1, 1, KV_W), lambda i, pt, ix, sl: (i, 0, 0))
    per_head = lambda w: pl.BlockSpec((1, B_KV_HEADS, 8, w), lambda i, pt, ix, sl: (i, 0, 0, 0))
    grid_spec = pltpu.PrefetchScalarGridSpec(
        num_scalar_prefetch=3,
        grid=(bd,),
        in_specs=[blk_spec(k) for k in range(n_blocks)] + [
            per_head(KV_W), row(), row(),
            pl.BlockSpec((1,) + win_state.shape[1:], lambda i, pt, ix, sl: (i, 0, 0)),
            per_head(128), per_head(KV_W)],
        out_specs=per_head(KV_W),
    )
    return pl.pallas_call(
        functools.partial(_nsa_sample_attend_kernel, past=past, nb_past=nb_past),
        grid_spec=grid_spec,
        out_shape=jax.ShapeDtypeStruct((bd, B_KV_HEADS, 8, KV_W), F32),
        compiler_params=_cparams(("arbitrary",), 48),
        name="nsa_sample_attend",
    )(phys, idx, slopes, *([pool] * n_blocks), q8, slc_new, win_new, win_state, gate8, o_cmp)


def _outproj_kernel(x_ref, a_ref, b_ref, w_ref, y_ref):
    na = a_ref.shape[1]
    y_ref[...] = x_ref[...] + _dot(a_ref[...], w_ref[0:na]) + _dot(b_ref[...], w_ref[na:])


def _outproj(x, oa, ob, w, tm):
    m = x.shape[0]
    row = lambda n: pl.BlockSpec((tm, n), lambda i: (i, 0))
    return pl.pallas_call(
        _outproj_kernel,
        grid=(m // tm,),
        in_specs=[row(D_MODEL), row(oa.shape[1]), row(ob.shape[1]), pl.BlockSpec(w.shape, lambda i: (0, 0))],
        out_specs=row(D_MODEL),
        out_shape=jax.ShapeDtypeStruct((m, D_MODEL), F32),
        compiler_params=_cparams(("parallel",), 32),
        name="outproj",
    )(x, oa, ob, w)


def _mixer_prompt_kernel(*refs, groups, convs, gated_ffn, final, tm, tf, f):
    x_ref, nw_ref, up_ref, cw_ref, down_ref = refs[0:5]
    pos = 5
    fw_ref = refs[pos] if final else None
    pos += int(final)
    y_ref, tail_ref = refs[pos], refs[pos + 1]
    act_ref, ubuf_ref, carry_ref = refs[pos + 2:]

    @pl.when(pl.program_id(1) == 0)
    def _():
        carry_ref[...] = jnp.zeros_like(carry_ref)

    x = x_ref[0]
    xn = _rms(x, nw_ref[...]).astype(BF16)
    for j in range(f // tf):
        u = [_dot(xn, up_ref[:, g * f + j * tf:g * f + (j + 1) * tf]) for g in range(groups)]
        conv_in = [u[0], u[1]] if gated_ffn else [u[1] * u[0]]
        conv_out = []
        for c, z in enumerate(conv_in):
            cols = slice(c * f + j * tf, c * f + (j + 1) * tf)
            slot = (j % 2) * convs + c
            ubuf_ref[slot, 0:8] = carry_ref[:, cols]
            ubuf_ref[slot, 8:tm + 8] = z
            w = cw_ref[:, cols]
            conv_out.append(w[0:1] * ubuf_ref[slot, pl.ds(6, tm), :] + w[1:2] * ubuf_ref[slot, pl.ds(7, tm), :]
                            + w[2:3] * z)
            last = z[tm - 8:tm]
            carry_ref[:, cols] = last
            tail_ref[0, 0, :, cols] = last
        if gated_ffn:
            act = conv_out[0] * jax.nn.sigmoid(conv_out[0]) * conv_out[1]
        else:
            act = u[2] * conv_out[0]
        act_ref[:, j * tf:(j + 1) * tf] = act.astype(BF16)
    y = x + _dot(act_ref[...], down_ref[...])
    y_ref[0] = _rms(y, fw_ref[...]) if final else y


def _mixer_prompt(x, nw, w_up, conv_w, w_down, gated_ffn, final_w=None, tm=512, tf=256):
    b, s, d = x.shape
    f = w_down.shape[0]
    groups = w_up.shape[1] // f
    convs = conv_w.shape[1] // f
    final = final_w is not None
    const = lambda a: pl.BlockSpec(a.shape, lambda i, t: (0,) * a.ndim, pipeline_mode=pl.Buffered(1))
    in_specs = [pl.BlockSpec((1, tm, d), lambda i, t: (i, t, 0)), const(nw), const(w_up), const(conv_w),
                const(w_down)]
    args = [x, nw, w_up, conv_w, w_down]
    if final:
        in_specs.append(const(final_w))
        args.append(final_w)
    y, tail = pl.pallas_call(
        functools.partial(_mixer_prompt_kernel, groups=groups, convs=convs, gated_ffn=gated_ffn,
                          final=final, tm=tm, tf=tf, f=f),
        grid=(b, s // tm),
        in_specs=in_specs,
        out_specs=[pl.BlockSpec((1, tm, d), lambda i, t: (i, t, 0)),
                   pl.BlockSpec((1, 1, 8, convs * f), lambda i, t: (i, t, 0, 0))],
        out_shape=[jax.ShapeDtypeStruct((b, s, d), F32), jax.ShapeDtypeStruct((b, s // tm, 8, convs * f), F32)],
        scratch_shapes=[pltpu.VMEM((tm, f), BF16), pltpu.VMEM((2 * convs, tm + 8, tf), F32),
                        pltpu.VMEM((8, convs * f), F32)],
        compiler_params=_cparams(("parallel", "arbitrary"), 56),
        name="ffn_prompt" if gated_ffn else "sconv_prompt",
    )(*args)
    return y, tail[:, -1, 8 - (CONV_W - 1):, :]


def _mixer_sample_kernel(*refs, groups, convs, gated_ffn, final):
    x_ref, nw_ref = refs[0:2]
    up = refs[2:2 + groups]
    cw = refs[2 + groups:2 + groups + convs]
    st = refs[2 + groups + convs:2 + groups + 2 * convs]
    down_ref = refs[2 + groups + 2 * convs]
    pos = 3 + groups + 2 * convs
    fw_ref = refs[pos] if final else None
    pos += int(final)
    y_ref = refs[pos]
    news = refs[pos + 1:pos + 1 + convs]
    xn_ref, acc_ref = refs[pos + 1 + convs:]
    j = pl.program_id(0)

    @pl.when(j == 0)
    def _():
        xn_ref[...] = _rms(x_ref[...], nw_ref[...]).astype(BF16)
        acc_ref[...] = jnp.zeros_like(acc_ref)

    xn = xn_ref[...]
    u = [_dot(xn, w[...]) for w in up]
    conv_in = [u[0], u[1]] if gated_ffn else [u[1] * u[0]]
    conv_out = []
    for c, z in enumerate(conv_in):
        w = cw[c][...]
        conv_out.append(w[0:1] * st[c][:, 0, :] + w[1:2] * st[c][:, 1, :] + w[2:3] * z)
        news[c][...] = z
    if gated_ffn:
        act = conv_out[0] * jax.nn.sigmoid(conv_out[0]) * conv_out[1]
    else:
        act = u[2] * conv_out[0]
    acc_ref[...] += _dot(act.astype(BF16), down_ref[...])

    @pl.when(j == pl.num_programs(0) - 1)
    def _():
        y = x_ref[...] + acc_ref[...]
        y_ref[...] = _rms(y, fw_ref[...]) if final else y


def _mixer_sample(x, nw, w_up, conv_w, w_down, state, gated_ffn, final_w=None, tf=256):
    bd, d = x.shape
    f = w_down.shape[0]
    groups = w_up.shape[1] // f
    convs = conv_w.shape[1] // f
    nf = f // tf
    final = final_w is not None
    in_specs = [pl.BlockSpec((bd, d), lambda j: (0, 0)), pl.BlockSpec(nw.shape, lambda j: (0, 0))]
    in_specs += [pl.BlockSpec((d, tf), lambda j, g=g: (0, g * nf + j)) for g in range(groups)]
    in_specs += [pl.BlockSpec((CONV_W, tf), lambda j, c=c: (0, c * nf + j)) for c in range(convs)]
    in_specs += [pl.BlockSpec((bd, CONV_W - 1, tf), lambda j, c=c: (0, 0, c * nf + j)) for c in range(convs)]
    in_specs += [pl.BlockSpec((tf, d), lambda j: (j, 0))]
    args = [x, nw] + [w_up] * groups + [conv_w] * convs + [state] * convs + [w_down]
    if final:
        in_specs.append(pl.BlockSpec(final_w.shape, lambda j: (0, 0)))
        args.append(final_w)
    out_specs = [pl.BlockSpec((bd, d), lambda j: (0, 0))]
    out_specs += [pl.BlockSpec((bd, tf), lambda j: (0, j))] * convs
    out_shape = [jax.ShapeDtypeStruct((bd, d), F32)] + [jax.ShapeDtypeStruct((bd, f), F32)] * convs
    res = pl.pallas_call(
        functools.partial(_mixer_sample_kernel, groups=groups, convs=convs, gated_ffn=gated_ffn, final=final),
        grid=(nf,),
        in_specs=in_specs,
        out_specs=out_specs,
        out_shape=out_shape,
        scratch_shapes=[pltpu.VMEM((bd, d), BF16), pltpu.VMEM((bd, d), F32)],
        compiler_params=_cparams(("arbitrary",), 32),
        name="ffn_sample" if gated_ffn else "sconv_sample",
    )(*args)
    new = jnp.concatenate(res[1:], axis=-1)
    return res[0], jnp.concatenate([state[:, 1:], new[:, None, :]], axis=1)


def _pad_in_weight(w):
    c = HG_W + QB_W + 3 * KV_W
    per = 3 * B_GROUP
    gates = [jnp.pad(w[:, c + h * per:c + (h + 1) * per], ((0, 0), (0, GATE_PAD - per)))
             for h in range(B_KV_HEADS)]
    return jnp.concatenate([w[:, :c]] + gates, axis=1).astype(BF16)


def _per_head_rows(z, width):
    bd = z.shape[0]
    z = z.reshape(bd, B_KV_HEADS, B_GROUP, -1)
    return jnp.pad(z, ((0, 0), (0, 0), (0, 8 - B_GROUP), (0, width - z.shape[-1])))


def kernel(x_prompt, x_sample, cache_cmp_kv, cache_slc_kv, state_win_kv, state_hgrn, state_sconv, state_ffn,
           page_table, norm_mix, norm_ffn, norm_final, ab_w_in, ab_w_out, hgrn_lb_logits, hgrn_norm,
           cmp_pe, cmp_w1, cmp_w2, c_w_in, c_conv, c_w_out, ffn_up, ffn_conv, ffn_down):
    b, s, d = x_prompt.shape
    bd, t, _ = x_sample.shape
    assert t == 1 and d == D_MODEL
    depth = norm_mix.shape[0]
    n_a = ab_w_in.shape[0]
    wb = state_win_kv.shape[2]
    n_pages = page_table.shape[1]
    past = n_pages * PAGE_SIZE
    assert wb == WINDOW and past % (PAGES_PER_STEP * PAGE_SIZE) == 0 and s % 1024 == 0
    hh = jnp.arange(1, B_HEADS + 1, dtype=F32)
    slopes = 2.0 ** (-8.0 * hh / B_HEADS)
    lb_logits = hgrn_lb_logits.astype(F32)
    row2 = lambda v: v.reshape(1, -1).astype(F32)

    xp = x_prompt.reshape(b * s, d)
    xs = x_sample.reshape(bd, d)
    outs_p = {k: [] for k in ("cmp", "slc", "win", "hg", "sc", "ff")}
    outs_s = {k: [] for k in ("cmp", "slc", "win", "hg", "sc", "ff")}
    y_prompt = y_sample = None
    for l in range(depth):
        last = l == depth - 1
        if l % 2 == 0:
            la = l // 2
            w_in = _pad_in_weight(ab_w_in[la])
            w_out = ab_w_out[la].astype(BF16)
            nw = row2(norm_mix[l])
            hnw = row2(hgrn_norm[la])
            pe_b, w1bd, w2_packed = _cmp_weights(cmp_pe[la], cmp_w1[la], cmp_w2[la], packed=True)
            _, _, w2_plain = _cmp_weights(cmp_pe[la], cmp_w1[la], cmp_w2[la], packed=False)
            hg, qb, kc, ksl, kw, gb, sk, sv, wk, wv = _inproj(xp, nw, w_in, tm=512, seq=s)
            sk, sv, wk, wv = [z.reshape(B_KV_HEADS, b, s, -1) for z in (sk, sv, wk, wv)]
            oa, hst = _hgrn_prompt(hg, lb_logits, hnw, la, b, s)
            ck, cv = _compress_prompt(kc.reshape(b, 2 * s, KV_W // 2), pe_b, w1bd, w2_packed)
            n_cmp = s // D_CMP
            nblk = s // L_SLC
            cover = _cover_matrix(n_cmp, nblk, nblk)
            ob = _nsa_prompt(slopes, qb.reshape(b, s, QB_W), gb.reshape(b, s, -1), ck, cv, sk, sv, wk, wv, cover)
            xp = _outproj(xp, oa, ob.reshape(b * s, QB_W), w_out, tm=512)
            kv6 = lambda z, n, tt: z.reshape(n, tt, 2, B_KV_HEADS, B_HD)
            outs_p["cmp"].append(kv6(kc, b, s))
            outs_p["slc"].append(kv6(ksl, b, s))
            kw6 = kv6(kw, b, s)
            outs_p["win"].append(jnp.pad(kw6, ((0, 0), (max(wb - s, 0), 0), (0, 0), (0, 0), (0, 0)))[:, -wb:])
            outs_p["hg"].append(hst)
            hg, qb, kc, ksl, kw, gb = _inproj(xs, nw, w_in, tm=bd)
            oa, hst = _hgrn_sample(hg, state_hgrn[la], lb_logits, hnw, la)
            pre = _cmp_pages(page_table, cache_cmp_kv, la, w1bd)
            n_cmp = pre.shape[1]
            nblk = past // L_SLC + 1
            cover = _cover_matrix(n_cmp, nblk, -(-nblk // 128) * 128)
            q8 = jnp.stack([jnp.pad(_per_head_rows(qb, B_HD)[:, h], ((0, 0), (0, 0), (h * B_HD, KV_W - (h + 1) * B_HD)))
                            for h in range(B_KV_HEADS)], axis=1)
            idx, phys, o_cmp = _nsa_sample_select(slopes, pre, kc.reshape(bd, 1, KV_W), q8, page_table, pe_b, w1bd,
                                                  w2_plain, cover, past)
            picks = lambda z: z[:, :B_KV_HEADS, :N_SEL].reshape(bd, B_KV_HEADS * N_SEL)
            idx, phys = picks(idx), picks(phys)
            gate8 = _per_head_rows(jnp.concatenate([gb[:, h * GATE_PAD:h * GATE_PAD + 3 * B_GROUP]
                                                    for h in range(B_KV_HEADS)], axis=1), 128)
            win_state = state_win_kv[la].transpose(0, 2, 3, 4, 1).reshape(bd, KV_W, wb)
            o8 = _nsa_sample_attend(phys, idx, slopes, cache_slc_kv, la, q8, ksl.reshape(bd, 1, KV_W),
                                    kw.reshape(bd, 1, KV_W), win_state, gate8, o_cmp, past)
            ob = jnp.concatenate([o8[:, h, :B_GROUP, 2 * B_HD + h * B_HD:2 * B_HD + (h + 1) * B_HD]
                                  .reshape(bd, B_GROUP * B_HD) for h in range(B_KV_HEADS)], axis=1)
            xs = _outproj(xs, oa, ob.astype(BF16), w_out, tm=bd)
            outs_s["cmp"].append(kv6(kc, bd, t))
            outs_s["slc"].append(kv6(ksl, bd, t))
            outs_s["win"].append(jnp.concatenate([state_win_kv[la], kv6(kw, bd, t)], axis=1)[:, t:])
            outs_s["hg"].append(hst)
        else:
            lc = l // 2
            xp3, tail = _mixer_prompt(xp.reshape(b, s, d), row2(norm_mix[l]), c_w_in[lc].astype(BF16),
                                      c_conv[lc], c_w_out[lc].astype(BF16), gated_ffn=False)
            xp = xp3.reshape(b * s, d)
            outs_p["sc"].append(tail)
            xs, new = _mixer_sample(xs, row2(norm_mix[l]), c_w_in[lc].astype(BF16), c_conv[lc],
                                    c_w_out[lc].astype(BF16), state_sconv[lc], gated_ffn=False)
            outs_s["sc"].append(new)
        fw = row2(norm_final) if last else None
        xp3, tail = _mixer_prompt(xp.reshape(b, s, d), row2(norm_ffn[l]), ffn_up[l].astype(BF16), ffn_conv[l],
                                  ffn_down[l].astype(BF16), gated_ffn=True, final_w=fw)
        xp = xp3.reshape(b * s, d)
        outs_p["ff"].append(tail)
        xs, new = _mixer_sample(xs, row2(norm_ffn[l]), ffn_up[l].astype(BF16), ffn_conv[l],
                                ffn_down[l].astype(BF16), state_ffn[l], gated_ffn=True, final_w=fw)
        outs_s["ff"].append(new)
    y_prompt = xp.reshape(b, s, d)
    y_sample = xs.reshape(bd, t, d)
    return (y_prompt, y_sample,
            jnp.stack(outs_p["cmp"], axis=2), jnp.stack(outs_p["slc"], axis=2), jnp.stack(outs_p["win"], axis=0),
            jnp.stack(outs_p["hg"], axis=0), jnp.stack(outs_p["sc"], axis=0), jnp.stack(outs_p["ff"], axis=0),
            jnp.stack(outs_s["cmp"], axis=2), jnp.stack(outs_s["slc"], axis=2), jnp.stack(outs_s["win"], axis=0),
            jnp.stack(outs_s["hg"], axis=0), jnp.stack(outs_s["sc"], axis=0), jnp.stack(outs_s["ff"], axis=0))
```

```python
import functools

import jax
import jax.numpy as jnp
from jax import lax
from jax.experimental import pallas as pl
from jax.experimental.pallas import tpu as pltpu

F32 = jnp.float32
BF16 = jnp.bfloat16
NEG_INF = float("-inf")

D_MODEL = 1024
PAGE_SIZE = 128
A_HEADS = 4
A_DK = 128
A_DV = 128
B_HEADS = 8
B_KV_HEADS = 2
B_GROUP = B_HEADS // B_KV_HEADS
B_HD = 64
L_CMP = 32
D_CMP = 16
CMP_HID = 64
L_SLC = 64
SLC_SHIFT = 6
N_SEL = 16
WINDOW = 512
SEL_FORCE = 1e9
CONV_W = 3
EPS = 1e-6
KV_W = 2 * B_KV_HEADS * B_HD
HG_W = 4 * A_HEADS * A_DK
QB_W = B_HEADS * B_HD
GATE_PAD = 128
IN_PAD = HG_W + QB_W + 3 * KV_W + B_KV_HEADS * GATE_PAD

V7X_VMEM_BYTES = 64 * 2**20


def _cparams(sem, vmem_mb):
    assert vmem_mb * 2**20 < V7X_VMEM_BYTES
    return pltpu.CompilerParams(dimension_semantics=sem, vmem_limit_bytes=vmem_mb * 2**20)


def _dot(a, b):
    return jnp.dot(a, b, preferred_element_type=F32)


def _dot_nt(a, b):
    return lax.dot_general(a, b, (((1,), (1,)), ((), ())), preferred_element_type=F32)


def _dot_tn(a, b):
    return lax.dot_general(a, b, (((0,), (0,)), ((), ())), preferred_element_type=F32)


def _split3(x):
    hi = x.astype(BF16)
    r1 = x - hi.astype(F32)
    mid = r1.astype(BF16)
    lo = (r1 - mid.astype(F32)).astype(BF16)
    return hi, mid, lo


def _dot3_lhs(x, w):
    hi, mid, lo = _split3(x)
    return _dot(hi, w) + _dot(mid, w) + _dot(lo, w)


def _dot3_rhs(w, x):
    hi, mid, lo = _split3(x)
    return _dot(w, hi) + _dot(w, mid) + _dot(w, lo)


def _rms(x, w):
    return x * lax.rsqrt(jnp.mean(x * x, axis=-1, keepdims=True) + EPS) * w


def _masked_softmax(s, mask):
    s = jnp.where(mask, s, NEG_INF)
    m = jnp.max(s, axis=-1, keepdims=True)
    m = jnp.where(m > NEG_INF, m, 0.0)
    e = jnp.exp(s - m)
    d = jnp.sum(e, axis=-1, keepdims=True)
    return e * (1.0 / jnp.where(d > 0, d, 1.0))


def _iota(shape, dim):
    return lax.broadcasted_iota(jnp.int32, shape, dim)


def _top_blocks(score, jb, n_sel):
    big = float(score.shape[-1])
    jbf = jb.astype(F32)

    def body(_, carry):
        sc, sel = carry
        m = jnp.max(sc, axis=-1, keepdims=True)
        idx = jnp.min(jnp.where(sc == m, jbf, big), axis=-1, keepdims=True)
        pick = jbf == idx
        return jnp.where(pick, NEG_INF, sc), jnp.where(pick, 1.0, sel)

    _, sel = lax.fori_loop(0, n_sel, body, (score, jnp.zeros_like(score)))
    return sel


def _attention_rows(kv, pos, with_blocks):
    tm = kv.shape[0]
    lane = _iota((tm, 128), 1)
    lo = lane < B_HD
    pos_cols = jnp.where(lane == B_HD, (pos >> 7).astype(F32),
                         jnp.where(lane == B_HD + 1, (pos & 127).astype(F32), 0.0))
    k_pair, v_pair = kv[:, 0:128], kv[:, 128:256]
    k_swap, v_swap = pltpu.roll(k_pair, B_HD, axis=1), pltpu.roll(v_pair, B_HD, axis=1)
    onehot = ((pos >> SLC_SHIFT) == lane).astype(BF16)
    keys, values = [], []
    for h, (k, v_lo, v_hi) in enumerate(((k_pair, v_pair, v_swap), (k_swap, v_swap, v_pair))):
        key = jnp.where(lo, k, pos_cols).astype(BF16)
        keys.append(jnp.concatenate([key, onehot], axis=1) if with_blocks else key)
        values.append(jnp.where(lo, v_lo, v_hi).astype(BF16))
    return keys, values


def _inproj_kernel(x_ref, nw_ref, w_ref, hg_ref, qb_ref, gb_ref, kc_ref, *refs, seq):
    xn = _rms(x_ref[...], nw_ref[...]).astype(BF16)

    def proj(c0, n):
        return _dot(xn, w_ref[:, c0:c0 + n])

    hg_ref[...] = proj(0, HG_W)
    qb_ref[...] = (proj(HG_W, QB_W) * (B_HD ** -0.5)).astype(BF16)
    c = HG_W + QB_W
    kc = proj(c, KV_W)
    ksl = proj(c + KV_W, KV_W)
    kw = proj(c + 2 * KV_W, KV_W)
    kc_ref[...] = kc
    gb_ref[...] = proj(c + 3 * KV_W, B_KV_HEADS * GATE_PAD)
    if seq is None:
        ksl_ref, kw_ref = refs
        ksl_ref[...] = ksl
        kw_ref[...] = kw
    else:
        kct_ref, kslt_ref, kwt_ref, sk_ref, sv_ref, wk_ref, wv_ref = refs
        kct_ref[0] = kc.T
        kslt_ref[0] = ksl.T
        kwt_ref[0] = kw.T
        tm = x_ref.shape[0]
        pos = (pl.program_id(0) * tm + _iota((tm, 1), 0)) & (seq - 1)
        for kv, k_ref, v_ref, with_blocks in ((ksl, sk_ref, sv_ref, True), (kw, wk_ref, wv_ref, False)):
            keys, values = _attention_rows(kv, pos, with_blocks)
            for h in range(B_KV_HEADS):
                k_ref[h] = keys[h]
                v_ref[h] = values[h]


def _inproj(x, nw, w_pad, tm, seq=None):
    m = x.shape[0]
    row = lambda n: pl.BlockSpec((tm, n), lambda i: (i, 0))
    full = lambda a: pl.BlockSpec(a.shape, lambda i: (0,) * a.ndim)
    widths = (HG_W, QB_W, B_KV_HEADS * GATE_PAD, KV_W)
    dtypes = (F32, BF16, F32, F32)
    out_specs = [row(n) for n in widths]
    out_shape = [jax.ShapeDtypeStruct((m, n), d) for n, d in zip(widths, dtypes)]
    if seq is None:
        out_specs += [row(KV_W)] * 2
        out_shape += [jax.ShapeDtypeStruct((m, KV_W), F32)] * 2
    else:
        assert seq // L_SLC == 128 and seq % tm == 0 and seq & (seq - 1) == 0
        nt = seq // tm
        out_specs += [pl.BlockSpec((1, KV_W, tm), lambda i: (i // nt, 0, i % nt))] * 3
        out_shape += [jax.ShapeDtypeStruct((m // seq, KV_W, seq), F32)] * 3
        for n in (256, 128, 128, 128):
            out_specs.append(pl.BlockSpec((B_KV_HEADS, tm, n), lambda i: (0, i, 0)))
            out_shape.append(jax.ShapeDtypeStruct((B_KV_HEADS, m, n), BF16))
    return pl.pallas_call(
        functools.partial(_inproj_kernel, seq=seq),
        grid=(m // tm,),
        in_specs=[row(D_MODEL), full(nw), full(w_pad)],
        out_specs=out_specs,
        out_shape=out_shape,
        compiler_params=_cparams(("parallel",), 48),
        name="inproj",
    )(x, nw, w_pad)


def _lower_bound(logit_ref, la):
    lg = logit_ref[...]
    e = jnp.exp(lg - jnp.max(lg, axis=0, keepdims=True))
    return jnp.sum(e[:la + 1], axis=0, keepdims=True) / jnp.sum(e, axis=0, keepdims=True)


def _hgrn_post(o, nw, ga):
    o = o * lax.rsqrt(jnp.mean(o * o, axis=-1, keepdims=True) + EPS)
    return o * nw * jax.nn.sigmoid(ga)


def _hgrn_prompt_kernel(hg_ref, lg_ref, nw_ref, oa_ref, st_ref, s_ref, *, la, chunk, sub):
    ci = pl.program_id(1)

    @pl.when(ci == 0)
    def _():
        s_ref[...] = jnp.zeros_like(s_ref)

    lb = _lower_bound(lg_ref, la)
    tri = (_iota((chunk, chunk), 0) >= _iota((chunk, chunk), 1)).astype(BF16)
    srow = _iota((sub, 1), 0)
    lane = _iota((sub, chunk), 1)
    for h in range(A_HEADS):
        hs = slice(h * A_DK, (h + 1) * A_DK)
        q = hg_ref[:, h * A_DK:(h + 1) * A_DK]
        fl = hg_ref[:, 512 + h * A_DK:512 + (h + 1) * A_DK]
        v = hg_ref[:, 1024 + h * A_DV:1024 + (h + 1) * A_DV]
        ga = hg_ref[:, 1536 + h * A_DV:1536 + (h + 1) * A_DV]
        lbh = lb[:, hs]
        f = lbh + (1.0 - lbh) * jax.nn.sigmoid(fl)
        a = _dot3_rhs(tri, jnp.log(f))
        k = 1.0 - f
        st = s_ref[h]
        vb = v.astype(BF16)
        o = _dot_nt((q * jnp.exp(a)).astype(BF16), st.astype(BF16))
        rows = []
        for i in range(chunk // sub):
            r0 = i * sub
            qi, ai = q[r0:r0 + sub], a[r0:r0 + sub]
            att = jnp.zeros((sub, chunk), F32)
            for s in range(sub):
                dec = jnp.exp(jnp.where(srow >= s, ai - a[r0 + s:r0 + s + 1], NEG_INF))
                col = jnp.sum(qi * dec * k[r0 + s:r0 + s + 1], axis=-1, keepdims=True)
                att = jnp.where(lane == r0 + s, col, att)
            if i > 0:
                ref_a = a[r0 - 1:r0]
                qt = qi * jnp.exp(ai - ref_a)
                kt = jnp.concatenate([k[:r0] * jnp.exp(ref_a - a[:r0]), jnp.zeros((chunk - r0, A_DK), F32)], axis=0)
                att = att + _dot_nt(qt.astype(BF16), kt.astype(BF16))
            rows.append(att)
        att = jnp.concatenate(rows, axis=0)
        o = o + _dot(att.astype(BF16), vb)
        a_end = a[chunk - 1:chunk]
        khat = k * jnp.exp(a_end - a)
        s_ref[h] = st * jnp.exp(a_end) + _dot_tn(vb, khat.astype(BF16))
        oa_ref[:, hs] = _hgrn_post(o, nw_ref[:, hs], ga).astype(BF16)

    @pl.when(ci == pl.num_programs(1) - 1)
    def _():
        for h in range(A_HEADS):
            st_ref[0, h] = s_ref[h].T


def _hgrn_prompt(hg, lb_logits, nw, la, b, s, chunk=128, sub=16):
    n = s // chunk
    full = lambda a: pl.BlockSpec(a.shape, lambda i, j: (0,) * a.ndim)
    return pl.pallas_call(
        functools.partial(_hgrn_prompt_kernel, la=la, chunk=chunk, sub=sub),
        grid=(b, n),
        in_specs=[pl.BlockSpec((chunk, HG_W), lambda i, j: (i * n + j, 0)), full(lb_logits), full(nw)],
        out_specs=[pl.BlockSpec((chunk, A_HEADS * A_DV), lambda i, j: (i * n + j, 0)),
                   pl.BlockSpec((1, A_HEADS, A_DK, A_DV), lambda i, j: (i, 0, 0, 0))],
        out_shape=[jax.ShapeDtypeStruct((b * s, A_HEADS * A_DV), BF16),
                   jax.ShapeDtypeStruct((b, A_HEADS, A_DK, A_DV), F32)],
        scratch_shapes=[pltpu.VMEM((A_HEADS, A_DV, A_DK), F32)],
        compiler_params=_cparams(("parallel", "arbitrary"), 32),
        name="hgrn_prompt",
    )(hg, lb_logits, nw)


def _hgrn_sample_kernel(hg_ref, s0_ref, lg_ref, nw_ref, oa_ref, st_ref, *, la):
    lb = _lower_bound(lg_ref, la)
    eye = (_iota((A_DK, A_DK), 0) == _iota((A_DK, A_DK), 1)).astype(F32)

    def col(r):
        return jnp.sum(eye * r, axis=1, keepdims=True)

    for h in range(A_HEADS):
        hs = slice(h * A_DK, (h + 1) * A_DK)
        q = hg_ref[0, :, h * A_DK:(h + 1) * A_DK]
        fl = hg_ref[0, :, 512 + h * A_DK:512 + (h + 1) * A_DK]
        v = hg_ref[0, :, 1024 + h * A_DV:1024 + (h + 1) * A_DV]
        ga = hg_ref[0, :, 1536 + h * A_DV:1536 + (h + 1) * A_DV]
        lbh = lb[:, hs]
        f = lbh + (1.0 - lbh) * jax.nn.sigmoid(fl)
        sn = col(f) * s0_ref[0, h] + col(1.0 - f) * v
        st_ref[0, h] = sn
        o = jnp.sum(col(q) * sn, axis=0, keepdims=True)
        oa_ref[0, :, hs] = _hgrn_post(o, nw_ref[:, hs], ga).astype(BF16)


def _hgrn_sample(hg, s0, lb_logits, nw, la):
    bd = hg.shape[0]
    full = lambda a: pl.BlockSpec(a.shape, lambda i: (0,) * a.ndim)
    st_spec = pl.BlockSpec((1, A_HEADS, A_DK, A_DV), lambda i: (i, 0, 0, 0))
    oa, st = pl.pallas_call(
        functools.partial(_hgrn_sample_kernel, la=la),
        grid=(bd,),
        in_specs=[pl.BlockSpec((1, 1, HG_W), lambda i: (i, 0, 0)), st_spec, full(lb_logits), full(nw)],
        out_specs=[pl.BlockSpec((1, 1, A_HEADS * A_DV), lambda i: (i, 0, 0)), st_spec],
        out_shape=[jax.ShapeDtypeStruct((bd, 1, A_HEADS * A_DV), BF16),
                   jax.ShapeDtypeStruct(s0.shape, F32)],
        compiler_params=_cparams(("parallel",), 16),
        name="hgrn_sample",
    )(hg.reshape(bd, 1, HG_W), s0, lb_logits, nw)
    return oa.reshape(bd, A_HEADS * A_DV), st


def _cmp_weights(pe, w1, w2, packed):
    eye = jnp.eye(B_KV_HEADS, dtype=F32)
    w1r = w1.reshape(2, L_CMP, B_HD, CMP_HID)
    w1bd = jnp.einsum("kidm,hH->ikhdHm", w1r, eye).reshape(L_CMP, 2, KV_W // 2, KV_W // 2)
    pe_b = jnp.broadcast_to(pe.transpose(1, 0, 2)[:, :, None, :], (L_CMP, 2, B_KV_HEADS, B_HD))
    if not packed:
        w2bd = jnp.einsum("kmd,kK,hH->khmKHd", w2, eye, eye).reshape(KV_W, KV_W)
        return pe_b.reshape(L_CMP, KV_W), w1bd.astype(BF16), w2bd.astype(BF16)
    half = KV_W // 2
    zero = jnp.zeros((half, KV_W), F32)
    k_blk = jnp.concatenate([w2[0], jnp.zeros_like(w2[0])], axis=1)
    v_blk = jnp.concatenate([w2[1], w2[1]], axis=1)
    w2k = jnp.concatenate([jnp.einsum("ml,hH->hmHl", k_blk, eye).reshape(half, KV_W), zero], axis=0)
    w2v = jnp.concatenate([zero, jnp.einsum("ml,hH->hmHl", v_blk, eye).reshape(half, KV_W)], axis=0)
    return pe_b.reshape(L_CMP, KV_W), w1bd.astype(BF16), (w2k.astype(BF16), w2v.astype(BF16))


def _chunk_preacts(load, n, pe_ref, w1_ref):
    half = KV_W // 2
    p0 = [jnp.zeros((n, half), F32) for _ in range(2)]
    p1 = [jnp.zeros((n, half), F32) for _ in range(2)]
    for i in range(D_CMP):
        for kv in range(2):
            x = load(i, kv)
            ls = slice(kv * half, (kv + 1) * half)
            p0[kv] = p0[kv] + _dot((x + pe_ref[i:i + 1, ls]).astype(BF16), w1_ref[i, kv])
            p1[kv] = p1[kv] + _dot((x + pe_ref[D_CMP + i:D_CMP + i + 1, ls]).astype(BF16), w1_ref[D_CMP + i, kv])
    return jnp.concatenate(p0, axis=1), jnp.concatenate(p1, axis=1)


def _with_position(k, pos, lane):
    col = lane & 127
    return jnp.where(col == B_HD, (pos >> 7).astype(F32), jnp.where(col == B_HD + 1, (pos & 127).astype(F32), k))


def _compress_kernel(kc_ref, pe_ref, w1_ref, w2k_ref, w2v_ref, ck_ref, cv_ref):
    n = ck_ref.shape[1]
    p0, p1 = _chunk_preacts(lambda i, kv: kc_ref[0, pl.ds(2 * i + kv, n, stride=2 * D_CMP), :], n, pe_ref, w1_ref)
    pre = p0 + pltpu.roll(p1, n - 1, axis=0)
    hid = (pre * jax.nn.sigmoid(pre)).astype(BF16)
    cpos = _iota((n, 1), 0) * D_CMP + (L_CMP - 1)
    ck_ref[0] = _with_position(_dot(hid, w2k_ref[...]), cpos, _iota((n, KV_W), 1)).astype(BF16)
    cv_ref[0] = _dot(hid, w2v_ref[...]).astype(BF16)


def _compress_prompt(kc, pe_b, w1bd, w2kv):
    b, s2, _ = kc.shape
    n = s2 // (2 * D_CMP)
    full = lambda a: pl.BlockSpec(a.shape, lambda i: (0,) * a.ndim)
    out = pl.BlockSpec((1, n, KV_W), lambda i: (i, 0, 0))
    return pl.pallas_call(
        _compress_kernel,
        grid=(b,),
        in_specs=[pl.BlockSpec((1, s2, KV_W // 2), lambda i: (i, 0, 0)), full(pe_b), full(w1bd),
                  full(w2kv[0]), full(w2kv[1])],
        out_specs=[out, out],
        out_shape=[jax.ShapeDtypeStruct((b, n, KV_W), BF16)] * 2,
        compiler_params=_cparams(("parallel",), 48),
        name="compress_prompt",
    )(kc, pe_b, w1bd, *w2kv)


MASK_BIAS = 2.0 ** 100

def _top_blocks_t(score_t, n_sel):
    nblk = score_t.shape[0]
    blk = _iota(score_t.shape, 0).astype(F32)

    def body(_, carry):
        sc, sel = carry
        m = jnp.max(sc, axis=0, keepdims=True)
        idx = jnp.min(jnp.where(sc == m, blk, float(nblk)), axis=0, keepdims=True)
        pick = blk == idx
        return jnp.where(pick, NEG_INF, sc), jnp.where(pick, 1.0, sel)

    _, sel = lax.fori_loop(0, n_sel, body, (score_t, jnp.zeros_like(score_t)), unroll=True)
    return sel


def _nsa_prompt_kernel(sl_ref, q_ref, g_ref, ck_ref, cv_ref, sk_ref, sv_ref, wk_ref, wv_ref, cover_ref, o_ref,
                       m_ref, l_ref, acc_ref, sa_ref, sb_ref, *, tq, tk):
    kvh = pl.program_id(1)
    q0 = pl.program_id(2) * tq
    rows = B_GROUP * tq
    lane = _iota((tq, 128), 1)
    lo = lane < B_HD
    qf = q_ref[0].astype(F32)
    left = []
    for g in range(B_GROUP):
        pair = qf[:, 128 * (g // 2):128 * (g // 2) + 128]
        if g % 2:
            pair = pltpu.roll(pair, B_HD, axis=1)
        slope = sl_ref[kvh * B_GROUP + g]
        pos_cols = jnp.where(lane == B_HD, slope * 128.0, jnp.where(lane == B_HD + 1, slope, 0.0))
        left.append(jnp.where(lo, pair, pos_cols).astype(BF16))
    ql = jnp.concatenate(left, axis=0)
    qpos = q0 + (_iota((rows, 1), 0) & (tq - 1))

    nc = ck_ref.shape[1]
    cpos = _iota((1, nc), 1) * D_CMP + (L_CMP - 1)
    span = WINDOW + tq
    start = pl.multiple_of(jnp.maximum(q0 - WINDOW, 0), 128)
    wpos = start + _iota((1, span), 1)
    s_cmp = _dot_nt(ql, ck_ref[0])
    s_win = _dot_nt(ql, wk_ref[0, 0, pl.ds(start, span), :])
    p = _masked_softmax(s_cmp, cpos <= qpos)
    pw = _masked_softmax(s_win, (wpos <= qpos) & (wpos > qpos - WINDOW))
    o_cmp = _dot(p.astype(BF16), cv_ref[0])
    o_win = _dot(pw.astype(BF16), wv_ref[0, 0, pl.ds(start, span), :])
    imp = p[0:tq]
    for g in range(1, B_GROUP):
        imp = imp + p[g * tq:(g + 1) * tq]

    nblk = cover_ref.shape[1]
    jb = _iota((1, nblk), 1)
    cur = (q0 + _iota((tq, 1), 0)) >> SLC_SHIFT
    forced = (jb == 0) | (jb == cur) | (jb == cur - 1)
    score = _dot3_lhs(imp, cover_ref[...])
    score = jnp.where(forced | (jb > cur), NEG_INF, score)
    n_forced = 3
    sel = jnp.where(forced, 1.0, _top_blocks_t(score.T, min(N_SEL, nblk) - n_forced).T)

    bias = ((sel - 1.0) * MASK_BIAS).astype(BF16)
    qs = jnp.concatenate([ql, jnp.concatenate([bias] * B_GROUP, axis=0)], axis=1)
    m_ref[...] = jnp.full_like(m_ref, NEG_INF)
    l_ref[...] = jnp.zeros_like(l_ref)
    acc_ref[...] = jnp.zeros_like(acc_ref)

    def scores(kt):
        return _dot_nt(qs, sk_ref[0, 0, pl.ds(pl.multiple_of(kt * tk, tk), tk), :])

    def update(s, kt, causal):
        k0 = pl.multiple_of(kt * tk, tk)
        if causal:
            s = jnp.where(k0 + _iota((1, tk), 1) <= qpos, s, NEG_INF)
        m_old = m_ref[...]
        m_new = jnp.maximum(m_old, jnp.max(s, axis=-1, keepdims=True))
        alpha = jnp.exp(m_old - m_new)
        e = jnp.exp(s - m_new)
        l_ref[...] = alpha * l_ref[...] + jnp.sum(e, axis=-1, keepdims=True)
        acc_ref[...] = alpha * acc_ref[...] + _dot(e.astype(BF16), sv_ref[0, 0, pl.ds(k0, tk), :])
        m_ref[...] = m_new

    n_full = q0 // tk

    sa_ref[...] = scores(0)

    def tile_pair(i, carry):
        sb_ref[...] = scores(2 * i + 1)
        update(sa_ref[...], 2 * i, False)
        sa_ref[...] = scores(2 * i + 2)
        update(sb_ref[...], 2 * i + 1, False)
        return carry

    lax.fori_loop(0, n_full // 2, tile_pair, 0)

    @pl.when(n_full % 2 == 1)
    def _():
        sb_ref[...] = scores(n_full)
        update(sa_ref[...], n_full - 1, False)
        update(sb_ref[...], n_full, True)

    @pl.when(n_full % 2 == 0)
    def _():
        update(sa_ref[...], n_full, True)

    o_slc = acc_ref[...] * (1.0 / l_ref[...])

    gate = jax.nn.sigmoid(g_ref[0])
    outs = []
    for g in range(B_GROUP):
        rs = slice(g * tq, (g + 1) * tq)
        outs.append(gate[:, 3 * g:3 * g + 1] * o_cmp[rs] + gate[:, 3 * g + 1:3 * g + 2] * o_slc[rs]
                    + gate[:, 3 * g + 2:3 * g + 3] * o_win[rs])
    for j in range(B_GROUP // 2):
        o_ref[0, :, 128 * j:128 * j + 128] = jnp.where(lo, outs[2 * j], outs[2 * j + 1]).astype(BF16)


def _nsa_prompt(slopes, qb, gb, ck, cv, sk, sv, wk, wv, cover, tq=128, tk=512):
    b, s, _ = qb.shape
    half = QB_W // B_KV_HEADS
    rows = B_GROUP * tq
    assert cover.shape[1] == 128 and s % tk == 0 and tk % tq == 0
    seq = lambda a: pl.BlockSpec((1, 1) + a.shape[2:], lambda i, h, t, sl: (h, i, 0, 0))
    grid_spec = pltpu.PrefetchScalarGridSpec(
        num_scalar_prefetch=1,
        grid=(b, B_KV_HEADS, s // tq),
        in_specs=[
            pl.BlockSpec((1, tq, half), lambda i, h, t, sl: (i, t, h)),
            pl.BlockSpec((1, tq, GATE_PAD), lambda i, h, t, sl: (i, t, h)),
            pl.BlockSpec((1, ck.shape[1], 128), lambda i, h, t, sl: (i, 0, h)),
            pl.BlockSpec((1, cv.shape[1], 128), lambda i, h, t, sl: (i, 0, h)),
            seq(sk), seq(sv), seq(wk), seq(wv),
            pl.BlockSpec(cover.shape, lambda i, h, t, sl: (0, 0)),
        ],
        out_specs=pl.BlockSpec((1, tq, half), lambda i, h, t, sl: (i, t, h)),
        scratch_shapes=[pltpu.VMEM((rows, 1), F32), pltpu.VMEM((rows, 1), F32), pltpu.VMEM((rows, 128), F32),
                        pltpu.VMEM((rows, tk), F32), pltpu.VMEM((rows, tk), F32)],
    )
    return pl.pallas_call(
        functools.partial(_nsa_prompt_kernel, tq=tq, tk=tk),
        grid_spec=grid_spec,
        out_shape=jax.ShapeDtypeStruct((b, s, QB_W), BF16),
        compiler_params=_cparams(("parallel", "parallel", "arbitrary"), 56),
        name="nsa_prompt",
    )(slopes, qb, gb, ck, cv, sk, sv, wk, wv, cover)


def _cover_matrix(n_cmp, n_blk, n_pad):
    ci = jnp.arange(n_cmp)[:, None] * D_CMP
    sj = jnp.arange(n_pad)[None, :] * L_SLC
    cov = (ci < sj + L_SLC) & (ci + L_CMP > sj) & (jnp.arange(n_pad)[None, :] < n_blk)
    return cov.astype(BF16)


PAGES_PER_STEP = 64
PAGE_GROUPS = 2


def _cmp_pages_kernel(pt_ref, *refs):
    pages = refs[:PAGES_PER_STEP]
    w_ref, out_ref = refs[PAGES_PER_STEP:PAGES_PER_STEP + 2]
    row_refs = refs[PAGES_PER_STEP + 2:]
    per_group = PAGES_PER_STEP // PAGE_GROUPS
    n = per_group * PAGE_SIZE // D_CMP
    half = KV_W // 2
    for grp, rows_ref in enumerate(row_refs):
        for k, pg in enumerate(pages[grp * per_group:(grp + 1) * per_group]):
            for kv in range(2):
                rows_ref[kv, k * PAGE_SIZE:(k + 1) * PAGE_SIZE, :] = pg[0, kv * half:(kv + 1) * half, :].T
        x = jnp.concatenate([rows_ref[kv, pl.ds(i, n, stride=D_CMP), :].astype(BF16)
                             for i in range(D_CMP) for kv in range(2)], axis=1)
        out_ref[0, grp * n:(grp + 1) * n, :] = _dot(x, w_ref[...])


def _pages_weight(w1bd):
    zero = jnp.zeros((D_CMP,) + w1bd.shape[2:], w1bd.dtype)

    def diag(a):
        return jnp.concatenate([jnp.concatenate([a[:, 0], zero], axis=2),
                                jnp.concatenate([zero, a[:, 1]], axis=2)], axis=1)

    w = jnp.concatenate([diag(w1bd[:D_CMP]), diag(w1bd[D_CMP:])], axis=2)
    return w.reshape(D_CMP * KV_W, 2 * KV_W)


def _feature_major_pages(cache):
    return cache.transpose(0, 2, 3, 4, 5, 1).reshape(cache.shape[0], -1, cache.shape[1])


def _cmp_pages(page_table, cache, li, w1bd):
    bd, n_pages = page_table.shape
    pool = _feature_major_pages(cache)
    steps = n_pages // PAGES_PER_STEP
    rows = PAGES_PER_STEP * PAGE_SIZE // D_CMP
    w = _pages_weight(w1bd)

    def page_spec(k):
        return pl.BlockSpec((1, KV_W, PAGE_SIZE), lambda i, j, pt: (pt[i, j * PAGES_PER_STEP + k], li, 0))

    grid_spec = pltpu.PrefetchScalarGridSpec(
        num_scalar_prefetch=1,
        grid=(bd, steps),
        in_specs=[page_spec(k) for k in range(PAGES_PER_STEP)] + [pl.BlockSpec(w.shape, lambda i, j, pt: (0, 0))],
        out_specs=pl.BlockSpec((1, rows, 2 * KV_W), lambda i, j, pt: (i, j, 0)),
        scratch_shapes=[pltpu.VMEM((2, PAGES_PER_STEP // PAGE_GROUPS * PAGE_SIZE, KV_W // 2), F32)] * PAGE_GROUPS,
    )
    return pl.pallas_call(
        _cmp_pages_kernel,
        grid_spec=grid_spec,
        out_shape=jax.ShapeDtypeStruct((bd, steps * rows, 2 * KV_W), F32),
        compiler_params=_cparams(("parallel", "arbitrary"), 48),
        name="cmp_pages",
    )(page_table, *([pool] * PAGES_PER_STEP), w)


def _nsa_sample_select_kernel(sl_ref, pre_ref, new_ref, q_ref, pt_ref, pe_ref, w1_ref, w2_ref, cover_ref,
                              idx_ref, phys_ref, ocmp_ref, tail_ref, *, past):
    half = KV_W // 2

    def row_act(i, r):
        r8 = jnp.broadcast_to(r, (8, KV_W)).astype(BF16)
        return jnp.concatenate([_dot(r8[:, kv * half:(kv + 1) * half], w1_ref[i, kv]) for kv in range(2)], axis=1)

    @pl.when(pl.program_id(0) == 0)
    def _():
        t = jnp.zeros((8, KV_W), F32)
        for i in range(L_CMP):
            t = t + row_act(i, pe_ref[i:i + 1])
        tail_ref[...] = t

    n = pre_ref.shape[1]
    p1 = pltpu.roll(pre_ref[0, :, KV_W:2 * KV_W], n - 1, axis=0)
    p1 = jnp.where(_iota((n, 1), 0) == n - 1, row_act(D_CMP, new_ref[0])[0:1], p1)
    pre = pre_ref[0, :, 0:KV_W] + p1 + tail_ref[0:1]
    comp = _dot((pre * jax.nn.sigmoid(pre)).astype(BF16), w2_ref[...]).astype(BF16)

    qpos = past
    cpos = _iota((1, n), 1) * D_CMP + (L_CMP - 1)
    cmask = cpos <= qpos
    cdist = (qpos - cpos).astype(F32)
    row8 = _iota((8, 1), 0)
    imp = jnp.zeros((8, n), F32)
    for kvh in range(B_KV_HEADS):
        slope = jnp.zeros((8, 1), F32)
        for g in range(B_GROUP):
            slope = jnp.where(row8 == g, sl_ref[kvh * B_GROUP + g], slope)
        p = _masked_softmax(_dot_nt(q_ref[0, kvh], comp) - slope * cdist, cmask)
        p = jnp.where(row8 < B_GROUP, p, 0.0)
        ocmp_ref[0, kvh] = _dot(p.astype(BF16), comp)
        imp = jnp.where(row8 == kvh, jnp.sum(p, axis=0, keepdims=True), imp)

    nblk = cover_ref.shape[1]
    jb = _iota((1, nblk), 1)
    cur = qpos // L_SLC
    score = _dot3_lhs(imp, cover_ref[...])
    score = jnp.where((jb == 0) | (jb == cur) | (jb == cur - 1), SEL_FORCE, score)
    score = jnp.where(jb > cur, NEG_INF, score)
    lane = _iota((1, 128), 1)
    jbf = jb.astype(F32)
    pages = pt_ref[0].astype(F32)
    bpp_shift = (PAGE_SIZE // L_SLC).bit_length() - 1

    def body(it, carry):
        sc, idx, phys = carry
        m = jnp.max(sc, axis=-1, keepdims=True)
        j = jnp.min(jnp.where(sc == m, jbf, float(nblk)), axis=-1, keepdims=True)
        logical = jnp.minimum(j.astype(jnp.int32), past // L_SLC - 1) >> bpp_shift
        page = jnp.sum(jnp.where(lane == logical, pages, 0.0), axis=-1, keepdims=True)
        return jnp.where(jbf == j, NEG_INF, sc), jnp.where(lane == it, j, idx), jnp.where(lane == it, page, phys)

    zero = jnp.zeros((8, 128), F32)
    _, idx, phys = lax.fori_loop(0, N_SEL, body, (score, zero, zero))
    idx_ref[0] = idx.astype(jnp.int32)
    phys_ref[0] = phys.astype(jnp.int32)


def _nsa_sample_select(slopes, pre, cmp_new, q8, page_table, pe_b, w1bd, w2bd, cover, past):
    bd = pre.shape[0]
    assert page_table.shape[1] == 128
    full = lambda a: pl.BlockSpec(a.shape, lambda i, sl: (0,) * a.ndim)
    picks = pl.BlockSpec((1, 8, 128), lambda i, sl: (i, 0, 0))
    grid_spec = pltpu.PrefetchScalarGridSpec(
        num_scalar_prefetch=1,
        grid=(bd,),
        in_specs=[
            pl.BlockSpec((1,) + pre.shape[1:], lambda i, sl: (i, 0, 0)),
            pl.BlockSpec((1, 1, KV_W), lambda i, sl: (i, 0, 0)),
            pl.BlockSpec((1, B_KV_HEADS, 8, KV_W), lambda i, sl: (i, 0, 0, 0)),
            pl.BlockSpec((1, 1, 128), lambda i, sl: (i, 0, 0)),
            full(pe_b), full(w1bd), full(w2bd), full(cover),
        ],
        out_specs=[picks, picks, pl.BlockSpec((1, B_KV_HEADS, 8, KV_W), lambda i, sl: (i, 0, 0, 0))],
        scratch_shapes=[pltpu.VMEM((8, KV_W), F32)],
    )
    return pl.pallas_call(
        functools.partial(_nsa_sample_select_kernel, past=past),
        grid_spec=grid_spec,
        out_shape=[jax.ShapeDtypeStruct((bd, 8, 128), jnp.int32)] * 2
        + [jax.ShapeDtypeStruct((bd, B_KV_HEADS, 8, KV_W), F32)],
        compiler_params=_cparams(("arbitrary",), 48),
        name="nsa_sample_select",
    )(slopes, pre, cmp_new, q8, page_table.reshape(bd, 1, 128), pe_b, w1bd, w2bd, cover)


def _nsa_sample_attend_kernel(pt_ref, ix_ref, sl_ref, *refs, past, nb_past):
    n_blocks = B_KV_HEADS * N_SEL
    pages = refs[:n_blocks]
    q_ref, slc_new_ref, win_new_ref, win_ref, g_ref, ocmp_ref, o_ref = refs[n_blocks:]
    b = pl.program_id(0)
    bpp = PAGE_SIZE // L_SLC
    half = KV_W // 2

    def token_rows(ref, c0):
        return jnp.concatenate([ref[0, kv * half:(kv + 1) * half, c0:c0 + 128].T for kv in range(2)], axis=1)

    row8 = _iota((8, 1), 0)
    first = _iota((L_SLC, 1), 0) == 0
    new_blk = jnp.where(first, slc_new_ref[0], 0.0).astype(BF16)
    wb = win_ref.shape[2]
    win = jnp.concatenate([token_rows(win_ref, c0) for c0 in range(0, wb, 128)], axis=0).astype(BF16)
    win_new = jnp.where(_iota((8, 1), 0) == 0, win_new_ref[0], 0.0).astype(BF16)
    for kvh in range(B_KV_HEADS):
        q = q_ref[0, kvh]
        slope = jnp.zeros((8, 1), F32)
        for g in range(B_GROUP):
            slope = jnp.where(row8 == g, sl_ref[kvh * B_GROUP + g], slope)
        m = jnp.full((8, 1), NEG_INF, F32)
        l = jnp.zeros((8, 1), F32)
        acc = jnp.zeros((8, KV_W), F32)
        for n in range(N_SEL):
            j = ix_ref[b, kvh * N_SEL + n]
            pg = token_rows(pages[kvh * N_SEL + n], 0).astype(BF16)
            blk = pg[0:L_SLC]
            for sub in range(1, bpp):
                blk = jnp.where(jnp.minimum(j, nb_past - 1) % bpp == sub, pg[sub * L_SLC:(sub + 1) * L_SLC], blk)
            blk = jnp.where(j >= nb_past, new_blk, blk)
            kpos = j * L_SLC + _iota((1, L_SLC), 1)
            s = _dot_nt(q, blk) - slope * (past - kpos).astype(F32)
            s = jnp.where(kpos <= past, s, NEG_INF)
            m_new = jnp.maximum(m, jnp.max(s, axis=-1, keepdims=True))
            m_safe = jnp.where(m_new > NEG_INF, m_new, 0.0)
            alpha = jnp.exp(m - m_safe)
            p = jnp.exp(s - m_safe)
            l = alpha * l + jnp.sum(p, axis=-1, keepdims=True)
            acc = alpha * acc + _dot(p.astype(BF16), blk)
            m = m_new
        o_slc = acc * (1.0 / jnp.where(l > 0, l, 1.0))
        wpos = past - wb + _iota((1, wb), 1)
        s1 = _dot_nt(q, win) - slope * (past - wpos).astype(F32)
        s1 = jnp.where(wpos > past - WINDOW, s1, NEG_INF)
        s2 = jnp.where(_iota((1, 8), 1) == 0, _dot_nt(q, win_new), NEG_INF)
        mw = jnp.maximum(jnp.max(s1, axis=-1, keepdims=True), jnp.max(s2, axis=-1, keepdims=True))
        e1 = jnp.exp(s1 - mw)
        e2 = jnp.exp(s2 - mw)
        dw = jnp.sum(e1, axis=-1, keepdims=True) + jnp.sum(e2, axis=-1, keepdims=True)
        o_win = (_dot(e1.astype(BF16), win) + _dot(e2.astype(BF16), win_new)) * (1.0 / dw)
        gate = jax.nn.sigmoid(g_ref[0, kvh])
        o_ref[0, kvh] = gate[:, 0:1] * ocmp_ref[0, kvh] + gate[:, 1:2] * o_slc + gate[:, 2:3] * o_win


def _nsa_sample_attend(phys, idx, slopes, cache, li, q8, slc_new, win_new, win_state, gate8, o_cmp, past):
    bd = q8.shape[0]
    bpp = PAGE_SIZE // L_SLC
    nb_past = past // L_SLC
    n_blocks = B_KV_HEADS * N_SEL
    pool = _feature_major_pages(cache)

    def blk_spec(k):
        return pl.BlockSpec((1, KV_W, PAGE_SIZE), lambda i, ph, ix, sl: (ph[i, k], li, 0))

    row = lambda: pl.BlockSpec((1, 1, KV_W), lambda i, pt, ix, sl: (i, 0, 0))
    per_head = lambda w: pl.BlockSpec((1, B_KV_HEADS, 8, w), lambda i, pt, ix, sl: (i, 0, 0, 0))
    grid_spec = pltpu.PrefetchScalarGridSpec(
        num_scalar_prefetch=3,
        grid=(bd,),
        in_specs=[blk_spec(k) for k in range(n_blocks)] + [
            per_head(KV_W), row(), row(),
            pl.BlockSpec((1,) + win_state.shape[1:], lambda i, pt, ix, sl: (i, 0, 0)),
            per_head(128), per_head(KV_W)],
        out_specs=per_head(KV_W),
    )
    return pl.pallas_call(
        functools.partial(_nsa_sample_attend_kernel, past=past, nb_past=nb_past),
        grid_spec=grid_spec,
        out_shape=jax.ShapeDtypeStruct((bd, B_KV_HEADS, 8, KV_W), F32),
        compiler_params=_cparams(("arbitrary",), 48),
        name="nsa_sample_attend",
    )(phys, idx, slopes, *([pool] * n_blocks), q8, slc_new, win_new, win_state, gate8, o_cmp)


def _outproj_kernel(x_ref, a_ref, b_ref, w_ref, y_ref):
    na = a_ref.shape[1]
    y_ref[...] = x_ref[...] + _dot(a_ref[...], w_ref[0:na]) + _dot(b_ref[...], w_ref[na:])


def _outproj(x, oa, ob, w, tm):
    m = x.shape[0]
    row = lambda n: pl.BlockSpec((tm, n), lambda i: (i, 0))
    return pl.pallas_call(
        _outproj_kernel,
        grid=(m // tm,),
        in_specs=[row(D_MODEL), row(oa.shape[1]), row(ob.shape[1]), pl.BlockSpec(w.shape, lambda i: (0, 0))],
        out_specs=row(D_MODEL),
        out_shape=jax.ShapeDtypeStruct((m, D_MODEL), F32),
        compiler_params=_cparams(("parallel",), 32),
        name="outproj",
    )(x, oa, ob, w)


def _mixer_prompt_kernel(*refs, groups, convs, gated_ffn, final, tm, tf, f):
    x_ref, nw_ref, up_ref, cw_ref, down_ref = refs[0:5]
    pos = 5
    fw_ref = refs[pos] if final else None
    pos += int(final)
    y_ref, tail_ref = refs[pos], refs[pos + 1]
    act_ref, ubuf_ref, carry_ref = refs[pos + 2:]

    @pl.when(pl.program_id(1) == 0)
    def _():
        carry_ref[...] = jnp.zeros_like(carry_ref)

    x = x_ref[0]
    xn = _rms(x, nw_ref[...]).astype(BF16)
    for j in range(f // tf):
        u = [_dot(xn, up_ref[:, g * f + j * tf:g * f + (j + 1) * tf]) for g in range(groups)]
        conv_in = [u[0], u[1]] if gated_ffn else [u[1] * u[0]]
        conv_out = []
        for c, z in enumerate(conv_in):
            cols = slice(c * f + j * tf, c * f + (j + 1) * tf)
            slot = (j % 2) * convs + c
            ubuf_ref[slot, 0:8] = carry_ref[:, cols]
            ubuf_ref[slot, 8:tm + 8] = z
            w = cw_ref[:, cols]
            conv_out.append(w[0:1] * ubuf_ref[slot, pl.ds(6, tm), :] + w[1:2] * ubuf_ref[slot, pl.ds(7, tm), :]
                            + w[2:3] * z)
            last = z[tm - 8:tm]
            carry_ref[:, cols] = last
            tail_ref[0, 0, :, cols] = last
        if gated_ffn:
            act = conv_out[0] * jax.nn.sigmoid(conv_out[0]) * conv_out[1]
        else:
            act = u[2] * conv_out[0]
        act_ref[:, j * tf:(j + 1) * tf] = act.astype(BF16)
    y = x + _dot(act_ref[...], down_ref[...])
    y_ref[0] = _rms(y, fw_ref[...]) if final else y


def _mixer_prompt(x, nw, w_up, conv_w, w_down, gated_ffn, final_w=None, tm=512, tf=256):
    b, s, d = x.shape
    f = w_down.shape[0]
    groups = w_up.shape[1] // f
    convs = conv_w.shape[1] // f
    final = final_w is not None
    const = lambda a: pl.BlockSpec(a.shape, lambda i, t: (0,) * a.ndim, pipeline_mode=pl.Buffered(1))
    in_specs = [pl.BlockSpec((1, tm, d), lambda i, t: (i, t, 0)), const(nw), const(w_up), const(conv_w),
                const(w_down)]
    args = [x, nw, w_up, conv_w, w_down]
    if final:
        in_specs.append(const(final_w))
        args.append(final_w)
    y, tail = pl.pallas_call(
        functools.partial(_mixer_prompt_kernel, groups=groups, convs=convs, gated_ffn=gated_ffn,
                          final=final, tm=tm, tf=tf, f=f),
        grid=(b, s // tm),
        in_specs=in_specs,
        out_specs=[pl.BlockSpec((1, tm, d), lambda i, t: (i, t, 0)),
                   pl.BlockSpec((1, 1, 8, convs * f), lambda i, t: (i, t, 0, 0))],
        out_shape=[jax.ShapeDtypeStruct((b, s, d), F32), jax.ShapeDtypeStruct((b, s // tm, 8, convs * f), F32)],
        scratch_shapes=[pltpu.VMEM((tm, f), BF16), pltpu.VMEM((2 * convs, tm + 8, tf), F32),
                        pltpu.VMEM((8, convs * f), F32)],
        compiler_params=_cparams(("parallel", "arbitrary"), 56),
        name="ffn_prompt" if gated_ffn else "sconv_prompt",
    )(*args)
    return y, tail[:, -1, 8 - (CONV_W - 1):, :]


def _mixer_sample_kernel(*refs, groups, convs, gated_ffn, final):
    x_ref, nw_ref = refs[0:2]
    up = refs[2:2 + groups]
    cw = refs[2 + groups:2 + groups + convs]
    st = refs[2 + groups + convs:2 + groups + 2 * convs]
    down_ref = refs[2 + groups + 2 * convs]
    pos = 3 + groups + 2 * convs
    fw_ref = refs[pos] if final else None
    pos += int(final)
    y_ref = refs[pos]
    news = refs[pos + 1:pos + 1 + convs]
    xn_ref, acc_ref = refs[pos + 1 + convs:]
    j = pl.program_id(0)

    @pl.when(j == 0)
    def _():
        xn_ref[...] = _rms(x_ref[...], nw_ref[...]).astype(BF16)
        acc_ref[...] = jnp.zeros_like(acc_ref)

    xn = xn_ref[...]
    u = [_dot(xn, w[...]) for w in up]
    conv_in = [u[0], u[1]] if gated_ffn else [u[1] * u[0]]
    conv_out = []
    for c, z in enumerate(conv_in):
        w = cw[c][...]
        conv_out.append(w[0:1] * st[c][:, 0, :] + w[1:2] * st[c][:, 1, :] + w[2:3] * z)
        news[c][...] = z
    if gated_ffn:
        act = conv_out[0] * jax.nn.sigmoid(conv_out[0]) * conv_out[1]
    else:
        act = u[2] * conv_out[0]
    acc_ref[...] += _dot(act.astype(BF16), down_ref[...])

    @pl.when(j == pl.num_programs(0) - 1)
    def _():
        y = x_ref[...] + acc_ref[...]
        y_ref[...] = _rms(y, fw_ref[...]) if final else y


def _mixer_sample(x, nw, w_up, conv_w, w_down, state, gated_ffn, final_w=None, tf=256):
    bd, d = x.shape
    f = w_down.shape[0]
    groups = w_up.shape[1] // f
    convs = conv_w.shape[1] // f
    nf = f // tf
    final = final_w is not None
    in_specs = [pl.BlockSpec((bd, d), lambda j: (0, 0)), pl.BlockSpec(nw.shape, lambda j: (0, 0))]
    in_specs += [pl.BlockSpec((d, tf), lambda j, g=g: (0, g * nf + j)) for g in range(groups)]
    in_specs += [pl.BlockSpec((CONV_W, tf), lambda j, c=c: (0, c * nf + j)) for c in range(convs)]
    in_specs += [pl.BlockSpec((bd, CONV_W - 1, tf), lambda j, c=c: (0, 0, c * nf + j)) for c in range(convs)]
    in_specs += [pl.BlockSpec((tf, d), lambda j: (j, 0))]
    args = [x, nw] + [w_up] * groups + [conv_w] * convs + [state] * convs + [w_down]
    if final:
        in_specs.append(pl.BlockSpec(final_w.shape, lambda j: (0, 0)))
        args.append(final_w)
    out_specs = [pl.BlockSpec((bd, d), lambda j: (0, 0))]
    out_specs += [pl.BlockSpec((bd, tf), lambda j: (0, j))] * convs
    out_shape = [jax.ShapeDtypeStruct((bd, d), F32)] + [jax.ShapeDtypeStruct((bd, f), F32)] * convs
    res = pl.pallas_call(
        functools.partial(_mixer_sample_kernel, groups=groups, convs=convs, gated_ffn=gated_ffn, final=final),
        grid=(nf,),
        in_specs=in_specs,
        out_specs=out_specs,
        out_shape=out_shape,
        scratch_shapes=[pltpu.VMEM((bd, d), BF16), pltpu.VMEM((bd, d), F32)],
        compiler_params=_cparams(("arbitrary",), 32),
        name="ffn_sample" if gated_ffn else "sconv_sample",
    )(*args)
    new = jnp.concatenate(res[1:], axis=-1)
    return res[0], jnp.concatenate([state[:, 1:], new[:, None, :]], axis=1)


def _pad_in_weight(w):
    c = HG_W + QB_W + 3 * KV_W
    per = 3 * B_GROUP
    gates = [jnp.pad(w[:, c + h * per:c + (h + 1) * per], ((0, 0), (0, GATE_PAD - per)))
             for h in range(B_KV_HEADS)]
    return jnp.concatenate([w[:, :c]] + gates, axis=1).astype(BF16)


def _per_head_rows(z, width):
    bd = z.shape[0]
    z = z.reshape(bd, B_KV_HEADS, B_GROUP, -1)
    return jnp.pad(z, ((0, 0), (0, 0), (0, 8 - B_GROUP), (0, width - z.shape[-1])))


def kernel(x_prompt, x_sample, cache_cmp_kv, cache_slc_kv, state_win_kv, state_hgrn, state_sconv, state_ffn,
           page_table, norm_mix, norm_ffn, norm_final, ab_w_in, ab_w_out, hgrn_lb_logits, hgrn_norm,
           cmp_pe, cmp_w1, cmp_w2, c_w_in, c_conv, c_w_out, ffn_up, ffn_conv, ffn_down):
    b, s, d = x_prompt.shape
    bd, t, _ = x_sample.shape
    assert t == 1 and d == D_MODEL
    depth = norm_mix.shape[0]
    n_a = ab_w_in.shape[0]
    wb = state_win_kv.shape[2]
    n_pages = page_table.shape[1]
    past = n_pages * PAGE_SIZE
    assert wb == WINDOW and past % (PAGES_PER_STEP * PAGE_SIZE) == 0 and s % 1024 == 0 and s >= wb
    hh = jnp.arange(1, B_HEADS + 1, dtype=F32)
    slopes = 2.0 ** (-8.0 * hh / B_HEADS)
    lb_logits = hgrn_lb_logits.astype(F32)
    row2 = lambda v: v.reshape(1, -1).astype(F32)

    xp = x_prompt.reshape(b * s, d)
    xs = x_sample.reshape(bd, d)
    outs_p = {k: [] for k in ("cmp", "slc", "win", "hg", "sc", "ff")}
    outs_s = {k: [] for k in ("cmp", "slc", "win", "hg", "sc", "ff")}
    y_prompt = y_sample = None
    for l in range(depth):
        last = l == depth - 1
        if l % 2 == 0:
            la = l // 2
            w_in = _pad_in_weight(ab_w_in[la])
            w_out = ab_w_out[la].astype(BF16)
            nw = row2(norm_mix[l])
            hnw = row2(hgrn_norm[la])
            pe_b, w1bd, w2_packed = _cmp_weights(cmp_pe[la], cmp_w1[la], cmp_w2[la], packed=True)
            _, _, w2_plain = _cmp_weights(cmp_pe[la], cmp_w1[la], cmp_w2[la], packed=False)
            hg, qb, gb, kc, kct, kslt, kwt, sk, sv, wk, wv = _inproj(xp, nw, w_in, tm=512, seq=s)
            sk, sv, wk, wv = [z.reshape(B_KV_HEADS, b, s, -1) for z in (sk, sv, wk, wv)]
            oa, hst = _hgrn_prompt(hg, lb_logits, hnw, la, b, s)
            ck, cv = _compress_prompt(kc.reshape(b, 2 * s, KV_W // 2), pe_b, w1bd, w2_packed)
            n_cmp = s // D_CMP
            nblk = s // L_SLC
            cover = _cover_matrix(n_cmp, nblk, nblk)
            ob = _nsa_prompt(slopes, qb.reshape(b, s, QB_W), gb.reshape(b, s, -1), ck, cv, sk, sv, wk, wv, cover)
            xp = _outproj(xp, oa, ob.reshape(b * s, QB_W), w_out, tm=512)
            kv6 = lambda z, n, tt: z.reshape(n, tt, 2, B_KV_HEADS, B_HD)
            from_t = lambda z: z.reshape(b, 2, B_KV_HEADS, B_HD, -1).transpose(0, 4, 1, 2, 3)
            outs_p["cmp"].append(from_t(kct))
            outs_p["slc"].append(from_t(kslt))
            outs_p["win"].append(from_t(kwt[:, :, s - wb:]))
            outs_p["hg"].append(hst)
            hg, qb, gb, kc, ksl, kw = _inproj(xs, nw, w_in, tm=bd)
            oa, hst = _hgrn_sample(hg, state_hgrn[la], lb_logits, hnw, la)
            pre = _cmp_pages(page_table, cache_cmp_kv, la, w1bd)
            n_cmp = pre.shape[1]
            nblk = past // L_SLC + 1
            cover = _cover_matrix(n_cmp, nblk, -(-nblk // 128) * 128)
            q8 = jnp.stack([jnp.pad(_per_head_rows(qb, B_HD)[:, h], ((0, 0), (0, 0), (h * B_HD, KV_W - (h + 1) * B_HD)))
                            for h in range(B_KV_HEADS)], axis=1)
            idx, phys, o_cmp = _nsa_sample_select(slopes, pre, kc.reshape(bd, 1, KV_W), q8, page_table, pe_b, w1bd,
                                                  w2_plain, cover, past)
            picks = lambda z: z[:, :B_KV_HEADS, :N_SEL].reshape(bd, B_KV_HEADS * N_SEL)
            idx, phys = picks(idx), picks(phys)
            gate8 = _per_head_rows(jnp.concatenate([gb[:, h * GATE_PAD:h * GATE_PAD + 3 * B_GROUP]
                                                    for h in range(B_KV_HEADS)], axis=1), 128)
            win_state = state_win_kv[la].transpose(0, 2, 3, 4, 1).reshape(bd, KV_W, wb)
            o8 = _nsa_sample_attend(phys, idx, slopes, cache_slc_kv, la, q8, ksl.reshape(bd, 1, KV_W),
                                    kw.reshape(bd, 1, KV_W), win_state, gate8, o_cmp, past)
            ob = jnp.concatenate([o8[:, h, :B_GROUP, 2 * B_HD + h * B_HD:2 * B_HD + (h + 1) * B_HD]
                                  .reshape(bd, B_GROUP * B_HD) for h in range(B_KV_HEADS)], axis=1)
            xs = _outproj(xs, oa, ob.astype(BF16), w_out, tm=bd)
            outs_s["cmp"].append(kv6(kc, bd, t))
            outs_s["slc"].append(kv6(ksl, bd, t))
            outs_s["win"].append(jnp.concatenate([state_win_kv[la], kv6(kw, bd, t)], axis=1)[:, t:])
            outs_s["hg"].append(hst)
        else:
            lc = l // 2
            xp3, tail = _mixer_prompt(xp.reshape(b, s, d), row2(norm_mix[l]), c_w_in[lc].astype(BF16),
                                      c_conv[lc], c_w_out[lc].astype(BF16), gated_ffn=False)
            xp = xp3.reshape(b * s, d)
            outs_p["sc"].append(tail)
            xs, new = _mixer_sample(xs, row2(norm_mix[l]), c_w_in[lc].astype(BF16), c_conv[lc],
                                    c_w_out[lc].astype(BF16), state_sconv[lc], gated_ffn=False)
            outs_s["sc"].append(new)
        fw = row2(norm_final) if last else None
        xp3, tail = _mixer_prompt(xp.reshape(b, s, d), row2(norm_ffn[l]), ffn_up[l].astype(BF16), ffn_conv[l],
                                  ffn_down[l].astype(BF16), gated_ffn=True, final_w=fw)
        xp = xp3.reshape(b * s, d)
        outs_p["ff"].append(tail)
        xs, new = _mixer_sample(xs, row2(norm_ffn[l]), ffn_up[l].astype(BF16), ffn_conv[l],
                                ffn_down[l].astype(BF16), state_ffn[l], gated_ffn=True, final_w=fw)
        outs_s["ff"].append(new)
    y_prompt = xp.reshape(b, s, d)
    y_sample = xs.reshape(bd, t, d)
    return (y_prompt, y_sample,
            jnp.stack(outs_p["cmp"], axis=2), jnp.stack(outs_p["slc"], axis=2), jnp.stack(outs_p["win"], axis=0),
            jnp.stack(outs_p["hg"], axis=0), jnp.stack(outs_p["sc"], axis=0), jnp.stack(outs_p["ff"], axis=0),
            jnp.stack(outs_s["cmp"], axis=2), jnp.stack(outs_s["slc"], axis=2), jnp.stack(outs_s["win"], axis=0),
            jnp.stack(outs_s["hg"], axis=0), jnp.stack(outs_s["sc"], axis=0), jnp.stack(outs_s["ff"], axis=0))
```

```python
import functools

import jax
import jax.numpy as jnp
from jax import lax
from jax.experimental import pallas as pl
from jax.experimental.pallas import tpu as pltpu

F32 = jnp.float32
BF16 = jnp.bfloat16
NEG_INF = float("-inf")

D_MODEL = 1024
PAGE_SIZE = 128
A_HEADS = 4
A_DK = 128
A_DV = 128
B_HEADS = 8
B_KV_HEADS = 2
B_GROUP = B_HEADS // B_KV_HEADS
B_HD = 64
L_CMP = 32
D_CMP = 16
CMP_HID = 64
L_SLC = 64
SLC_SHIFT = 6
N_SEL = 16
WINDOW = 512
SEL_FORCE = 1e9
CONV_W = 3
EPS = 1e-6
KV_W = 2 * B_KV_HEADS * B_HD
HG_W = 4 * A_HEADS * A_DK
QB_W = B_HEADS * B_HD
GATE_PAD = 128
IN_PAD = HG_W + QB_W + 3 * KV_W + B_KV_HEADS * GATE_PAD

V7X_VMEM_BYTES = 64 * 2**20


def _cparams(sem, vmem_mb):
    assert vmem_mb * 2**20 < V7X_VMEM_BYTES
    return pltpu.CompilerParams(dimension_semantics=sem, vmem_limit_bytes=vmem_mb * 2**20)


def _dot(a, b):
    return jnp.dot(a, b, preferred_element_type=F32)


def _dot_nt(a, b):
    return lax.dot_general(a, b, (((1,), (1,)), ((), ())), preferred_element_type=F32)


def _dot_tn(a, b):
    return lax.dot_general(a, b, (((0,), (0,)), ((), ())), preferred_element_type=F32)


def _split3(x):
    hi = x.astype(BF16)
    r1 = x - hi.astype(F32)
    mid = r1.astype(BF16)
    lo = (r1 - mid.astype(F32)).astype(BF16)
    return hi, mid, lo


def _dot3_lhs(x, w):
    hi, mid, lo = _split3(x)
    return _dot(hi, w) + _dot(mid, w) + _dot(lo, w)


def _dot3_rhs(w, x):
    hi, mid, lo = _split3(x)
    return _dot(w, hi) + _dot(w, mid) + _dot(w, lo)


def _rms(x, w):
    return x * lax.rsqrt(jnp.mean(x * x, axis=-1, keepdims=True) + EPS) * w


def _masked_softmax(s, mask):
    s = jnp.where(mask, s, NEG_INF)
    m = jnp.max(s, axis=-1, keepdims=True)
    m = jnp.where(m > NEG_INF, m, 0.0)
    e = jnp.exp(s - m)
    d = jnp.sum(e, axis=-1, keepdims=True)
    return e * (1.0 / jnp.where(d > 0, d, 1.0))


def _iota(shape, dim):
    return lax.broadcasted_iota(jnp.int32, shape, dim)


def _top_blocks(score, jb, n_sel):
    big = float(score.shape[-1])
    jbf = jb.astype(F32)

    def body(_, carry):
        sc, sel = carry
        m = jnp.max(sc, axis=-1, keepdims=True)
        idx = jnp.min(jnp.where(sc == m, jbf, big), axis=-1, keepdims=True)
        pick = jbf == idx
        return jnp.where(pick, NEG_INF, sc), jnp.where(pick, 1.0, sel)

    _, sel = lax.fori_loop(0, n_sel, body, (score, jnp.zeros_like(score)))
    return sel


def _attention_rows(kv, pos, with_blocks):
    tm = kv.shape[0]
    lane = _iota((tm, 128), 1)
    lo = lane < B_HD
    pos_cols = jnp.where(lane == B_HD, (pos >> 7).astype(F32),
                         jnp.where(lane == B_HD + 1, (pos & 127).astype(F32), 0.0))
    k_pair, v_pair = kv[:, 0:128], kv[:, 128:256]
    k_swap, v_swap = pltpu.roll(k_pair, B_HD, axis=1), pltpu.roll(v_pair, B_HD, axis=1)
    onehot = ((pos >> SLC_SHIFT) == lane).astype(BF16)
    keys, values = [], []
    for h, (k, v_lo, v_hi) in enumerate(((k_pair, v_pair, v_swap), (k_swap, v_swap, v_pair))):
        key = jnp.where(lo, k, pos_cols).astype(BF16)
        keys.append(jnp.concatenate([key, onehot], axis=1) if with_blocks else key)
        values.append(jnp.where(lo, v_lo, v_hi).astype(BF16))
    return keys, values


def _inproj_kernel(x_ref, nw_ref, w_ref, hg_ref, qb_ref, gb_ref, kc_ref, *refs, seq):
    xn = _rms(x_ref[...], nw_ref[...]).astype(BF16)

    def proj(c0, n):
        return _dot(xn, w_ref[:, c0:c0 + n])

    hg_ref[...] = proj(0, HG_W)
    qb_ref[...] = (proj(HG_W, QB_W) * (B_HD ** -0.5)).astype(BF16)
    c = HG_W + QB_W
    kc = proj(c, KV_W)
    ksl = proj(c + KV_W, KV_W)
    kw = proj(c + 2 * KV_W, KV_W)
    kc_ref[...] = kc
    gb_ref[...] = proj(c + 3 * KV_W, B_KV_HEADS * GATE_PAD)
    if seq is None:
        ksl_ref, kw_ref = refs
        ksl_ref[...] = ksl
        kw_ref[...] = kw
    else:
        kct_ref, kslt_ref, kwt_ref, sk_ref, sv_ref, wk_ref, wv_ref = refs
        kct_ref[0] = kc.T
        kslt_ref[0] = ksl.T
        kwt_ref[0] = kw.T
        tm = x_ref.shape[0]
        pos = (pl.program_id(0) * tm + _iota((tm, 1), 0)) & (seq - 1)
        for kv, k_ref, v_ref, with_blocks in ((ksl, sk_ref, sv_ref, True), (kw, wk_ref, wv_ref, False)):
            keys, values = _attention_rows(kv, pos, with_blocks)
            for h in range(B_KV_HEADS):
                k_ref[h] = keys[h]
                v_ref[h] = values[h]


def _inproj(x, nw, w_pad, tm, seq=None):
    m = x.shape[0]
    row = lambda n: pl.BlockSpec((tm, n), lambda i: (i, 0))
    full = lambda a: pl.BlockSpec(a.shape, lambda i: (0,) * a.ndim)
    widths = (HG_W, QB_W, B_KV_HEADS * GATE_PAD, KV_W)
    dtypes = (F32, BF16, F32, F32)
    out_specs = [row(n) for n in widths]
    out_shape = [jax.ShapeDtypeStruct((m, n), d) for n, d in zip(widths, dtypes)]
    if seq is None:
        out_specs += [row(KV_W)] * 2
        out_shape += [jax.ShapeDtypeStruct((m, KV_W), F32)] * 2
    else:
        assert seq // L_SLC == 128 and seq % tm == 0 and seq & (seq - 1) == 0
        nt = seq // tm
        out_specs += [pl.BlockSpec((1, KV_W, tm), lambda i: (i // nt, 0, i % nt))] * 3
        out_shape += [jax.ShapeDtypeStruct((m // seq, KV_W, seq), F32)] * 3
        for n in (256, 128, 128, 128):
            out_specs.append(pl.BlockSpec((B_KV_HEADS, tm, n), lambda i: (0, i, 0)))
            out_shape.append(jax.ShapeDtypeStruct((B_KV_HEADS, m, n), BF16))
    return pl.pallas_call(
        functools.partial(_inproj_kernel, seq=seq),
        grid=(m // tm,),
        in_specs=[row(D_MODEL), full(nw), full(w_pad)],
        out_specs=out_specs,
        out_shape=out_shape,
        compiler_params=_cparams(("parallel",), 48),
        name="inproj",
    )(x, nw, w_pad)


def _lower_bound(logit_ref, la):
    lg = logit_ref[...]
    e = jnp.exp(lg - jnp.max(lg, axis=0, keepdims=True))
    return jnp.sum(e[:la + 1], axis=0, keepdims=True) / jnp.sum(e, axis=0, keepdims=True)


def _hgrn_post(o, nw, ga):
    o = o * lax.rsqrt(jnp.mean(o * o, axis=-1, keepdims=True) + EPS)
    return o * nw * jax.nn.sigmoid(ga)


def _hgrn_prompt_kernel(hg_ref, lg_ref, nw_ref, oa_ref, st_ref, s_ref, *, la, chunk, sub):
    ci = pl.program_id(1)

    @pl.when(ci == 0)
    def _():
        s_ref[...] = jnp.zeros_like(s_ref)

    lb = _lower_bound(lg_ref, la)
    tri = (_iota((chunk, chunk), 0) >= _iota((chunk, chunk), 1)).astype(BF16)
    srow = _iota((sub, 1), 0)
    lane = _iota((sub, chunk), 1)
    for h in range(A_HEADS):
        hs = slice(h * A_DK, (h + 1) * A_DK)
        q = hg_ref[:, h * A_DK:(h + 1) * A_DK]
        fl = hg_ref[:, 512 + h * A_DK:512 + (h + 1) * A_DK]
        v = hg_ref[:, 1024 + h * A_DV:1024 + (h + 1) * A_DV]
        ga = hg_ref[:, 1536 + h * A_DV:1536 + (h + 1) * A_DV]
        lbh = lb[:, hs]
        f = lbh + (1.0 - lbh) * jax.nn.sigmoid(fl)
        a = _dot3_rhs(tri, jnp.log(f))
        k = 1.0 - f
        st = s_ref[h]
        vb = v.astype(BF16)
        o = _dot_nt((q * jnp.exp(a)).astype(BF16), st.astype(BF16))
        rows = []
        for i in range(chunk // sub):
            r0 = i * sub
            qi, ai = q[r0:r0 + sub], a[r0:r0 + sub]
            att = jnp.zeros((sub, chunk), F32)
            for s in range(sub):
                dec = jnp.exp(jnp.where(srow >= s, ai - a[r0 + s:r0 + s + 1], NEG_INF))
                col = jnp.sum(qi * dec * k[r0 + s:r0 + s + 1], axis=-1, keepdims=True)
                att = jnp.where(lane == r0 + s, col, att)
            if i > 0:
                ref_a = a[r0 - 1:r0]
                qt = qi * jnp.exp(ai - ref_a)
                kt = jnp.concatenate([k[:r0] * jnp.exp(ref_a - a[:r0]), jnp.zeros((chunk - r0, A_DK), F32)], axis=0)
                att = att + _dot_nt(qt.astype(BF16), kt.astype(BF16))
            rows.append(att)
        att = jnp.concatenate(rows, axis=0)
        o = o + _dot(att.astype(BF16), vb)
        a_end = a[chunk - 1:chunk]
        khat = k * jnp.exp(a_end - a)
        s_ref[h] = st * jnp.exp(a_end) + _dot_tn(vb, khat.astype(BF16))
        oa_ref[:, hs] = _hgrn_post(o, nw_ref[:, hs], ga).astype(BF16)

    @pl.when(ci == pl.num_programs(1) - 1)
    def _():
        for h in range(A_HEADS):
            st_ref[0, h] = s_ref[h].T


def _hgrn_prompt(hg, lb_logits, nw, la, b, s, chunk=128, sub=16):
    n = s // chunk
    full = lambda a: pl.BlockSpec(a.shape, lambda i, j: (0,) * a.ndim)
    return pl.pallas_call(
        functools.partial(_hgrn_prompt_kernel, la=la, chunk=chunk, sub=sub),
        grid=(b, n),
        in_specs=[pl.BlockSpec((chunk, HG_W), lambda i, j: (i * n + j, 0)), full(lb_logits), full(nw)],
        out_specs=[pl.BlockSpec((chunk, A_HEADS * A_DV), lambda i, j: (i * n + j, 0)),
                   pl.BlockSpec((1, A_HEADS, A_DK, A_DV), lambda i, j: (i, 0, 0, 0))],
        out_shape=[jax.ShapeDtypeStruct((b * s, A_HEADS * A_DV), BF16),
                   jax.ShapeDtypeStruct((b, A_HEADS, A_DK, A_DV), F32)],
        scratch_shapes=[pltpu.VMEM((A_HEADS, A_DV, A_DK), F32)],
        compiler_params=_cparams(("parallel", "arbitrary"), 32),
        name="hgrn_prompt",
    )(hg, lb_logits, nw)


def _hgrn_sample_kernel(hg_ref, s0_ref, lg_ref, nw_ref, oa_ref, st_ref, *, la):
    lb = _lower_bound(lg_ref, la)
    eye = (_iota((A_DK, A_DK), 0) == _iota((A_DK, A_DK), 1)).astype(F32)

    def col(r):
        return jnp.sum(eye * r, axis=1, keepdims=True)

    for h in range(A_HEADS):
        hs = slice(h * A_DK, (h + 1) * A_DK)
        q = hg_ref[0, :, h * A_DK:(h + 1) * A_DK]
        fl = hg_ref[0, :, 512 + h * A_DK:512 + (h + 1) * A_DK]
        v = hg_ref[0, :, 1024 + h * A_DV:1024 + (h + 1) * A_DV]
        ga = hg_ref[0, :, 1536 + h * A_DV:1536 + (h + 1) * A_DV]
        lbh = lb[:, hs]
        f = lbh + (1.0 - lbh) * jax.nn.sigmoid(fl)
        sn = col(f) * s0_ref[0, h] + col(1.0 - f) * v
        st_ref[0, h] = sn
        o = jnp.sum(col(q) * sn, axis=0, keepdims=True)
        oa_ref[0, :, hs] = _hgrn_post(o, nw_ref[:, hs], ga).astype(BF16)


def _hgrn_sample(hg, s0, lb_logits, nw, la):
    bd = hg.shape[0]
    full = lambda a: pl.BlockSpec(a.shape, lambda i: (0,) * a.ndim)
    st_spec = pl.BlockSpec((1, A_HEADS, A_DK, A_DV), lambda i: (i, 0, 0, 0))
    oa, st = pl.pallas_call(
        functools.partial(_hgrn_sample_kernel, la=la),
        grid=(bd,),
        in_specs=[pl.BlockSpec((1, 1, HG_W), lambda i: (i, 0, 0)), st_spec, full(lb_logits), full(nw)],
        out_specs=[pl.BlockSpec((1, 1, A_HEADS * A_DV), lambda i: (i, 0, 0)), st_spec],
        out_shape=[jax.ShapeDtypeStruct((bd, 1, A_HEADS * A_DV), BF16),
                   jax.ShapeDtypeStruct(s0.shape, F32)],
        compiler_params=_cparams(("parallel",), 16),
        name="hgrn_sample",
    )(hg.reshape(bd, 1, HG_W), s0, lb_logits, nw)
    return oa.reshape(bd, A_HEADS * A_DV), st


def _cmp_weights(pe, w1, w2, packed):
    eye = jnp.eye(B_KV_HEADS, dtype=F32)
    w1r = w1.reshape(2, L_CMP, B_HD, CMP_HID)
    w1bd = jnp.einsum("kidm,hH->ikhdHm", w1r, eye).reshape(L_CMP, 2, KV_W // 2, KV_W // 2)
    pe_b = jnp.broadcast_to(pe.transpose(1, 0, 2)[:, :, None, :], (L_CMP, 2, B_KV_HEADS, B_HD))
    if not packed:
        w2bd = jnp.einsum("kmd,kK,hH->khmKHd", w2, eye, eye).reshape(KV_W, KV_W)
        return pe_b.reshape(L_CMP, KV_W), w1bd.astype(BF16), w2bd.astype(BF16)
    half = KV_W // 2
    zero = jnp.zeros((half, KV_W), F32)
    k_blk = jnp.concatenate([w2[0], jnp.zeros_like(w2[0])], axis=1)
    v_blk = jnp.concatenate([w2[1], w2[1]], axis=1)
    w2k = jnp.concatenate([jnp.einsum("ml,hH->hmHl", k_blk, eye).reshape(half, KV_W), zero], axis=0)
    w2v = jnp.concatenate([zero, jnp.einsum("ml,hH->hmHl", v_blk, eye).reshape(half, KV_W)], axis=0)
    return pe_b.reshape(L_CMP, KV_W), w1bd.astype(BF16), (w2k.astype(BF16), w2v.astype(BF16))


def _chunk_preacts(load, n, pe_ref, w1_ref):
    half = KV_W // 2
    p0 = [jnp.zeros((n, half), F32) for _ in range(2)]
    p1 = [jnp.zeros((n, half), F32) for _ in range(2)]
    for i in range(D_CMP):
        for kv in range(2):
            x = load(i, kv)
            ls = slice(kv * half, (kv + 1) * half)
            p0[kv] = p0[kv] + _dot((x + pe_ref[i:i + 1, ls]).astype(BF16), w1_ref[i, kv])
            p1[kv] = p1[kv] + _dot((x + pe_ref[D_CMP + i:D_CMP + i + 1, ls]).astype(BF16), w1_ref[D_CMP + i, kv])
    return jnp.concatenate(p0, axis=1), jnp.concatenate(p1, axis=1)


def _with_position(k, pos, lane):
    col = lane & 127
    return jnp.where(col == B_HD, (pos >> 7).astype(F32), jnp.where(col == B_HD + 1, (pos & 127).astype(F32), k))


def _compress_kernel(kc_ref, pe_ref, w1_ref, w2k_ref, w2v_ref, ck_ref, cv_ref):
    n = ck_ref.shape[1]
    p0, p1 = _chunk_preacts(lambda i, kv: kc_ref[0, pl.ds(2 * i + kv, n, stride=2 * D_CMP), :], n, pe_ref, w1_ref)
    pre = p0 + pltpu.roll(p1, n - 1, axis=0)
    hid = (pre * jax.nn.sigmoid(pre)).astype(BF16)
    cpos = _iota((n, 1), 0) * D_CMP + (L_CMP - 1)
    ck_ref[0] = _with_position(_dot(hid, w2k_ref[...]), cpos, _iota((n, KV_W), 1)).astype(BF16)
    cv_ref[0] = _dot(hid, w2v_ref[...]).astype(BF16)


def _compress_prompt(kc, pe_b, w1bd, w2kv):
    b, s2, _ = kc.shape
    n = s2 // (2 * D_CMP)
    full = lambda a: pl.BlockSpec(a.shape, lambda i: (0,) * a.ndim)
    out = pl.BlockSpec((1, n, KV_W), lambda i: (i, 0, 0))
    return pl.pallas_call(
        _compress_kernel,
        grid=(b,),
        in_specs=[pl.BlockSpec((1, s2, KV_W // 2), lambda i: (i, 0, 0)), full(pe_b), full(w1bd),
                  full(w2kv[0]), full(w2kv[1])],
        out_specs=[out, out],
        out_shape=[jax.ShapeDtypeStruct((b, n, KV_W), BF16)] * 2,
        compiler_params=_cparams(("parallel",), 48),
        name="compress_prompt",
    )(kc, pe_b, w1bd, *w2kv)


MASK_BIAS = 2.0 ** 100

def _top_blocks_t(score_t, n_sel):
    nblk = score_t.shape[0]
    blk = _iota(score_t.shape, 0).astype(F32)

    def body(_, carry):
        sc, sel = carry
        m = jnp.max(sc, axis=0, keepdims=True)
        idx = jnp.min(jnp.where(sc == m, blk, float(nblk)), axis=0, keepdims=True)
        pick = blk == idx
        return jnp.where(pick, NEG_INF, sc), jnp.where(pick, 1.0, sel)

    _, sel = lax.fori_loop(0, n_sel, body, (score_t, jnp.zeros_like(score_t)), unroll=True)
    return sel


def _nsa_prompt_kernel(sl_ref, q_ref, g_ref, ck_ref, cv_ref, sk_ref, sv_ref, wk_ref, wv_ref, cover_ref, o_ref,
                       m_ref, l_ref, acc_ref, sa_ref, sb_ref, tiles_ref, *, tq, tk):
    kvh = pl.program_id(1)
    q0 = pl.program_id(2) * tq
    rows = B_GROUP * tq
    lane = _iota((tq, 128), 1)
    lo = lane < B_HD
    qf = q_ref[0].astype(F32)
    left = []
    for g in range(B_GROUP):
        pair = qf[:, 128 * (g // 2):128 * (g // 2) + 128]
        if g % 2:
            pair = pltpu.roll(pair, B_HD, axis=1)
        slope = sl_ref[kvh * B_GROUP + g]
        pos_cols = jnp.where(lane == B_HD, slope * 128.0, jnp.where(lane == B_HD + 1, slope, 0.0))
        left.append(jnp.where(lo, pair, pos_cols).astype(BF16))
    ql = jnp.concatenate(left, axis=0)
    qpos = q0 + (_iota((rows, 1), 0) & (tq - 1))

    nc = ck_ref.shape[1]
    cpos = _iota((1, nc), 1) * D_CMP + (L_CMP - 1)
    span = WINDOW + tq
    start = pl.multiple_of(jnp.maximum(q0 - WINDOW, 0), 128)
    wpos = start + _iota((1, span), 1)
    s_cmp = _dot_nt(ql, ck_ref[0])
    s_win = _dot_nt(ql, wk_ref[0, 0, pl.ds(start, span), :])
    p = _masked_softmax(s_cmp, cpos <= qpos)
    pw = _masked_softmax(s_win, (wpos <= qpos) & (wpos > qpos - WINDOW))
    o_cmp = _dot(p.astype(BF16), cv_ref[0])
    o_win = _dot(pw.astype(BF16), wv_ref[0, 0, pl.ds(start, span), :])
    imp = p[0:tq]
    for g in range(1, B_GROUP):
        imp = imp + p[g * tq:(g + 1) * tq]

    nblk = cover_ref.shape[1]
    jb = _iota((1, nblk), 1)
    cur = (q0 + _iota((tq, 1), 0)) >> SLC_SHIFT
    forced = (jb == 0) | (jb == cur) | (jb == cur - 1)
    score = _dot3_lhs(imp, cover_ref[...])
    score = jnp.where(forced | (jb > cur), NEG_INF, score)
    n_forced = 3
    sel = jnp.where(forced, 1.0, _top_blocks_t(score.T, min(N_SEL, nblk) - n_forced).T)

    bias = ((sel - 1.0) * MASK_BIAS).astype(BF16)
    qs = jnp.concatenate([ql, jnp.concatenate([bias] * B_GROUP, axis=0)], axis=1)
    m_ref[...] = jnp.full_like(m_ref, NEG_INF)
    l_ref[...] = jnp.zeros_like(l_ref)
    acc_ref[...] = jnp.zeros_like(acc_ref)

    def scores(kt):
        return _dot_nt(qs, sk_ref[0, 0, pl.ds(pl.multiple_of(kt * tk, tk), tk), :])

    def update(s, kt, causal):
        k0 = pl.multiple_of(kt * tk, tk)
        if causal:
            s = jnp.where(k0 + _iota((1, tk), 1) <= qpos, s, NEG_INF)
        m_old = m_ref[...]
        m_new = jnp.maximum(m_old, jnp.max(s, axis=-1, keepdims=True))
        alpha = jnp.exp(m_old - m_new)
        e = jnp.exp(s - m_new)
        l_ref[...] = alpha * l_ref[...] + jnp.sum(e, axis=-1, keepdims=True)
        acc_ref[...] = alpha * acc_ref[...] + _dot(e.astype(BF16), sv_ref[0, 0, pl.ds(k0, tk), :])
        m_ref[...] = m_new

    n_diag = q0 // tk
    bpt = tk // L_SLC
    picked = jnp.max(sel, axis=0, keepdims=True)
    n_full = jnp.int32(0)
    for kt in range(nblk * L_SLC // tk):
        in_tile = (jb >= kt * bpt) & (jb < (kt + 1) * bpt)
        wanted = (jnp.max(jnp.where(in_tile, picked, 0.0)) > 0.5) & (kt < n_diag)
        tiles_ref[n_full] = jnp.int32(kt)
        n_full = n_full + wanted.astype(jnp.int32)
    tiles_ref[n_full] = n_diag

    sa_ref[...] = scores(tiles_ref[0])

    def tile_pair(i, carry):
        sb_ref[...] = scores(tiles_ref[2 * i + 1])
        update(sa_ref[...], tiles_ref[2 * i], False)
        sa_ref[...] = scores(tiles_ref[2 * i + 2])
        update(sb_ref[...], tiles_ref[2 * i + 1], False)
        return carry

    lax.fori_loop(0, n_full // 2, tile_pair, 0)

    @pl.when(n_full % 2 == 1)
    def _():
        sb_ref[...] = scores(n_diag)
        update(sa_ref[...], tiles_ref[n_full - 1], False)
        update(sb_ref[...], n_diag, True)

    @pl.when(n_full % 2 == 0)
    def _():
        update(sa_ref[...], n_diag, True)

    o_slc = acc_ref[...] * (1.0 / l_ref[...])

    gate = jax.nn.sigmoid(g_ref[0])
    outs = []
    for g in range(B_GROUP):
        rs = slice(g * tq, (g + 1) * tq)
        outs.append(gate[:, 3 * g:3 * g + 1] * o_cmp[rs] + gate[:, 3 * g + 1:3 * g + 2] * o_slc[rs]
                    + gate[:, 3 * g + 2:3 * g + 3] * o_win[rs])
    for j in range(B_GROUP // 2):
        o_ref[0, :, 128 * j:128 * j + 128] = jnp.where(lo, outs[2 * j], outs[2 * j + 1]).astype(BF16)


def _nsa_prompt(slopes, qb, gb, ck, cv, sk, sv, wk, wv, cover, tq=128, tk=512):
    b, s, _ = qb.shape
    half = QB_W // B_KV_HEADS
    rows = B_GROUP * tq
    assert cover.shape[1] == 128 and s % tk == 0 and tk % tq == 0
    seq = lambda a: pl.BlockSpec((1, 1) + a.shape[2:], lambda i, h, t, sl: (h, i, 0, 0))
    grid_spec = pltpu.PrefetchScalarGridSpec(
        num_scalar_prefetch=1,
        grid=(b, B_KV_HEADS, s // tq),
        in_specs=[
            pl.BlockSpec((1, tq, half), lambda i, h, t, sl: (i, t, h)),
            pl.BlockSpec((1, tq, GATE_PAD), lambda i, h, t, sl: (i, t, h)),
            pl.BlockSpec((1, ck.shape[1], 128), lambda i, h, t, sl: (i, 0, h)),
            pl.BlockSpec((1, cv.shape[1], 128), lambda i, h, t, sl: (i, 0, h)),
            seq(sk), seq(sv), seq(wk), seq(wv),
            pl.BlockSpec(cover.shape, lambda i, h, t, sl: (0, 0)),
        ],
        out_specs=pl.BlockSpec((1, tq, half), lambda i, h, t, sl: (i, t, h)),
        scratch_shapes=[pltpu.VMEM((rows, 1), F32), pltpu.VMEM((rows, 1), F32), pltpu.VMEM((rows, 128), F32),
                        pltpu.VMEM((rows, tk), F32), pltpu.VMEM((rows, tk), F32),
                        pltpu.SMEM((s // tk + 1,), jnp.int32)],
    )
    return pl.pallas_call(
        functools.partial(_nsa_prompt_kernel, tq=tq, tk=tk),
        grid_spec=grid_spec,
        out_shape=jax.ShapeDtypeStruct((b, s, QB_W), BF16),
        compiler_params=_cparams(("parallel", "parallel", "arbitrary"), 56),
        name="nsa_prompt",
    )(slopes, qb, gb, ck, cv, sk, sv, wk, wv, cover)


def _cover_matrix(n_cmp, n_blk, n_pad):
    ci = jnp.arange(n_cmp)[:, None] * D_CMP
    sj = jnp.arange(n_pad)[None, :] * L_SLC
    cov = (ci < sj + L_SLC) & (ci + L_CMP > sj) & (jnp.arange(n_pad)[None, :] < n_blk)
    return cov.astype(BF16)


PAGES_PER_STEP = 64
PAGE_GROUPS = 2


def _cmp_pages_kernel(pt_ref, *refs):
    pages = refs[:PAGES_PER_STEP]
    w_ref, out_ref = refs[PAGES_PER_STEP:PAGES_PER_STEP + 2]
    row_refs = refs[PAGES_PER_STEP + 2:]
    per_group = PAGES_PER_STEP // PAGE_GROUPS
    n = per_group * PAGE_SIZE // D_CMP
    half = KV_W // 2
    for grp, rows_ref in enumerate(row_refs):
        for k, pg in enumerate(pages[grp * per_group:(grp + 1) * per_group]):
            for kv in range(2):
                rows_ref[kv, k * PAGE_SIZE:(k + 1) * PAGE_SIZE, :] = pg[0, kv * half:(kv + 1) * half, :].T
        x = jnp.concatenate([rows_ref[kv, pl.ds(i, n, stride=D_CMP), :].astype(BF16)
                             for i in range(D_CMP) for kv in range(2)], axis=1)
        out_ref[0, grp * n:(grp + 1) * n, :] = _dot(x, w_ref[...])


def _pages_weight(w1bd):
    zero = jnp.zeros((D_CMP,) + w1bd.shape[2:], w1bd.dtype)

    def diag(a):
        return jnp.concatenate([jnp.concatenate([a[:, 0], zero], axis=2),
                                jnp.concatenate([zero, a[:, 1]], axis=2)], axis=1)

    w = jnp.concatenate([diag(w1bd[:D_CMP]), diag(w1bd[D_CMP:])], axis=2)
    return w.reshape(D_CMP * KV_W, 2 * KV_W)


def _feature_major_pages(cache):
    return cache.transpose(0, 2, 3, 4, 5, 1).reshape(cache.shape[0], -1, cache.shape[1])


def _cmp_pages(page_table, cache, li, w1bd):
    bd, n_pages = page_table.shape
    pool = _feature_major_pages(cache)
    steps = n_pages // PAGES_PER_STEP
    rows = PAGES_PER_STEP * PAGE_SIZE // D_CMP
    w = _pages_weight(w1bd)

    def page_spec(k):
        return pl.BlockSpec((1, KV_W, PAGE_SIZE), lambda i, j, pt: (pt[i, j * PAGES_PER_STEP + k], li, 0))

    grid_spec = pltpu.PrefetchScalarGridSpec(
        num_scalar_prefetch=1,
        grid=(bd, steps),
        in_specs=[page_spec(k) for k in range(PAGES_PER_STEP)] + [pl.BlockSpec(w.shape, lambda i, j, pt: (0, 0))],
        out_specs=pl.BlockSpec((1, rows, 2 * KV_W), lambda i, j, pt: (i, j, 0)),
        scratch_shapes=[pltpu.VMEM((2, PAGES_PER_STEP // PAGE_GROUPS * PAGE_SIZE, KV_W // 2), F32)] * PAGE_GROUPS,
    )
    return pl.pallas_call(
        _cmp_pages_kernel,
        grid_spec=grid_spec,
        out_shape=jax.ShapeDtypeStruct((bd, steps * rows, 2 * KV_W), F32),
        compiler_params=_cparams(("parallel", "arbitrary"), 48),
        name="cmp_pages",
    )(page_table, *([pool] * PAGES_PER_STEP), w)


def _nsa_sample_select_kernel(sl_ref, pre_ref, new_ref, q_ref, pt_ref, pe_ref, w1_ref, w2_ref, cover_ref,
                              idx_ref, phys_ref, ocmp_ref, tail_ref, *, past):
    half = KV_W // 2

    def row_act(i, r):
        r8 = jnp.broadcast_to(r, (8, KV_W)).astype(BF16)
        return jnp.concatenate([_dot(r8[:, kv * half:(kv + 1) * half], w1_ref[i, kv]) for kv in range(2)], axis=1)

    @pl.when(pl.program_id(0) == 0)
    def _():
        t = jnp.zeros((8, KV_W), F32)
        for i in range(L_CMP):
            t = t + row_act(i, pe_ref[i:i + 1])
        tail_ref[...] = t

    n = pre_ref.shape[1]
    p1 = pltpu.roll(pre_ref[0, :, KV_W:2 * KV_W], n - 1, axis=0)
    p1 = jnp.where(_iota((n, 1), 0) == n - 1, row_act(D_CMP, new_ref[0])[0:1], p1)
    pre = pre_ref[0, :, 0:KV_W] + p1 + tail_ref[0:1]
    comp = _dot((pre * jax.nn.sigmoid(pre)).astype(BF16), w2_ref[...]).astype(BF16)

    qpos = past
    cpos = _iota((1, n), 1) * D_CMP + (L_CMP - 1)
    cmask = cpos <= qpos
    cdist = (qpos - cpos).astype(F32)
    row8 = _iota((8, 1), 0)
    imp = jnp.zeros((8, n), F32)
    for kvh in range(B_KV_HEADS):
        slope = jnp.zeros((8, 1), F32)
        for g in range(B_GROUP):
            slope = jnp.where(row8 == g, sl_ref[kvh * B_GROUP + g], slope)
        p = _masked_softmax(_dot_nt(q_ref[0, kvh], comp) - slope * cdist, cmask)
        p = jnp.where(row8 < B_GROUP, p, 0.0)
        ocmp_ref[0, kvh] = _dot(p.astype(BF16), comp)
        imp = jnp.where(row8 == kvh, jnp.sum(p, axis=0, keepdims=True), imp)

    nblk = cover_ref.shape[1]
    jb = _iota((1, nblk), 1)
    cur = qpos // L_SLC
    score = _dot3_lhs(imp, cover_ref[...])
    score = jnp.where((jb == 0) | (jb == cur) | (jb == cur - 1), SEL_FORCE, score)
    score = jnp.where(jb > cur, NEG_INF, score)
    lane = _iota((1, 128), 1)
    jbf = jb.astype(F32)
    pages = pt_ref[0].astype(F32)
    bpp_shift = (PAGE_SIZE // L_SLC).bit_length() - 1

    def body(it, carry):
        sc, idx, phys = carry
        m = jnp.max(sc, axis=-1, keepdims=True)
        j = jnp.min(jnp.where(sc == m, jbf, float(nblk)), axis=-1, keepdims=True)
        logical = jnp.minimum(j.astype(jnp.int32), past // L_SLC - 1) >> bpp_shift
        page = jnp.sum(jnp.where(lane == logical, pages, 0.0), axis=-1, keepdims=True)
        return jnp.where(jbf == j, NEG_INF, sc), jnp.where(lane == it, j, idx), jnp.where(lane == it, page, phys)

    zero = jnp.zeros((8, 128), F32)
    _, idx, phys = lax.fori_loop(0, N_SEL, body, (score, zero, zero))
    idx_ref[0] = idx.astype(jnp.int32)
    phys_ref[0] = phys.astype(jnp.int32)


def _nsa_sample_select(slopes, pre, cmp_new, q8, page_table, pe_b, w1bd, w2bd, cover, past):
    bd = pre.shape[0]
    assert page_table.shape[1] == 128
    full = lambda a: pl.BlockSpec(a.shape, lambda i, sl: (0,) * a.ndim)
    picks = pl.BlockSpec((1, 8, 128), lambda i, sl: (i, 0, 0))
    grid_spec = pltpu.PrefetchScalarGridSpec(
        num_scalar_prefetch=1,
        grid=(bd,),
        in_specs=[
            pl.BlockSpec((1,) + pre.shape[1:], lambda i, sl: (i, 0, 0)),
            pl.BlockSpec((1, 1, KV_W), lambda i, sl: (i, 0, 0)),
            pl.BlockSpec((1, B_KV_HEADS, 8, KV_W), lambda i, sl: (i, 0, 0, 0)),
            pl.BlockSpec((1, 1, 128), lambda i, sl: (i, 0, 0)),
            full(pe_b), full(w1bd), full(w2bd), full(cover),
        ],
        out_specs=[picks, picks, pl.BlockSpec((1, B_KV_HEADS, 8, KV_W), lambda i, sl: (i, 0, 0, 0))],
        scratch_shapes=[pltpu.VMEM((8, KV_W), F32)],
    )
    return pl.pallas_call(
        functools.partial(_nsa_sample_select_kernel, past=past),
        grid_spec=grid_spec,
        out_shape=[jax.ShapeDtypeStruct((bd, 8, 128), jnp.int32)] * 2
        + [jax.ShapeDtypeStruct((bd, B_KV_HEADS, 8, KV_W), F32)],
        compiler_params=_cparams(("arbitrary",), 48),
        name="nsa_sample_select",
    )(slopes, pre, cmp_new, q8, page_table.reshape(bd, 1, 128), pe_b, w1bd, w2bd, cover)


def _nsa_sample_attend_kernel(pt_ref, ix_ref, sl_ref, *refs, past, nb_past):
    n_blocks = B_KV_HEADS * N_SEL
    pages = refs[:n_blocks]
    q_ref, slc_new_ref, win_new_ref, win_ref, g_ref, ocmp_ref, o_ref = refs[n_blocks:]
    b = pl.program_id(0)
    bpp = PAGE_SIZE // L_SLC
    half = KV_W // 2

    def token_rows(ref, c0):
        return jnp.concatenate([ref[0, kv * half:(kv + 1) * half, c0:c0 + 128].T for kv in range(2)], axis=1)

    row8 = _iota((8, 1), 0)
    first = _iota((L_SLC, 1), 0) == 0
    new_blk = jnp.where(first, slc_new_ref[0], 0.0).astype(BF16)
    wb = win_ref.shape[2]
    win = jnp.concatenate([token_rows(win_ref, c0) for c0 in range(0, wb, 128)], axis=0).astype(BF16)
    win_new = jnp.where(_iota((8, 1), 0) == 0, win_new_ref[0], 0.0).astype(BF16)
    for kvh in range(B_KV_HEADS):
        q = q_ref[0, kvh]
        slope = jnp.zeros((8, 1), F32)
        for g in range(B_GROUP):
            slope = jnp.where(row8 == g, sl_ref[kvh * B_GROUP + g], slope)
        m = jnp.full((8, 1), NEG_INF, F32)
        l = jnp.zeros((8, 1), F32)
        acc = jnp.zeros((8, KV_W), F32)
        for n in range(N_SEL):
            j = ix_ref[b, kvh * N_SEL + n]
            pg = token_rows(pages[kvh * N_SEL + n], 0).astype(BF16)
            blk = pg[0:L_SLC]
            for sub in range(1, bpp):
                blk = jnp.where(jnp.minimum(j, nb_past - 1) % bpp == sub, pg[sub * L_SLC:(sub + 1) * L_SLC], blk)
            blk = jnp.where(j >= nb_past, new_blk, blk)
            kpos = j * L_SLC + _iota((1, L_SLC), 1)
            s = _dot_nt(q, blk) - slope * (past - kpos).astype(F32)
            s = jnp.where(kpos <= past, s, NEG_INF)
            m_new = jnp.maximum(m, jnp.max(s, axis=-1, keepdims=True))
            m_safe = jnp.where(m_new > NEG_INF, m_new, 0.0)
            alpha = jnp.exp(m - m_safe)
            p = jnp.exp(s - m_safe)
            l = alpha * l + jnp.sum(p, axis=-1, keepdims=True)
            acc = alpha * acc + _dot(p.astype(BF16), blk)
            m = m_new
        o_slc = acc * (1.0 / jnp.where(l > 0, l, 1.0))
        wpos = past - wb + _iota((1, wb), 1)
        s1 = _dot_nt(q, win) - slope * (past - wpos).astype(F32)
        s1 = jnp.where(wpos > past - WINDOW, s1, NEG_INF)
        s2 = jnp.where(_iota((1, 8), 1) == 0, _dot_nt(q, win_new), NEG_INF)
        mw = jnp.maximum(jnp.max(s1, axis=-1, keepdims=True), jnp.max(s2, axis=-1, keepdims=True))
        e1 = jnp.exp(s1 - mw)
        e2 = jnp.exp(s2 - mw)
        dw = jnp.sum(e1, axis=-1, keepdims=True) + jnp.sum(e2, axis=-1, keepdims=True)
        o_win = (_dot(e1.astype(BF16), win) + _dot(e2.astype(BF16), win_new)) * (1.0 / dw)
        gate = jax.nn.sigmoid(g_ref[0, kvh])
        o_ref[0, kvh] = gate[:, 0:1] * ocmp_ref[0, kvh] + gate[:, 1:2] * o_slc + gate[:, 2:3] * o_win


def _nsa_sample_attend(phys, idx, slopes, cache, li, q8, slc_new, win_new, win_state, gate8, o_cmp, past):
    bd = q8.shape[0]
    bpp = PAGE_SIZE // L_SLC
    nb_past = past // L_SLC
    n_blocks = B_KV_HEADS * N_SEL
    pool = _feature_major_pages(cache)

    def blk_spec(k):
        return pl.BlockSpec((1, KV_W, PAGE_SIZE), lambda i, ph, ix, sl: (ph[i, k], li, 0))

    row = lambda: pl.BlockSpec((1, 1, KV_W), lambda i, pt, ix, sl: (i, 0, 0))
    per_head = lambda w: pl.BlockSpec((1, B_KV_HEADS, 8, w), lambda i, pt, ix, sl: (i, 0, 0, 0))
    grid_spec = pltpu.PrefetchScalarGridSpec(
        num_scalar_prefetch=3,
        grid=(bd,),
        in_specs=[blk_spec(k) for k in range(n_blocks)] + [
            per_head(KV_W), row(), row(),
            pl.BlockSpec((1,) + win_state.shape[1:], lambda i, pt, ix, sl: (i, 0, 0)),
            per_head(128), per_head(KV_W)],
        out_specs=per_head(KV_W),
    )
    return pl.pallas_call(
        functools.partial(_nsa_sample_attend_kernel, past=past, nb_past=nb_past),
        grid_spec=grid_spec,
        out_shape=jax.ShapeDtypeStruct((bd, B_KV_HEADS, 8, KV_W), F32),
        compiler_params=_cparams(("arbitrary",), 48),
        name="nsa_sample_attend",
    )(phys, idx, slopes, *([pool] * n_blocks), q8, slc_new, win_new, win_state, gate8, o_cmp)


def _outproj_kernel(x_ref, a_ref, b_ref, w_ref, y_ref):
    na = a_ref.shape[1]
    y_ref[...] = x_ref[...] + _dot(a_ref[...], w_ref[0:na]) + _dot(b_ref[...], w_ref[na:])


def _outproj(x, oa, ob, w, tm):
    m = x.shape[0]
    row = lambda n: pl.BlockSpec((tm, n), lambda i: (i, 0))
    return pl.pallas_call(
        _outproj_kernel,
        grid=(m // tm,),
        in_specs=[row(D_MODEL), row(oa.shape[1]), row(ob.shape[1]), pl.BlockSpec(w.shape, lambda i: (0, 0))],
        out_specs=row(D_MODEL),
        out_shape=jax.ShapeDtypeStruct((m, D_MODEL), F32),
        compiler_params=_cparams(("parallel",), 32),
        name="outproj",
    )(x, oa, ob, w)


def _mixer_prompt_kernel(*refs, groups, convs, gated_ffn, final, tm, tf, f):
    x_ref, nw_ref, up_ref, cw_ref, down_ref = refs[0:5]
    pos = 5
    fw_ref = refs[pos] if final else None
    pos += int(final)
    y_ref, tail_ref = refs[pos], refs[pos + 1]
    act_ref, ubuf_ref, carry_ref = refs[pos + 2:]

    @pl.when(pl.program_id(1) == 0)
    def _():
        carry_ref[...] = jnp.zeros_like(carry_ref)

    x = x_ref[0]
    xn = _rms(x, nw_ref[...]).astype(BF16)
    for j in range(f // tf):
        u = [_dot(xn, up_ref[:, g * f + j * tf:g * f + (j + 1) * tf]) for g in range(groups)]
        conv_in = [u[0], u[1]] if gated_ffn else [u[1] * u[0]]
        conv_out = []
        for c, z in enumerate(conv_in):
            cols = slice(c * f + j * tf, c * f + (j + 1) * tf)
            slot = (j % 2) * convs + c
            ubuf_ref[slot, 0:8] = carry_ref[:, cols]
            ubuf_ref[slot, 8:tm + 8] = z
            w = cw_ref[:, cols]
            conv_out.append(w[0:1] * ubuf_ref[slot, pl.ds(6, tm), :] + w[1:2] * ubuf_ref[slot, pl.ds(7, tm), :]
                            + w[2:3] * z)
            last = z[tm - 8:tm]
            carry_ref[:, cols] = last
            tail_ref[0, 0, :, cols] = last
        if gated_ffn:
            act = conv_out[0] * jax.nn.sigmoid(conv_out[0]) * conv_out[1]
        else:
            act = u[2] * conv_out[0]
        act_ref[:, j * tf:(j + 1) * tf] = act.astype(BF16)
    y = x + _dot(act_ref[...], down_ref[...])
    y_ref[0] = _rms(y, fw_ref[...]) if final else y


def _mixer_prompt(x, nw, w_up, conv_w, w_down, gated_ffn, final_w=None, tm=512, tf=256):
    b, s, d = x.shape
    f = w_down.shape[0]
    groups = w_up.shape[1] // f
    convs = conv_w.shape[1] // f
    final = final_w is not None
    const = lambda a: pl.BlockSpec(a.shape, lambda i, t: (0,) * a.ndim, pipeline_mode=pl.Buffered(1))
    in_specs = [pl.BlockSpec((1, tm, d), lambda i, t: (i, t, 0)), const(nw), const(w_up), const(conv_w),
                const(w_down)]
    args = [x, nw, w_up, conv_w, w_down]
    if final:
        in_specs.append(const(final_w))
        args.append(final_w)
    y, tail = pl.pallas_call(
        functools.partial(_mixer_prompt_kernel, groups=groups, convs=convs, gated_ffn=gated_ffn,
                          final=final, tm=tm, tf=tf, f=f),
        grid=(b, s // tm),
        in_specs=in_specs,
        out_specs=[pl.BlockSpec((1, tm, d), lambda i, t: (i, t, 0)),
                   pl.BlockSpec((1, 1, 8, convs * f), lambda i, t: (i, t, 0, 0))],
        out_shape=[jax.ShapeDtypeStruct((b, s, d), F32), jax.ShapeDtypeStruct((b, s // tm, 8, convs * f), F32)],
        scratch_shapes=[pltpu.VMEM((tm, f), BF16), pltpu.VMEM((2 * convs, tm + 8, tf), F32),
                        pltpu.VMEM((8, convs * f), F32)],
        compiler_params=_cparams(("parallel", "arbitrary"), 56),
        name="ffn_prompt" if gated_ffn else "sconv_prompt",
    )(*args)
    return y, tail[:, -1, 8 - (CONV_W - 1):, :]


def _mixer_sample_kernel(*refs, groups, convs, gated_ffn, final):
    x_ref, nw_ref = refs[0:2]
    up = refs[2:2 + groups]
    cw = refs[2 + groups:2 + groups + convs]
    st = refs[2 + groups + convs:2 + groups + 2 * convs]
    down_ref = refs[2 + groups + 2 * convs]
    pos = 3 + groups + 2 * convs
    fw_ref = refs[pos] if final else None
    pos += int(final)
    y_ref = refs[pos]
    news = refs[pos + 1:pos + 1 + convs]
    xn_ref, acc_ref = refs[pos + 1 + convs:]
    j = pl.program_id(0)

    @pl.when(j == 0)
    def _():
        xn_ref[...] = _rms(x_ref[...], nw_ref[...]).astype(BF16)
        acc_ref[...] = jnp.zeros_like(acc_ref)

    xn = xn_ref[...]
    u = [_dot(xn, w[...]) for w in up]
    conv_in = [u[0], u[1]] if gated_ffn else [u[1] * u[0]]
    conv_out = []
    for c, z in enumerate(conv_in):
        w = cw[c][...]
        conv_out.append(w[0:1] * st[c][:, 0, :] + w[1:2] * st[c][:, 1, :] + w[2:3] * z)
        news[c][...] = z
    if gated_ffn:
        act = conv_out[0] * jax.nn.sigmoid(conv_out[0]) * conv_out[1]
    else:
        act = u[2] * conv_out[0]
    acc_ref[...] += _dot(act.astype(BF16), down_ref[...])

    @pl.when(j == pl.num_programs(0) - 1)
    def _():
        y = x_ref[...] + acc_ref[...]
        y_ref[...] = _rms(y, fw_ref[...]) if final else y


def _mixer_sample(x, nw, w_up, conv_w, w_down, state, gated_ffn, final_w=None, tf=256):
    bd, d = x.shape
    f = w_down.shape[0]
    groups = w_up.shape[1] // f
    convs = conv_w.shape[1] // f
    nf = f // tf
    final = final_w is not None
    in_specs = [pl.BlockSpec((bd, d), lambda j: (0, 0)), pl.BlockSpec(nw.shape, lambda j: (0, 0))]
    in_specs += [pl.BlockSpec((d, tf), lambda j, g=g: (0, g * nf + j)) for g in range(groups)]
    in_specs += [pl.BlockSpec((CONV_W, tf), lambda j, c=c: (0, c * nf + j)) for c in range(convs)]
    in_specs += [pl.BlockSpec((bd, CONV_W - 1, tf), lambda j, c=c: (0, 0, c * nf + j)) for c in range(convs)]
    in_specs += [pl.BlockSpec((tf, d), lambda j: (j, 0))]
    args = [x, nw] + [w_up] * groups + [conv_w] * convs + [state] * convs + [w_down]
    if final:
        in_specs.append(pl.BlockSpec(final_w.shape, lambda j: (0, 0)))
        args.append(final_w)
    out_specs = [pl.BlockSpec((bd, d), lambda j: (0, 0))]
    out_specs += [pl.BlockSpec((bd, tf), lambda j: (0, j))] * convs
    out_shape = [jax.ShapeDtypeStruct((bd, d), F32)] + [jax.ShapeDtypeStruct((bd, f), F32)] * convs
    res = pl.pallas_call(
        functools.partial(_mixer_sample_kernel, groups=groups, convs=convs, gated_ffn=gated_ffn, final=final),
        grid=(nf,),
        in_specs=in_specs,
        out_specs=out_specs,
        out_shape=out_shape,
        scratch_shapes=[pltpu.VMEM((bd, d), BF16), pltpu.VMEM((bd, d), F32)],
        compiler_params=_cparams(("arbitrary",), 32),
        name="ffn_sample" if gated_ffn else "sconv_sample",
    )(*args)
    new = jnp.concatenate(res[1:], axis=-1)
    return res[0], jnp.concatenate([state[:, 1:], new[:, None, :]], axis=1)


def _pad_in_weight(w):
    c = HG_W + QB_W + 3 * KV_W
    per = 3 * B_GROUP
    gates = [jnp.pad(w[:, c + h * per:c + (h + 1) * per], ((0, 0), (0, GATE_PAD - per)))
             for h in range(B_KV_HEADS)]
    return jnp.concatenate([w[:, :c]] + gates, axis=1).astype(BF16)


def _per_head_rows(z, width):
    bd = z.shape[0]
    z = z.reshape(bd, B_KV_HEADS, B_GROUP, -1)
    return jnp.pad(z, ((0, 0), (0, 0), (0, 8 - B_GROUP), (0, width - z.shape[-1])))


def kernel(x_prompt, x_sample, cache_cmp_kv, cache_slc_kv, state_win_kv, state_hgrn, state_sconv, state_ffn,
           page_table, norm_mix, norm_ffn, norm_final, ab_w_in, ab_w_out, hgrn_lb_logits, hgrn_norm,
           cmp_pe, cmp_w1, cmp_w2, c_w_in, c_conv, c_w_out, ffn_up, ffn_conv, ffn_down):
    b, s, d = x_prompt.shape
    bd, t, _ = x_sample.shape
    assert t == 1 and d == D_MODEL
    depth = norm_mix.shape[0]
    n_a = ab_w_in.shape[0]
    wb = state_win_kv.shape[2]
    n_pages = page_table.shape[1]
    past = n_pages * PAGE_SIZE
    assert wb == WINDOW and past % (PAGES_PER_STEP * PAGE_SIZE) == 0 and s % 1024 == 0 and s >= wb
    hh = jnp.arange(1, B_HEADS + 1, dtype=F32)
    slopes = 2.0 ** (-8.0 * hh / B_HEADS)
    lb_logits = hgrn_lb_logits.astype(F32)
    row2 = lambda v: v.reshape(1, -1).astype(F32)

    xp = x_prompt.reshape(b * s, d)
    xs = x_sample.reshape(bd, d)
    outs_p = {k: [] for k in ("cmp", "slc", "win", "hg", "sc", "ff")}
    outs_s = {k: [] for k in ("cmp", "slc", "win", "hg", "sc", "ff")}
    y_prompt = y_sample = None
    for l in range(depth):
        last = l == depth - 1
        if l % 2 == 0:
            la = l // 2
            w_in = _pad_in_weight(ab_w_in[la])
            w_out = ab_w_out[la].astype(BF16)
            nw = row2(norm_mix[l])
            hnw = row2(hgrn_norm[la])
            pe_b, w1bd, w2_packed = _cmp_weights(cmp_pe[la], cmp_w1[la], cmp_w2[la], packed=True)
            _, _, w2_plain = _cmp_weights(cmp_pe[la], cmp_w1[la], cmp_w2[la], packed=False)
            hg, qb, gb, kc, kct, kslt, kwt, sk, sv, wk, wv = _inproj(xp, nw, w_in, tm=512, seq=s)
            sk, sv, wk, wv = [z.reshape(B_KV_HEADS, b, s, -1) for z in (sk, sv, wk, wv)]
            oa, hst = _hgrn_prompt(hg, lb_logits, hnw, la, b, s)
            ck, cv = _compress_prompt(kc.reshape(b, 2 * s, KV_W // 2), pe_b, w1bd, w2_packed)
            n_cmp = s // D_CMP
            nblk = s // L_SLC
            cover = _cover_matrix(n_cmp, nblk, nblk)
            ob = _nsa_prompt(slopes, qb.reshape(b, s, QB_W), gb.reshape(b, s, -1), ck, cv, sk, sv, wk, wv, cover)
            xp = _outproj(xp, oa, ob.reshape(b * s, QB_W), w_out, tm=512)
            kv6 = lambda z, n, tt: z.reshape(n, tt, 2, B_KV_HEADS, B_HD)
            from_t = lambda z: z.reshape(b, 2, B_KV_HEADS, B_HD, -1).transpose(0, 4, 1, 2, 3)
            outs_p["cmp"].append(from_t(kct))
            outs_p["slc"].append(from_t(kslt))
            outs_p["win"].append(from_t(kwt[:, :, s - wb:]))
            outs_p["hg"].append(hst)
            hg, qb, gb, kc, ksl, kw = _inproj(xs, nw, w_in, tm=bd)
            oa, hst = _hgrn_sample(hg, state_hgrn[la], lb_logits, hnw, la)
            pre = _cmp_pages(page_table, cache_cmp_kv, la, w1bd)
            n_cmp = pre.shape[1]
            nblk = past // L_SLC + 1
            cover = _cover_matrix(n_cmp, nblk, -(-nblk // 128) * 128)
            q8 = jnp.stack([jnp.pad(_per_head_rows(qb, B_HD)[:, h], ((0, 0), (0, 0), (h * B_HD, KV_W - (h + 1) * B_HD)))
                            for h in range(B_KV_HEADS)], axis=1)
            idx, phys, o_cmp = _nsa_sample_select(slopes, pre, kc.reshape(bd, 1, KV_W), q8, page_table, pe_b, w1bd,
                                                  w2_plain, cover, past)
            picks = lambda z: z[:, :B_KV_HEADS, :N_SEL].reshape(bd, B_KV_HEADS * N_SEL)
            idx, phys = picks(idx), picks(phys)
            gate8 = _per_head_rows(jnp.concatenate([gb[:, h * GATE_PAD:h * GATE_PAD + 3 * B_GROUP]
                                                    for h in range(B_KV_HEADS)], axis=1), 128)
            win_state = state_win_kv[la].transpose(0, 2, 3, 4, 1).reshape(bd, KV_W, wb)
            o8 = _nsa_sample_attend(phys, idx, slopes, cache_slc_kv, la, q8, ksl.reshape(bd, 1, KV_W),
                                    kw.reshape(bd, 1, KV_W), win_state, gate8, o_cmp, past)
            ob = jnp.concatenate([o8[:, h, :B_GROUP, 2 * B_HD + h * B_HD:2 * B_HD + (h + 1) * B_HD]
                                  .reshape(bd, B_GROUP * B_HD) for h in range(B_KV_HEADS)], axis=1)
            xs = _outproj(xs, oa, ob.astype(BF16), w_out, tm=bd)
            outs_s["cmp"].append(kv6(kc, bd, t))
            outs_s["slc"].append(kv6(ksl, bd, t))
            outs_s["win"].append(jnp.concatenate([state_win_kv[la], kv6(kw, bd, t)], axis=1)[:, t:])
            outs_s["hg"].append(hst)
        else:
            lc = l // 2
            xp3, tail = _mixer_prompt(xp.reshape(b, s, d), row2(norm_mix[l]), c_w_in[lc].astype(BF16),
                                      c_conv[lc], c_w_out[lc].astype(BF16), gated_ffn=False)
            xp = xp3.reshape(b * s, d)
            outs_p["sc"].append(tail)
            xs, new = _mixer_sample(xs, row2(norm_mix[l]), c_w_in[lc].astype(BF16), c_conv[lc],
                                    c_w_out[lc].astype(BF16), state_sconv[lc], gated_ffn=False)
            outs_s["sc"].append(new)
        fw = row2(norm_final) if last else None
        xp3, tail = _mixer_prompt(xp.reshape(b, s, d), row2(norm_ffn[l]), ffn_up[l].astype(BF16), ffn_conv[l],
                                  ffn_down[l].astype(BF16), gated_ffn=True, final_w=fw)
        xp = xp3.reshape(b * s, d)
        outs_p["ff"].append(tail)
        xs, new = _mixer_sample(xs, row2(norm_ffn[l]), ffn_up[l].astype(BF16), ffn_conv[l],
                                ffn_down[l].astype(BF16), state_ffn[l], gated_ffn=True, final_w=fw)
        outs_s["ff"].append(new)
    y_prompt = xp.reshape(b, s, d)
    y_sample = xs.reshape(bd, t, d)
    return (y_prompt, y_sample,
            jnp.stack(outs_p["cmp"], axis=2), jnp.stack(outs_p["slc"], axis=2), jnp.stack(outs_p["win"], axis=0),
            jnp.stack(outs_p["hg"], axis=0), jnp.stack(outs_p["sc"], axis=0), jnp.stack(outs_p["ff"], axis=0),
            jnp.stack(outs_s["cmp"], axis=2), jnp.stack(outs_s["slc"], axis=2), jnp.stack(outs_s["win"], axis=0),
            jnp.stack(outs_s["hg"], axis=0), jnp.stack(outs_s["sc"], axis=0), jnp.stack(outs_s["ff"], axis=0))
```

```python
import functools

import jax
import jax.numpy as jnp
from jax import lax
from jax.experimental import pallas as pl
from jax.experimental.pallas import tpu as pltpu

F32 = jnp.float32
BF16 = jnp.bfloat16
NEG_INF = float("-inf")

D_MODEL = 1024
PAGE_SIZE = 128
A_HEADS = 4
A_DK = 128
A_DV = 128
B_HEADS = 8
B_KV_HEADS = 2
B_GROUP = B_HEADS // B_KV_HEADS
B_HD = 64
L_CMP = 32
D_CMP = 16
CMP_HID = 64
L_SLC = 64
SLC_SHIFT = 6
N_SEL = 16
WINDOW = 512
SEL_FORCE = 1e9
CONV_W = 3
EPS = 1e-6
KV_W = 2 * B_KV_HEADS * B_HD
HG_W = 4 * A_HEADS * A_DK
QB_W = B_HEADS * B_HD
GATE_PAD = 128
IN_PAD = HG_W + QB_W + 3 * KV_W + B_KV_HEADS * GATE_PAD

V7X_VMEM_BYTES = 64 * 2**20


def _cparams(sem, vmem_mb):
    assert vmem_mb * 2**20 < V7X_VMEM_BYTES
    return pltpu.CompilerParams(dimension_semantics=sem, vmem_limit_bytes=vmem_mb * 2**20)


def _dot(a, b):
    return jnp.dot(a, b, preferred_element_type=F32)


def _dot_nt(a, b):
    return lax.dot_general(a, b, (((1,), (1,)), ((), ())), preferred_element_type=F32)


def _dot_tn(a, b):
    return lax.dot_general(a, b, (((0,), (0,)), ((), ())), preferred_element_type=F32)


def _split3(x):
    hi = x.astype(BF16)
    r1 = x - hi.astype(F32)
    mid = r1.astype(BF16)
    lo = (r1 - mid.astype(F32)).astype(BF16)
    return hi, mid, lo


def _dot3_lhs(x, w):
    hi, mid, lo = _split3(x)
    return _dot(hi, w) + _dot(mid, w) + _dot(lo, w)


def _dot3_rhs(w, x):
    hi, mid, lo = _split3(x)
    return _dot(w, hi) + _dot(w, mid) + _dot(w, lo)


def _rms(x, w):
    return x * lax.rsqrt(jnp.mean(x * x, axis=-1, keepdims=True) + EPS) * w


def _masked_softmax(s, mask):
    s = jnp.where(mask, s, NEG_INF)
    m = jnp.max(s, axis=-1, keepdims=True)
    m = jnp.where(m > NEG_INF, m, 0.0)
    e = jnp.exp(s - m)
    d = jnp.sum(e, axis=-1, keepdims=True)
    return e * (1.0 / jnp.where(d > 0, d, 1.0))


def _iota(shape, dim):
    return lax.broadcasted_iota(jnp.int32, shape, dim)


def _top_blocks(score, jb, n_sel):
    big = float(score.shape[-1])
    jbf = jb.astype(F32)

    def body(_, carry):
        sc, sel = carry
        m = jnp.max(sc, axis=-1, keepdims=True)
        idx = jnp.min(jnp.where(sc == m, jbf, big), axis=-1, keepdims=True)
        pick = jbf == idx
        return jnp.where(pick, NEG_INF, sc), jnp.where(pick, 1.0, sel)

    _, sel = lax.fori_loop(0, n_sel, body, (score, jnp.zeros_like(score)))
    return sel


def _attention_rows(kv, pos, with_blocks):
    tm = kv.shape[0]
    lane = _iota((tm, 128), 1)
    lo = lane < B_HD
    pos_cols = jnp.where(lane == B_HD, (pos >> 7).astype(F32),
                         jnp.where(lane == B_HD + 1, (pos & 127).astype(F32), 0.0))
    k_pair, v_pair = kv[:, 0:128], kv[:, 128:256]
    k_swap, v_swap = pltpu.roll(k_pair, B_HD, axis=1), pltpu.roll(v_pair, B_HD, axis=1)
    onehot = ((pos >> SLC_SHIFT) == lane).astype(BF16)
    keys, values = [], []
    for h, (k, v_lo, v_hi) in enumerate(((k_pair, v_pair, v_swap), (k_swap, v_swap, v_pair))):
        key = jnp.where(lo, k, pos_cols).astype(BF16)
        keys.append(jnp.concatenate([key, onehot], axis=1) if with_blocks else key)
        values.append(jnp.where(lo, v_lo, v_hi).astype(BF16))
    return keys, values


def _inproj_kernel(x_ref, nw_ref, w_ref, hg_ref, qb_ref, gb_ref, kc_ref, *refs, seq):
    xn = _rms(x_ref[...], nw_ref[...]).astype(BF16)

    def proj(c0, n):
        return _dot(xn, w_ref[:, c0:c0 + n])

    hg_ref[...] = proj(0, HG_W)
    qb_ref[...] = (proj(HG_W, QB_W) * (B_HD ** -0.5)).astype(BF16)
    c = HG_W + QB_W
    kc = proj(c, KV_W)
    ksl = proj(c + KV_W, KV_W)
    kw = proj(c + 2 * KV_W, KV_W)
    kc_ref[...] = kc
    gb_ref[...] = proj(c + 3 * KV_W, B_KV_HEADS * GATE_PAD)
    if seq is None:
        ksl_ref, kw_ref = refs
        ksl_ref[...] = ksl
        kw_ref[...] = kw
    else:
        kct_ref, kslt_ref, kwt_ref, sk_ref, sv_ref, wk_ref, wv_ref = refs
        kct_ref[0] = kc.T
        kslt_ref[0] = ksl.T
        kwt_ref[0] = kw.T
        tm = x_ref.shape[0]
        pos = (pl.program_id(0) * tm + _iota((tm, 1), 0)) & (seq - 1)
        for kv, k_ref, v_ref, with_blocks in ((ksl, sk_ref, sv_ref, True), (kw, wk_ref, wv_ref, False)):
            keys, values = _attention_rows(kv, pos, with_blocks)
            for h in range(B_KV_HEADS):
                k_ref[h] = keys[h]
                v_ref[h] = values[h]


def _inproj(x, nw, w_pad, tm, seq=None):
    m = x.shape[0]
    row = lambda n: pl.BlockSpec((tm, n), lambda i: (i, 0))
    full = lambda a: pl.BlockSpec(a.shape, lambda i: (0,) * a.ndim)
    widths = (HG_W, QB_W, B_KV_HEADS * GATE_PAD, KV_W)
    dtypes = (F32, BF16, F32, F32)
    out_specs = [row(n) for n in widths]
    out_shape = [jax.ShapeDtypeStruct((m, n), d) for n, d in zip(widths, dtypes)]
    if seq is None:
        out_specs += [row(KV_W)] * 2
        out_shape += [jax.ShapeDtypeStruct((m, KV_W), F32)] * 2
    else:
        assert seq // L_SLC == 128 and seq % tm == 0 and seq & (seq - 1) == 0
        nt = seq // tm
        out_specs += [pl.BlockSpec((1, KV_W, tm), lambda i: (i // nt, 0, i % nt))] * 3
        out_shape += [jax.ShapeDtypeStruct((m // seq, KV_W, seq), F32)] * 3
        for n in (256, 128, 128, 128):
            out_specs.append(pl.BlockSpec((B_KV_HEADS, tm, n), lambda i: (0, i, 0)))
            out_shape.append(jax.ShapeDtypeStruct((B_KV_HEADS, m, n), BF16))
    return pl.pallas_call(
        functools.partial(_inproj_kernel, seq=seq),
        grid=(m // tm,),
        in_specs=[row(D_MODEL), full(nw), full(w_pad)],
        out_specs=out_specs,
        out_shape=out_shape,
        compiler_params=_cparams(("parallel",), 48),
        name="inproj",
    )(x, nw, w_pad)


def _lower_bound(logit_ref, la):
    lg = logit_ref[...]
    e = jnp.exp(lg - jnp.max(lg, axis=0, keepdims=True))
    return jnp.sum(e[:la + 1], axis=0, keepdims=True) / jnp.sum(e, axis=0, keepdims=True)


def _hgrn_post(o, nw, ga):
    o = o * lax.rsqrt(jnp.mean(o * o, axis=-1, keepdims=True) + EPS)
    return o * nw * jax.nn.sigmoid(ga)


def _hgrn_prompt_kernel(hg_ref, lg_ref, nw_ref, oa_ref, st_ref, s_ref, *, la, chunk, sub):
    ci = pl.program_id(1)

    @pl.when(ci == 0)
    def _():
        s_ref[...] = jnp.zeros_like(s_ref)

    lb = _lower_bound(lg_ref, la)
    tri = (_iota((chunk, chunk), 0) >= _iota((chunk, chunk), 1)).astype(BF16)
    srow = _iota((sub, 1), 0)
    lane = _iota((sub, chunk), 1)
    for h in range(A_HEADS):
        hs = slice(h * A_DK, (h + 1) * A_DK)
        q = hg_ref[:, h * A_DK:(h + 1) * A_DK]
        fl = hg_ref[:, 512 + h * A_DK:512 + (h + 1) * A_DK]
        v = hg_ref[:, 1024 + h * A_DV:1024 + (h + 1) * A_DV]
        ga = hg_ref[:, 1536 + h * A_DV:1536 + (h + 1) * A_DV]
        lbh = lb[:, hs]
        f = lbh + (1.0 - lbh) * jax.nn.sigmoid(fl)
        a = _dot3_rhs(tri, jnp.log(f))
        k = 1.0 - f
        st = s_ref[h]
        vb = v.astype(BF16)
        o = _dot_nt((q * jnp.exp(a)).astype(BF16), st.astype(BF16))
        rows = []
        for i in range(chunk // sub):
            r0 = i * sub
            qi, ai = q[r0:r0 + sub], a[r0:r0 + sub]
            att = jnp.zeros((sub, chunk), F32)
            for s in range(sub):
                dec = jnp.exp(jnp.where(srow >= s, ai - a[r0 + s:r0 + s + 1], NEG_INF))
                col = jnp.sum(qi * dec * k[r0 + s:r0 + s + 1], axis=-1, keepdims=True)
                att = jnp.where(lane == r0 + s, col, att)
            if i > 0:
                ref_a = a[r0 - 1:r0]
                qt = qi * jnp.exp(ai - ref_a)
                kt = jnp.concatenate([k[:r0] * jnp.exp(ref_a - a[:r0]), jnp.zeros((chunk - r0, A_DK), F32)], axis=0)
                att = att + _dot_nt(qt.astype(BF16), kt.astype(BF16))
            rows.append(att)
        att = jnp.concatenate(rows, axis=0)
        o = o + _dot(att.astype(BF16), vb)
        a_end = a[chunk - 1:chunk]
        khat = k * jnp.exp(a_end - a)
        s_ref[h] = st * jnp.exp(a_end) + _dot_tn(vb, khat.astype(BF16))
        oa_ref[:, hs] = _hgrn_post(o, nw_ref[:, hs], ga).astype(BF16)

    @pl.when(ci == pl.num_programs(1) - 1)
    def _():
        for h in range(A_HEADS):
            st_ref[0, h] = s_ref[h].T


def _hgrn_prompt(hg, lb_logits, nw, la, b, s, chunk=128, sub=16):
    n = s // chunk
    full = lambda a: pl.BlockSpec(a.shape, lambda i, j: (0,) * a.ndim)
    return pl.pallas_call(
        functools.partial(_hgrn_prompt_kernel, la=la, chunk=chunk, sub=sub),
        grid=(b, n),
        in_specs=[pl.BlockSpec((chunk, HG_W), lambda i, j: (i * n + j, 0)), full(lb_logits), full(nw)],
        out_specs=[pl.BlockSpec((chunk, A_HEADS * A_DV), lambda i, j: (i * n + j, 0)),
                   pl.BlockSpec((1, A_HEADS, A_DK, A_DV), lambda i, j: (i, 0, 0, 0))],
        out_shape=[jax.ShapeDtypeStruct((b * s, A_HEADS * A_DV), BF16),
                   jax.ShapeDtypeStruct((b, A_HEADS, A_DK, A_DV), F32)],
        scratch_shapes=[pltpu.VMEM((A_HEADS, A_DV, A_DK), F32)],
        compiler_params=_cparams(("parallel", "arbitrary"), 32),
        name="hgrn_prompt",
    )(hg, lb_logits, nw)


def _hgrn_sample_kernel(hg_ref, s0_ref, lg_ref, nw_ref, oa_ref, st_ref, *, la):
    lb = _lower_bound(lg_ref, la)
    eye = (_iota((A_DK, A_DK), 0) == _iota((A_DK, A_DK), 1)).astype(F32)

    def col(r):
        return jnp.sum(eye * r, axis=1, keepdims=True)

    for h in range(A_HEADS):
        hs = slice(h * A_DK, (h + 1) * A_DK)
        q = hg_ref[0, :, h * A_DK:(h + 1) * A_DK]
        fl = hg_ref[0, :, 512 + h * A_DK:512 + (h + 1) * A_DK]
        v = hg_ref[0, :, 1024 + h * A_DV:1024 + (h + 1) * A_DV]
        ga = hg_ref[0, :, 1536 + h * A_DV:1536 + (h + 1) * A_DV]
        lbh = lb[:, hs]
        f = lbh + (1.0 - lbh) * jax.nn.sigmoid(fl)
        sn = col(f) * s0_ref[0, h] + col(1.0 - f) * v
        st_ref[0, h] = sn
        o = jnp.sum(col(q) * sn, axis=0, keepdims=True)
        oa_ref[0, :, hs] = _hgrn_post(o, nw_ref[:, hs], ga).astype(BF16)


def _hgrn_sample(hg, s0, lb_logits, nw, la):
    bd = hg.shape[0]
    full = lambda a: pl.BlockSpec(a.shape, lambda i: (0,) * a.ndim)
    st_spec = pl.BlockSpec((1, A_HEADS, A_DK, A_DV), lambda i: (i, 0, 0, 0))
    oa, st = pl.pallas_call(
        functools.partial(_hgrn_sample_kernel, la=la),
        grid=(bd,),
        in_specs=[pl.BlockSpec((1, 1, HG_W), lambda i: (i, 0, 0)), st_spec, full(lb_logits), full(nw)],
        out_specs=[pl.BlockSpec((1, 1, A_HEADS * A_DV), lambda i: (i, 0, 0)), st_spec],
        out_shape=[jax.ShapeDtypeStruct((bd, 1, A_HEADS * A_DV), BF16),
                   jax.ShapeDtypeStruct(s0.shape, F32)],
        compiler_params=_cparams(("parallel",), 16),
        name="hgrn_sample",
    )(hg.reshape(bd, 1, HG_W), s0, lb_logits, nw)
    return oa.reshape(bd, A_HEADS * A_DV), st


def _cmp_weights(pe, w1, w2, packed):
    eye = jnp.eye(B_KV_HEADS, dtype=F32)
    w1r = w1.reshape(2, L_CMP, B_HD, CMP_HID)
    w1bd = jnp.einsum("kidm,hH->ikhdHm", w1r, eye).reshape(L_CMP, 2, KV_W // 2, KV_W // 2)
    pe_b = jnp.broadcast_to(pe.transpose(1, 0, 2)[:, :, None, :], (L_CMP, 2, B_KV_HEADS, B_HD))
    if not packed:
        w2bd = jnp.einsum("kmd,kK,hH->khmKHd", w2, eye, eye).reshape(KV_W, KV_W)
        return pe_b.reshape(L_CMP, KV_W), w1bd.astype(BF16), w2bd.astype(BF16)
    half = KV_W // 2
    zero = jnp.zeros((half, KV_W), F32)
    k_blk = jnp.concatenate([w2[0], jnp.zeros_like(w2[0])], axis=1)
    v_blk = jnp.concatenate([w2[1], w2[1]], axis=1)
    w2k = jnp.concatenate([jnp.einsum("ml,hH->hmHl", k_blk, eye).reshape(half, KV_W), zero], axis=0)
    w2v = jnp.concatenate([zero, jnp.einsum("ml,hH->hmHl", v_blk, eye).reshape(half, KV_W)], axis=0)
    return pe_b.reshape(L_CMP, KV_W), w1bd.astype(BF16), (w2k.astype(BF16), w2v.astype(BF16))


def _chunk_preacts(load, n, pe_ref, w1_ref):
    half = KV_W // 2
    p0 = [jnp.zeros((n, half), F32) for _ in range(2)]
    p1 = [jnp.zeros((n, half), F32) for _ in range(2)]
    for i in range(D_CMP):
        for kv in range(2):
            x = load(i, kv)
            ls = slice(kv * half, (kv + 1) * half)
            p0[kv] = p0[kv] + _dot((x + pe_ref[i:i + 1, ls]).astype(BF16), w1_ref[i, kv])
            p1[kv] = p1[kv] + _dot((x + pe_ref[D_CMP + i:D_CMP + i + 1, ls]).astype(BF16), w1_ref[D_CMP + i, kv])
    return jnp.concatenate(p0, axis=1), jnp.concatenate(p1, axis=1)


def _with_position(k, pos, lane):
    col = lane & 127
    return jnp.where(col == B_HD, (pos >> 7).astype(F32), jnp.where(col == B_HD + 1, (pos & 127).astype(F32), k))


def _compress_kernel(kc_ref, pe_ref, w1_ref, w2k_ref, w2v_ref, ck_ref, cv_ref):
    n = ck_ref.shape[1]
    p0, p1 = _chunk_preacts(lambda i, kv: kc_ref[0, pl.ds(2 * i + kv, n, stride=2 * D_CMP), :], n, pe_ref, w1_ref)
    pre = p0 + pltpu.roll(p1, n - 1, axis=0)
    hid = (pre * jax.nn.sigmoid(pre)).astype(BF16)
    cpos = _iota((n, 1), 0) * D_CMP + (L_CMP - 1)
    ck_ref[0] = _with_position(_dot(hid, w2k_ref[...]), cpos, _iota((n, KV_W), 1)).astype(BF16)
    cv_ref[0] = _dot(hid, w2v_ref[...]).astype(BF16)


def _compress_prompt(kc, pe_b, w1bd, w2kv):
    b, s2, _ = kc.shape
    n = s2 // (2 * D_CMP)
    full = lambda a: pl.BlockSpec(a.shape, lambda i: (0,) * a.ndim)
    out = pl.BlockSpec((1, n, KV_W), lambda i: (i, 0, 0))
    return pl.pallas_call(
        _compress_kernel,
        grid=(b,),
        in_specs=[pl.BlockSpec((1, s2, KV_W // 2), lambda i: (i, 0, 0)), full(pe_b), full(w1bd),
                  full(w2kv[0]), full(w2kv[1])],
        out_specs=[out, out],
        out_shape=[jax.ShapeDtypeStruct((b, n, KV_W), BF16)] * 2,
        compiler_params=_cparams(("parallel",), 48),
        name="compress_prompt",
    )(kc, pe_b, w1bd, *w2kv)


MASK_BIAS = 2.0 ** 100

def _top_blocks_t(score_t, n_sel):
    nblk = score_t.shape[0]
    blk = _iota(score_t.shape, 0).astype(F32)

    def body(_, carry):
        sc, sel = carry
        m = jnp.max(sc, axis=0, keepdims=True)
        idx = jnp.min(jnp.where(sc == m, blk, float(nblk)), axis=0, keepdims=True)
        pick = blk == idx
        return jnp.where(pick, NEG_INF, sc), jnp.where(pick, 1.0, sel)

    _, sel = lax.fori_loop(0, n_sel, body, (score_t, jnp.zeros_like(score_t)), unroll=True)
    return sel


def _nsa_prompt_kernel(sl_ref, q_ref, g_ref, ck_ref, cv_ref, sk_ref, sv_ref, wk_ref, wv_ref, cover_ref, o_ref,
                       m_ref, l_ref, acc_ref, sa_ref, sb_ref, tiles_ref, *, tq, tk):
    kvh = pl.program_id(1)
    q0 = pl.program_id(2) * tq
    rows = B_GROUP * tq
    lane = _iota((tq, 128), 1)
    lo = lane < B_HD
    qf = q_ref[0].astype(F32)
    left = []
    for g in range(B_GROUP):
        pair = qf[:, 128 * (g // 2):128 * (g // 2) + 128]
        if g % 2:
            pair = pltpu.roll(pair, B_HD, axis=1)
        slope = sl_ref[kvh * B_GROUP + g]
        pos_cols = jnp.where(lane == B_HD, slope * 128.0, jnp.where(lane == B_HD + 1, slope, 0.0))
        left.append(jnp.where(lo, pair, pos_cols).astype(BF16))
    ql = jnp.concatenate(left, axis=0)
    qpos = q0 + (_iota((rows, 1), 0) & (tq - 1))

    nc = ck_ref.shape[1]
    cpos = _iota((1, nc), 1) * D_CMP + (L_CMP - 1)
    span = WINDOW + tq
    start = pl.multiple_of(jnp.maximum(q0 - WINDOW, 0), 128)
    wpos = start + _iota((1, span), 1)
    s_cmp = _dot_nt(ql, ck_ref[0])
    s_win = _dot_nt(ql, wk_ref[0, 0, pl.ds(start, span), :])
    p = _masked_softmax(s_cmp, cpos <= qpos)
    pw = _masked_softmax(s_win, (wpos <= qpos) & (wpos > qpos - WINDOW))
    o_cmp = _dot(p.astype(BF16), cv_ref[0])
    o_win = _dot(pw.astype(BF16), wv_ref[0, 0, pl.ds(start, span), :])
    imp = p[0:tq]
    for g in range(1, B_GROUP):
        imp = imp + p[g * tq:(g + 1) * tq]

    nblk = cover_ref.shape[1]
    jb = _iota((1, nblk), 1)
    cur = (q0 + _iota((tq, 1), 0)) >> SLC_SHIFT
    forced = (jb == 0) | (jb == cur) | (jb == cur - 1)
    score = _dot3_lhs(imp, cover_ref[...])
    score = jnp.where(forced | (jb > cur), NEG_INF, score)
    n_forced = 3
    sel = jnp.where(forced, 1.0, _top_blocks_t(score.T, min(N_SEL, nblk) - n_forced).T)

    bias = ((sel - 1.0) * MASK_BIAS).astype(BF16)
    qs = jnp.concatenate([ql, jnp.concatenate([bias] * B_GROUP, axis=0)], axis=1)
    m_ref[...] = jnp.full_like(m_ref, NEG_INF)
    l_ref[...] = jnp.zeros_like(l_ref)
    acc_ref[...] = jnp.zeros_like(acc_ref)

    def scores(kt):
        return _dot_nt(qs, sk_ref[0, 0, pl.ds(pl.multiple_of(kt * tk, tk), tk), :])

    def update(s, kt, causal):
        k0 = pl.multiple_of(kt * tk, tk)
        if causal:
            s = jnp.where(k0 + _iota((1, tk), 1) <= qpos, s, NEG_INF)
        m_old = m_ref[...]
        m_new = jnp.maximum(m_old, jnp.max(s, axis=-1, keepdims=True))
        alpha = jnp.exp(m_old - m_new)
        e = jnp.exp(s - m_new)
        l_ref[...] = alpha * l_ref[...] + jnp.sum(e, axis=-1, keepdims=True)
        acc_ref[...] = alpha * acc_ref[...] + _dot(e.astype(BF16), sv_ref[0, 0, pl.ds(k0, tk), :])
        m_ref[...] = m_new

    n_diag = q0 // tk
    bpt = tk // L_SLC
    picked = jnp.max(sel, axis=0, keepdims=True)
    n_full = jnp.int32(0)
    for kt in range(nblk * L_SLC // tk):
        in_tile = (jb >= kt * bpt) & (jb < (kt + 1) * bpt)
        wanted = (jnp.max(jnp.where(in_tile, picked, 0.0)) > 0.5) & (kt < n_diag)
        tiles_ref[n_full] = jnp.int32(kt)
        n_full = n_full + wanted.astype(jnp.int32)
    tiles_ref[n_full] = n_diag

    sa_ref[...] = scores(tiles_ref[0])

    def tile_pair(i, carry):
        sb_ref[...] = scores(tiles_ref[2 * i + 1])
        update(sa_ref[...], tiles_ref[2 * i], False)
        sa_ref[...] = scores(tiles_ref[2 * i + 2])
        update(sb_ref[...], tiles_ref[2 * i + 1], False)
        return carry

    lax.fori_loop(0, n_full // 2, tile_pair, 0)

    @pl.when(n_full % 2 == 1)
    def _():
        sb_ref[...] = scores(n_diag)
        update(sa_ref[...], tiles_ref[n_full - 1], False)
        update(sb_ref[...], n_diag, True)

    @pl.when(n_full % 2 == 0)
    def _():
        update(sa_ref[...], n_diag, True)

    o_slc = acc_ref[...] * (1.0 / l_ref[...])

    gate = jax.nn.sigmoid(g_ref[0])
    outs = []
    for g in range(B_GROUP):
        rs = slice(g * tq, (g + 1) * tq)
        outs.append(gate[:, 3 * g:3 * g + 1] * o_cmp[rs] + gate[:, 3 * g + 1:3 * g + 2] * o_slc[rs]
                    + gate[:, 3 * g + 2:3 * g + 3] * o_win[rs])
    for j in range(B_GROUP // 2):
        o_ref[0, :, 128 * j:128 * j + 128] = jnp.where(lo, outs[2 * j], outs[2 * j + 1]).astype(BF16)


def _nsa_prompt(slopes, qb, gb, ck, cv, sk, sv, wk, wv, cover, tq=128, tk=512):
    b, s, _ = qb.shape
    half = QB_W // B_KV_HEADS
    rows = B_GROUP * tq
    assert cover.shape[1] == 128 and s % tk == 0 and tk % tq == 0
    seq = lambda a: pl.BlockSpec((1, 1) + a.shape[2:], lambda i, h, t, sl: (h, i, 0, 0))
    grid_spec = pltpu.PrefetchScalarGridSpec(
        num_scalar_prefetch=1,
        grid=(b, B_KV_HEADS, s // tq),
        in_specs=[
            pl.BlockSpec((1, tq, half), lambda i, h, t, sl: (i, t, h)),
            pl.BlockSpec((1, tq, GATE_PAD), lambda i, h, t, sl: (i, t, h)),
            pl.BlockSpec((1, ck.shape[1], 128), lambda i, h, t, sl: (i, 0, h)),
            pl.BlockSpec((1, cv.shape[1], 128), lambda i, h, t, sl: (i, 0, h)),
            seq(sk), seq(sv), seq(wk), seq(wv),
            pl.BlockSpec(cover.shape, lambda i, h, t, sl: (0, 0)),
        ],
        out_specs=pl.BlockSpec((1, tq, half), lambda i, h, t, sl: (i, t, h)),
        scratch_shapes=[pltpu.VMEM((rows, 1), F32), pltpu.VMEM((rows, 1), F32), pltpu.VMEM((rows, 128), F32),
                        pltpu.VMEM((rows, tk), F32), pltpu.VMEM((rows, tk), F32),
                        pltpu.SMEM((s // tk + 1,), jnp.int32)],
    )
    return pl.pallas_call(
        functools.partial(_nsa_prompt_kernel, tq=tq, tk=tk),
        grid_spec=grid_spec,
        out_shape=jax.ShapeDtypeStruct((b, s, QB_W), BF16),
        compiler_params=_cparams(("parallel", "parallel", "arbitrary"), 56),
        name="nsa_prompt",
    )(slopes, qb, gb, ck, cv, sk, sv, wk, wv, cover)


def _cover_matrix(n_cmp, n_blk, n_pad):
    ci = jnp.arange(n_cmp)[:, None] * D_CMP
    sj = jnp.arange(n_pad)[None, :] * L_SLC
    cov = (ci < sj + L_SLC) & (ci + L_CMP > sj) & (jnp.arange(n_pad)[None, :] < n_blk)
    return cov.astype(BF16)


PAGES_PER_STEP = 64
PAGE_GROUPS = 2


def _cmp_pages_kernel(pt_ref, *refs):
    pages = refs[:PAGES_PER_STEP]
    w_ref, out_ref = refs[PAGES_PER_STEP:PAGES_PER_STEP + 2]
    row_refs = refs[PAGES_PER_STEP + 2:]
    per_group = PAGES_PER_STEP // PAGE_GROUPS
    n = per_group * PAGE_SIZE // D_CMP
    half = KV_W // 2
    for grp, rows_ref in enumerate(row_refs):
        for k, pg in enumerate(pages[grp * per_group:(grp + 1) * per_group]):
            for kv in range(2):
                rows_ref[kv, k * PAGE_SIZE:(k + 1) * PAGE_SIZE, :] = pg[0, kv * half:(kv + 1) * half, :].T
        x = jnp.concatenate([rows_ref[kv, pl.ds(i, n, stride=D_CMP), :].astype(BF16)
                             for i in range(D_CMP) for kv in range(2)], axis=1)
        out_ref[0, grp * n:(grp + 1) * n, :] = _dot(x, w_ref[...])


def _pages_weight(w1bd):
    zero = jnp.zeros((D_CMP,) + w1bd.shape[2:], w1bd.dtype)

    def diag(a):
        return jnp.concatenate([jnp.concatenate([a[:, 0], zero], axis=2),
                                jnp.concatenate([zero, a[:, 1]], axis=2)], axis=1)

    w = jnp.concatenate([diag(w1bd[:D_CMP]), diag(w1bd[D_CMP:])], axis=2)
    return w.reshape(D_CMP * KV_W, 2 * KV_W)


def _feature_major_pages(cache):
    return cache.transpose(0, 2, 3, 4, 5, 1).reshape(cache.shape[0], -1, cache.shape[1])


def _cmp_pages(page_table, cache, li, w1bd):
    bd, n_pages = page_table.shape
    pool = _feature_major_pages(cache)
    steps = n_pages // PAGES_PER_STEP
    rows = PAGES_PER_STEP * PAGE_SIZE // D_CMP
    w = _pages_weight(w1bd)

    def page_spec(k):
        return pl.BlockSpec((1, KV_W, PAGE_SIZE), lambda i, j, pt: (pt[i, j * PAGES_PER_STEP + k], li, 0))

    grid_spec = pltpu.PrefetchScalarGridSpec(
        num_scalar_prefetch=1,
        grid=(bd, steps),
        in_specs=[page_spec(k) for k in range(PAGES_PER_STEP)] + [pl.BlockSpec(w.shape, lambda i, j, pt: (0, 0))],
        out_specs=pl.BlockSpec((1, rows, 2 * KV_W), lambda i, j, pt: (i, j, 0)),
        scratch_shapes=[pltpu.VMEM((2, PAGES_PER_STEP // PAGE_GROUPS * PAGE_SIZE, KV_W // 2), F32)] * PAGE_GROUPS,
    )
    return pl.pallas_call(
        _cmp_pages_kernel,
        grid_spec=grid_spec,
        out_shape=jax.ShapeDtypeStruct((bd, steps * rows, 2 * KV_W), F32),
        compiler_params=_cparams(("parallel", "arbitrary"), 48),
        name="cmp_pages",
    )(page_table, *([pool] * PAGES_PER_STEP), w)


def _nsa_sample_select_kernel(sl_ref, pre_ref, new_ref, q_ref, pt_ref, pe_ref, w1_ref, w2_ref, cover_ref,
                              idx_ref, phys_ref, ocmp_ref, tail_ref, *, past):
    half = KV_W // 2

    def row_act(i, r):
        r8 = jnp.broadcast_to(r, (8, KV_W)).astype(BF16)
        return jnp.concatenate([_dot(r8[:, kv * half:(kv + 1) * half], w1_ref[i, kv]) for kv in range(2)], axis=1)

    @pl.when(pl.program_id(0) == 0)
    def _():
        t = jnp.zeros((8, KV_W), F32)
        for i in range(L_CMP):
            t = t + row_act(i, pe_ref[i:i + 1])
        tail_ref[...] = t

    n = pre_ref.shape[1]
    p1 = pltpu.roll(pre_ref[0, :, KV_W:2 * KV_W], n - 1, axis=0)
    p1 = jnp.where(_iota((n, 1), 0) == n - 1, row_act(D_CMP, new_ref[0])[0:1], p1)
    pre = pre_ref[0, :, 0:KV_W] + p1 + tail_ref[0:1]
    comp = _dot((pre * jax.nn.sigmoid(pre)).astype(BF16), w2_ref[...]).astype(BF16)

    qpos = past
    cpos = _iota((1, n), 1) * D_CMP + (L_CMP - 1)
    cmask = cpos <= qpos
    cdist = (qpos - cpos).astype(F32)
    row8 = _iota((8, 1), 0)
    imp = jnp.zeros((8, n), F32)
    for kvh in range(B_KV_HEADS):
        slope = jnp.zeros((8, 1), F32)
        for g in range(B_GROUP):
            slope = jnp.where(row8 == g, sl_ref[kvh * B_GROUP + g], slope)
        p = _masked_softmax(_dot_nt(q_ref[0, kvh], comp) - slope * cdist, cmask)
        p = jnp.where(row8 < B_GROUP, p, 0.0)
        ocmp_ref[0, kvh] = _dot(p.astype(BF16), comp)
        imp = jnp.where(row8 == kvh, jnp.sum(p, axis=0, keepdims=True), imp)

    nblk = cover_ref.shape[1]
    jb = _iota((1, nblk), 1)
    cur = qpos // L_SLC
    score = _dot3_lhs(imp, cover_ref[...])
    score = jnp.where((jb == 0) | (jb == cur) | (jb == cur - 1), SEL_FORCE, score)
    score = jnp.where(jb > cur, NEG_INF, score)
    lane = _iota((1, 128), 1)
    jbf = jb.astype(F32)
    pages = pt_ref[0].astype(F32)
    bpp_shift = (PAGE_SIZE // L_SLC).bit_length() - 1

    def body(it, carry):
        sc, idx, phys = carry
        m = jnp.max(sc, axis=-1, keepdims=True)
        j = jnp.min(jnp.where(sc == m, jbf, float(nblk)), axis=-1, keepdims=True)
        logical = jnp.minimum(j.astype(jnp.int32), past // L_SLC - 1) >> bpp_shift
        page = jnp.sum(jnp.where(lane == logical, pages, 0.0), axis=-1, keepdims=True)
        return jnp.where(jbf == j, NEG_INF, sc), jnp.where(lane == it, j, idx), jnp.where(lane == it, page, phys)

    zero = jnp.zeros((8, 128), F32)
    _, idx, phys = lax.fori_loop(0, N_SEL, body, (score, zero, zero))
    idx_ref[0] = idx.astype(jnp.int32)
    phys_ref[0] = phys.astype(jnp.int32)


def _nsa_sample_select(slopes, pre, cmp_new, q8, page_table, pe_b, w1bd, w2bd, cover, past):
    bd = pre.shape[0]
    assert page_table.shape[1] == 128
    full = lambda a: pl.BlockSpec(a.shape, lambda i, sl: (0,) * a.ndim)
    picks = pl.BlockSpec((1, 8, 128), lambda i, sl: (i, 0, 0))
    grid_spec = pltpu.PrefetchScalarGridSpec(
        num_scalar_prefetch=1,
        grid=(bd,),
        in_specs=[
            pl.BlockSpec((1,) + pre.shape[1:], lambda i, sl: (i, 0, 0)),
            pl.BlockSpec((1, 1, KV_W), lambda i, sl: (i, 0, 0)),
            pl.BlockSpec((1, B_KV_HEADS, 8, KV_W), lambda i, sl: (i, 0, 0, 0)),
            pl.BlockSpec((1, 1, 128), lambda i, sl: (i, 0, 0)),
            full(pe_b), full(w1bd), full(w2bd), full(cover),
        ],
        out_specs=[picks, picks, pl.BlockSpec((1, B_KV_HEADS, 8, KV_W), lambda i, sl: (i, 0, 0, 0))],
        scratch_shapes=[pltpu.VMEM((8, KV_W), F32)],
    )
    return pl.pallas_call(
        functools.partial(_nsa_sample_select_kernel, past=past),
        grid_spec=grid_spec,
        out_shape=[jax.ShapeDtypeStruct((bd, 8, 128), jnp.int32)] * 2
        + [jax.ShapeDtypeStruct((bd, B_KV_HEADS, 8, KV_W), F32)],
        compiler_params=_cparams(("arbitrary",), 48),
        name="nsa_sample_select",
    )(slopes, pre, cmp_new, q8, page_table.reshape(bd, 1, 128), pe_b, w1bd, w2bd, cover)


def _nsa_sample_attend_kernel(pt_ref, ix_ref, sl_ref, *refs, past, nb_past):
    n_blocks = B_KV_HEADS * N_SEL
    pages = refs[:n_blocks]
    q_ref, slc_new_ref, win_new_ref, win_ref, g_ref, ocmp_ref, o_ref = refs[n_blocks:]
    b = pl.program_id(0)
    bpp = PAGE_SIZE // L_SLC
    row8 = _iota((8, 1), 0)
    page_row = _iota((1, PAGE_SIZE), 1)
    wb = win_ref.shape[2]
    win = win_ref[0].astype(BF16)
    slc_new = slc_new_ref[0]
    win_new = win_new_ref[0]
    for kvh in range(B_KV_HEADS):
        q = q_ref[0, kvh]
        qf = q.astype(F32)
        slope = jnp.zeros((8, 1), F32)
        for g in range(B_GROUP):
            slope = jnp.where(row8 == g, sl_ref[kvh * B_GROUP + g], slope)
        kv = jnp.concatenate([pages[kvh * N_SEL + n][0].astype(BF16) for n in range(N_SEL)], axis=1)
        kpos, blk, live = [], [], []
        j_max = ix_ref[b, kvh * N_SEL]
        for n in range(N_SEL):
            j = ix_ref[b, kvh * N_SEL + n]
            j_max = jnp.maximum(j_max, j)
            jc = jnp.minimum(j, nb_past - 1)
            kpos.append((jc // bpp) * PAGE_SIZE + page_row)
            blk.append(jnp.broadcast_to(jc, (1, PAGE_SIZE)))
            live.append(jnp.broadcast_to((j < nb_past).astype(jnp.int32), (1, PAGE_SIZE)))
        kpos, blk, live = [jnp.concatenate(z, axis=1) for z in (kpos, blk, live)]
        s = _dot(q, kv) - slope * (past - kpos).astype(F32)
        s = jnp.where(((kpos >> SLC_SHIFT) == blk) & (live > 0), s, NEG_INF)
        s_new = jnp.where(j_max >= nb_past, jnp.sum(qf * slc_new, axis=-1, keepdims=True), NEG_INF)
        m = jnp.maximum(jnp.max(s, axis=-1, keepdims=True), s_new)
        m = jnp.where(m > NEG_INF, m, 0.0)
        e = jnp.exp(s - m)
        e_new = jnp.exp(s_new - m)
        l = jnp.sum(e, axis=-1, keepdims=True) + e_new
        o_slc = (_dot_nt(e.astype(BF16), kv) + e_new * slc_new) * (1.0 / jnp.where(l > 0, l, 1.0))
        wpos = past - wb + _iota((1, wb), 1)
        s1 = _dot(q, win) - slope * (past - wpos).astype(F32)
        s1 = jnp.where(wpos > past - WINDOW, s1, NEG_INF)
        s2 = jnp.sum(qf * win_new, axis=-1, keepdims=True)
        mw = jnp.maximum(jnp.max(s1, axis=-1, keepdims=True), s2)
        e1 = jnp.exp(s1 - mw)
        e2 = jnp.exp(s2 - mw)
        dw = jnp.sum(e1, axis=-1, keepdims=True) + e2
        o_win = (_dot_nt(e1.astype(BF16), win) + e2 * win_new) * (1.0 / dw)
        gate = jax.nn.sigmoid(g_ref[0, kvh])
        o_ref[0, kvh] = gate[:, 0:1] * ocmp_ref[0, kvh] + gate[:, 1:2] * o_slc + gate[:, 2:3] * o_win


def _nsa_sample_attend(phys, idx, slopes, cache, li, q8, slc_new, win_new, win_state, gate8, o_cmp, past):
    bd = q8.shape[0]
    bpp = PAGE_SIZE // L_SLC
    nb_past = past // L_SLC
    n_blocks = B_KV_HEADS * N_SEL
    pool = _feature_major_pages(cache)

    def blk_spec(k):
        return pl.BlockSpec((1, KV_W, PAGE_SIZE), lambda i, ph, ix, sl: (ph[i, k], li, 0))

    row = lambda: pl.BlockSpec((1, 1, KV_W), lambda i, pt, ix, sl: (i, 0, 0))
    per_head = lambda w: pl.BlockSpec((1, B_KV_HEADS, 8, w), lambda i, pt, ix, sl: (i, 0, 0, 0))
    grid_spec = pltpu.PrefetchScalarGridSpec(
        num_scalar_prefetch=3,
        grid=(bd,),
        in_specs=[blk_spec(k) for k in range(n_blocks)] + [
            per_head(KV_W), row(), row(),
            pl.BlockSpec((1,) + win_state.shape[1:], lambda i, pt, ix, sl: (i, 0, 0)),
            per_head(128), per_head(KV_W)],
        out_specs=per_head(KV_W),
    )
    return pl.pallas_call(
        functools.partial(_nsa_sample_attend_kernel, past=past, nb_past=nb_past),
        grid_spec=grid_spec,
        out_shape=jax.ShapeDtypeStruct((bd, B_KV_HEADS, 8, KV_W), F32),
        compiler_params=_cparams(("arbitrary",), 48),
        name="nsa_sample_attend",
    )(phys, idx, slopes, *([pool] * n_blocks), q8, slc_new, win_new, win_state, gate8, o_cmp)


def _outproj_kernel(x_ref, a_ref, b_ref, w_ref, y_ref):
    na = a_ref.shape[1]
    y_ref[...] = x_ref[...] + _dot(a_ref[...], w_ref[0:na]) + _dot(b_ref[...], w_ref[na:])


def _outproj(x, oa, ob, w, tm):
    m = x.shape[0]
    row = lambda n: pl.BlockSpec((tm, n), lambda i: (i, 0))
    return pl.pallas_call(
        _outproj_kernel,
        grid=(m // tm,),
        in_specs=[row(D_MODEL), row(oa.shape[1]), row(ob.shape[1]), pl.BlockSpec(w.shape, lambda i: (0, 0))],
        out_specs=row(D_MODEL),
        out_shape=jax.ShapeDtypeStruct((m, D_MODEL), F32),
        compiler_params=_cparams(("parallel",), 32),
        name="outproj",
    )(x, oa, ob, w)


def _mixer_prompt_kernel(*refs, groups, convs, gated_ffn, final, tm, tf, f):
    x_ref, nw_ref, up_ref, cw_ref, down_ref = refs[0:5]
    pos = 5
    fw_ref = refs[pos] if final else None
    pos += int(final)
    y_ref, tail_ref = refs[pos], refs[pos + 1]
    act_ref, ubuf_ref, carry_ref = refs[pos + 2:]

    @pl.when(pl.program_id(1) == 0)
    def _():
        carry_ref[...] = jnp.zeros_like(carry_ref)

    x = x_ref[0]
    xn = _rms(x, nw_ref[...]).astype(BF16)
    for j in range(f // tf):
        u = [_dot(xn, up_ref[:, g * f + j * tf:g * f + (j + 1) * tf]) for g in range(groups)]
        conv_in = [u[0], u[1]] if gated_ffn else [u[1] * u[0]]
        conv_out = []
        for c, z in enumerate(conv_in):
            cols = slice(c * f + j * tf, c * f + (j + 1) * tf)
            slot = (j % 2) * convs + c
            ubuf_ref[slot, 0:8] = carry_ref[:, cols]
            ubuf_ref[slot, 8:tm + 8] = z
            w = cw_ref[:, cols]
            conv_out.append(w[0:1] * ubuf_ref[slot, pl.ds(6, tm), :] + w[1:2] * ubuf_ref[slot, pl.ds(7, tm), :]
                            + w[2:3] * z)
            last = z[tm - 8:tm]
            carry_ref[:, cols] = last
            tail_ref[0, 0, :, cols] = last
        if gated_ffn:
            act = conv_out[0] * jax.nn.sigmoid(conv_out[0]) * conv_out[1]
        else:
            act = u[2] * conv_out[0]
        act_ref[:, j * tf:(j + 1) * tf] = act.astype(BF16)
    y = x + _dot(act_ref[...], down_ref[...])
    y_ref[0] = _rms(y, fw_ref[...]) if final else y


def _mixer_prompt(x, nw, w_up, conv_w, w_down, gated_ffn, final_w=None, tm=512, tf=256):
    b, s, d = x.shape
    f = w_down.shape[0]
    groups = w_up.shape[1] // f
    convs = conv_w.shape[1] // f
    final = final_w is not None
    const = lambda a: pl.BlockSpec(a.shape, lambda i, t: (0,) * a.ndim, pipeline_mode=pl.Buffered(1))
    in_specs = [pl.BlockSpec((1, tm, d), lambda i, t: (i, t, 0)), const(nw), const(w_up), const(conv_w),
                const(w_down)]
    args = [x, nw, w_up, conv_w, w_down]
    if final:
        in_specs.append(const(final_w))
        args.append(final_w)
    y, tail = pl.pallas_call(
        functools.partial(_mixer_prompt_kernel, groups=groups, convs=convs, gated_ffn=gated_ffn,
                          final=final, tm=tm, tf=tf, f=f),
        grid=(b, s // tm),
        in_specs=in_specs,
        out_specs=[pl.BlockSpec((1, tm, d), lambda i, t: (i, t, 0)),
                   pl.BlockSpec((1, 1, 8, convs * f), lambda i, t: (i, t, 0, 0))],
        out_shape=[jax.ShapeDtypeStruct((b, s, d), F32), jax.ShapeDtypeStruct((b, s // tm, 8, convs * f), F32)],
        scratch_shapes=[pltpu.VMEM((tm, f), BF16), pltpu.VMEM((2 * convs, tm + 8, tf), F32),
                        pltpu.VMEM((8, convs * f), F32)],
        compiler_params=_cparams(("parallel", "arbitrary"), 56),
        name="ffn_prompt" if gated_ffn else "sconv_prompt",
    )(*args)
    return y, tail[:, -1, 8 - (CONV_W - 1):, :]


def _mixer_sample_kernel(*refs, groups, convs, gated_ffn, final):
    x_ref, nw_ref = refs[0:2]
    up = refs[2:2 + groups]
    cw = refs[2 + groups:2 + groups + convs]
    st = refs[2 + groups + convs:2 + groups + 2 * convs]
    down_ref = refs[2 + groups + 2 * convs]
    pos = 3 + groups + 2 * convs
    fw_ref = refs[pos] if final else None
    pos += int(final)
    y_ref = refs[pos]
    news = refs[pos + 1:pos + 1 + convs]
    xn_ref, acc_ref = refs[pos + 1 + convs:]
    j = pl.program_id(0)

    @pl.when(j == 0)
    def _():
        xn_ref[...] = _rms(x_ref[...], nw_ref[...]).astype(BF16)
        acc_ref[...] = jnp.zeros_like(acc_ref)

    xn = xn_ref[...]
    u = [_dot(xn, w[...]) for w in up]
    conv_in = [u[0], u[1]] if gated_ffn else [u[1] * u[0]]
    conv_out = []
    for c, z in enumerate(conv_in):
        w = cw[c][...]
        conv_out.append(w[0:1] * st[c][:, 0, :] + w[1:2] * st[c][:, 1, :] + w[2:3] * z)
        news[c][...] = z
    if gated_ffn:
        act = conv_out[0] * jax.nn.sigmoid(conv_out[0]) * conv_out[1]
    else:
        act = u[2] * conv_out[0]
    acc_ref[...] += _dot(act.astype(BF16), down_ref[...])

    @pl.when(j == pl.num_programs(0) - 1)
    def _():
        y = x_ref[...] + acc_ref[...]
        y_ref[...] = _rms(y, fw_ref[...]) if final else y


def _mixer_sample(x, nw, w_up, conv_w, w_down, state, gated_ffn, final_w=None, tf=256):
    bd, d = x.shape
    f = w_down.shape[0]
    groups = w_up.shape[1] // f
    convs = conv_w.shape[1] // f
    nf = f // tf
    final = final_w is not None
    in_specs = [pl.BlockSpec((bd, d), lambda j: (0, 0)), pl.BlockSpec(nw.shape, lambda j: (0, 0))]
    in_specs += [pl.BlockSpec((d, tf), lambda j, g=g: (0, g * nf + j)) for g in range(groups)]
    in_specs += [pl.BlockSpec((CONV_W, tf), lambda j, c=c: (0, c * nf + j)) for c in range(convs)]
    in_specs += [pl.BlockSpec((bd, CONV_W - 1, tf), lambda j, c=c: (0, 0, c * nf + j)) for c in range(convs)]
    in_specs += [pl.BlockSpec((tf, d), lambda j: (j, 0))]
    args = [x, nw] + [w_up] * groups + [conv_w] * convs + [state] * convs + [w_down]
    if final:
        in_specs.append(pl.BlockSpec(final_w.shape, lambda j: (0, 0)))
        args.append(final_w)
    out_specs = [pl.BlockSpec((bd, d), lambda j: (0, 0))]
    out_specs += [pl.BlockSpec((bd, tf), lambda j: (0, j))] * convs
    out_shape = [jax.ShapeDtypeStruct((bd, d), F32)] + [jax.ShapeDtypeStruct((bd, f), F32)] * convs
    res = pl.pallas_call(
        functools.partial(_mixer_sample_kernel, groups=groups, convs=convs, gated_ffn=gated_ffn, final=final),
        grid=(nf,),
        in_specs=in_specs,
        out_specs=out_specs,
        out_shape=out_shape,
        scratch_shapes=[pltpu.VMEM((bd, d), BF16), pltpu.VMEM((bd, d), F32)],
        compiler_params=_cparams(("arbitrary",), 32),
        name="ffn_sample" if gated_ffn else "sconv_sample",
    )(*args)
    new = jnp.concatenate(res[1:], axis=-1)
    return res[0], jnp.concatenate([state[:, 1:], new[:, None, :]], axis=1)


def _pad_in_weight(w):
    c = HG_W + QB_W + 3 * KV_W
    per = 3 * B_GROUP
    gates = [jnp.pad(w[:, c + h * per:c + (h + 1) * per], ((0, 0), (0, GATE_PAD - per)))
             for h in range(B_KV_HEADS)]
    return jnp.concatenate([w[:, :c]] + gates, axis=1).astype(BF16)


def _per_head_rows(z, width):
    bd = z.shape[0]
    z = z.reshape(bd, B_KV_HEADS, B_GROUP, -1)
    return jnp.pad(z, ((0, 0), (0, 0), (0, 8 - B_GROUP), (0, width - z.shape[-1])))


def kernel(x_prompt, x_sample, cache_cmp_kv, cache_slc_kv, state_win_kv, state_hgrn, state_sconv, state_ffn,
           page_table, norm_mix, norm_ffn, norm_final, ab_w_in, ab_w_out, hgrn_lb_logits, hgrn_norm,
           cmp_pe, cmp_w1, cmp_w2, c_w_in, c_conv, c_w_out, ffn_up, ffn_conv, ffn_down):
    b, s, d = x_prompt.shape
    bd, t, _ = x_sample.shape
    assert t == 1 and d == D_MODEL
    depth = norm_mix.shape[0]
    n_a = ab_w_in.shape[0]
    wb = state_win_kv.shape[2]
    n_pages = page_table.shape[1]
    past = n_pages * PAGE_SIZE
    assert wb == WINDOW and past % (PAGES_PER_STEP * PAGE_SIZE) == 0 and s % 1024 == 0 and s >= wb
    hh = jnp.arange(1, B_HEADS + 1, dtype=F32)
    slopes = 2.0 ** (-8.0 * hh / B_HEADS)
    lb_logits = hgrn_lb_logits.astype(F32)
    row2 = lambda v: v.reshape(1, -1).astype(F32)

    xp = x_prompt.reshape(b * s, d)
    xs = x_sample.reshape(bd, d)
    outs_p = {k: [] for k in ("cmp", "slc", "win", "hg", "sc", "ff")}
    outs_s = {k: [] for k in ("cmp", "slc", "win", "hg", "sc", "ff")}
    y_prompt = y_sample = None
    for l in range(depth):
        last = l == depth - 1
        if l % 2 == 0:
            la = l // 2
            w_in = _pad_in_weight(ab_w_in[la])
            w_out = ab_w_out[la].astype(BF16)
            nw = row2(norm_mix[l])
            hnw = row2(hgrn_norm[la])
            pe_b, w1bd, w2_packed = _cmp_weights(cmp_pe[la], cmp_w1[la], cmp_w2[la], packed=True)
            _, _, w2_plain = _cmp_weights(cmp_pe[la], cmp_w1[la], cmp_w2[la], packed=False)
            hg, qb, gb, kc, kct, kslt, kwt, sk, sv, wk, wv = _inproj(xp, nw, w_in, tm=512, seq=s)
            sk, sv, wk, wv = [z.reshape(B_KV_HEADS, b, s, -1) for z in (sk, sv, wk, wv)]
            oa, hst = _hgrn_prompt(hg, lb_logits, hnw, la, b, s)
            ck, cv = _compress_prompt(kc.reshape(b, 2 * s, KV_W // 2), pe_b, w1bd, w2_packed)
            n_cmp = s // D_CMP
            nblk = s // L_SLC
            cover = _cover_matrix(n_cmp, nblk, nblk)
            ob = _nsa_prompt(slopes, qb.reshape(b, s, QB_W), gb.reshape(b, s, -1), ck, cv, sk, sv, wk, wv, cover)
            xp = _outproj(xp, oa, ob.reshape(b * s, QB_W), w_out, tm=512)
            kv6 = lambda z, n, tt: z.reshape(n, tt, 2, B_KV_HEADS, B_HD)
            from_t = lambda z: z.reshape(b, 2, B_KV_HEADS, B_HD, -1).transpose(0, 4, 1, 2, 3)
            outs_p["cmp"].append(from_t(kct))
            outs_p["slc"].append(from_t(kslt))
            outs_p["win"].append(from_t(kwt[:, :, s - wb:]))
            outs_p["hg"].append(hst)
            hg, qb, gb, kc, ksl, kw = _inproj(xs, nw, w_in, tm=bd)
            oa, hst = _hgrn_sample(hg, state_hgrn[la], lb_logits, hnw, la)
            pre = _cmp_pages(page_table, cache_cmp_kv, la, w1bd)
            n_cmp = pre.shape[1]
            nblk = past // L_SLC + 1
            cover = _cover_matrix(n_cmp, nblk, -(-nblk // 128) * 128)
            q8 = jnp.stack([jnp.pad(_per_head_rows(qb, B_HD)[:, h], ((0, 0), (0, 0), (h * B_HD, KV_W - (h + 1) * B_HD)))
                            for h in range(B_KV_HEADS)], axis=1)
            idx, phys, o_cmp = _nsa_sample_select(slopes, pre, kc.reshape(bd, 1, KV_W), q8, page_table, pe_b, w1bd,
                                                  w2_plain, cover, past)
            picks = lambda z: z[:, :B_KV_HEADS, :N_SEL].reshape(bd, B_KV_HEADS * N_SEL)
            idx, phys = picks(idx), picks(phys)
            gate8 = _per_head_rows(jnp.concatenate([gb[:, h * GATE_PAD:h * GATE_PAD + 3 * B_GROUP]
                                                    for h in range(B_KV_HEADS)], axis=1), 128)
            win_state = state_win_kv[la].transpose(0, 2, 3, 4, 1).reshape(bd, KV_W, wb)
            o8 = _nsa_sample_attend(phys, idx, slopes, cache_slc_kv, la, q8, ksl.reshape(bd, 1, KV_W),
                                    kw.reshape(bd, 1, KV_W), win_state, gate8, o_cmp, past)
            ob = jnp.concatenate([o8[:, h, :B_GROUP, 2 * B_HD + h * B_HD:2 * B_HD + (h + 1) * B_HD]
                                  .reshape(bd, B_GROUP * B_HD) for h in range(B_KV_HEADS)], axis=1)
            xs = _outproj(xs, oa, ob.astype(BF16), w_out, tm=bd)
            outs_s["cmp"].append(kv6(kc, bd, t))
            outs_s["slc"].append(kv6(ksl, bd, t))
            outs_s["win"].append(jnp.concatenate([state_win_kv[la], kv6(kw, bd, t)], axis=1)[:, t:])
            outs_s["hg"].append(hst)
        else:
            lc = l // 2
            xp3, tail = _mixer_prompt(xp.reshape(b, s, d), row2(norm_mix[l]), c_w_in[lc].astype(BF16),
                                      c_conv[lc], c_w_out[lc].astype(BF16), gated_ffn=False)
            xp = xp3.reshape(b * s, d)
            outs_p["sc"].append(tail)
            xs, new = _mixer_sample(xs, row2(norm_mix[l]), c_w_in[lc].astype(BF16), c_conv[lc],
                                    c_w_out[lc].astype(BF16), state_sconv[lc], gated_ffn=False)
            outs_s["sc"].append(new)
        fw = row2(norm_final) if last else None
        xp3, tail = _mixer_prompt(xp.reshape(b, s, d), row2(norm_ffn[l]), ffn_up[l].astype(BF16), ffn_conv[l],
                                  ffn_down[l].astype(BF16), gated_ffn=True, final_w=fw)
        xp = xp3.reshape(b * s, d)
        outs_p["ff"].append(tail)
        xs, new = _mixer_sample(xs, row2(norm_ffn[l]), ffn_up[l].astype(BF16), ffn_conv[l],
                                ffn_down[l].astype(BF16), state_ffn[l], gated_ffn=True, final_w=fw)
        outs_s["ff"].append(new)
    y_prompt = xp.reshape(b, s, d)
    y_sample = xs.reshape(bd, t, d)
    return (y_prompt, y_sample,
            jnp.stack(outs_p["cmp"], axis=2), jnp.stack(outs_p["slc"], axis=2), jnp.stack(outs_p["win"], axis=0),
            jnp.stack(outs_p["hg"], axis=0), jnp.stack(outs_p["sc"], axis=0), jnp.stack(outs_p["ff"], axis=0),
            jnp.stack(outs_s["cmp"], axis=2), jnp.stack(outs_s["slc"], axis=2), jnp.stack(outs_s["win"], axis=0),
            jnp.stack(outs_s["hg"], axis=0), jnp.stack(outs_s["sc"], axis=0), jnp.stack(outs_s["ff"], axis=0))
```

```python
import functools

import jax
import jax.numpy as jnp
from jax import lax
from jax.experimental import pallas as pl
from jax.experimental.pallas import tpu as pltpu

F32 = jnp.float32
BF16 = jnp.bfloat16
NEG_INF = float("-inf")

D_MODEL = 1024
PAGE_SIZE = 128
A_HEADS = 4
A_DK = 128
A_DV = 128
B_HEADS = 8
B_KV_HEADS = 2
B_GROUP = B_HEADS // B_KV_HEADS
B_HD = 64
L_CMP = 32
D_CMP = 16
CMP_HID = 64
L_SLC = 64
SLC_SHIFT = 6
N_SEL = 16
WINDOW = 512
SEL_FORCE = 1e9
CONV_W = 3
EPS = 1e-6
KV_W = 2 * B_KV_HEADS * B_HD
HG_W = 4 * A_HEADS * A_DK
QB_W = B_HEADS * B_HD
GATE_PAD = 128
IN_PAD = HG_W + QB_W + 3 * KV_W + B_KV_HEADS * GATE_PAD

V7X_VMEM_BYTES = 64 * 2**20


def _cparams(sem, vmem_mb):
    assert vmem_mb * 2**20 < V7X_VMEM_BYTES
    return pltpu.CompilerParams(dimension_semantics=sem, vmem_limit_bytes=vmem_mb * 2**20)


def _dot(a, b):
    return jnp.dot(a, b, preferred_element_type=F32)


def _dot_nt(a, b):
    return lax.dot_general(a, b, (((1,), (1,)), ((), ())), preferred_element_type=F32)


def _dot_tn(a, b):
    return lax.dot_general(a, b, (((0,), (0,)), ((), ())), preferred_element_type=F32)


def _split3(x):
    hi = x.astype(BF16)
    r1 = x - hi.astype(F32)
    mid = r1.astype(BF16)
    lo = (r1 - mid.astype(F32)).astype(BF16)
    return hi, mid, lo


def _dot3_lhs(x, w):
    hi, mid, lo = _split3(x)
    return _dot(hi, w) + _dot(mid, w) + _dot(lo, w)


def _dot3_rhs(w, x):
    hi, mid, lo = _split3(x)
    return _dot(w, hi) + _dot(w, mid) + _dot(w, lo)


def _rms(x, w):
    return x * lax.rsqrt(jnp.mean(x * x, axis=-1, keepdims=True) + EPS) * w


def _masked_softmax(s, mask):
    s = jnp.where(mask, s, NEG_INF)
    m = jnp.max(s, axis=-1, keepdims=True)
    m = jnp.where(m > NEG_INF, m, 0.0)
    e = jnp.exp(s - m)
    d = jnp.sum(e, axis=-1, keepdims=True)
    return e * (1.0 / jnp.where(d > 0, d, 1.0))


def _iota(shape, dim):
    return lax.broadcasted_iota(jnp.int32, shape, dim)


def _top_blocks(score, jb, n_sel):
    big = float(score.shape[-1])
    jbf = jb.astype(F32)

    def body(_, carry):
        sc, sel = carry
        m = jnp.max(sc, axis=-1, keepdims=True)
        idx = jnp.min(jnp.where(sc == m, jbf, big), axis=-1, keepdims=True)
        pick = jbf == idx
        return jnp.where(pick, NEG_INF, sc), jnp.where(pick, 1.0, sel)

    _, sel = lax.fori_loop(0, n_sel, body, (score, jnp.zeros_like(score)))
    return sel


def _attention_rows(kv, pos, with_blocks):
    tm = kv.shape[0]
    lane = _iota((tm, 128), 1)
    lo = lane < B_HD
    pos_cols = jnp.where(lane == B_HD, (pos >> 7).astype(F32),
                         jnp.where(lane == B_HD + 1, (pos & 127).astype(F32), 0.0))
    k_pair, v_pair = kv[:, 0:128], kv[:, 128:256]
    k_swap, v_swap = pltpu.roll(k_pair, B_HD, axis=1), pltpu.roll(v_pair, B_HD, axis=1)
    onehot = ((pos >> SLC_SHIFT) == lane).astype(BF16)
    keys, values = [], []
    for h, (k, v_lo, v_hi) in enumerate(((k_pair, v_pair, v_swap), (k_swap, v_swap, v_pair))):
        key = jnp.where(lo, k, pos_cols).astype(BF16)
        keys.append(jnp.concatenate([key, onehot], axis=1) if with_blocks else key)
        values.append(jnp.where(lo, v_lo, v_hi).astype(BF16))
    return keys, values


def _inproj_kernel(x_ref, nw_ref, w_ref, hg_ref, qb_ref, gb_ref, kc_ref, *refs, seq):
    xn = _rms(x_ref[...], nw_ref[...]).astype(BF16)

    def proj(c0, n):
        return _dot(xn, w_ref[:, c0:c0 + n])

    hg_ref[...] = proj(0, HG_W)
    qb_ref[...] = (proj(HG_W, QB_W) * (B_HD ** -0.5)).astype(BF16)
    c = HG_W + QB_W
    kc = proj(c, KV_W)
    ksl = proj(c + KV_W, KV_W)
    kw = proj(c + 2 * KV_W, KV_W)
    kc_ref[...] = kc
    gb_ref[...] = proj(c + 3 * KV_W, B_KV_HEADS * GATE_PAD)
    if seq is None:
        ksl_ref, kw_ref = refs
        ksl_ref[...] = ksl
        kw_ref[...] = kw
    else:
        kct_ref, kslt_ref, kwt_ref, sk_ref, sv_ref, wk_ref, wv_ref = refs
        kct_ref[0] = kc.T
        kslt_ref[0] = ksl.T
        kwt_ref[0] = kw.T
        tm = x_ref.shape[0]
        pos = (pl.program_id(0) * tm + _iota((tm, 1), 0)) & (seq - 1)
        for kv, k_ref, v_ref, with_blocks in ((ksl, sk_ref, sv_ref, True), (kw, wk_ref, wv_ref, False)):
            keys, values = _attention_rows(kv, pos, with_blocks)
            for h in range(B_KV_HEADS):
                k_ref[h] = keys[h]
                v_ref[h] = values[h]


def _inproj(x, nw, w_pad, tm, seq=None):
    m = x.shape[0]
    row = lambda n: pl.BlockSpec((tm, n), lambda i: (i, 0))
    full = lambda a: pl.BlockSpec(a.shape, lambda i: (0,) * a.ndim)
    widths = (HG_W, QB_W, B_KV_HEADS * GATE_PAD, KV_W)
    dtypes = (F32, BF16, F32, F32)
    out_specs = [row(n) for n in widths]
    out_shape = [jax.ShapeDtypeStruct((m, n), d) for n, d in zip(widths, dtypes)]
    if seq is None:
        out_specs += [row(KV_W)] * 2
        out_shape += [jax.ShapeDtypeStruct((m, KV_W), F32)] * 2
    else:
        assert seq // L_SLC == 128 and seq % tm == 0 and seq & (seq - 1) == 0
        nt = seq // tm
        out_specs += [pl.BlockSpec((1, KV_W, tm), lambda i: (i // nt, 0, i % nt))] * 3
        out_shape += [jax.ShapeDtypeStruct((m // seq, KV_W, seq), F32)] * 3
        for n in (256, 128, 128, 128):
            out_specs.append(pl.BlockSpec((B_KV_HEADS, tm, n), lambda i: (0, i, 0)))
            out_shape.append(jax.ShapeDtypeStruct((B_KV_HEADS, m, n), BF16))
    return pl.pallas_call(
        functools.partial(_inproj_kernel, seq=seq),
        grid=(m // tm,),
        in_specs=[row(D_MODEL), full(nw), full(w_pad)],
        out_specs=out_specs,
        out_shape=out_shape,
        compiler_params=_cparams(("parallel",), 48),
        name="inproj",
    )(x, nw, w_pad)


def _lower_bound(logit_ref, la):
    lg = logit_ref[...]
    e = jnp.exp(lg - jnp.max(lg, axis=0, keepdims=True))
    return jnp.sum(e[:la + 1], axis=0, keepdims=True) / jnp.sum(e, axis=0, keepdims=True)


def _hgrn_post(o, nw, ga):
    o = o * lax.rsqrt(jnp.mean(o * o, axis=-1, keepdims=True) + EPS)
    return o * nw * jax.nn.sigmoid(ga)


def _hgrn_prompt_kernel(hg_ref, lg_ref, nw_ref, oa_ref, st_ref, s_ref, *, la, chunk, sub):
    ci = pl.program_id(1)

    @pl.when(ci == 0)
    def _():
        s_ref[...] = jnp.zeros_like(s_ref)

    lb = _lower_bound(lg_ref, la)
    tri = (_iota((chunk, chunk), 0) >= _iota((chunk, chunk), 1)).astype(BF16)
    srow = _iota((sub, 1), 0)
    lane = _iota((sub, chunk), 1)
    for h in range(A_HEADS):
        hs = slice(h * A_DK, (h + 1) * A_DK)
        q = hg_ref[:, h * A_DK:(h + 1) * A_DK]
        fl = hg_ref[:, 512 + h * A_DK:512 + (h + 1) * A_DK]
        v = hg_ref[:, 1024 + h * A_DV:1024 + (h + 1) * A_DV]
        ga = hg_ref[:, 1536 + h * A_DV:1536 + (h + 1) * A_DV]
        lbh = lb[:, hs]
        f = lbh + (1.0 - lbh) * jax.nn.sigmoid(fl)
        a = _dot3_rhs(tri, jnp.log(f))
        k = 1.0 - f
        st = s_ref[h]
        vb = v.astype(BF16)
        o = _dot_nt((q * jnp.exp(a)).astype(BF16), st.astype(BF16))
        rows = []
        for i in range(chunk // sub):
            r0 = i * sub
            qi, ai = q[r0:r0 + sub], a[r0:r0 + sub]
            att = jnp.zeros((sub, chunk), F32)
            for s in range(sub):
                dec = jnp.exp(jnp.where(srow >= s, ai - a[r0 + s:r0 + s + 1], NEG_INF))
                col = jnp.sum(qi * dec * k[r0 + s:r0 + s + 1], axis=-1, keepdims=True)
                att = jnp.where(lane == r0 + s, col, att)
            if i > 0:
                ref_a = a[r0 - 1:r0]
                qt = qi * jnp.exp(ai - ref_a)
                kt = jnp.concatenate([k[:r0] * jnp.exp(ref_a - a[:r0]), jnp.zeros((chunk - r0, A_DK), F32)], axis=0)
                att = att + _dot_nt(qt.astype(BF16), kt.astype(BF16))
            rows.append(att)
        att = jnp.concatenate(rows, axis=0)
        o = o + _dot(att.astype(BF16), vb)
        a_end = a[chunk - 1:chunk]
        khat = k * jnp.exp(a_end - a)
        s_ref[h] = st * jnp.exp(a_end) + _dot_tn(vb, khat.astype(BF16))
        oa_ref[:, hs] = _hgrn_post(o, nw_ref[:, hs], ga).astype(BF16)

    @pl.when(ci == pl.num_programs(1) - 1)
    def _():
        for h in range(A_HEADS):
            st_ref[0, h] = s_ref[h].T


def _hgrn_prompt(hg, lb_logits, nw, la, b, s, chunk=128, sub=16):
    n = s // chunk
    full = lambda a: pl.BlockSpec(a.shape, lambda i, j: (0,) * a.ndim)
    return pl.pallas_call(
        functools.partial(_hgrn_prompt_kernel, la=la, chunk=chunk, sub=sub),
        grid=(b, n),
        in_specs=[pl.BlockSpec((chunk, HG_W), lambda i, j: (i * n + j, 0)), full(lb_logits), full(nw)],
        out_specs=[pl.BlockSpec((chunk, A_HEADS * A_DV), lambda i, j: (i * n + j, 0)),
                   pl.BlockSpec((1, A_HEADS, A_DK, A_DV), lambda i, j: (i, 0, 0, 0))],
        out_shape=[jax.ShapeDtypeStruct((b * s, A_HEADS * A_DV), BF16),
                   jax.ShapeDtypeStruct((b, A_HEADS, A_DK, A_DV), F32)],
        scratch_shapes=[pltpu.VMEM((A_HEADS, A_DV, A_DK), F32)],
        compiler_params=_cparams(("parallel", "arbitrary"), 32),
        name="hgrn_prompt",
    )(hg, lb_logits, nw)


def _hgrn_sample_kernel(hg_ref, s0_ref, lg_ref, nw_ref, oa_ref, st_ref, *, la):
    lb = _lower_bound(lg_ref, la)
    eye = (_iota((A_DK, A_DK), 0) == _iota((A_DK, A_DK), 1)).astype(F32)

    def col(r):
        return jnp.sum(eye * r, axis=1, keepdims=True)

    for h in range(A_HEADS):
        hs = slice(h * A_DK, (h + 1) * A_DK)
        q = hg_ref[0, :, h * A_DK:(h + 1) * A_DK]
        fl = hg_ref[0, :, 512 + h * A_DK:512 + (h + 1) * A_DK]
        v = hg_ref[0, :, 1024 + h * A_DV:1024 + (h + 1) * A_DV]
        ga = hg_ref[0, :, 1536 + h * A_DV:1536 + (h + 1) * A_DV]
        lbh = lb[:, hs]
        f = lbh + (1.0 - lbh) * jax.nn.sigmoid(fl)
        sn = col(f) * s0_ref[0, h] + col(1.0 - f) * v
        st_ref[0, h] = sn
        o = jnp.sum(col(q) * sn, axis=0, keepdims=True)
        oa_ref[0, :, hs] = _hgrn_post(o, nw_ref[:, hs], ga).astype(BF16)


def _hgrn_sample(hg, s0, lb_logits, nw, la):
    bd = hg.shape[0]
    full = lambda a: pl.BlockSpec(a.shape, lambda i: (0,) * a.ndim)
    st_spec = pl.BlockSpec((1, A_HEADS, A_DK, A_DV), lambda i: (i, 0, 0, 0))
    oa, st = pl.pallas_call(
        functools.partial(_hgrn_sample_kernel, la=la),
        grid=(bd,),
        in_specs=[pl.BlockSpec((1, 1, HG_W), lambda i: (i, 0, 0)), st_spec, full(lb_logits), full(nw)],
        out_specs=[pl.BlockSpec((1, 1, A_HEADS * A_DV), lambda i: (i, 0, 0)), st_spec],
        out_shape=[jax.ShapeDtypeStruct((bd, 1, A_HEADS * A_DV), BF16),
                   jax.ShapeDtypeStruct(s0.shape, F32)],
        compiler_params=_cparams(("parallel",), 16),
        name="hgrn_sample",
    )(hg.reshape(bd, 1, HG_W), s0, lb_logits, nw)
    return oa.reshape(bd, A_HEADS * A_DV), st


def _cmp_weights(pe, w1, w2, packed):
    eye = jnp.eye(B_KV_HEADS, dtype=F32)
    w1r = w1.reshape(2, L_CMP, B_HD, CMP_HID)
    w1bd = jnp.einsum("kidm,hH->ikhdHm", w1r, eye).reshape(L_CMP, 2, KV_W // 2, KV_W // 2)
    pe_b = jnp.broadcast_to(pe.transpose(1, 0, 2)[:, :, None, :], (L_CMP, 2, B_KV_HEADS, B_HD))
    if not packed:
        w2bd = jnp.einsum("kmd,kK,hH->khmKHd", w2, eye, eye).reshape(KV_W, KV_W)
        return pe_b.reshape(L_CMP, KV_W), w1bd.astype(BF16), w2bd.astype(BF16)
    half = KV_W // 2
    zero = jnp.zeros((half, KV_W), F32)
    k_blk = jnp.concatenate([w2[0], jnp.zeros_like(w2[0])], axis=1)
    v_blk = jnp.concatenate([w2[1], w2[1]], axis=1)
    w2k = jnp.concatenate([jnp.einsum("ml,hH->hmHl", k_blk, eye).reshape(half, KV_W), zero], axis=0)
    w2v = jnp.concatenate([zero, jnp.einsum("ml,hH->hmHl", v_blk, eye).reshape(half, KV_W)], axis=0)
    return pe_b.reshape(L_CMP, KV_W), w1bd.astype(BF16), (w2k.astype(BF16), w2v.astype(BF16))


def _chunk_preacts(load, n, pe_ref, w1_ref):
    half = KV_W // 2
    p0 = [jnp.zeros((n, half), F32) for _ in range(2)]
    p1 = [jnp.zeros((n, half), F32) for _ in range(2)]
    for i in range(D_CMP):
        for kv in range(2):
            x = load(i, kv)
            ls = slice(kv * half, (kv + 1) * half)
            p0[kv] = p0[kv] + _dot((x + pe_ref[i:i + 1, ls]).astype(BF16), w1_ref[i, kv])
            p1[kv] = p1[kv] + _dot((x + pe_ref[D_CMP + i:D_CMP + i + 1, ls]).astype(BF16), w1_ref[D_CMP + i, kv])
    return jnp.concatenate(p0, axis=1), jnp.concatenate(p1, axis=1)


def _with_position(k, pos, lane):
    col = lane & 127
    return jnp.where(col == B_HD, (pos >> 7).astype(F32), jnp.where(col == B_HD + 1, (pos & 127).astype(F32), k))


def _compress_kernel(kc_ref, pe_ref, w1_ref, w2k_ref, w2v_ref, ck_ref, cv_ref):
    n = ck_ref.shape[1]
    p0, p1 = _chunk_preacts(lambda i, kv: kc_ref[0, pl.ds(2 * i + kv, n, stride=2 * D_CMP), :], n, pe_ref, w1_ref)
    pre = p0 + pltpu.roll(p1, n - 1, axis=0)
    hid = (pre * jax.nn.sigmoid(pre)).astype(BF16)
    cpos = _iota((n, 1), 0) * D_CMP + (L_CMP - 1)
    ck_ref[0] = _with_position(_dot(hid, w2k_ref[...]), cpos, _iota((n, KV_W), 1)).astype(BF16)
    cv_ref[0] = _dot(hid, w2v_ref[...]).astype(BF16)


def _compress_prompt(kc, pe_b, w1bd, w2kv):
    b, s2, _ = kc.shape
    n = s2 // (2 * D_CMP)
    full = lambda a: pl.BlockSpec(a.shape, lambda i: (0,) * a.ndim)
    out = pl.BlockSpec((1, n, KV_W), lambda i: (i, 0, 0))
    return pl.pallas_call(
        _compress_kernel,
        grid=(b,),
        in_specs=[pl.BlockSpec((1, s2, KV_W // 2), lambda i: (i, 0, 0)), full(pe_b), full(w1bd),
                  full(w2kv[0]), full(w2kv[1])],
        out_specs=[out, out],
        out_shape=[jax.ShapeDtypeStruct((b, n, KV_W), BF16)] * 2,
        compiler_params=_cparams(("parallel",), 48),
        name="compress_prompt",
    )(kc, pe_b, w1bd, *w2kv)


MASK_BIAS = 2.0 ** 100

def _top_blocks_t(score_t, n_sel):
    nblk = score_t.shape[0]
    blk = _iota(score_t.shape, 0).astype(F32)

    def body(_, carry):
        sc, sel = carry
        m = jnp.max(sc, axis=0, keepdims=True)
        idx = jnp.min(jnp.where(sc == m, blk, float(nblk)), axis=0, keepdims=True)
        pick = blk == idx
        return jnp.where(pick, NEG_INF, sc), jnp.where(pick, 1.0, sel)

    _, sel = lax.fori_loop(0, n_sel, body, (score_t, jnp.zeros_like(score_t)), unroll=True)
    return sel


def _nsa_prompt_kernel(sl_ref, q_ref, g_ref, ck_ref, cv_ref, sk_ref, sv_ref, wk_ref, wv_ref, cover_ref, o_ref,
                       m_ref, l_ref, acc_ref, sa_ref, sb_ref, tiles_ref, *, tq, tk):
    kvh = pl.program_id(1)
    q0 = pl.program_id(2) * tq
    rows = B_GROUP * tq
    lane = _iota((tq, 128), 1)
    lo = lane < B_HD
    qf = q_ref[0].astype(F32)
    left = []
    for g in range(B_GROUP):
        pair = qf[:, 128 * (g // 2):128 * (g // 2) + 128]
        if g % 2:
            pair = pltpu.roll(pair, B_HD, axis=1)
        slope = sl_ref[kvh * B_GROUP + g]
        pos_cols = jnp.where(lane == B_HD, slope * 128.0, jnp.where(lane == B_HD + 1, slope, 0.0))
        left.append(jnp.where(lo, pair, pos_cols).astype(BF16))
    ql = jnp.concatenate(left, axis=0)
    qpos = q0 + (_iota((rows, 1), 0) & (tq - 1))

    nc = ck_ref.shape[1]
    cpos = _iota((1, nc), 1) * D_CMP + (L_CMP - 1)
    span = WINDOW + tq
    start = pl.multiple_of(jnp.maximum(q0 - WINDOW, 0), 128)
    wpos = start + _iota((1, span), 1)
    s_cmp = _dot_nt(ql, ck_ref[0])
    s_win = _dot_nt(ql, wk_ref[0, 0, pl.ds(start, span), :])
    p = _masked_softmax(s_cmp, cpos <= qpos)
    pw = _masked_softmax(s_win, (wpos <= qpos) & (wpos > qpos - WINDOW))
    o_cmp = _dot(p.astype(BF16), cv_ref[0])
    o_win = _dot(pw.astype(BF16), wv_ref[0, 0, pl.ds(start, span), :])
    imp = p[0:tq]
    for g in range(1, B_GROUP):
        imp = imp + p[g * tq:(g + 1) * tq]

    nblk = cover_ref.shape[1]
    jb = _iota((1, nblk), 1)
    cur = (q0 + _iota((tq, 1), 0)) >> SLC_SHIFT
    forced = (jb == 0) | (jb == cur) | (jb == cur - 1)
    score = _dot3_lhs(imp, cover_ref[...])
    score = jnp.where(forced | (jb > cur), NEG_INF, score)
    n_forced = 3
    sel = jnp.where(forced, 1.0, _top_blocks_t(score.T, min(N_SEL, nblk) - n_forced).T)

    bias = ((sel - 1.0) * MASK_BIAS).astype(BF16)
    qs = jnp.concatenate([ql, jnp.concatenate([bias] * B_GROUP, axis=0)], axis=1)
    m_ref[...] = jnp.full_like(m_ref, NEG_INF)
    l_ref[...] = jnp.zeros_like(l_ref)
    acc_ref[...] = jnp.zeros_like(acc_ref)

    def scores(kt):
        return _dot_nt(qs, sk_ref[0, 0, pl.ds(pl.multiple_of(kt * tk, tk), tk), :])

    def update(s, kt, causal):
        k0 = pl.multiple_of(kt * tk, tk)
        if causal:
            s = jnp.where(k0 + _iota((1, tk), 1) <= qpos, s, NEG_INF)
        m_old = m_ref[...]
        m_new = jnp.maximum(m_old, jnp.max(s, axis=-1, keepdims=True))
        alpha = jnp.exp(m_old - m_new)
        e = jnp.exp(s - m_new)
        l_ref[...] = alpha * l_ref[...] + jnp.sum(e, axis=-1, keepdims=True)
        acc_ref[...] = alpha * acc_ref[...] + _dot(e.astype(BF16), sv_ref[0, 0, pl.ds(k0, tk), :])
        m_ref[...] = m_new

    n_diag = q0 // tk
    bpt = tk // L_SLC
    picked = jnp.max(sel, axis=0, keepdims=True)
    n_full = jnp.int32(0)
    for kt in range(nblk * L_SLC // tk):
        in_tile = (jb >= kt * bpt) & (jb < (kt + 1) * bpt)
        wanted = (jnp.max(jnp.where(in_tile, picked, 0.0)) > 0.5) & (kt < n_diag)
        tiles_ref[n_full] = jnp.int32(kt)
        n_full = n_full + wanted.astype(jnp.int32)
    tiles_ref[n_full] = n_diag

    sa_ref[...] = scores(tiles_ref[0])

    def tile_pair(i, carry):
        sb_ref[...] = scores(tiles_ref[2 * i + 1])
        update(sa_ref[...], tiles_ref[2 * i], False)
        sa_ref[...] = scores(tiles_ref[2 * i + 2])
        update(sb_ref[...], tiles_ref[2 * i + 1], False)
        return carry

    lax.fori_loop(0, n_full // 2, tile_pair, 0)

    @pl.when(n_full % 2 == 1)
    def _():
        sb_ref[...] = scores(n_diag)
        update(sa_ref[...], tiles_ref[n_full - 1], False)
        update(sb_ref[...], n_diag, True)

    @pl.when(n_full % 2 == 0)
    def _():
        update(sa_ref[...], n_diag, True)

    o_slc = acc_ref[...] * (1.0 / l_ref[...])

    gate = jax.nn.sigmoid(g_ref[0])
    outs = []
    for g in range(B_GROUP):
        rs = slice(g * tq, (g + 1) * tq)
        outs.append(gate[:, 3 * g:3 * g + 1] * o_cmp[rs] + gate[:, 3 * g + 1:3 * g + 2] * o_slc[rs]
                    + gate[:, 3 * g + 2:3 * g + 3] * o_win[rs])
    for j in range(B_GROUP // 2):
        o_ref[0, :, 128 * j:128 * j + 128] = jnp.where(lo, outs[2 * j], outs[2 * j + 1]).astype(BF16)


def _nsa_prompt(slopes, qb, gb, ck, cv, sk, sv, wk, wv, cover, tq=128, tk=512):
    b, s, _ = qb.shape
    half = QB_W // B_KV_HEADS
    rows = B_GROUP * tq
    assert cover.shape[1] == 128 and s % tk == 0 and tk % tq == 0
    seq = lambda a: pl.BlockSpec((1, 1) + a.shape[2:], lambda i, h, t, sl: (h, i, 0, 0))
    grid_spec = pltpu.PrefetchScalarGridSpec(
        num_scalar_prefetch=1,
        grid=(b, B_KV_HEADS, s // tq),
        in_specs=[
            pl.BlockSpec((1, tq, half), lambda i, h, t, sl: (i, t, h)),
            pl.BlockSpec((1, tq, GATE_PAD), lambda i, h, t, sl: (i, t, h)),
            pl.BlockSpec((1, ck.shape[1], 128), lambda i, h, t, sl: (i, 0, h)),
            pl.BlockSpec((1, cv.shape[1], 128), lambda i, h, t, sl: (i, 0, h)),
            seq(sk), seq(sv), seq(wk), seq(wv),
            pl.BlockSpec(cover.shape, lambda i, h, t, sl: (0, 0)),
        ],
        out_specs=pl.BlockSpec((1, tq, half), lambda i, h, t, sl: (i, t, h)),
        scratch_shapes=[pltpu.VMEM((rows, 1), F32), pltpu.VMEM((rows, 1), F32), pltpu.VMEM((rows, 128), F32),
                        pltpu.VMEM((rows, tk), F32), pltpu.VMEM((rows, tk), F32),
                        pltpu.SMEM((s // tk + 1,), jnp.int32)],
    )
    return pl.pallas_call(
        functools.partial(_nsa_prompt_kernel, tq=tq, tk=tk),
        grid_spec=grid_spec,
        out_shape=jax.ShapeDtypeStruct((b, s, QB_W), BF16),
        compiler_params=_cparams(("parallel", "parallel", "arbitrary"), 56),
        name="nsa_prompt",
    )(slopes, qb, gb, ck, cv, sk, sv, wk, wv, cover)


def _cover_matrix(n_cmp, n_blk, n_pad):
    ci = jnp.arange(n_cmp)[:, None] * D_CMP
    sj = jnp.arange(n_pad)[None, :] * L_SLC
    cov = (ci < sj + L_SLC) & (ci + L_CMP > sj) & (jnp.arange(n_pad)[None, :] < n_blk)
    return cov.astype(BF16)


PAGES_PER_STEP = 64
PAGE_GROUPS = 2


def _cmp_pages_kernel(pt_ref, *refs):
    pages = refs[:PAGES_PER_STEP]
    w_ref, out_ref = refs[PAGES_PER_STEP:PAGES_PER_STEP + 2]
    row_refs = refs[PAGES_PER_STEP + 2:]
    per_group = PAGES_PER_STEP // PAGE_GROUPS
    n = per_group * PAGE_SIZE // D_CMP
    half = KV_W // 2
    for grp, rows_ref in enumerate(row_refs):
        for k, pg in enumerate(pages[grp * per_group:(grp + 1) * per_group]):
            for kv in range(2):
                rows_ref[kv, k * PAGE_SIZE:(k + 1) * PAGE_SIZE, :] = pg[0, kv * half:(kv + 1) * half, :].T
        for kv in range(2):
            x = jnp.concatenate([rows_ref[kv, pl.ds(i, n, stride=D_CMP), :].astype(BF16) for i in range(D_CMP)],
                                axis=1)
            y = _dot(x, w_ref[kv])
            for part in range(2):
                c0 = part * KV_W + kv * half
                out_ref[0, grp * n:(grp + 1) * n, c0:c0 + half] = y[:, part * half:(part + 1) * half]


def _pages_weight(w1bd):
    w = jnp.concatenate([w1bd[:D_CMP], w1bd[D_CMP:]], axis=3)
    return w.transpose(1, 0, 2, 3).reshape(2, D_CMP * (KV_W // 2), KV_W)


def _feature_major_pages(cache):
    return cache.transpose(0, 2, 3, 4, 5, 1).reshape(cache.shape[0], -1, cache.shape[1])


def _cmp_pages(page_table, cache, li, w1bd):
    bd, n_pages = page_table.shape
    pool = _feature_major_pages(cache)
    steps = n_pages // PAGES_PER_STEP
    rows = PAGES_PER_STEP * PAGE_SIZE // D_CMP
    w = _pages_weight(w1bd)

    def page_spec(k):
        return pl.BlockSpec((1, KV_W, PAGE_SIZE), lambda i, j, pt: (pt[i, j * PAGES_PER_STEP + k], li, 0))

    grid_spec = pltpu.PrefetchScalarGridSpec(
        num_scalar_prefetch=1,
        grid=(bd, steps),
        in_specs=[page_spec(k) for k in range(PAGES_PER_STEP)] + [pl.BlockSpec(w.shape, lambda i, j, pt: (0, 0, 0))],
        out_specs=pl.BlockSpec((1, rows, 2 * KV_W), lambda i, j, pt: (i, j, 0)),
        scratch_shapes=[pltpu.VMEM((2, PAGES_PER_STEP // PAGE_GROUPS * PAGE_SIZE, KV_W // 2), F32)] * PAGE_GROUPS,
    )
    return pl.pallas_call(
        _cmp_pages_kernel,
        grid_spec=grid_spec,
        out_shape=jax.ShapeDtypeStruct((bd, steps * rows, 2 * KV_W), F32),
        compiler_params=_cparams(("parallel", "arbitrary"), 48),
        name="cmp_pages",
    )(page_table, *([pool] * PAGES_PER_STEP), w)


def _nsa_sample_select_kernel(sl_ref, pre_ref, new_ref, q_ref, pt_ref, pe_ref, w1_ref, w2_ref, cover_ref,
                              idx_ref, phys_ref, ocmp_ref, tail_ref, score_ref, *, past):
    half = KV_W // 2

    def row_act(i, r):
        r8 = jnp.broadcast_to(r, (8, KV_W)).astype(BF16)
        return jnp.concatenate([_dot(r8[:, kv * half:(kv + 1) * half], w1_ref[i, kv]) for kv in range(2)], axis=1)

    @pl.when(pl.program_id(0) == 0)
    def _():
        t = jnp.zeros((8, KV_W), F32)
        for i in range(L_CMP):
            t = t + row_act(i, pe_ref[i:i + 1])
        tail_ref[...] = t

    n = pre_ref.shape[1]
    p1 = pltpu.roll(pre_ref[0, :, KV_W:2 * KV_W], n - 1, axis=0)
    p1 = jnp.where(_iota((n, 1), 0) == n - 1, row_act(D_CMP, new_ref[0])[0:1], p1)
    pre = pre_ref[0, :, 0:KV_W] + p1 + tail_ref[0:1]
    comp = _dot((pre * jax.nn.sigmoid(pre)).astype(BF16), w2_ref[...]).astype(BF16)

    qpos = past
    cpos = _iota((1, n), 1) * D_CMP + (L_CMP - 1)
    cmask = cpos <= qpos
    cdist = (qpos - cpos).astype(F32)
    row8 = _iota((8, 1), 0)
    imp = jnp.zeros((8, n), F32)
    for kvh in range(B_KV_HEADS):
        slope = jnp.zeros((8, 1), F32)
        for g in range(B_GROUP):
            slope = jnp.where(row8 == g, sl_ref[kvh * B_GROUP + g], slope)
        p = _masked_softmax(_dot_nt(q_ref[0, kvh], comp) - slope * cdist, cmask)
        p = jnp.where(row8 < B_GROUP, p, 0.0)
        ocmp_ref[0, kvh] = _dot(p.astype(BF16), comp)
        imp = jnp.where(row8 == kvh, jnp.sum(p, axis=0, keepdims=True), imp)

    nblk = cover_ref.shape[1]
    jb = _iota((1, nblk), 1)
    cur = qpos // L_SLC
    score = _dot3_lhs(imp, cover_ref[...])
    score = jnp.where((jb == 0) | (jb == cur) | (jb == cur - 1), SEL_FORCE, score)
    b = pl.program_id(0)
    score_ref[b] = jnp.where(jb > cur, NEG_INF, score)

    @pl.when(b == pl.num_programs(0) - 1)
    def _():
        bd = score_ref.shape[0]
        lane = _iota((1, 128), 1)
        jbf = jb.astype(F32)
        pages = jnp.broadcast_to(pt_ref[...].astype(F32), (bd, 8, 128)).reshape(bd * 8, 128)
        bpp_shift = (PAGE_SIZE // L_SLC).bit_length() - 1

        def body(it, carry):
            sc, idx, phys = carry
            m = jnp.max(sc, axis=-1, keepdims=True)
            j = jnp.min(jnp.where(sc == m, jbf, float(nblk)), axis=-1, keepdims=True)
            logical = jnp.minimum(j.astype(jnp.int32), past // L_SLC - 1) >> bpp_shift
            page = jnp.sum(jnp.where(lane == logical, pages, 0.0), axis=-1, keepdims=True)
            return (jnp.where(jbf == j, NEG_INF, sc), jnp.where(lane == it, j, idx),
                    jnp.where(lane == it, page, phys))

        zero = jnp.zeros((bd * 8, 128), F32)
        _, idx, phys = lax.fori_loop(0, N_SEL, body, (score_ref[...].reshape(bd * 8, nblk), zero, zero))
        idx_ref[...] = idx.astype(jnp.int32).reshape(bd, 8, 128)
        phys_ref[...] = phys.astype(jnp.int32).reshape(bd, 8, 128)


def _nsa_sample_select(slopes, pre, cmp_new, q8, page_table, pe_b, w1bd, w2bd, cover, past):
    bd = pre.shape[0]
    assert page_table.shape[1] == 128
    full = lambda a: pl.BlockSpec(a.shape, lambda i, sl: (0,) * a.ndim)
    picks = pl.BlockSpec((bd, 8, 128), lambda i, sl: (0, 0, 0))
    pt = page_table.reshape(bd, 1, 128)
    grid_spec = pltpu.PrefetchScalarGridSpec(
        num_scalar_prefetch=1,
        grid=(bd,),
        in_specs=[
            pl.BlockSpec((1,) + pre.shape[1:], lambda i, sl: (i, 0, 0)),
            pl.BlockSpec((1, 1, KV_W), lambda i, sl: (i, 0, 0)),
            pl.BlockSpec((1, B_KV_HEADS, 8, KV_W), lambda i, sl: (i, 0, 0, 0)),
            full(pt), full(pe_b), full(w1bd), full(w2bd), full(cover),
        ],
        out_specs=[picks, picks, pl.BlockSpec((1, B_KV_HEADS, 8, KV_W), lambda i, sl: (i, 0, 0, 0))],
        scratch_shapes=[pltpu.VMEM((8, KV_W), F32), pltpu.VMEM((bd, 8, cover.shape[1]), F32)],
    )
    return pl.pallas_call(
        functools.partial(_nsa_sample_select_kernel, past=past),
        grid_spec=grid_spec,
        out_shape=[jax.ShapeDtypeStruct((bd, 8, 128), jnp.int32)] * 2
        + [jax.ShapeDtypeStruct((bd, B_KV_HEADS, 8, KV_W), F32)],
        compiler_params=_cparams(("arbitrary",), 48),
        name="nsa_sample_select",
    )(slopes, pre, cmp_new, q8, pt, pe_b, w1bd, w2bd, cover)


def _nsa_sample_attend_kernel(pt_ref, ix_ref, sl_ref, *refs, past, nb_past):
    n_blocks = B_KV_HEADS * N_SEL
    pages = refs[:n_blocks]
    q_ref, slc_new_ref, win_new_ref, win_ref, g_ref, ocmp_ref, o_ref = refs[n_blocks:]
    b = pl.program_id(0)
    bpp = PAGE_SIZE // L_SLC
    row8 = _iota((8, 1), 0)
    page_row = _iota((1, PAGE_SIZE), 1)
    wb = win_ref.shape[2]
    win = win_ref[0].astype(BF16)
    slc_new = slc_new_ref[0]
    win_new = win_new_ref[0]
    for kvh in range(B_KV_HEADS):
        q = q_ref[0, kvh]
        qf = q.astype(F32)
        slope = jnp.zeros((8, 1), F32)
        for g in range(B_GROUP):
            slope = jnp.where(row8 == g, sl_ref[kvh * B_GROUP + g], slope)
        kv = jnp.concatenate([pages[kvh * N_SEL + n][0].astype(BF16) for n in range(N_SEL)], axis=1)
        kpos, blk, live = [], [], []
        j_max = ix_ref[b, kvh * N_SEL]
        for n in range(N_SEL):
            j = ix_ref[b, kvh * N_SEL + n]
            j_max = jnp.maximum(j_max, j)
            jc = jnp.minimum(j, nb_past - 1)
            kpos.append((jc // bpp) * PAGE_SIZE + page_row)
            blk.append(jnp.broadcast_to(jc, (1, PAGE_SIZE)))
            live.append(jnp.broadcast_to((j < nb_past).astype(jnp.int32), (1, PAGE_SIZE)))
        kpos, blk, live = [jnp.concatenate(z, axis=1) for z in (kpos, blk, live)]
        s = _dot(q, kv) - slope * (past - kpos).astype(F32)
        s = jnp.where(((kpos >> SLC_SHIFT) == blk) & (live > 0), s, NEG_INF)
        s_new = jnp.where(j_max >= nb_past, jnp.sum(qf * slc_new, axis=-1, keepdims=True), NEG_INF)
        m = jnp.maximum(jnp.max(s, axis=-1, keepdims=True), s_new)
        m = jnp.where(m > NEG_INF, m, 0.0)
        e = jnp.exp(s - m)
        e_new = jnp.exp(s_new - m)
        l = jnp.sum(e, axis=-1, keepdims=True) + e_new
        o_slc = (_dot_nt(e.astype(BF16), kv) + e_new * slc_new) * (1.0 / jnp.where(l > 0, l, 1.0))
        wpos = past - wb + _iota((1, wb), 1)
        s1 = _dot(q, win) - slope * (past - wpos).astype(F32)
        s1 = jnp.where(wpos > past - WINDOW, s1, NEG_INF)
        s2 = jnp.sum(qf * win_new, axis=-1, keepdims=True)
        mw = jnp.maximum(jnp.max(s1, axis=-1, keepdims=True), s2)
        e1 = jnp.exp(s1 - mw)
        e2 = jnp.exp(s2 - mw)
        dw = jnp.sum(e1, axis=-1, keepdims=True) + e2
        o_win = (_dot_nt(e1.astype(BF16), win) + e2 * win_new) * (1.0 / dw)
        gate = jax.nn.sigmoid(g_ref[0, kvh])
        o_ref[0, kvh] = gate[:, 0:1] * ocmp_ref[0, kvh] + gate[:, 1:2] * o_slc + gate[:, 2:3] * o_win


def _nsa_sample_attend(phys, idx, slopes, cache, li, q8, slc_new, win_new, win_state, gate8, o_cmp, past):
    bd = q8.shape[0]
    bpp = PAGE_SIZE // L_SLC
    nb_past = past // L_SLC
    n_blocks = B_KV_HEADS * N_SEL
    pool = _feature_major_pages(cache)

    def blk_spec(k):
        return pl.BlockSpec((1, KV_W, PAGE_SIZE), lambda i, ph, ix, sl: (ph[i, k], li, 0))

    row = lambda: pl.BlockSpec((1, 1, KV_W), lambda i, pt, ix, sl: (i, 0, 0))
    per_head = lambda w: pl.BlockSpec((1, B_KV_HEADS, 8, w), lambda i, pt, ix, sl: (i, 0, 0, 0))
    grid_spec = pltpu.PrefetchScalarGridSpec(
        num_scalar_prefetch=3,
        grid=(bd,),
        in_specs=[blk_spec(k) for k in range(n_blocks)] + [
            per_head(KV_W), row(), row(),
            pl.BlockSpec((1,) + win_state.shape[1:], lambda i, pt, ix, sl: (i, 0, 0)),
            per_head(128), per_head(KV_W)],
        out_specs=per_head(KV_W),
    )
    return pl.pallas_call(
        functools.partial(_nsa_sample_attend_kernel, past=past, nb_past=nb_past),
        grid_spec=grid_spec,
        out_shape=jax.ShapeDtypeStruct((bd, B_KV_HEADS, 8, KV_W), F32),
        compiler_params=_cparams(("arbitrary",), 48),
        name="nsa_sample_attend",
    )(phys, idx, slopes, *([pool] * n_blocks), q8, slc_new, win_new, win_state, gate8, o_cmp)


def _outproj_kernel(x_ref, a_ref, b_ref, w_ref, y_ref):
    na = a_ref.shape[1]
    y_ref[...] = x_ref[...] + _dot(a_ref[...], w_ref[0:na]) + _dot(b_ref[...], w_ref[na:])


def _outproj(x, oa, ob, w, tm):
    m = x.shape[0]
    row = lambda n: pl.BlockSpec((tm, n), lambda i: (i, 0))
    return pl.pallas_call(
        _outproj_kernel,
        grid=(m // tm,),
        in_specs=[row(D_MODEL), row(oa.shape[1]), row(ob.shape[1]), pl.BlockSpec(w.shape, lambda i: (0, 0))],
        out_specs=row(D_MODEL),
        out_shape=jax.ShapeDtypeStruct((m, D_MODEL), F32),
        compiler_params=_cparams(("parallel",), 32),
        name="outproj",
    )(x, oa, ob, w)


def _mixer_prompt_kernel(*refs, groups, convs, gated_ffn, final, tm, tf, f):
    x_ref, nw_ref, up_ref, cw_ref, down_ref = refs[0:5]
    pos = 5
    fw_ref = refs[pos] if final else None
    pos += int(final)
    y_ref, tail_ref = refs[pos], refs[pos + 1]
    act_ref, ubuf_ref, carry_ref = refs[pos + 2:]

    @pl.when(pl.program_id(1) == 0)
    def _():
        carry_ref[...] = jnp.zeros_like(carry_ref)

    x = x_ref[0]
    xn = _rms(x, nw_ref[...]).astype(BF16)
    for j in range(f // tf):
        u = [_dot(xn, up_ref[:, g * f + j * tf:g * f + (j + 1) * tf]) for g in range(groups)]
        conv_in = [u[0], u[1]] if gated_ffn else [u[1] * u[0]]
        conv_out = []
        for c, z in enumerate(conv_in):
            cols = slice(c * f + j * tf, c * f + (j + 1) * tf)
            slot = (j % 2) * convs + c
            ubuf_ref[slot, 0:8] = carry_ref[:, cols]
            ubuf_ref[slot, 8:tm + 8] = z
            w = cw_ref[:, cols]
            conv_out.append(w[0:1] * ubuf_ref[slot, pl.ds(6, tm), :] + w[1:2] * ubuf_ref[slot, pl.ds(7, tm), :]
                            + w[2:3] * z)
            last = z[tm - 8:tm]
            carry_ref[:, cols] = last
            tail_ref[0, 0, :, cols] = last
        if gated_ffn:
            act = conv_out[0] * jax.nn.sigmoid(conv_out[0]) * conv_out[1]
        else:
            act = u[2] * conv_out[0]
        act_ref[:, j * tf:(j + 1) * tf] = act.astype(BF16)
    y = x + _dot(act_ref[...], down_ref[...])
    y_ref[0] = _rms(y, fw_ref[...]) if final else y


def _mixer_prompt(x, nw, w_up, conv_w, w_down, gated_ffn, final_w=None, tm=512, tf=256):
    b, s, d = x.shape
    f = w_down.shape[0]
    groups = w_up.shape[1] // f
    convs = conv_w.shape[1] // f
    final = final_w is not None
    const = lambda a: pl.BlockSpec(a.shape, lambda i, t: (0,) * a.ndim, pipeline_mode=pl.Buffered(1))
    in_specs = [pl.BlockSpec((1, tm, d), lambda i, t: (i, t, 0)), const(nw), const(w_up), const(conv_w),
                const(w_down)]
    args = [x, nw, w_up, conv_w, w_down]
    if final:
        in_specs.append(const(final_w))
        args.append(final_w)
    y, tail = pl.pallas_call(
        functools.partial(_mixer_prompt_kernel, groups=groups, convs=convs, gated_ffn=gated_ffn,
                          final=final, tm=tm, tf=tf, f=f),
        grid=(b, s // tm),
        in_specs=in_specs,
        out_specs=[pl.BlockSpec((1, tm, d), lambda i, t: (i, t, 0)),
                   pl.BlockSpec((1, 1, 8, convs * f), lambda i, t: (i, t, 0, 0))],
        out_shape=[jax.ShapeDtypeStruct((b, s, d), F32), jax.ShapeDtypeStruct((b, s // tm, 8, convs * f), F32)],
        scratch_shapes=[pltpu.VMEM((tm, f), BF16), pltpu.VMEM((2 * convs, tm + 8, tf), F32),
                        pltpu.VMEM((8, convs * f), F32)],
        compiler_params=_cparams(("parallel", "arbitrary"), 56),
        name="ffn_prompt" if gated_ffn else "sconv_prompt",
    )(*args)
    return y, tail[:, -1, 8 - (CONV_W - 1):, :]


def _mixer_sample_kernel(*refs, groups, convs, gated_ffn, final):
    x_ref, nw_ref = refs[0:2]
    up = refs[2:2 + groups]
    cw = refs[2 + groups:2 + groups + convs]
    st = refs[2 + groups + convs:2 + groups + 2 * convs]
    down_ref = refs[2 + groups + 2 * convs]
    pos = 3 + groups + 2 * convs
    fw_ref = refs[pos] if final else None
    pos += int(final)
    y_ref = refs[pos]
    news = refs[pos + 1:pos + 1 + convs]
    xn_ref, acc_ref = refs[pos + 1 + convs:]
    j = pl.program_id(0)

    @pl.when(j == 0)
    def _():
        xn_ref[...] = _rms(x_ref[...], nw_ref[...]).astype(BF16)
        acc_ref[...] = jnp.zeros_like(acc_ref)

    xn = xn_ref[...]
    u = [_dot(xn, w[...]) for w in up]
    conv_in = [u[0], u[1]] if gated_ffn else [u[1] * u[0]]
    conv_out = []
    for c, z in enumerate(conv_in):
        w = cw[c][...]
        conv_out.append(w[0:1] * st[c][:, 0, :] + w[1:2] * st[c][:, 1, :] + w[2:3] * z)
        news[c][...] = z
    if gated_ffn:
        act = conv_out[0] * jax.nn.sigmoid(conv_out[0]) * conv_out[1]
    else:
        act = u[2] * conv_out[0]
    acc_ref[...] += _dot(act.astype(BF16), down_ref[...])

    @pl.when(j == pl.num_programs(0) - 1)
    def _():
        y = x_ref[...] + acc_ref[...]
        y_ref[...] = _rms(y, fw_ref[...]) if final else y


def _mixer_sample(x, nw, w_up, conv_w, w_down, state, gated_ffn, final_w=None, tf=256):
    bd, d = x.shape
    f = w_down.shape[0]
    groups = w_up.shape[1] // f
    convs = conv_w.shape[1] // f
    nf = f // tf
    final = final_w is not None
    in_specs = [pl.BlockSpec((bd, d), lambda j: (0, 0)), pl.BlockSpec(nw.shape, lambda j: (0, 0))]
    in_specs += [pl.BlockSpec((d, tf), lambda j, g=g: (0, g * nf + j)) for g in range(groups)]
    in_specs += [pl.BlockSpec((CONV_W, tf), lambda j, c=c: (0, c * nf + j)) for c in range(convs)]
    in_specs += [pl.BlockSpec((bd, CONV_W - 1, tf), lambda j, c=c: (0, 0, c * nf + j)) for c in range(convs)]
    in_specs += [pl.BlockSpec((tf, d), lambda j: (j, 0))]
    args = [x, nw] + [w_up] * groups + [conv_w] * convs + [state] * convs + [w_down]
    if final:
        in_specs.append(pl.BlockSpec(final_w.shape, lambda j: (0, 0)))
        args.append(final_w)
    out_specs = [pl.BlockSpec((bd, d), lambda j: (0, 0))]
    out_specs += [pl.BlockSpec((bd, tf), lambda j: (0, j))] * convs
    out_shape = [jax.ShapeDtypeStruct((bd, d), F32)] + [jax.ShapeDtypeStruct((bd, f), F32)] * convs
    res = pl.pallas_call(
        functools.partial(_mixer_sample_kernel, groups=groups, convs=convs, gated_ffn=gated_ffn, final=final),
        grid=(nf,),
        in_specs=in_specs,
        out_specs=out_specs,
        out_shape=out_shape,
        scratch_shapes=[pltpu.VMEM((bd, d), BF16), pltpu.VMEM((bd, d), F32)],
        compiler_params=_cparams(("arbitrary",), 32),
        name="ffn_sample" if gated_ffn else "sconv_sample",
    )(*args)
    new = jnp.concatenate(res[1:], axis=-1)
    return res[0], jnp.concatenate([state[:, 1:], new[:, None, :]], axis=1)


def _pad_in_weight(w):
    c = HG_W + QB_W + 3 * KV_W
    per = 3 * B_GROUP
    gates = [jnp.pad(w[:, c + h * per:c + (h + 1) * per], ((0, 0), (0, GATE_PAD - per)))
             for h in range(B_KV_HEADS)]
    return jnp.concatenate([w[:, :c]] + gates, axis=1).astype(BF16)


def _per_head_rows(z, width):
    bd = z.shape[0]
    z = z.reshape(bd, B_KV_HEADS, B_GROUP, -1)
    return jnp.pad(z, ((0, 0), (0, 0), (0, 8 - B_GROUP), (0, width - z.shape[-1])))


def kernel(x_prompt, x_sample, cache_cmp_kv, cache_slc_kv, state_win_kv, state_hgrn, state_sconv, state_ffn,
           page_table, norm_mix, norm_ffn, norm_final, ab_w_in, ab_w_out, hgrn_lb_logits, hgrn_norm,
           cmp_pe, cmp_w1, cmp_w2, c_w_in, c_conv, c_w_out, ffn_up, ffn_conv, ffn_down):
    b, s, d = x_prompt.shape
    bd, t, _ = x_sample.shape
    assert t == 1 and d == D_MODEL
    depth = norm_mix.shape[0]
    n_a = ab_w_in.shape[0]
    wb = state_win_kv.shape[2]
    n_pages = page_table.shape[1]
    past = n_pages * PAGE_SIZE
    assert wb == WINDOW and past % (PAGES_PER_STEP * PAGE_SIZE) == 0 and s % 1024 == 0 and s >= wb
    hh = jnp.arange(1, B_HEADS + 1, dtype=F32)
    slopes = 2.0 ** (-8.0 * hh / B_HEADS)
    lb_logits = hgrn_lb_logits.astype(F32)
    row2 = lambda v: v.reshape(1, -1).astype(F32)

    xp = x_prompt.reshape(b * s, d)
    xs = x_sample.reshape(bd, d)
    outs_p = {k: [] for k in ("cmp", "slc", "win", "hg", "sc", "ff")}
    outs_s = {k: [] for k in ("cmp", "slc", "win", "hg", "sc", "ff")}
    y_prompt = y_sample = None
    for l in range(depth):
        last = l == depth - 1
        if l % 2 == 0:
            la = l // 2
            w_in = _pad_in_weight(ab_w_in[la])
            w_out = ab_w_out[la].astype(BF16)
            nw = row2(norm_mix[l])
            hnw = row2(hgrn_norm[la])
            pe_b, w1bd, w2_packed = _cmp_weights(cmp_pe[la], cmp_w1[la], cmp_w2[la], packed=True)
            _, _, w2_plain = _cmp_weights(cmp_pe[la], cmp_w1[la], cmp_w2[la], packed=False)
            hg, qb, gb, kc, kct, kslt, kwt, sk, sv, wk, wv = _inproj(xp, nw, w_in, tm=512, seq=s)
            sk, sv, wk, wv = [z.reshape(B_KV_HEADS, b, s, -1) for z in (sk, sv, wk, wv)]
            oa, hst = _hgrn_prompt(hg, lb_logits, hnw, la, b, s)
            ck, cv = _compress_prompt(kc.reshape(b, 2 * s, KV_W // 2), pe_b, w1bd, w2_packed)
            n_cmp = s // D_CMP
            nblk = s // L_SLC
            cover = _cover_matrix(n_cmp, nblk, nblk)
            ob = _nsa_prompt(slopes, qb.reshape(b, s, QB_W), gb.reshape(b, s, -1), ck, cv, sk, sv, wk, wv, cover)
            xp = _outproj(xp, oa, ob.reshape(b * s, QB_W), w_out, tm=512)
            kv6 = lambda z, n, tt: z.reshape(n, tt, 2, B_KV_HEADS, B_HD)
            from_t = lambda z: z.reshape(b, 2, B_KV_HEADS, B_HD, -1).transpose(0, 4, 1, 2, 3)
            outs_p["cmp"].append(from_t(kct))
            outs_p["slc"].append(from_t(kslt))
            outs_p["win"].append(from_t(kwt[:, :, s - wb:]))
            outs_p["hg"].append(hst)
            hg, qb, gb, kc, ksl, kw = _inproj(xs, nw, w_in, tm=bd)
            oa, hst = _hgrn_sample(hg, state_hgrn[la], lb_logits, hnw, la)
            pre = _cmp_pages(page_table, cache_cmp_kv, la, w1bd)
            n_cmp = pre.shape[1]
            nblk = past // L_SLC + 1
            cover = _cover_matrix(n_cmp, nblk, -(-nblk // 128) * 128)
            q8 = jnp.stack([jnp.pad(_per_head_rows(qb, B_HD)[:, h], ((0, 0), (0, 0), (h * B_HD, KV_W - (h + 1) * B_HD)))
                            for h in range(B_KV_HEADS)], axis=1)
            idx, phys, o_cmp = _nsa_sample_select(slopes, pre, kc.reshape(bd, 1, KV_W), q8, page_table, pe_b, w1bd,
                                                  w2_plain, cover, past)
            picks = lambda z: z[:, :B_KV_HEADS, :N_SEL].reshape(bd, B_KV_HEADS * N_SEL)
            idx, phys = picks(idx), picks(phys)
            gate8 = _per_head_rows(jnp.concatenate([gb[:, h * GATE_PAD:h * GATE_PAD + 3 * B_GROUP]
                                                    for h in range(B_KV_HEADS)], axis=1), 128)
            win_state = state_win_kv[la].transpose(0, 2, 3, 4, 1).reshape(bd, KV_W, wb)
            o8 = _nsa_sample_attend(phys, idx, slopes, cache_slc_kv, la, q8, ksl.reshape(bd, 1, KV_W),
                                    kw.reshape(bd, 1, KV_W), win_state, gate8, o_cmp, past)
            ob = jnp.concatenate([o8[:, h, :B_GROUP, 2 * B_HD + h * B_HD:2 * B_HD + (h + 1) * B_HD]
                                  .reshape(bd, B_GROUP * B_HD) for h in range(B_KV_HEADS)], axis=1)
            xs = _outproj(xs, oa, ob.astype(BF16), w_out, tm=bd)
            outs_s["cmp"].append(kv6(kc, bd, t))
            outs_s["slc"].append(kv6(ksl, bd, t))
            outs_s["win"].append(jnp.concatenate([state_win_kv[la], kv6(kw, bd, t)], axis=1)[:, t:])
            outs_s["hg"].append(hst)
        else:
            lc = l // 2
            xp3, tail = _mixer_prompt(xp.reshape(b, s, d), row2(norm_mix[l]), c_w_in[lc].astype(BF16),
                                      c_conv[lc], c_w_out[lc].astype(BF16), gated_ffn=False)
            xp = xp3.reshape(b * s, d)
            outs_p["sc"].append(tail)
            xs, new = _mixer_sample(xs, row2(norm_mix[l]), c_w_in[lc].astype(BF16), c_conv[lc],
                                    c_w_out[lc].astype(BF16), state_sconv[lc], gated_ffn=False)
            outs_s["sc"].append(new)
        fw = row2(norm_final) if last else None
        xp3, tail = _mixer_prompt(xp.reshape(b, s, d), row2(norm_ffn[l]), ffn_up[l].astype(BF16), ffn_conv[l],
                                  ffn_down[l].astype(BF16), gated_ffn=True, final_w=fw)
        xp = xp3.reshape(b * s, d)
        outs_p["ff"].append(tail)
        xs, new = _mixer_sample(xs, row2(norm_ffn[l]), ffn_up[l].astype(BF16), ffn_conv[l],
                                ffn_down[l].astype(BF16), state_ffn[l], gated_ffn=True, final_w=fw)
        outs_s["ff"].append(new)
    y_prompt = xp.reshape(b, s, d)
    y_sample = xs.reshape(bd, t, d)
    return (y_prompt, y_sample,
            jnp.stack(outs_p["cmp"], axis=2), jnp.stack(outs_p["slc"], axis=2), jnp.stack(outs_p["win"], axis=0),
            jnp.stack(outs_p["hg"], axis=0), jnp.stack(outs_p["sc"], axis=0), jnp.stack(outs_p["ff"], axis=0),
            jnp.stack(outs_s["cmp"], axis=2), jnp.stack(outs_s["slc"], axis=2), jnp.stack(outs_s["win"], axis=0),
            jnp.stack(outs_s["hg"], axis=0), jnp.stack(outs_s["sc"], axis=0), jnp.stack(outs_s["ff"], axis=0))
```

```python
import functools

import jax
import jax.numpy as jnp
from jax import lax
from jax.experimental import pallas as pl
from jax.experimental.pallas import tpu as pltpu

F32 = jnp.float32
BF16 = jnp.bfloat16
NEG_INF = float("-inf")

D_MODEL = 1024
PAGE_SIZE = 128
A_HEADS = 4
A_DK = 128
A_DV = 128
B_HEADS = 8
B_KV_HEADS = 2
B_GROUP = B_HEADS // B_KV_HEADS
B_HD = 64
L_CMP = 32
D_CMP = 16
CMP_HID = 64
L_SLC = 64
SLC_SHIFT = 6
N_SEL = 16
WINDOW = 512
SEL_FORCE = 1e9
CONV_W = 3
EPS = 1e-6
KV_W = 2 * B_KV_HEADS * B_HD
HG_W = 4 * A_HEADS * A_DK
QB_W = B_HEADS * B_HD
GATE_PAD = 128
IN_PAD = HG_W + QB_W + 3 * KV_W + B_KV_HEADS * GATE_PAD

V7X_VMEM_BYTES = 64 * 2**20


def _cparams(sem, vmem_mb):
    assert vmem_mb * 2**20 < V7X_VMEM_BYTES
    return pltpu.CompilerParams(dimension_semantics=sem, vmem_limit_bytes=vmem_mb * 2**20)


def _dot(a, b):
    return jnp.dot(a, b, preferred_element_type=F32)


def _dot_nt(a, b):
    return lax.dot_general(a, b, (((1,), (1,)), ((), ())), preferred_element_type=F32)


def _dot_tn(a, b):
    return lax.dot_general(a, b, (((0,), (0,)), ((), ())), preferred_element_type=F32)


def _split3(x):
    hi = x.astype(BF16)
    r1 = x - hi.astype(F32)
    mid = r1.astype(BF16)
    lo = (r1 - mid.astype(F32)).astype(BF16)
    return hi, mid, lo


def _dot3_lhs(x, w):
    hi, mid, lo = _split3(x)
    return _dot(hi, w) + _dot(mid, w) + _dot(lo, w)


def _dot3_rhs(w, x):
    hi, mid, lo = _split3(x)
    return _dot(w, hi) + _dot(w, mid) + _dot(w, lo)


def _rms(x, w):
    return x * lax.rsqrt(jnp.mean(x * x, axis=-1, keepdims=True) + EPS) * w


def _masked_softmax(s, mask):
    s = jnp.where(mask, s, NEG_INF)
    m = jnp.max(s, axis=-1, keepdims=True)
    m = jnp.where(m > NEG_INF, m, 0.0)
    e = jnp.exp(s - m)
    d = jnp.sum(e, axis=-1, keepdims=True)
    return e * (1.0 / jnp.where(d > 0, d, 1.0))


def _iota(shape, dim):
    return lax.broadcasted_iota(jnp.int32, shape, dim)


def _top_blocks(score, jb, n_sel):
    big = float(score.shape[-1])
    jbf = jb.astype(F32)

    def body(_, carry):
        sc, sel = carry
        m = jnp.max(sc, axis=-1, keepdims=True)
        idx = jnp.min(jnp.where(sc == m, jbf, big), axis=-1, keepdims=True)
        pick = jbf == idx
        return jnp.where(pick, NEG_INF, sc), jnp.where(pick, 1.0, sel)

    _, sel = lax.fori_loop(0, n_sel, body, (score, jnp.zeros_like(score)))
    return sel


def _attention_rows(kv, pos, with_blocks):
    tm = kv.shape[0]
    lane = _iota((tm, 128), 1)
    lo = lane < B_HD
    pos_cols = jnp.where(lane == B_HD, (pos >> 7).astype(F32),
                         jnp.where(lane == B_HD + 1, (pos & 127).astype(F32), 0.0))
    k_pair, v_pair = kv[:, 0:128], kv[:, 128:256]
    k_swap, v_swap = pltpu.roll(k_pair, B_HD, axis=1), pltpu.roll(v_pair, B_HD, axis=1)
    onehot = ((pos >> SLC_SHIFT) == lane).astype(BF16)
    keys, values = [], []
    for h, (k, v_lo, v_hi) in enumerate(((k_pair, v_pair, v_swap), (k_swap, v_swap, v_pair))):
        key = jnp.where(lo, k, pos_cols).astype(BF16)
        keys.append(jnp.concatenate([key, onehot], axis=1) if with_blocks else key)
        values.append(jnp.where(lo, v_lo, v_hi).astype(BF16))
    return keys, values


def _inproj_kernel(x_ref, nw_ref, w_ref, hg_ref, qb_ref, gb_ref, *refs, seq):
    xn = _rms(x_ref[...], nw_ref[...]).astype(BF16)

    def proj(c0, n):
        return _dot(xn, w_ref[:, c0:c0 + n])

    hg_ref[...] = proj(0, HG_W)
    qb_ref[...] = (proj(HG_W, QB_W) * (B_HD ** -0.5)).astype(BF16)
    c = HG_W + QB_W
    kc = proj(c, KV_W)
    ksl = proj(c + KV_W, KV_W)
    kw = proj(c + 2 * KV_W, KV_W)
    gb_ref[...] = proj(c + 3 * KV_W, B_KV_HEADS * GATE_PAD)
    if seq is None:
        kc_ref, ksl_ref, kw_ref = refs
        kc_ref[...] = kc
        ksl_ref[...] = ksl
        kw_ref[...] = kw
    else:
        kck_ref, kcv_ref, kct_ref, kslt_ref, kwt_ref, sk_ref, sv_ref, wk_ref, wv_ref = refs
        kck_ref[...] = kc[:, 0:KV_W // 2]
        kcv_ref[...] = kc[:, KV_W // 2:KV_W]
        kct_ref[0] = kc.T
        kslt_ref[0] = ksl.T
        kwt_ref[0] = kw.T
        tm = x_ref.shape[0]
        pos = (pl.program_id(0) * tm + _iota((tm, 1), 0)) & (seq - 1)
        for kv, k_ref, v_ref, with_blocks in ((ksl, sk_ref, sv_ref, True), (kw, wk_ref, wv_ref, False)):
            keys, values = _attention_rows(kv, pos, with_blocks)
            for h in range(B_KV_HEADS):
                k_ref[h] = keys[h]
                v_ref[h] = values[h]


def _inproj(x, nw, w_pad, tm, seq=None):
    m = x.shape[0]
    row = lambda n: pl.BlockSpec((tm, n), lambda i: (i, 0))
    full = lambda a: pl.BlockSpec(a.shape, lambda i: (0,) * a.ndim)
    widths = (HG_W, QB_W, B_KV_HEADS * GATE_PAD)
    dtypes = (F32, BF16, F32)
    out_specs = [row(n) for n in widths]
    out_shape = [jax.ShapeDtypeStruct((m, n), d) for n, d in zip(widths, dtypes)]
    if seq is None:
        out_specs += [row(KV_W)] * 3
        out_shape += [jax.ShapeDtypeStruct((m, KV_W), F32)] * 3
    else:
        assert seq // L_SLC == 128 and seq % tm == 0 and seq & (seq - 1) == 0
        nt = seq // tm
        out_specs += [row(KV_W // 2)] * 2
        out_shape += [jax.ShapeDtypeStruct((m, KV_W // 2), F32)] * 2
        out_specs += [pl.BlockSpec((1, KV_W, tm), lambda i: (i // nt, 0, i % nt))] * 3
        out_shape += [jax.ShapeDtypeStruct((m // seq, KV_W, seq), F32)] * 3
        for n in (256, 128, 128, 128):
            out_specs.append(pl.BlockSpec((B_KV_HEADS, tm, n), lambda i: (0, i, 0)))
            out_shape.append(jax.ShapeDtypeStruct((B_KV_HEADS, m, n), BF16))
    return pl.pallas_call(
        functools.partial(_inproj_kernel, seq=seq),
        grid=(m // tm,),
        in_specs=[row(D_MODEL), full(nw), full(w_pad)],
        out_specs=out_specs,
        out_shape=out_shape,
        compiler_params=_cparams(("parallel",), 48),
        name="inproj",
    )(x, nw, w_pad)


def _lower_bound(logit_ref, la):
    lg = logit_ref[...]
    e = jnp.exp(lg - jnp.max(lg, axis=0, keepdims=True))
    return jnp.sum(e[:la + 1], axis=0, keepdims=True) / jnp.sum(e, axis=0, keepdims=True)


def _hgrn_post(o, nw, ga):
    o = o * lax.rsqrt(jnp.mean(o * o, axis=-1, keepdims=True) + EPS)
    return o * nw * jax.nn.sigmoid(ga)


def _hgrn_prompt_kernel(hg_ref, lg_ref, nw_ref, oa_ref, st_ref, s_ref, *, la, chunk, sub):
    ci = pl.program_id(1)

    @pl.when(ci == 0)
    def _():
        s_ref[...] = jnp.zeros_like(s_ref)

    lb = _lower_bound(lg_ref, la)
    tri = (_iota((chunk, chunk), 0) >= _iota((chunk, chunk), 1)).astype(BF16)
    srow = _iota((sub, 1), 0)
    lane = _iota((sub, chunk), 1)
    for h in range(A_HEADS):
        hs = slice(h * A_DK, (h + 1) * A_DK)
        q = hg_ref[:, h * A_DK:(h + 1) * A_DK]
        fl = hg_ref[:, 512 + h * A_DK:512 + (h + 1) * A_DK]
        v = hg_ref[:, 1024 + h * A_DV:1024 + (h + 1) * A_DV]
        ga = hg_ref[:, 1536 + h * A_DV:1536 + (h + 1) * A_DV]
        lbh = lb[:, hs]
        f = lbh + (1.0 - lbh) * jax.nn.sigmoid(fl)
        a = _dot3_rhs(tri, jnp.log(f))
        k = 1.0 - f
        st = s_ref[h]
        vb = v.astype(BF16)
        o = _dot_nt((q * jnp.exp(a)).astype(BF16), st.astype(BF16))
        rows = []
        for i in range(chunk // sub):
            r0 = i * sub
            qi, ai = q[r0:r0 + sub], a[r0:r0 + sub]
            att = jnp.zeros((sub, chunk), F32)
            for s in range(sub):
                dec = jnp.exp(jnp.where(srow >= s, ai - a[r0 + s:r0 + s + 1], NEG_INF))
                col = jnp.sum(qi * dec * k[r0 + s:r0 + s + 1], axis=-1, keepdims=True)
                att = jnp.where(lane == r0 + s, col, att)
            if i > 0:
                ref_a = a[r0 - 1:r0]
                qt = qi * jnp.exp(ai - ref_a)
                kt = jnp.concatenate([k[:r0] * jnp.exp(ref_a - a[:r0]), jnp.zeros((chunk - r0, A_DK), F32)], axis=0)
                att = att + _dot_nt(qt.astype(BF16), kt.astype(BF16))
            rows.append(att)
        att = jnp.concatenate(rows, axis=0)
        o = o + _dot(att.astype(BF16), vb)
        a_end = a[chunk - 1:chunk]
        khat = k * jnp.exp(a_end - a)
        s_ref[h] = st * jnp.exp(a_end) + _dot_tn(vb, khat.astype(BF16))
        oa_ref[:, hs] = _hgrn_post(o, nw_ref[:, hs], ga).astype(BF16)

    @pl.when(ci == pl.num_programs(1) - 1)
    def _():
        for h in range(A_HEADS):
            st_ref[0, h] = s_ref[h].T


def _hgrn_prompt(hg, lb_logits, nw, la, b, s, chunk=128, sub=16):
    n = s // chunk
    full = lambda a: pl.BlockSpec(a.shape, lambda i, j: (0,) * a.ndim)
    return pl.pallas_call(
        functools.partial(_hgrn_prompt_kernel, la=la, chunk=chunk, sub=sub),
        grid=(b, n),
        in_specs=[pl.BlockSpec((chunk, HG_W), lambda i, j: (i * n + j, 0)), full(lb_logits), full(nw)],
        out_specs=[pl.BlockSpec((chunk, A_HEADS * A_DV), lambda i, j: (i * n + j, 0)),
                   pl.BlockSpec((1, A_HEADS, A_DK, A_DV), lambda i, j: (i, 0, 0, 0))],
        out_shape=[jax.ShapeDtypeStruct((b * s, A_HEADS * A_DV), BF16),
                   jax.ShapeDtypeStruct((b, A_HEADS, A_DK, A_DV), F32)],
        scratch_shapes=[pltpu.VMEM((A_HEADS, A_DV, A_DK), F32)],
        compiler_params=_cparams(("parallel", "arbitrary"), 32),
        name="hgrn_prompt",
    )(hg, lb_logits, nw)


def _hgrn_sample_kernel(hg_ref, s0_ref, lg_ref, nw_ref, oa_ref, st_ref, *, la):
    lb = _lower_bound(lg_ref, la)
    eye = (_iota((A_DK, A_DK), 0) == _iota((A_DK, A_DK), 1)).astype(F32)

    def col(r):
        return jnp.sum(eye * r, axis=1, keepdims=True)

    for h in range(A_HEADS):
        hs = slice(h * A_DK, (h + 1) * A_DK)
        q = hg_ref[0, :, h * A_DK:(h + 1) * A_DK]
        fl = hg_ref[0, :, 512 + h * A_DK:512 + (h + 1) * A_DK]
        v = hg_ref[0, :, 1024 + h * A_DV:1024 + (h + 1) * A_DV]
        ga = hg_ref[0, :, 1536 + h * A_DV:1536 + (h + 1) * A_DV]
        lbh = lb[:, hs]
        f = lbh + (1.0 - lbh) * jax.nn.sigmoid(fl)
        sn = col(f) * s0_ref[0, h] + col(1.0 - f) * v
        st_ref[0, h] = sn
        o = jnp.sum(col(q) * sn, axis=0, keepdims=True)
        oa_ref[0, :, hs] = _hgrn_post(o, nw_ref[:, hs], ga).astype(BF16)


def _hgrn_sample(hg, s0, lb_logits, nw, la):
    bd = hg.shape[0]
    full = lambda a: pl.BlockSpec(a.shape, lambda i: (0,) * a.ndim)
    st_spec = pl.BlockSpec((1, A_HEADS, A_DK, A_DV), lambda i: (i, 0, 0, 0))
    oa, st = pl.pallas_call(
        functools.partial(_hgrn_sample_kernel, la=la),
        grid=(bd,),
        in_specs=[pl.BlockSpec((1, 1, HG_W), lambda i: (i, 0, 0)), st_spec, full(lb_logits), full(nw)],
        out_specs=[pl.BlockSpec((1, 1, A_HEADS * A_DV), lambda i: (i, 0, 0)), st_spec],
        out_shape=[jax.ShapeDtypeStruct((bd, 1, A_HEADS * A_DV), BF16),
                   jax.ShapeDtypeStruct(s0.shape, F32)],
        compiler_params=_cparams(("parallel",), 16),
        name="hgrn_sample",
    )(hg.reshape(bd, 1, HG_W), s0, lb_logits, nw)
    return oa.reshape(bd, A_HEADS * A_DV), st


def _cmp_weights(pe, w1, w2, packed):
    eye = jnp.eye(B_KV_HEADS, dtype=F32)
    w1r = w1.reshape(2, L_CMP, B_HD, CMP_HID)
    w1bd = jnp.einsum("kidm,hH->ikhdHm", w1r, eye).reshape(L_CMP, 2, KV_W // 2, KV_W // 2)
    pe_b = jnp.broadcast_to(pe.transpose(1, 0, 2)[:, :, None, :], (L_CMP, 2, B_KV_HEADS, B_HD))
    if not packed:
        w2bd = jnp.einsum("kmd,kK,hH->khmKHd", w2, eye, eye).reshape(KV_W, KV_W)
        return pe_b.reshape(L_CMP, KV_W), w1bd.astype(BF16), w2bd.astype(BF16)
    half = KV_W // 2
    zero = jnp.zeros((half, KV_W), F32)
    k_blk = jnp.concatenate([w2[0], jnp.zeros_like(w2[0])], axis=1)
    v_blk = jnp.concatenate([w2[1], w2[1]], axis=1)
    w2k = jnp.concatenate([jnp.einsum("ml,hH->hmHl", k_blk, eye).reshape(half, KV_W), zero], axis=0)
    w2v = jnp.concatenate([zero, jnp.einsum("ml,hH->hmHl", v_blk, eye).reshape(half, KV_W)], axis=0)
    return pe_b.reshape(L_CMP, KV_W), w1bd.astype(BF16), (w2k.astype(BF16), w2v.astype(BF16))


def _chunk_preacts(load, n, pe_ref, w1_ref):
    half = KV_W // 2
    p0 = [jnp.zeros((n, half), F32) for _ in range(2)]
    p1 = [jnp.zeros((n, half), F32) for _ in range(2)]
    for i in range(D_CMP):
        for kv in range(2):
            x = load(i, kv)
            ls = slice(kv * half, (kv + 1) * half)
            p0[kv] = p0[kv] + _dot((x + pe_ref[i:i + 1, ls]).astype(BF16), w1_ref[i, kv])
            p1[kv] = p1[kv] + _dot((x + pe_ref[D_CMP + i:D_CMP + i + 1, ls]).astype(BF16), w1_ref[D_CMP + i, kv])
    return jnp.concatenate(p0, axis=1), jnp.concatenate(p1, axis=1)


def _with_position(k, pos, lane):
    col = lane & 127
    return jnp.where(col == B_HD, (pos >> 7).astype(F32), jnp.where(col == B_HD + 1, (pos & 127).astype(F32), k))


def _compress_kernel(kck_ref, kcv_ref, pe_ref, w1_ref, w2k_ref, w2v_ref, ck_ref, cv_ref):
    n = ck_ref.shape[1]
    p0, p1 = _chunk_preacts(lambda i, kv: (kck_ref, kcv_ref)[kv][0, pl.ds(i, n, stride=D_CMP), :], n, pe_ref, w1_ref)
    pre = p0 + pltpu.roll(p1, n - 1, axis=0)
    hid = (pre * jax.nn.sigmoid(pre)).astype(BF16)
    cpos = _iota((n, 1), 0) * D_CMP + (L_CMP - 1)
    ck_ref[0] = _with_position(_dot(hid, w2k_ref[...]), cpos, _iota((n, KV_W), 1)).astype(BF16)
    cv_ref[0] = _dot(hid, w2v_ref[...]).astype(BF16)


def _compress_prompt(kck, kcv, pe_b, w1bd, w2kv):
    b, s, _ = kck.shape
    n = s // D_CMP
    full = lambda a: pl.BlockSpec(a.shape, lambda i: (0,) * a.ndim)
    out = pl.BlockSpec((1, n, KV_W), lambda i: (i, 0, 0))
    seq = pl.BlockSpec((1, s, KV_W // 2), lambda i: (i, 0, 0))
    return pl.pallas_call(
        _compress_kernel,
        grid=(b,),
        in_specs=[seq, seq, full(pe_b), full(w1bd), full(w2kv[0]), full(w2kv[1])],
        out_specs=[out, out],
        out_shape=[jax.ShapeDtypeStruct((b, n, KV_W), BF16)] * 2,
        compiler_params=_cparams(("parallel",), 48),
        name="compress_prompt",
    )(kck, kcv, pe_b, w1bd, *w2kv)


MASK_BIAS = 2.0 ** 100

def _top_blocks_t(score_t, n_sel):
    nblk = score_t.shape[0]
    blk = _iota(score_t.shape, 0).astype(F32)

    def body(_, carry):
        sc, sel = carry
        m = jnp.max(sc, axis=0, keepdims=True)
        idx = jnp.min(jnp.where(sc == m, blk, float(nblk)), axis=0, keepdims=True)
        pick = blk == idx
        return jnp.where(pick, NEG_INF, sc), jnp.where(pick, 1.0, sel)

    _, sel = lax.fori_loop(0, n_sel, body, (score_t, jnp.zeros_like(score_t)), unroll=True)
    return sel


def _nsa_prompt_kernel(sl_ref, q_ref, g_ref, ck_ref, cv_ref, sk_ref, sv_ref, wk_ref, wv_ref, cover_ref, o_ref,
                       m_ref, l_ref, acc_ref, sa_ref, sb_ref, tiles_ref, *, tq, tk):
    kvh = pl.program_id(1)
    q0 = pl.program_id(2) * tq
    rows = B_GROUP * tq
    lane = _iota((tq, 128), 1)
    lo = lane < B_HD
    qf = q_ref[0].astype(F32)
    left = []
    for g in range(B_GROUP):
        pair = qf[:, 128 * (g // 2):128 * (g // 2) + 128]
        if g % 2:
            pair = pltpu.roll(pair, B_HD, axis=1)
        slope = sl_ref[kvh * B_GROUP + g]
        pos_cols = jnp.where(lane == B_HD, slope * 128.0, jnp.where(lane == B_HD + 1, slope, 0.0))
        left.append(jnp.where(lo, pair, pos_cols).astype(BF16))
    ql = jnp.concatenate(left, axis=0)
    qpos = q0 + (_iota((rows, 1), 0) & (tq - 1))

    nc = ck_ref.shape[1]
    cpos = _iota((1, nc), 1) * D_CMP + (L_CMP - 1)
    span = WINDOW + tq
    start = pl.multiple_of(jnp.maximum(q0 - WINDOW, 0), 128)
    wpos = start + _iota((1, span), 1)
    s_cmp = _dot_nt(ql, ck_ref[0])
    s_win = _dot_nt(ql, wk_ref[0, 0, pl.ds(start, span), :])
    p = _masked_softmax(s_cmp, cpos <= qpos)
    pw = _masked_softmax(s_win, (wpos <= qpos) & (wpos > qpos - WINDOW))
    o_cmp = _dot(p.astype(BF16), cv_ref[0])
    o_win = _dot(pw.astype(BF16), wv_ref[0, 0, pl.ds(start, span), :])
    imp = p[0:tq]
    for g in range(1, B_GROUP):
        imp = imp + p[g * tq:(g + 1) * tq]

    nblk = cover_ref.shape[1]
    jb = _iota((1, nblk), 1)
    cur = (q0 + _iota((tq, 1), 0)) >> SLC_SHIFT
    forced = (jb == 0) | (jb == cur) | (jb == cur - 1)
    score = _dot3_lhs(imp, cover_ref[...])
    score = jnp.where(forced | (jb > cur), NEG_INF, score)
    n_forced = 3
    sel = jnp.where(forced, 1.0, _top_blocks_t(score.T, min(N_SEL, nblk) - n_forced).T)

    bias = ((sel - 1.0) * MASK_BIAS).astype(BF16)
    qs = jnp.concatenate([ql, jnp.concatenate([bias] * B_GROUP, axis=0)], axis=1)
    m_ref[...] = jnp.full_like(m_ref, NEG_INF)
    l_ref[...] = jnp.zeros_like(l_ref)
    acc_ref[...] = jnp.zeros_like(acc_ref)

    def scores(kt):
        return _dot_nt(qs, sk_ref[0, 0, pl.ds(pl.multiple_of(kt * tk, tk), tk), :])

    def update(s, kt, causal):
        k0 = pl.multiple_of(kt * tk, tk)
        if causal:
            s = jnp.where(k0 + _iota((1, tk), 1) <= qpos, s, NEG_INF)
        m_old = m_ref[...]
        m_new = jnp.maximum(m_old, jnp.max(s, axis=-1, keepdims=True))
        alpha = jnp.exp(m_old - m_new)
        e = jnp.exp(s - m_new)
        l_ref[...] = alpha * l_ref[...] + jnp.sum(e, axis=-1, keepdims=True)
        acc_ref[...] = alpha * acc_ref[...] + _dot(e.astype(BF16), sv_ref[0, 0, pl.ds(k0, tk), :])
        m_ref[...] = m_new

    n_diag = q0 // tk
    bpt = tk // L_SLC
    picked = jnp.max(sel, axis=0, keepdims=True)
    n_full = jnp.int32(0)
    for kt in range(nblk * L_SLC // tk):
        in_tile = (jb >= kt * bpt) & (jb < (kt + 1) * bpt)
        wanted = (jnp.max(jnp.where(in_tile, picked, 0.0)) > 0.5) & (kt < n_diag)
        tiles_ref[n_full] = jnp.int32(kt)
        n_full = n_full + wanted.astype(jnp.int32)
    tiles_ref[n_full] = n_diag

    sa_ref[...] = scores(tiles_ref[0])

    def tile_pair(i, carry):
        sb_ref[...] = scores(tiles_ref[2 * i + 1])
        update(sa_ref[...], tiles_ref[2 * i], False)
        sa_ref[...] = scores(tiles_ref[2 * i + 2])
        update(sb_ref[...], tiles_ref[2 * i + 1], False)
        return carry

    lax.fori_loop(0, n_full // 2, tile_pair, 0)

    @pl.when(n_full % 2 == 1)
    def _():
        sb_ref[...] = scores(n_diag)
        update(sa_ref[...], tiles_ref[n_full - 1], False)
        update(sb_ref[...], n_diag, True)

    @pl.when(n_full % 2 == 0)
    def _():
        update(sa_ref[...], n_diag, True)

    o_slc = acc_ref[...] * (1.0 / l_ref[...])

    gate = jax.nn.sigmoid(g_ref[0])
    outs = []
    for g in range(B_GROUP):
        rs = slice(g * tq, (g + 1) * tq)
        outs.append(gate[:, 3 * g:3 * g + 1] * o_cmp[rs] + gate[:, 3 * g + 1:3 * g + 2] * o_slc[rs]
                    + gate[:, 3 * g + 2:3 * g + 3] * o_win[rs])
    for j in range(B_GROUP // 2):
        o_ref[0, :, 128 * j:128 * j + 128] = jnp.where(lo, outs[2 * j], outs[2 * j + 1]).astype(BF16)


def _nsa_prompt(slopes, qb, gb, ck, cv, sk, sv, wk, wv, cover, tq=128, tk=512):
    b, s, _ = qb.shape
    half = QB_W // B_KV_HEADS
    rows = B_GROUP * tq
    assert cover.shape[1] == 128 and s % tk == 0 and tk % tq == 0
    seq = lambda a: pl.BlockSpec((1, 1) + a.shape[2:], lambda i, h, t, sl: (h, i, 0, 0))
    grid_spec = pltpu.PrefetchScalarGridSpec(
        num_scalar_prefetch=1,
        grid=(b, B_KV_HEADS, s // tq),
        in_specs=[
            pl.BlockSpec((1, tq, half), lambda i, h, t, sl: (i, t, h)),
            pl.BlockSpec((1, tq, GATE_PAD), lambda i, h, t, sl: (i, t, h)),
            pl.BlockSpec((1, ck.shape[1], 128), lambda i, h, t, sl: (i, 0, h)),
            pl.BlockSpec((1, cv.shape[1], 128), lambda i, h, t, sl: (i, 0, h)),
            seq(sk), seq(sv), seq(wk), seq(wv),
            pl.BlockSpec(cover.shape, lambda i, h, t, sl: (0, 0)),
        ],
        out_specs=pl.BlockSpec((1, tq, half), lambda i, h, t, sl: (i, t, h)),
        scratch_shapes=[pltpu.VMEM((rows, 1), F32), pltpu.VMEM((rows, 1), F32), pltpu.VMEM((rows, 128), F32),
                        pltpu.VMEM((rows, tk), F32), pltpu.VMEM((rows, tk), F32),
                        pltpu.SMEM((s // tk + 1,), jnp.int32)],
    )
    return pl.pallas_call(
        functools.partial(_nsa_prompt_kernel, tq=tq, tk=tk),
        grid_spec=grid_spec,
        out_shape=jax.ShapeDtypeStruct((b, s, QB_W), BF16),
        compiler_params=_cparams(("parallel", "parallel", "arbitrary"), 56),
        name="nsa_prompt",
    )(slopes, qb, gb, ck, cv, sk, sv, wk, wv, cover)


def _cover_matrix(n_cmp, n_blk, n_pad):
    ci = jnp.arange(n_cmp)[:, None] * D_CMP
    sj = jnp.arange(n_pad)[None, :] * L_SLC
    cov = (ci < sj + L_SLC) & (ci + L_CMP > sj) & (jnp.arange(n_pad)[None, :] < n_blk)
    return cov.astype(BF16)


PAGES_PER_STEP = 64
PAGE_GROUPS = 2


def _cmp_pages_kernel(pt_ref, *refs):
    pages = refs[:PAGES_PER_STEP]
    w_ref, out_ref = refs[PAGES_PER_STEP:PAGES_PER_STEP + 2]
    row_refs = refs[PAGES_PER_STEP + 2:]
    per_group = PAGES_PER_STEP // PAGE_GROUPS
    n = per_group * PAGE_SIZE // D_CMP
    half = KV_W // 2
    for grp, rows_ref in enumerate(row_refs):
        for k, pg in enumerate(pages[grp * per_group:(grp + 1) * per_group]):
            for kv in range(2):
                rows_ref[kv, k * PAGE_SIZE:(k + 1) * PAGE_SIZE, :] = pg[0, kv * half:(kv + 1) * half, :].T
        for kv in range(2):
            x = jnp.concatenate([rows_ref[kv, pl.ds(i, n, stride=D_CMP), :].astype(BF16) for i in range(D_CMP)],
                                axis=1)
            y = _dot(x, w_ref[kv])
            for part in range(2):
                c0 = part * KV_W + kv * half
                out_ref[0, grp * n:(grp + 1) * n, c0:c0 + half] = y[:, part * half:(part + 1) * half]


def _pages_weight(w1bd):
    w = jnp.concatenate([w1bd[:D_CMP], w1bd[D_CMP:]], axis=3)
    return w.transpose(1, 0, 2, 3).reshape(2, D_CMP * (KV_W // 2), KV_W)


def _feature_major_pages(cache):
    return cache.transpose(0, 2, 3, 4, 5, 1).reshape(cache.shape[0], -1, cache.shape[1])


def _cmp_pages(page_table, cache, li, w1bd):
    bd, n_pages = page_table.shape
    pool = _feature_major_pages(cache)
    steps = n_pages // PAGES_PER_STEP
    rows = PAGES_PER_STEP * PAGE_SIZE // D_CMP
    w = _pages_weight(w1bd)

    def page_spec(k):
        return pl.BlockSpec((1, KV_W, PAGE_SIZE), lambda i, j, pt: (pt[i, j * PAGES_PER_STEP + k], li, 0))

    grid_spec = pltpu.PrefetchScalarGridSpec(
        num_scalar_prefetch=1,
        grid=(bd, steps),
        in_specs=[page_spec(k) for k in range(PAGES_PER_STEP)] + [pl.BlockSpec(w.shape, lambda i, j, pt: (0, 0, 0))],
        out_specs=pl.BlockSpec((1, rows, 2 * KV_W), lambda i, j, pt: (i, j, 0)),
        scratch_shapes=[pltpu.VMEM((2, PAGES_PER_STEP // PAGE_GROUPS * PAGE_SIZE, KV_W // 2), F32)] * PAGE_GROUPS,
    )
    return pl.pallas_call(
        _cmp_pages_kernel,
        grid_spec=grid_spec,
        out_shape=jax.ShapeDtypeStruct((bd, steps * rows, 2 * KV_W), F32),
        compiler_params=_cparams(("parallel", "arbitrary"), 48),
        name="cmp_pages",
    )(page_table, *([pool] * PAGES_PER_STEP), w)


def _nsa_sample_select_kernel(sl_ref, pre_ref, new_ref, q_ref, pt_ref, pe_ref, w1_ref, w2_ref, cover_ref,
                              idx_ref, phys_ref, ocmp_ref, tail_ref, score_ref, *, past):
    half = KV_W // 2

    def row_act(i, r):
        r8 = jnp.broadcast_to(r, (8, KV_W)).astype(BF16)
        return jnp.concatenate([_dot(r8[:, kv * half:(kv + 1) * half], w1_ref[i, kv]) for kv in range(2)], axis=1)

    @pl.when(pl.program_id(0) == 0)
    def _():
        t = jnp.zeros((8, KV_W), F32)
        for i in range(L_CMP):
            t = t + row_act(i, pe_ref[i:i + 1])
        tail_ref[...] = t

    n = pre_ref.shape[1]
    p1 = pltpu.roll(pre_ref[0, :, KV_W:2 * KV_W], n - 1, axis=0)
    p1 = jnp.where(_iota((n, 1), 0) == n - 1, row_act(D_CMP, new_ref[0])[0:1], p1)
    pre = pre_ref[0, :, 0:KV_W] + p1 + tail_ref[0:1]
    comp = _dot((pre * jax.nn.sigmoid(pre)).astype(BF16), w2_ref[...]).astype(BF16)

    qpos = past
    cpos = _iota((1, n), 1) * D_CMP + (L_CMP - 1)
    cmask = cpos <= qpos
    cdist = (qpos - cpos).astype(F32)
    row8 = _iota((8, 1), 0)
    imp = jnp.zeros((8, n), F32)
    for kvh in range(B_KV_HEADS):
        slope = jnp.zeros((8, 1), F32)
        for g in range(B_GROUP):
            slope = jnp.where(row8 == g, sl_ref[kvh * B_GROUP + g], slope)
        p = _masked_softmax(_dot_nt(q_ref[0, kvh], comp) - slope * cdist, cmask)
        p = jnp.where(row8 < B_GROUP, p, 0.0)
        ocmp_ref[0, kvh] = _dot(p.astype(BF16), comp)
        imp = jnp.where(row8 == kvh, jnp.sum(p, axis=0, keepdims=True), imp)

    nblk = cover_ref.shape[1]
    jb = _iota((1, nblk), 1)
    cur = qpos // L_SLC
    score = _dot3_lhs(imp, cover_ref[...])
    score = jnp.where((jb == 0) | (jb == cur) | (jb == cur - 1), SEL_FORCE, score)
    b = pl.program_id(0)
    score_ref[b] = jnp.where(jb > cur, NEG_INF, score)

    @pl.when(b == pl.num_programs(0) - 1)
    def _():
        bd = score_ref.shape[0]
        lane = _iota((1, 128), 1)
        jbf = jb.astype(F32)
        pages = jnp.broadcast_to(pt_ref[...].astype(F32), (bd, 8, 128)).reshape(bd * 8, 128)
        bpp_shift = (PAGE_SIZE // L_SLC).bit_length() - 1

        def body(it, carry):
            sc, idx, phys = carry
            m = jnp.max(sc, axis=-1, keepdims=True)
            j = jnp.min(jnp.where(sc == m, jbf, float(nblk)), axis=-1, keepdims=True)
            logical = jnp.minimum(j.astype(jnp.int32), past // L_SLC - 1) >> bpp_shift
            page = jnp.sum(jnp.where(lane == logical, pages, 0.0), axis=-1, keepdims=True)
            return (jnp.where(jbf == j, NEG_INF, sc), jnp.where(lane == it, j, idx),
                    jnp.where(lane == it, page, phys))

        zero = jnp.zeros((bd * 8, 128), F32)
        _, idx, phys = lax.fori_loop(0, N_SEL, body, (score_ref[...].reshape(bd * 8, nblk), zero, zero))
        idx_ref[...] = idx.astype(jnp.int32).reshape(bd, 8, 128)
        phys_ref[...] = phys.astype(jnp.int32).reshape(bd, 8, 128)


def _nsa_sample_select(slopes, pre, cmp_new, q8, page_table, pe_b, w1bd, w2bd, cover, past):
    bd = pre.shape[0]
    assert page_table.shape[1] == 128
    full = lambda a: pl.BlockSpec(a.shape, lambda i, sl: (0,) * a.ndim)
    picks = pl.BlockSpec((bd, 8, 128), lambda i, sl: (0, 0, 0))
    pt = page_table.reshape(bd, 1, 128)
    grid_spec = pltpu.PrefetchScalarGridSpec(
        num_scalar_prefetch=1,
        grid=(bd,),
        in_specs=[
            pl.BlockSpec((1,) + pre.shape[1:], lambda i, sl: (i, 0, 0)),
            pl.BlockSpec((1, 1, KV_W), lambda i, sl: (i, 0, 0)),
            pl.BlockSpec((1, B_KV_HEADS, 8, KV_W), lambda i, sl: (i, 0, 0, 0)),
            full(pt), full(pe_b), full(w1bd), full(w2bd), full(cover),
        ],
        out_specs=[picks, picks, pl.BlockSpec((1, B_KV_HEADS, 8, KV_W), lambda i, sl: (i, 0, 0, 0))],
        scratch_shapes=[pltpu.VMEM((8, KV_W), F32), pltpu.VMEM((bd, 8, cover.shape[1]), F32)],
    )
    return pl.pallas_call(
        functools.partial(_nsa_sample_select_kernel, past=past),
        grid_spec=grid_spec,
        out_shape=[jax.ShapeDtypeStruct((bd, 8, 128), jnp.int32)] * 2
        + [jax.ShapeDtypeStruct((bd, B_KV_HEADS, 8, KV_W), F32)],
        compiler_params=_cparams(("arbitrary",), 48),
        name="nsa_sample_select",
    )(slopes, pre, cmp_new, q8, pt, pe_b, w1bd, w2bd, cover)


def _nsa_sample_attend_kernel(pt_ref, ix_ref, sl_ref, *refs, past, nb_past):
    n_blocks = B_KV_HEADS * N_SEL
    pages = refs[:n_blocks]
    q_ref, slc_new_ref, win_new_ref, win_ref, g_ref, ocmp_ref, o_ref = refs[n_blocks:]
    b = pl.program_id(0)
    bpp = PAGE_SIZE // L_SLC
    row8 = _iota((8, 1), 0)
    page_row = _iota((1, PAGE_SIZE), 1)
    wb = win_ref.shape[2]
    win = win_ref[0].astype(BF16)
    slc_new = slc_new_ref[0]
    win_new = win_new_ref[0]
    for kvh in range(B_KV_HEADS):
        q = q_ref[0, kvh]
        qf = q.astype(F32)
        slope = jnp.zeros((8, 1), F32)
        for g in range(B_GROUP):
            slope = jnp.where(row8 == g, sl_ref[kvh * B_GROUP + g], slope)
        kv = jnp.concatenate([pages[kvh * N_SEL + n][0].astype(BF16) for n in range(N_SEL)], axis=1)
        kpos, blk, live = [], [], []
        j_max = ix_ref[b, kvh * N_SEL]
        for n in range(N_SEL):
            j = ix_ref[b, kvh * N_SEL + n]
            j_max = jnp.maximum(j_max, j)
            jc = jnp.minimum(j, nb_past - 1)
            kpos.append((jc // bpp) * PAGE_SIZE + page_row)
            blk.append(jnp.broadcast_to(jc, (1, PAGE_SIZE)))
            live.append(jnp.broadcast_to((j < nb_past).astype(jnp.int32), (1, PAGE_SIZE)))
        kpos, blk, live = [jnp.concatenate(z, axis=1) for z in (kpos, blk, live)]
        s = _dot(q, kv) - slope * (past - kpos).astype(F32)
        s = jnp.where(((kpos >> SLC_SHIFT) == blk) & (live > 0), s, NEG_INF)
        s_new = jnp.where(j_max >= nb_past, jnp.sum(qf * slc_new, axis=-1, keepdims=True), NEG_INF)
        m = jnp.maximum(jnp.max(s, axis=-1, keepdims=True), s_new)
        m = jnp.where(m > NEG_INF, m, 0.0)
        e = jnp.exp(s - m)
        e_new = jnp.exp(s_new - m)
        l = jnp.sum(e, axis=-1, keepdims=True) + e_new
        o_slc = (_dot_nt(e.astype(BF16), kv) + e_new * slc_new) * (1.0 / jnp.where(l > 0, l, 1.0))
        wpos = past - wb + _iota((1, wb), 1)
        s1 = _dot(q, win) - slope * (past - wpos).astype(F32)
        s1 = jnp.where(wpos > past - WINDOW, s1, NEG_INF)
        s2 = jnp.sum(qf * win_new, axis=-1, keepdims=True)
        mw = jnp.maximum(jnp.max(s1, axis=-1, keepdims=True), s2)
        e1 = jnp.exp(s1 - mw)
        e2 = jnp.exp(s2 - mw)
        dw = jnp.sum(e1, axis=-1, keepdims=True) + e2
        o_win = (_dot_nt(e1.astype(BF16), win) + e2 * win_new) * (1.0 / dw)
        gate = jax.nn.sigmoid(g_ref[0, kvh])
        o_ref[0, kvh] = gate[:, 0:1] * ocmp_ref[0, kvh] + gate[:, 1:2] * o_slc + gate[:, 2:3] * o_win


def _nsa_sample_attend(phys, idx, slopes, cache, li, q8, slc_new, win_new, win_state, gate8, o_cmp, past):
    bd = q8.shape[0]
    bpp = PAGE_SIZE // L_SLC
    nb_past = past // L_SLC
    n_blocks = B_KV_HEADS * N_SEL
    pool = _feature_major_pages(cache)

    def blk_spec(k):
        return pl.BlockSpec((1, KV_W, PAGE_SIZE), lambda i, ph, ix, sl: (ph[i, k], li, 0))

    row = lambda: pl.BlockSpec((1, 1, KV_W), lambda i, pt, ix, sl: (i, 0, 0))
    per_head = lambda w: pl.BlockSpec((1, B_KV_HEADS, 8, w), lambda i, pt, ix, sl: (i, 0, 0, 0))
    grid_spec = pltpu.PrefetchScalarGridSpec(
        num_scalar_prefetch=3,
        grid=(bd,),
        in_specs=[blk_spec(k) for k in range(n_blocks)] + [
            per_head(KV_W), row(), row(),
            pl.BlockSpec((1,) + win_state.shape[1:], lambda i, pt, ix, sl: (i, 0, 0)),
            per_head(128), per_head(KV_W)],
        out_specs=per_head(KV_W),
    )
    return pl.pallas_call(
        functools.partial(_nsa_sample_attend_kernel, past=past, nb_past=nb_past),
        grid_spec=grid_spec,
        out_shape=jax.ShapeDtypeStruct((bd, B_KV_HEADS, 8, KV_W), F32),
        compiler_params=_cparams(("arbitrary",), 48),
        name="nsa_sample_attend",
    )(phys, idx, slopes, *([pool] * n_blocks), q8, slc_new, win_new, win_state, gate8, o_cmp)


def _outproj_kernel(x_ref, a_ref, b_ref, w_ref, y_ref):
    na = a_ref.shape[1]
    y_ref[...] = x_ref[...] + _dot(a_ref[...], w_ref[0:na]) + _dot(b_ref[...], w_ref[na:])


def _outproj(x, oa, ob, w, tm):
    m = x.shape[0]
    row = lambda n: pl.BlockSpec((tm, n), lambda i: (i, 0))
    return pl.pallas_call(
        _outproj_kernel,
        grid=(m // tm,),
        in_specs=[row(D_MODEL), row(oa.shape[1]), row(ob.shape[1]), pl.BlockSpec(w.shape, lambda i: (0, 0))],
        out_specs=row(D_MODEL),
        out_shape=jax.ShapeDtypeStruct((m, D_MODEL), F32),
        compiler_params=_cparams(("parallel",), 32),
        name="outproj",
    )(x, oa, ob, w)


def _mixer_prompt_kernel(*refs, groups, convs, gated_ffn, final, attn, tm, tf, f):
    x_ref, nw_ref, up_ref, cw_ref, down_ref = refs[0:5]
    pos = 5
    fw_ref = refs[pos] if final else None
    pos += int(final)
    attn_refs = refs[pos:pos + 3] if attn else None
    pos += 3 * int(attn)
    y_ref, tail_ref = refs[pos], refs[pos + 1]
    act_ref, ubuf_ref, carry_ref = refs[pos + 2:]

    @pl.when(pl.program_id(1) == 0)
    def _():
        carry_ref[...] = jnp.zeros_like(carry_ref)

    x = x_ref[0]
    if attn:
        oa_ref, ob_ref, wo_ref = attn_refs
        na = oa_ref.shape[2]
        x = x + _dot(oa_ref[0], wo_ref[0:na]) + _dot(ob_ref[0], wo_ref[na:])
    xn = _rms(x, nw_ref[...]).astype(BF16)
    for j in range(f // tf):
        u = [_dot(xn, up_ref[:, g * f + j * tf:g * f + (j + 1) * tf]) for g in range(groups)]
        conv_in = [u[0], u[1]] if gated_ffn else [u[1] * u[0]]
        conv_out = []
        for c, z in enumerate(conv_in):
            cols = slice(c * f + j * tf, c * f + (j + 1) * tf)
            slot = (j % 2) * convs + c
            ubuf_ref[slot, 0:8] = carry_ref[:, cols]
            ubuf_ref[slot, 8:tm + 8] = z
            w = cw_ref[:, cols]
            conv_out.append(w[0:1] * ubuf_ref[slot, pl.ds(6, tm), :] + w[1:2] * ubuf_ref[slot, pl.ds(7, tm), :]
                            + w[2:3] * z)
            last = z[tm - 8:tm]
            carry_ref[:, cols] = last
            tail_ref[0, 0, :, cols] = last
        if gated_ffn:
            act = conv_out[0] * jax.nn.sigmoid(conv_out[0]) * conv_out[1]
        else:
            act = u[2] * conv_out[0]
        act_ref[:, j * tf:(j + 1) * tf] = act.astype(BF16)
    y = x + _dot(act_ref[...], down_ref[...])
    y_ref[0] = _rms(y, fw_ref[...]) if final else y


def _mixer_prompt(x, nw, w_up, conv_w, w_down, gated_ffn, final_w=None, attn=None, tm=512, tf=256):
    b, s, d = x.shape
    f = w_down.shape[0]
    groups = w_up.shape[1] // f
    convs = conv_w.shape[1] // f
    final = final_w is not None
    const = lambda a: pl.BlockSpec(a.shape, lambda i, t: (0,) * a.ndim, pipeline_mode=pl.Buffered(1))
    tile = lambda a: pl.BlockSpec((1, tm, a.shape[2]), lambda i, t: (i, t, 0))
    in_specs = [tile(x), const(nw), const(w_up), const(conv_w), const(w_down)]
    args = [x, nw, w_up, conv_w, w_down]
    if final:
        in_specs.append(const(final_w))
        args.append(final_w)
    if attn is not None:
        oa, ob, w_out = attn
        in_specs += [tile(oa), tile(ob), const(w_out)]
        args += [oa, ob, w_out]
    y, tail = pl.pallas_call(
        functools.partial(_mixer_prompt_kernel, groups=groups, convs=convs, gated_ffn=gated_ffn,
                          final=final, attn=attn is not None, tm=tm, tf=tf, f=f),
        grid=(b, s // tm),
        in_specs=in_specs,
        out_specs=[pl.BlockSpec((1, tm, d), lambda i, t: (i, t, 0)),
                   pl.BlockSpec((1, 1, 8, convs * f), lambda i, t: (i, t, 0, 0))],
        out_shape=[jax.ShapeDtypeStruct((b, s, d), F32), jax.ShapeDtypeStruct((b, s // tm, 8, convs * f), F32)],
        scratch_shapes=[pltpu.VMEM((tm, f), BF16), pltpu.VMEM((2 * convs, tm + 8, tf), F32),
                        pltpu.VMEM((8, convs * f), F32)],
        compiler_params=_cparams(("parallel", "arbitrary"), 56),
        name="ffn_prompt" if gated_ffn else "sconv_prompt",
    )(*args)
    return y, tail[:, -1, 8 - (CONV_W - 1):, :]


def _mixer_sample_kernel(*refs, groups, convs, gated_ffn, final):
    x_ref, nw_ref = refs[0:2]
    up = refs[2:2 + groups]
    cw = refs[2 + groups:2 + groups + convs]
    st = refs[2 + groups + convs:2 + groups + 2 * convs]
    down_ref = refs[2 + groups + 2 * convs]
    pos = 3 + groups + 2 * convs
    fw_ref = refs[pos] if final else None
    pos += int(final)
    y_ref = refs[pos]
    news = refs[pos + 1:pos + 1 + convs]
    xn_ref, acc_ref = refs[pos + 1 + convs:]
    j = pl.program_id(0)

    @pl.when(j == 0)
    def _():
        xn_ref[...] = _rms(x_ref[...], nw_ref[...]).astype(BF16)
        acc_ref[...] = jnp.zeros_like(acc_ref)

    xn = xn_ref[...]
    u = [_dot(xn, w[...]) for w in up]
    conv_in = [u[0], u[1]] if gated_ffn else [u[1] * u[0]]
    conv_out = []
    for c, z in enumerate(conv_in):
        w = cw[c][...]
        conv_out.append(w[0:1] * st[c][:, 0, :] + w[1:2] * st[c][:, 1, :] + w[2:3] * z)
        news[c][...] = z
    if gated_ffn:
        act = conv_out[0] * jax.nn.sigmoid(conv_out[0]) * conv_out[1]
    else:
        act = u[2] * conv_out[0]
    acc_ref[...] += _dot(act.astype(BF16), down_ref[...])

    @pl.when(j == pl.num_programs(0) - 1)
    def _():
        y = x_ref[...] + acc_ref[...]
        y_ref[...] = _rms(y, fw_ref[...]) if final else y


def _mixer_sample(x, nw, w_up, conv_w, w_down, state, gated_ffn, final_w=None, tf=256):
    bd, d = x.shape
    f = w_down.shape[0]
    groups = w_up.shape[1] // f
    convs = conv_w.shape[1] // f
    nf = f // tf
    final = final_w is not None
    in_specs = [pl.BlockSpec((bd, d), lambda j: (0, 0)), pl.BlockSpec(nw.shape, lambda j: (0, 0))]
    in_specs += [pl.BlockSpec((d, tf), lambda j, g=g: (0, g * nf + j)) for g in range(groups)]
    in_specs += [pl.BlockSpec((CONV_W, tf), lambda j, c=c: (0, c * nf + j)) for c in range(convs)]
    in_specs += [pl.BlockSpec((bd, CONV_W - 1, tf), lambda j, c=c: (0, 0, c * nf + j)) for c in range(convs)]
    in_specs += [pl.BlockSpec((tf, d), lambda j: (j, 0))]
    args = [x, nw] + [w_up] * groups + [conv_w] * convs + [state] * convs + [w_down]
    if final:
        in_specs.append(pl.BlockSpec(final_w.shape, lambda j: (0, 0)))
        args.append(final_w)
    out_specs = [pl.BlockSpec((bd, d), lambda j: (0, 0))]
    out_specs += [pl.BlockSpec((bd, tf), lambda j: (0, j))] * convs
    out_shape = [jax.ShapeDtypeStruct((bd, d), F32)] + [jax.ShapeDtypeStruct((bd, f), F32)] * convs
    res = pl.pallas_call(
        functools.partial(_mixer_sample_kernel, groups=groups, convs=convs, gated_ffn=gated_ffn, final=final),
        grid=(nf,),
        in_specs=in_specs,
        out_specs=out_specs,
        out_shape=out_shape,
        scratch_shapes=[pltpu.VMEM((bd, d), BF16), pltpu.VMEM((bd, d), F32)],
        compiler_params=_cparams(("arbitrary",), 32),
        name="ffn_sample" if gated_ffn else "sconv_sample",
    )(*args)
    new = jnp.concatenate(res[1:], axis=-1)
    return res[0], jnp.concatenate([state[:, 1:], new[:, None, :]], axis=1)


def _pad_in_weight(w):
    c = HG_W + QB_W + 3 * KV_W
    per = 3 * B_GROUP
    gates = [jnp.pad(w[:, c + h * per:c + (h + 1) * per], ((0, 0), (0, GATE_PAD - per)))
             for h in range(B_KV_HEADS)]
    return jnp.concatenate([w[:, :c]] + gates, axis=1).astype(BF16)


def _per_head_rows(z, width):
    bd = z.shape[0]
    z = z.reshape(bd, B_KV_HEADS, B_GROUP, -1)
    return jnp.pad(z, ((0, 0), (0, 0), (0, 8 - B_GROUP), (0, width - z.shape[-1])))


def kernel(x_prompt, x_sample, cache_cmp_kv, cache_slc_kv, state_win_kv, state_hgrn, state_sconv, state_ffn,
           page_table, norm_mix, norm_ffn, norm_final, ab_w_in, ab_w_out, hgrn_lb_logits, hgrn_norm,
           cmp_pe, cmp_w1, cmp_w2, c_w_in, c_conv, c_w_out, ffn_up, ffn_conv, ffn_down):
    b, s, d = x_prompt.shape
    bd, t, _ = x_sample.shape
    assert t == 1 and d == D_MODEL
    depth = norm_mix.shape[0]
    n_a = ab_w_in.shape[0]
    wb = state_win_kv.shape[2]
    n_pages = page_table.shape[1]
    past = n_pages * PAGE_SIZE
    assert wb == WINDOW and past % (PAGES_PER_STEP * PAGE_SIZE) == 0 and s % 1024 == 0 and s >= wb
    hh = jnp.arange(1, B_HEADS + 1, dtype=F32)
    slopes = 2.0 ** (-8.0 * hh / B_HEADS)
    lb_logits = hgrn_lb_logits.astype(F32)
    row2 = lambda v: v.reshape(1, -1).astype(F32)

    xp = x_prompt.reshape(b * s, d)
    xs = x_sample.reshape(bd, d)
    outs_p = {k: [] for k in ("cmp", "slc", "win", "hg", "sc", "ff")}
    outs_s = {k: [] for k in ("cmp", "slc", "win", "hg", "sc", "ff")}
    attn_p = None
    for l in range(depth):
        last = l == depth - 1
        if l % 2 == 0:
            la = l // 2
            w_in = _pad_in_weight(ab_w_in[la])
            w_out = ab_w_out[la].astype(BF16)
            nw = row2(norm_mix[l])
            hnw = row2(hgrn_norm[la])
            pe_b, w1bd, w2_packed = _cmp_weights(cmp_pe[la], cmp_w1[la], cmp_w2[la], packed=True)
            _, _, w2_plain = _cmp_weights(cmp_pe[la], cmp_w1[la], cmp_w2[la], packed=False)
            hg, qb, gb, kck, kcv, kct, kslt, kwt, sk, sv, wk, wv = _inproj(xp, nw, w_in, tm=512, seq=s)
            sk, sv, wk, wv = [z.reshape(B_KV_HEADS, b, s, -1) for z in (sk, sv, wk, wv)]
            oa, hst = _hgrn_prompt(hg, lb_logits, hnw, la, b, s)
            ck, cv = _compress_prompt(kck.reshape(b, s, -1), kcv.reshape(b, s, -1), pe_b, w1bd, w2_packed)
            n_cmp = s // D_CMP
            nblk = s // L_SLC
            cover = _cover_matrix(n_cmp, nblk, nblk)
            ob = _nsa_prompt(slopes, qb.reshape(b, s, QB_W), gb.reshape(b, s, -1), ck, cv, sk, sv, wk, wv, cover)
            attn_p = (oa.reshape(b, s, -1), ob, w_out)
            kv6 = lambda z, n, tt: z.reshape(n, tt, 2, B_KV_HEADS, B_HD)
            from_t = lambda z: z.reshape(b, 2, B_KV_HEADS, B_HD, -1).transpose(0, 4, 1, 2, 3)
            outs_p["cmp"].append(from_t(kct))
            outs_p["slc"].append(from_t(kslt))
            outs_p["win"].append(from_t(kwt[:, :, s - wb:]))
            outs_p["hg"].append(hst)
            hg, qb, gb, kc, ksl, kw = _inproj(xs, nw, w_in, tm=bd)
            oa, hst = _hgrn_sample(hg, state_hgrn[la], lb_logits, hnw, la)
            pre = _cmp_pages(page_table, cache_cmp_kv, la, w1bd)
            n_cmp = pre.shape[1]
            nblk = past // L_SLC + 1
            cover = _cover_matrix(n_cmp, nblk, -(-nblk // 128) * 128)
            q8 = jnp.stack([jnp.pad(_per_head_rows(qb, B_HD)[:, h], ((0, 0), (0, 0), (h * B_HD, KV_W - (h + 1) * B_HD)))
                            for h in range(B_KV_HEADS)], axis=1)
            idx, phys, o_cmp = _nsa_sample_select(slopes, pre, kc.reshape(bd, 1, KV_W), q8, page_table, pe_b, w1bd,
                                                  w2_plain, cover, past)
            picks = lambda z: z[:, :B_KV_HEADS, :N_SEL].reshape(bd, B_KV_HEADS * N_SEL)
            idx, phys = picks(idx), picks(phys)
            gate8 = _per_head_rows(jnp.concatenate([gb[:, h * GATE_PAD:h * GATE_PAD + 3 * B_GROUP]
                                                    for h in range(B_KV_HEADS)], axis=1), 128)
            win_state = state_win_kv[la].transpose(0, 2, 3, 4, 1).reshape(bd, KV_W, wb)
            o8 = _nsa_sample_attend(phys, idx, slopes, cache_slc_kv, la, q8, ksl.reshape(bd, 1, KV_W),
                                    kw.reshape(bd, 1, KV_W), win_state, gate8, o_cmp, past)
            ob = jnp.concatenate([o8[:, h, :B_GROUP, 2 * B_HD + h * B_HD:2 * B_HD + (h + 1) * B_HD]
                                  .reshape(bd, B_GROUP * B_HD) for h in range(B_KV_HEADS)], axis=1)
            xs = _outproj(xs, oa, ob.astype(BF16), w_out, tm=bd)
            outs_s["cmp"].append(kv6(kc, bd, t))
            outs_s["slc"].append(kv6(ksl, bd, t))
            outs_s["win"].append(jnp.concatenate([state_win_kv[la], kv6(kw, bd, t)], axis=1)[:, t:])
            outs_s["hg"].append(hst)
        else:
            lc = l // 2
            xp3, tail = _mixer_prompt(xp.reshape(b, s, d), row2(norm_mix[l]), c_w_in[lc].astype(BF16),
                                      c_conv[lc], c_w_out[lc].astype(BF16), gated_ffn=False)
            xp = xp3.reshape(b * s, d)
            outs_p["sc"].append(tail)
            xs, new = _mixer_sample(xs, row2(norm_mix[l]), c_w_in[lc].astype(BF16), c_conv[lc],
                                    c_w_out[lc].astype(BF16), state_sconv[lc], gated_ffn=False)
            outs_s["sc"].append(new)
        fw = row2(norm_final) if last else None
        xp3, tail = _mixer_prompt(xp.reshape(b, s, d), row2(norm_ffn[l]), ffn_up[l].astype(BF16), ffn_conv[l],
                                  ffn_down[l].astype(BF16), gated_ffn=True, final_w=fw, attn=attn_p)
        attn_p = None
        xp = xp3.reshape(b * s, d)
        outs_p["ff"].append(tail)
        xs, new = _mixer_sample(xs, row2(norm_ffn[l]), ffn_up[l].astype(BF16), ffn_conv[l],
                                ffn_down[l].astype(BF16), state_ffn[l], gated_ffn=True, final_w=fw)
        outs_s["ff"].append(new)
    y_prompt = xp.reshape(b, s, d)
    y_sample = xs.reshape(bd, t, d)
    return (y_prompt, y_sample,
            jnp.stack(outs_p["cmp"], axis=2), jnp.stack(outs_p["slc"], axis=2), jnp.stack(outs_p["win"], axis=0),
            jnp.stack(outs_p["hg"], axis=0), jnp.stack(outs_p["sc"], axis=0), jnp.stack(outs_p["ff"], axis=0),
            jnp.stack(outs_s["cmp"], axis=2), jnp.stack(outs_s["slc"], axis=2), jnp.stack(outs_s["win"], axis=0),
            jnp.stack(outs_s["hg"], axis=0), jnp.stack(outs_s["sc"], axis=0), jnp.stack(outs_s["ff"], axis=0))
```

```python
import functools

import jax
import jax.numpy as jnp
from jax import lax
from jax.experimental import pallas as pl
from jax.experimental.pallas import tpu as pltpu

F32 = jnp.float32
BF16 = jnp.bfloat16
NEG_INF = float("-inf")

D_MODEL = 1024
PAGE_SIZE = 128
A_HEADS = 4
A_DK = 128
A_DV = 128
B_HEADS = 8
B_KV_HEADS = 2
B_GROUP = B_HEADS // B_KV_HEADS
B_HD = 64
L_CMP = 32
D_CMP = 16
CMP_HID = 64
L_SLC = 64
SLC_SHIFT = 6
N_SEL = 16
WINDOW = 512
SEL_FORCE = 1e9
CONV_W = 3
EPS = 1e-6
KV_W = 2 * B_KV_HEADS * B_HD
HG_W = 4 * A_HEADS * A_DK
QB_W = B_HEADS * B_HD
GATE_PAD = 128

V7X_VMEM_BYTES = 64 * 2**20


def _cparams(sem, vmem_mb):
    assert vmem_mb * 2**20 < V7X_VMEM_BYTES
    return pltpu.CompilerParams(dimension_semantics=sem, vmem_limit_bytes=vmem_mb * 2**20)


def _dot(a, b):
    return jnp.dot(a, b, preferred_element_type=F32)


def _dot_nt(a, b):
    return lax.dot_general(a, b, (((1,), (1,)), ((), ())), preferred_element_type=F32)


def _dot_tn(a, b):
    return lax.dot_general(a, b, (((0,), (0,)), ((), ())), preferred_element_type=F32)


def _split3(x):
    hi = x.astype(BF16)
    r1 = x - hi.astype(F32)
    mid = r1.astype(BF16)
    lo = (r1 - mid.astype(F32)).astype(BF16)
    return hi, mid, lo


def _dot3_lhs(x, w):
    hi, mid, lo = _split3(x)
    return _dot(hi, w) + _dot(mid, w) + _dot(lo, w)


def _dot3_rhs(w, x):
    hi, mid, lo = _split3(x)
    return _dot(w, hi) + _dot(w, mid) + _dot(w, lo)


def _rms(x, w):
    return x * lax.rsqrt(jnp.mean(x * x, axis=-1, keepdims=True) + EPS) * w


def _masked_softmax(s, mask):
    s = jnp.where(mask, s, NEG_INF)
    m = jnp.max(s, axis=-1, keepdims=True)
    m = jnp.where(m > NEG_INF, m, 0.0)
    e = jnp.exp(s - m)
    d = jnp.sum(e, axis=-1, keepdims=True)
    return e * (1.0 / jnp.where(d > 0, d, 1.0))


def _iota(shape, dim):
    return lax.broadcasted_iota(jnp.int32, shape, dim)


def _attention_rows(kv, pos, with_blocks):
    tm = kv.shape[0]
    lane = _iota((tm, 128), 1)
    lo = lane < B_HD
    pos_cols = jnp.where(lane == B_HD, (pos >> 7).astype(F32),
                         jnp.where(lane == B_HD + 1, (pos & 127).astype(F32), 0.0))
    k_pair, v_pair = kv[:, 0:128], kv[:, 128:256]
    k_swap, v_swap = pltpu.roll(k_pair, B_HD, axis=1), pltpu.roll(v_pair, B_HD, axis=1)
    onehot = ((pos >> SLC_SHIFT) == lane).astype(BF16)
    keys, values = [], []
    for h, (k, v_lo, v_hi) in enumerate(((k_pair, v_pair, v_swap), (k_swap, v_swap, v_pair))):
        key = jnp.where(lo, k, pos_cols).astype(BF16)
        keys.append(jnp.concatenate([key, onehot], axis=1) if with_blocks else key)
        values.append(jnp.where(lo, v_lo, v_hi).astype(BF16))
    return keys, values


def _inproj_kernel(x_ref, nw_ref, w_ref, hg_ref, qb_ref, gb_ref, *refs, seq):
    xn = _rms(x_ref[...], nw_ref[...]).astype(BF16)

    def proj(c0, n):
        return _dot(xn, w_ref[:, c0:c0 + n])

    hg_ref[...] = proj(0, HG_W)
    qb_ref[...] = (proj(HG_W, QB_W) * (B_HD ** -0.5)).astype(BF16)
    c = HG_W + QB_W
    kc = proj(c, KV_W)
    ksl = proj(c + KV_W, KV_W)
    kw = proj(c + 2 * KV_W, KV_W)
    gb_ref[...] = proj(c + 3 * KV_W, B_KV_HEADS * GATE_PAD)
    if seq is None:
        kc_ref, ksl_ref, kw_ref = refs
        kc_ref[...] = kc
        ksl_ref[...] = ksl
        kw_ref[...] = kw
    else:
        kck_ref, kcv_ref, kct_ref, kslt_ref, kwt_ref, sk_ref, sv_ref, wk_ref, wv_ref = refs
        kck_ref[...] = kc[:, 0:KV_W // 2]
        kcv_ref[...] = kc[:, KV_W // 2:KV_W]
        kct_ref[0] = kc.T
        kslt_ref[0] = ksl.T
        kwt_ref[0] = kw.T
        tm = x_ref.shape[0]
        pos = (pl.program_id(0) * tm + _iota((tm, 1), 0)) & (seq - 1)
        for kv, k_ref, v_ref, with_blocks in ((ksl, sk_ref, sv_ref, True), (kw, wk_ref, wv_ref, False)):
            keys, values = _attention_rows(kv, pos, with_blocks)
            for h in range(B_KV_HEADS):
                k_ref[h] = keys[h]
                v_ref[h] = values[h]


def _inproj(x, nw, w_pad, tm, seq=None):
    m = x.shape[0]
    row = lambda n: pl.BlockSpec((tm, n), lambda i: (i, 0))
    full = lambda a: pl.BlockSpec(a.shape, lambda i: (0,) * a.ndim)
    widths = (HG_W, QB_W, B_KV_HEADS * GATE_PAD)
    dtypes = (F32, BF16, F32)
    out_specs = [row(n) for n in widths]
    out_shape = [jax.ShapeDtypeStruct((m, n), d) for n, d in zip(widths, dtypes)]
    if seq is None:
        out_specs += [row(KV_W)] * 3
        out_shape += [jax.ShapeDtypeStruct((m, KV_W), F32)] * 3
    else:
        assert seq // L_SLC == 128 and seq % tm == 0 and seq & (seq - 1) == 0
        nt = seq // tm
        out_specs += [row(KV_W // 2)] * 2
        out_shape += [jax.ShapeDtypeStruct((m, KV_W // 2), F32)] * 2
        out_specs += [pl.BlockSpec((1, KV_W, tm), lambda i: (i // nt, 0, i % nt))] * 3
        out_shape += [jax.ShapeDtypeStruct((m // seq, KV_W, seq), F32)] * 3
        for n in (256, 128, 128, 128):
            out_specs.append(pl.BlockSpec((B_KV_HEADS, tm, n), lambda i: (0, i, 0)))
            out_shape.append(jax.ShapeDtypeStruct((B_KV_HEADS, m, n), BF16))
    return pl.pallas_call(
        functools.partial(_inproj_kernel, seq=seq),
        grid=(m // tm,),
        in_specs=[row(D_MODEL), full(nw), full(w_pad)],
        out_specs=out_specs,
        out_shape=out_shape,
        compiler_params=_cparams(("parallel",), 48),
        name="inproj",
    )(x, nw, w_pad)


def _lower_bound(logit_ref, la):
    lg = logit_ref[...]
    e = jnp.exp(lg - jnp.max(lg, axis=0, keepdims=True))
    return jnp.sum(e[:la + 1], axis=0, keepdims=True) / jnp.sum(e, axis=0, keepdims=True)


def _hgrn_post(o, nw, ga):
    o = o * lax.rsqrt(jnp.mean(o * o, axis=-1, keepdims=True) + EPS)
    return o * nw * jax.nn.sigmoid(ga)


def _hgrn_prompt_kernel(hg_ref, lg_ref, nw_ref, oa_ref, st_ref, s_ref, *, la, chunk, sub):
    ci = pl.program_id(1)

    @pl.when(ci == 0)
    def _():
        s_ref[...] = jnp.zeros_like(s_ref)

    lb = _lower_bound(lg_ref, la)
    tri = (_iota((chunk, chunk), 0) >= _iota((chunk, chunk), 1)).astype(BF16)
    srow = _iota((sub, 1), 0)
    lane = _iota((sub, chunk), 1)
    for h in range(A_HEADS):
        hs = slice(h * A_DK, (h + 1) * A_DK)
        q = hg_ref[:, h * A_DK:(h + 1) * A_DK]
        fl = hg_ref[:, 512 + h * A_DK:512 + (h + 1) * A_DK]
        v = hg_ref[:, 1024 + h * A_DV:1024 + (h + 1) * A_DV]
        ga = hg_ref[:, 1536 + h * A_DV:1536 + (h + 1) * A_DV]
        lbh = lb[:, hs]
        f = lbh + (1.0 - lbh) * jax.nn.sigmoid(fl)
        a = _dot3_rhs(tri, jnp.log(f))
        k = 1.0 - f
        st = s_ref[h]
        vb = v.astype(BF16)
        o = _dot_nt((q * jnp.exp(a)).astype(BF16), st.astype(BF16))
        rows = []
        for i in range(chunk // sub):
            r0 = i * sub
            qi, ai = q[r0:r0 + sub], a[r0:r0 + sub]
            att = jnp.zeros((sub, chunk), F32)
            for s in range(sub):
                dec = jnp.exp(jnp.where(srow >= s, ai - a[r0 + s:r0 + s + 1], NEG_INF))
                col = jnp.sum(qi * dec * k[r0 + s:r0 + s + 1], axis=-1, keepdims=True)
                att = jnp.where(lane == r0 + s, col, att)
            if i > 0:
                ref_a = a[r0 - 1:r0]
                qt = qi * jnp.exp(ai - ref_a)
                kt = jnp.concatenate([k[:r0] * jnp.exp(ref_a - a[:r0]), jnp.zeros((chunk - r0, A_DK), F32)], axis=0)
                att = att + _dot_nt(qt.astype(BF16), kt.astype(BF16))
            rows.append(att)
        att = jnp.concatenate(rows, axis=0)
        o = o + _dot(att.astype(BF16), vb)
        a_end = a[chunk - 1:chunk]
        khat = k * jnp.exp(a_end - a)
        s_ref[h] = st * jnp.exp(a_end) + _dot_tn(vb, khat.astype(BF16))
        oa_ref[:, hs] = _hgrn_post(o, nw_ref[:, hs], ga).astype(BF16)

    @pl.when(ci == pl.num_programs(1) - 1)
    def _():
        for h in range(A_HEADS):
            st_ref[0, h] = s_ref[h].T


def _hgrn_prompt(hg, lb_logits, nw, la, b, s, chunk=128, sub=16):
    n = s // chunk
    full = lambda a: pl.BlockSpec(a.shape, lambda i, j: (0,) * a.ndim)
    return pl.pallas_call(
        functools.partial(_hgrn_prompt_kernel, la=la, chunk=chunk, sub=sub),
        grid=(b, n),
        in_specs=[pl.BlockSpec((chunk, HG_W), lambda i, j: (i * n + j, 0)), full(lb_logits), full(nw)],
        out_specs=[pl.BlockSpec((chunk, A_HEADS * A_DV), lambda i, j: (i * n + j, 0)),
                   pl.BlockSpec((1, A_HEADS, A_DK, A_DV), lambda i, j: (i, 0, 0, 0))],
        out_shape=[jax.ShapeDtypeStruct((b * s, A_HEADS * A_DV), BF16),
                   jax.ShapeDtypeStruct((b, A_HEADS, A_DK, A_DV), F32)],
        scratch_shapes=[pltpu.VMEM((A_HEADS, A_DV, A_DK), F32)],
        compiler_params=_cparams(("parallel", "arbitrary"), 32),
        name="hgrn_prompt",
    )(hg, lb_logits, nw)


def _hgrn_sample_kernel(hg_ref, s0_ref, lg_ref, nw_ref, oa_ref, st_ref, *, la):
    lb = _lower_bound(lg_ref, la)
    eye = (_iota((A_DK, A_DK), 0) == _iota((A_DK, A_DK), 1)).astype(F32)

    def col(r):
        return jnp.sum(eye * r, axis=1, keepdims=True)

    for h in range(A_HEADS):
        hs = slice(h * A_DK, (h + 1) * A_DK)
        q = hg_ref[0, :, h * A_DK:(h + 1) * A_DK]
        fl = hg_ref[0, :, 512 + h * A_DK:512 + (h + 1) * A_DK]
        v = hg_ref[0, :, 1024 + h * A_DV:1024 + (h + 1) * A_DV]
        ga = hg_ref[0, :, 1536 + h * A_DV:1536 + (h + 1) * A_DV]
        lbh = lb[:, hs]
        f = lbh + (1.0 - lbh) * jax.nn.sigmoid(fl)
        sn = col(f) * s0_ref[0, h] + col(1.0 - f) * v
        st_ref[0, h] = sn
        o = jnp.sum(col(q) * sn, axis=0, keepdims=True)
        oa_ref[0, :, hs] = _hgrn_post(o, nw_ref[:, hs], ga).astype(BF16)


def _hgrn_sample(hg, s0, lb_logits, nw, la):
    bd = hg.shape[0]
    full = lambda a: pl.BlockSpec(a.shape, lambda i: (0,) * a.ndim)
    st_spec = pl.BlockSpec((1, A_HEADS, A_DK, A_DV), lambda i: (i, 0, 0, 0))
    oa, st = pl.pallas_call(
        functools.partial(_hgrn_sample_kernel, la=la),
        grid=(bd,),
        in_specs=[pl.BlockSpec((1, 1, HG_W), lambda i: (i, 0, 0)), st_spec, full(lb_logits), full(nw)],
        out_specs=[pl.BlockSpec((1, 1, A_HEADS * A_DV), lambda i: (i, 0, 0)), st_spec],
        out_shape=[jax.ShapeDtypeStruct((bd, 1, A_HEADS * A_DV), BF16),
                   jax.ShapeDtypeStruct(s0.shape, F32)],
        compiler_params=_cparams(("parallel",), 16),
        name="hgrn_sample",
    )(hg.reshape(bd, 1, HG_W), s0, lb_logits, nw)
    return oa.reshape(bd, A_HEADS * A_DV), st


def _cmp_weights(pe, w1, w2, packed):
    eye = jnp.eye(B_KV_HEADS, dtype=F32)
    w1r = w1.reshape(2, L_CMP, B_HD, CMP_HID)
    w1bd = jnp.einsum("kidm,hH->ikhdHm", w1r, eye).reshape(L_CMP, 2, KV_W // 2, KV_W // 2)
    pe_b = jnp.broadcast_to(pe.transpose(1, 0, 2)[:, :, None, :], (L_CMP, 2, B_KV_HEADS, B_HD))
    if not packed:
        w2bd = jnp.einsum("kmd,kK,hH->khmKHd", w2, eye, eye).reshape(KV_W, KV_W)
        return pe_b.reshape(L_CMP, KV_W), w1bd.astype(BF16), w2bd.astype(BF16)
    half = KV_W // 2
    zero = jnp.zeros((half, KV_W), F32)
    k_blk = jnp.concatenate([w2[0], jnp.zeros_like(w2[0])], axis=1)
    v_blk = jnp.concatenate([w2[1], w2[1]], axis=1)
    w2k = jnp.concatenate([jnp.einsum("ml,hH->hmHl", k_blk, eye).reshape(half, KV_W), zero], axis=0)
    w2v = jnp.concatenate([zero, jnp.einsum("ml,hH->hmHl", v_blk, eye).reshape(half, KV_W)], axis=0)
    return pe_b.reshape(L_CMP, KV_W), w1bd.astype(BF16), (w2k.astype(BF16), w2v.astype(BF16))


def _chunk_preacts(load, n, pe_ref, w1_ref):
    half = KV_W // 2
    p0 = [jnp.zeros((n, half), F32) for _ in range(2)]
    p1 = [jnp.zeros((n, half), F32) for _ in range(2)]
    for i in range(D_CMP):
        for kv in range(2):
            x = load(i, kv)
            ls = slice(kv * half, (kv + 1) * half)
            p0[kv] = p0[kv] + _dot((x + pe_ref[i:i + 1, ls]).astype(BF16), w1_ref[i, kv])
            p1[kv] = p1[kv] + _dot((x + pe_ref[D_CMP + i:D_CMP + i + 1, ls]).astype(BF16), w1_ref[D_CMP + i, kv])
    return jnp.concatenate(p0, axis=1), jnp.concatenate(p1, axis=1)


def _with_position(k, pos, lane):
    col = lane & 127
    return jnp.where(col == B_HD, (pos >> 7).astype(F32), jnp.where(col == B_HD + 1, (pos & 127).astype(F32), k))


def _compress_kernel(kck_ref, kcv_ref, pe_ref, w1_ref, w2k_ref, w2v_ref, ck_ref, cv_ref):
    n = ck_ref.shape[1]
    p0, p1 = _chunk_preacts(lambda i, kv: (kck_ref, kcv_ref)[kv][0, pl.ds(i, n, stride=D_CMP), :], n, pe_ref, w1_ref)
    pre = p0 + pltpu.roll(p1, n - 1, axis=0)
    hid = (pre * jax.nn.sigmoid(pre)).astype(BF16)
    cpos = _iota((n, 1), 0) * D_CMP + (L_CMP - 1)
    ck_ref[0] = _with_position(_dot(hid, w2k_ref[...]), cpos, _iota((n, KV_W), 1)).astype(BF16)
    cv_ref[0] = _dot(hid, w2v_ref[...]).astype(BF16)


def _compress_prompt(kck, kcv, pe_b, w1bd, w2kv):
    b, s, _ = kck.shape
    n = s // D_CMP
    full = lambda a: pl.BlockSpec(a.shape, lambda i: (0,) * a.ndim)
    out = pl.BlockSpec((1, n, KV_W), lambda i: (i, 0, 0))
    seq = pl.BlockSpec((1, s, KV_W // 2), lambda i: (i, 0, 0))
    return pl.pallas_call(
        _compress_kernel,
        grid=(b,),
        in_specs=[seq, seq, full(pe_b), full(w1bd), full(w2kv[0]), full(w2kv[1])],
        out_specs=[out, out],
        out_shape=[jax.ShapeDtypeStruct((b, n, KV_W), BF16)] * 2,
        compiler_params=_cparams(("parallel",), 48),
        name="compress_prompt",
    )(kck, kcv, pe_b, w1bd, *w2kv)


MASK_BIAS = 2.0 ** 100

def _top_blocks_t(score_t, n_sel):
    nblk = score_t.shape[0]
    blk = _iota(score_t.shape, 0).astype(F32)

    def body(_, carry):
        sc, sel = carry
        m = jnp.max(sc, axis=0, keepdims=True)
        idx = jnp.min(jnp.where(sc == m, blk, float(nblk)), axis=0, keepdims=True)
        pick = blk == idx
        return jnp.where(pick, NEG_INF, sc), jnp.where(pick, 1.0, sel)

    _, sel = lax.fori_loop(0, n_sel, body, (score_t, jnp.zeros_like(score_t)), unroll=True)
    return sel


def _nsa_prompt_kernel(sl_ref, q_ref, g_ref, ck_ref, cv_ref, sk_ref, sv_ref, wk_ref, wv_ref, cover_ref, o_ref,
                       m_ref, l_ref, acc_ref, sa_ref, sb_ref, tiles_ref, *, tq, tk):
    kvh = pl.program_id(1)
    q0 = pl.program_id(2) * tq
    rows = B_GROUP * tq
    lane = _iota((tq, 128), 1)
    lo = lane < B_HD
    qf = q_ref[0].astype(F32)
    left = []
    for g in range(B_GROUP):
        pair = qf[:, 128 * (g // 2):128 * (g // 2) + 128]
        if g % 2:
            pair = pltpu.roll(pair, B_HD, axis=1)
        slope = sl_ref[kvh * B_GROUP + g]
        pos_cols = jnp.where(lane == B_HD, slope * 128.0, jnp.where(lane == B_HD + 1, slope, 0.0))
        left.append(jnp.where(lo, pair, pos_cols).astype(BF16))
    ql = jnp.concatenate(left, axis=0)
    qpos = q0 + (_iota((rows, 1), 0) & (tq - 1))

    nc = ck_ref.shape[1]
    cpos = _iota((1, nc), 1) * D_CMP + (L_CMP - 1)
    span = WINDOW + tq
    start = pl.multiple_of(jnp.maximum(q0 - WINDOW, 0), 128)
    wpos = start + _iota((1, span), 1)
    s_cmp = _dot_nt(ql, ck_ref[0])
    s_win = _dot_nt(ql, wk_ref[0, 0, pl.ds(start, span), :])
    p = _masked_softmax(s_cmp, cpos <= qpos)
    pw = _masked_softmax(s_win, (wpos <= qpos) & (wpos > qpos - WINDOW))
    o_cmp = _dot(p.astype(BF16), cv_ref[0])
    o_win = _dot(pw.astype(BF16), wv_ref[0, 0, pl.ds(start, span), :])
    imp = p[0:tq]
    for g in range(1, B_GROUP):
        imp = imp + p[g * tq:(g + 1) * tq]

    nblk = cover_ref.shape[1]
    jb = _iota((1, nblk), 1)
    cur = (q0 + _iota((tq, 1), 0)) >> SLC_SHIFT
    forced = (jb == 0) | (jb == cur) | (jb == cur - 1)
    score = _dot3_lhs(imp, cover_ref[...])
    score = jnp.where(forced | (jb > cur), NEG_INF, score)
    n_forced = 3
    sel = jnp.where(forced, 1.0, _top_blocks_t(score.T, min(N_SEL, nblk) - n_forced).T)

    bias = ((sel - 1.0) * MASK_BIAS).astype(BF16)
    qs = jnp.concatenate([ql, jnp.concatenate([bias] * B_GROUP, axis=0)], axis=1)
    m_ref[...] = jnp.full_like(m_ref, NEG_INF)
    l_ref[...] = jnp.zeros_like(l_ref)
    acc_ref[...] = jnp.zeros_like(acc_ref)

    def scores(kt):
        return _dot_nt(qs, sk_ref[0, 0, pl.ds(pl.multiple_of(kt * tk, tk), tk), :])

    def update(s, kt, causal):
        k0 = pl.multiple_of(kt * tk, tk)
        if causal:
            s = jnp.where(k0 + _iota((1, tk), 1) <= qpos, s, NEG_INF)
        m_old = m_ref[...]
        m_new = jnp.maximum(m_old, jnp.max(s, axis=-1, keepdims=True))
        alpha = jnp.exp(m_old - m_new)
        e = jnp.exp(s - m_new)
        l_ref[...] = alpha * l_ref[...] + jnp.sum(e, axis=-1, keepdims=True)
        acc_ref[...] = alpha * acc_ref[...] + _dot(e.astype(BF16), sv_ref[0, 0, pl.ds(k0, tk), :])
        m_ref[...] = m_new

    n_diag = q0 // tk
    bpt = tk // L_SLC
    picked = jnp.max(sel, axis=0, keepdims=True)
    n_full = jnp.int32(0)
    for kt in range(nblk * L_SLC // tk):
        in_tile = (jb >= kt * bpt) & (jb < (kt + 1) * bpt)
        wanted = (jnp.max(jnp.where(in_tile, picked, 0.0)) > 0.5) & (kt < n_diag)
        tiles_ref[n_full] = jnp.int32(kt)
        n_full = n_full + wanted.astype(jnp.int32)
    tiles_ref[n_full] = n_diag

    sa_ref[...] = scores(tiles_ref[0])

    def tile_pair(i, carry):
        sb_ref[...] = scores(tiles_ref[2 * i + 1])
        update(sa_ref[...], tiles_ref[2 * i], False)
        sa_ref[...] = scores(tiles_ref[2 * i + 2])
        update(sb_ref[...], tiles_ref[2 * i + 1], False)
        return carry

    lax.fori_loop(0, n_full // 2, tile_pair, 0)

    @pl.when(n_full % 2 == 1)
    def _():
        sb_ref[...] = scores(n_diag)
        update(sa_ref[...], tiles_ref[n_full - 1], False)
        update(sb_ref[...], n_diag, True)

    @pl.when(n_full % 2 == 0)
    def _():
        update(sa_ref[...], n_diag, True)

    o_slc = acc_ref[...] * (1.0 / l_ref[...])

    gate = jax.nn.sigmoid(g_ref[0])
    outs = []
    for g in range(B_GROUP):
        rs = slice(g * tq, (g + 1) * tq)
        outs.append(gate[:, 3 * g:3 * g + 1] * o_cmp[rs] + gate[:, 3 * g + 1:3 * g + 2] * o_slc[rs]
                    + gate[:, 3 * g + 2:3 * g + 3] * o_win[rs])
    for j in range(B_GROUP // 2):
        o_ref[0, :, 128 * j:128 * j + 128] = jnp.where(lo, outs[2 * j], outs[2 * j + 1]).astype(BF16)


def _nsa_prompt(slopes, qb, gb, ck, cv, sk, sv, wk, wv, cover, tq=128, tk=512):
    b, s, _ = qb.shape
    half = QB_W // B_KV_HEADS
    rows = B_GROUP * tq
    assert cover.shape[1] == 128 and s % tk == 0 and tk % tq == 0
    seq = lambda a: pl.BlockSpec((1, 1) + a.shape[2:], lambda i, h, t, sl: (h, i, 0, 0))
    grid_spec = pltpu.PrefetchScalarGridSpec(
        num_scalar_prefetch=1,
        grid=(b, B_KV_HEADS, s // tq),
        in_specs=[
            pl.BlockSpec((1, tq, half), lambda i, h, t, sl: (i, t, h)),
            pl.BlockSpec((1, tq, GATE_PAD), lambda i, h, t, sl: (i, t, h)),
            pl.BlockSpec((1, ck.shape[1], 128), lambda i, h, t, sl: (i, 0, h)),
            pl.BlockSpec((1, cv.shape[1], 128), lambda i, h, t, sl: (i, 0, h)),
            seq(sk), seq(sv), seq(wk), seq(wv),
            pl.BlockSpec(cover.shape, lambda i, h, t, sl: (0, 0)),
        ],
        out_specs=pl.BlockSpec((1, tq, half), lambda i, h, t, sl: (i, t, h)),
        scratch_shapes=[pltpu.VMEM((rows, 1), F32), pltpu.VMEM((rows, 1), F32), pltpu.VMEM((rows, 128), F32),
                        pltpu.VMEM((rows, tk), F32), pltpu.VMEM((rows, tk), F32),
                        pltpu.SMEM((s // tk + 1,), jnp.int32)],
    )
    return pl.pallas_call(
        functools.partial(_nsa_prompt_kernel, tq=tq, tk=tk),
        grid_spec=grid_spec,
        out_shape=jax.ShapeDtypeStruct((b, s, QB_W), BF16),
        compiler_params=_cparams(("parallel", "parallel", "arbitrary"), 56),
        name="nsa_prompt",
    )(slopes, qb, gb, ck, cv, sk, sv, wk, wv, cover)


def _cover_matrix(n_cmp, n_blk, n_pad):
    ci = jnp.arange(n_cmp)[:, None] * D_CMP
    sj = jnp.arange(n_pad)[None, :] * L_SLC
    cov = (ci < sj + L_SLC) & (ci + L_CMP > sj) & (jnp.arange(n_pad)[None, :] < n_blk)
    return cov.astype(BF16)


PAGES_PER_STEP = 64
PAGE_GROUPS = 2


def _cmp_pages_kernel(pt_ref, *refs):
    pages = refs[:PAGES_PER_STEP]
    w_ref, out_ref = refs[PAGES_PER_STEP:PAGES_PER_STEP + 2]
    row_refs = refs[PAGES_PER_STEP + 2:]
    per_group = PAGES_PER_STEP // PAGE_GROUPS
    n = per_group * PAGE_SIZE // D_CMP
    half = KV_W // 2
    for grp, rows_ref in enumerate(row_refs):
        for k, pg in enumerate(pages[grp * per_group:(grp + 1) * per_group]):
            for kv in range(2):
                rows_ref[kv, k * PAGE_SIZE:(k + 1) * PAGE_SIZE, :] = pg[0, kv * half:(kv + 1) * half, :].T
        for kv in range(2):
            x = jnp.concatenate([rows_ref[kv, pl.ds(i, n, stride=D_CMP), :].astype(BF16) for i in range(D_CMP)],
                                axis=1)
            y = _dot(x, w_ref[kv])
            for part in range(2):
                c0 = part * KV_W + kv * half
                out_ref[0, grp * n:(grp + 1) * n, c0:c0 + half] = y[:, part * half:(part + 1) * half]


def _pages_weight(w1bd):
    w = jnp.concatenate([w1bd[:D_CMP], w1bd[D_CMP:]], axis=3)
    return w.transpose(1, 0, 2, 3).reshape(2, D_CMP * (KV_W // 2), KV_W)


def _feature_major_pages(cache):
    return cache.transpose(0, 2, 3, 4, 5, 1).reshape(cache.shape[0], -1, cache.shape[1])


def _cmp_pages(page_table, cache, li, w1bd):
    bd, n_pages = page_table.shape
    pool = _feature_major_pages(cache)
    steps = n_pages // PAGES_PER_STEP
    rows = PAGES_PER_STEP * PAGE_SIZE // D_CMP
    w = _pages_weight(w1bd)

    def page_spec(k):
        return pl.BlockSpec((1, KV_W, PAGE_SIZE), lambda i, j, pt: (pt[i, j * PAGES_PER_STEP + k], li, 0))

    grid_spec = pltpu.PrefetchScalarGridSpec(
        num_scalar_prefetch=1,
        grid=(bd, steps),
        in_specs=[page_spec(k) for k in range(PAGES_PER_STEP)] + [pl.BlockSpec(w.shape, lambda i, j, pt: (0, 0, 0))],
        out_specs=pl.BlockSpec((1, rows, 2 * KV_W), lambda i, j, pt: (i, j, 0)),
        scratch_shapes=[pltpu.VMEM((2, PAGES_PER_STEP // PAGE_GROUPS * PAGE_SIZE, KV_W // 2), F32)] * PAGE_GROUPS,
    )
    return pl.pallas_call(
        _cmp_pages_kernel,
        grid_spec=grid_spec,
        out_shape=jax.ShapeDtypeStruct((bd, steps * rows, 2 * KV_W), F32),
        compiler_params=_cparams(("parallel", "arbitrary"), 48),
        name="cmp_pages",
    )(page_table, *([pool] * PAGES_PER_STEP), w)


def _nsa_sample_select_kernel(sl_ref, pre_ref, new_ref, q_ref, pt_ref, pe_ref, w1_ref, w2_ref, cover_ref,
                              idx_ref, phys_ref, ocmp_ref, tail_ref, score_ref, *, past):
    half = KV_W // 2

    def row_act(i, r):
        r8 = jnp.broadcast_to(r, (8, KV_W)).astype(BF16)
        return jnp.concatenate([_dot(r8[:, kv * half:(kv + 1) * half], w1_ref[i, kv]) for kv in range(2)], axis=1)

    @pl.when(pl.program_id(0) == 0)
    def _():
        t = jnp.zeros((8, KV_W), F32)
        for i in range(L_CMP):
            t = t + row_act(i, pe_ref[i:i + 1])
        tail_ref[...] = t

    n = pre_ref.shape[1]
    p1 = pltpu.roll(pre_ref[0, :, KV_W:2 * KV_W], n - 1, axis=0)
    p1 = jnp.where(_iota((n, 1), 0) == n - 1, row_act(D_CMP, new_ref[0])[0:1], p1)
    pre = pre_ref[0, :, 0:KV_W] + p1 + tail_ref[0:1]
    comp = _dot((pre * jax.nn.sigmoid(pre)).astype(BF16), w2_ref[...]).astype(BF16)

    qpos = past
    cpos = _iota((1, n), 1) * D_CMP + (L_CMP - 1)
    cmask = cpos <= qpos
    cdist = (qpos - cpos).astype(F32)
    row8 = _iota((8, 1), 0)
    imp = jnp.zeros((8, n), F32)
    for kvh in range(B_KV_HEADS):
        slope = jnp.zeros((8, 1), F32)
        for g in range(B_GROUP):
            slope = jnp.where(row8 == g, sl_ref[kvh * B_GROUP + g], slope)
        p = _masked_softmax(_dot_nt(q_ref[0, kvh], comp) - slope * cdist, cmask)
        p = jnp.where(row8 < B_GROUP, p, 0.0)
        ocmp_ref[0, kvh] = _dot(p.astype(BF16), comp)
        imp = jnp.where(row8 == kvh, jnp.sum(p, axis=0, keepdims=True), imp)

    nblk = cover_ref.shape[1]
    jb = _iota((1, nblk), 1)
    cur = qpos // L_SLC
    score = _dot3_lhs(imp, cover_ref[...])
    score = jnp.where((jb == 0) | (jb == cur) | (jb == cur - 1), SEL_FORCE, score)
    b = pl.program_id(0)
    score_ref[b] = jnp.where(jb > cur, NEG_INF, score)

    @pl.when(b == pl.num_programs(0) - 1)
    def _():
        bd = score_ref.shape[0]
        lane = _iota((1, 128), 1)
        jbf = jb.astype(F32)
        pages = jnp.broadcast_to(pt_ref[...].astype(F32), (bd, 8, 128)).reshape(bd * 8, 128)
        bpp_shift = (PAGE_SIZE // L_SLC).bit_length() - 1

        def body(it, carry):
            sc, idx, phys = carry
            m = jnp.max(sc, axis=-1, keepdims=True)
            j = jnp.min(jnp.where(sc == m, jbf, float(nblk)), axis=-1, keepdims=True)
            logical = jnp.minimum(j.astype(jnp.int32), past // L_SLC - 1) >> bpp_shift
            page = jnp.sum(jnp.where(lane == logical, pages, 0.0), axis=-1, keepdims=True)
            return (jnp.where(jbf == j, NEG_INF, sc), jnp.where(lane == it, j, idx),
                    jnp.where(lane == it, page, phys))

        zero = jnp.zeros((bd * 8, 128), F32)
        _, idx, phys = lax.fori_loop(0, N_SEL, body, (score_ref[...].reshape(bd * 8, nblk), zero, zero))
        idx_ref[...] = idx.astype(jnp.int32).reshape(bd, 8, 128)
        phys_ref[...] = phys.astype(jnp.int32).reshape(bd, 8, 128)


def _nsa_sample_select(slopes, pre, cmp_new, q8, page_table, pe_b, w1bd, w2bd, cover, past):
    bd = pre.shape[0]
    assert page_table.shape[1] == 128
    full = lambda a: pl.BlockSpec(a.shape, lambda i, sl: (0,) * a.ndim)
    picks = pl.BlockSpec((bd, 8, 128), lambda i, sl: (0, 0, 0))
    pt = page_table.reshape(bd, 1, 128)
    grid_spec = pltpu.PrefetchScalarGridSpec(
        num_scalar_prefetch=1,
        grid=(bd,),
        in_specs=[
            pl.BlockSpec((1,) + pre.shape[1:], lambda i, sl: (i, 0, 0)),
            pl.BlockSpec((1, 1, KV_W), lambda i, sl: (i, 0, 0)),
            pl.BlockSpec((1, B_KV_HEADS, 8, KV_W), lambda i, sl: (i, 0, 0, 0)),
            full(pt), full(pe_b), full(w1bd), full(w2bd), full(cover),
        ],
        out_specs=[picks, picks, pl.BlockSpec((1, B_KV_HEADS, 8, KV_W), lambda i, sl: (i, 0, 0, 0))],
        scratch_shapes=[pltpu.VMEM((8, KV_W), F32), pltpu.VMEM((bd, 8, cover.shape[1]), F32)],
    )
    return pl.pallas_call(
        functools.partial(_nsa_sample_select_kernel, past=past),
        grid_spec=grid_spec,
        out_shape=[jax.ShapeDtypeStruct((bd, 8, 128), jnp.int32)] * 2
        + [jax.ShapeDtypeStruct((bd, B_KV_HEADS, 8, KV_W), F32)],
        compiler_params=_cparams(("arbitrary",), 48),
        name="nsa_sample_select",
    )(slopes, pre, cmp_new, q8, pt, pe_b, w1bd, w2bd, cover)


def _nsa_sample_attend_kernel(pt_ref, ix_ref, sl_ref, *refs, past, nb_past):
    n_blocks = B_KV_HEADS * N_SEL
    pages = refs[:n_blocks]
    q_ref, slc_new_ref, win_new_ref, win_ref, g_ref, ocmp_ref, o_ref = refs[n_blocks:]
    b = pl.program_id(0)
    bpp = PAGE_SIZE // L_SLC
    row8 = _iota((8, 1), 0)
    page_row = _iota((1, PAGE_SIZE), 1)
    wb = win_ref.shape[2]
    win = win_ref[0].astype(BF16)
    slc_new = slc_new_ref[0]
    win_new = win_new_ref[0]
    for kvh in range(B_KV_HEADS):
        q = q_ref[0, kvh]
        qf = q.astype(F32)
        slope = jnp.zeros((8, 1), F32)
        for g in range(B_GROUP):
            slope = jnp.where(row8 == g, sl_ref[kvh * B_GROUP + g], slope)
        kv = jnp.concatenate([pages[kvh * N_SEL + n][0].astype(BF16) for n in range(N_SEL)], axis=1)
        kpos, blk, live = [], [], []
        j_max = ix_ref[b, kvh * N_SEL]
        for n in range(N_SEL):
            j = ix_ref[b, kvh * N_SEL + n]
            j_max = jnp.maximum(j_max, j)
            jc = jnp.minimum(j, nb_past - 1)
            kpos.append((jc // bpp) * PAGE_SIZE + page_row)
            blk.append(jnp.broadcast_to(jc, (1, PAGE_SIZE)))
            live.append(jnp.broadcast_to((j < nb_past).astype(jnp.int32), (1, PAGE_SIZE)))
        kpos, blk, live = [jnp.concatenate(z, axis=1) for z in (kpos, blk, live)]
        s = _dot(q, kv) - slope * (past - kpos).astype(F32)
        s = jnp.where(((kpos >> SLC_SHIFT) == blk) & (live > 0), s, NEG_INF)
        s_new = jnp.where(j_max >= nb_past, jnp.sum(qf * slc_new, axis=-1, keepdims=True), NEG_INF)
        m = jnp.maximum(jnp.max(s, axis=-1, keepdims=True), s_new)
        m = jnp.where(m > NEG_INF, m, 0.0)
        e = jnp.exp(s - m)
        e_new = jnp.exp(s_new - m)
        l = jnp.sum(e, axis=-1, keepdims=True) + e_new
        o_slc = (_dot_nt(e.astype(BF16), kv) + e_new * slc_new) * (1.0 / jnp.where(l > 0, l, 1.0))
        wpos = past - wb + _iota((1, wb), 1)
        s1 = _dot(q, win) - slope * (past - wpos).astype(F32)
        s1 = jnp.where(wpos > past - WINDOW, s1, NEG_INF)
        s2 = jnp.sum(qf * win_new, axis=-1, keepdims=True)
        mw = jnp.maximum(jnp.max(s1, axis=-1, keepdims=True), s2)
        e1 = jnp.exp(s1 - mw)
        e2 = jnp.exp(s2 - mw)
        dw = jnp.sum(e1, axis=-1, keepdims=True) + e2
        o_win = (_dot_nt(e1.astype(BF16), win) + e2 * win_new) * (1.0 / dw)
        gate = jax.nn.sigmoid(g_ref[0, kvh])
        o_ref[0, kvh] = gate[:, 0:1] * ocmp_ref[0, kvh] + gate[:, 1:2] * o_slc + gate[:, 2:3] * o_win


def _nsa_sample_attend(phys, idx, slopes, cache, li, q8, slc_new, win_new, win_state, gate8, o_cmp, past):
    bd = q8.shape[0]
    bpp = PAGE_SIZE // L_SLC
    nb_past = past // L_SLC
    n_blocks = B_KV_HEADS * N_SEL
    pool = _feature_major_pages(cache)

    def blk_spec(k):
        return pl.BlockSpec((1, KV_W, PAGE_SIZE), lambda i, ph, ix, sl: (ph[i, k], li, 0))

    row = lambda: pl.BlockSpec((1, 1, KV_W), lambda i, pt, ix, sl: (i, 0, 0))
    per_head = lambda w: pl.BlockSpec((1, B_KV_HEADS, 8, w), lambda i, pt, ix, sl: (i, 0, 0, 0))
    grid_spec = pltpu.PrefetchScalarGridSpec(
        num_scalar_prefetch=3,
        grid=(bd,),
        in_specs=[blk_spec(k) for k in range(n_blocks)] + [
            per_head(KV_W), row(), row(),
            pl.BlockSpec((1,) + win_state.shape[1:], lambda i, pt, ix, sl: (i, 0, 0)),
            per_head(128), per_head(KV_W)],
        out_specs=per_head(KV_W),
    )
    return pl.pallas_call(
        functools.partial(_nsa_sample_attend_kernel, past=past, nb_past=nb_past),
        grid_spec=grid_spec,
        out_shape=jax.ShapeDtypeStruct((bd, B_KV_HEADS, 8, KV_W), F32),
        compiler_params=_cparams(("arbitrary",), 48),
        name="nsa_sample_attend",
    )(phys, idx, slopes, *([pool] * n_blocks), q8, slc_new, win_new, win_state, gate8, o_cmp)


def _outproj_kernel(x_ref, a_ref, b_ref, w_ref, y_ref):
    na = a_ref.shape[1]
    y_ref[...] = x_ref[...] + _dot(a_ref[...], w_ref[0:na]) + _dot(b_ref[...], w_ref[na:])


def _outproj(x, oa, ob, w, tm):
    m = x.shape[0]
    row = lambda n: pl.BlockSpec((tm, n), lambda i: (i, 0))
    return pl.pallas_call(
        _outproj_kernel,
        grid=(m // tm,),
        in_specs=[row(D_MODEL), row(oa.shape[1]), row(ob.shape[1]), pl.BlockSpec(w.shape, lambda i: (0, 0))],
        out_specs=row(D_MODEL),
        out_shape=jax.ShapeDtypeStruct((m, D_MODEL), F32),
        compiler_params=_cparams(("parallel",), 32),
        name="outproj",
    )(x, oa, ob, w)


def _mixer_prompt_kernel(*refs, groups, convs, gated_ffn, final, attn, tm, tf, f):
    x_ref, nw_ref, up_ref, cw_ref, down_ref = refs[0:5]
    pos = 5
    fw_ref = refs[pos] if final else None
    pos += int(final)
    attn_refs = refs[pos:pos + 3] if attn else None
    pos += 3 * int(attn)
    y_ref, tail_ref = refs[pos], refs[pos + 1]
    act_ref, ubuf_ref, carry_ref = refs[pos + 2:]

    @pl.when(pl.program_id(1) == 0)
    def _():
        carry_ref[...] = jnp.zeros_like(carry_ref)

    x = x_ref[0]
    if attn:
        oa_ref, ob_ref, wo_ref = attn_refs
        na = oa_ref.shape[2]
        x = x + _dot(oa_ref[0], wo_ref[0:na]) + _dot(ob_ref[0], wo_ref[na:])
    xn = _rms(x, nw_ref[...]).astype(BF16)
    for j in range(f // tf):
        u = [_dot(xn, up_ref[:, g * f + j * tf:g * f + (j + 1) * tf]) for g in range(groups)]
        conv_in = [u[0], u[1]] if gated_ffn else [u[1] * u[0]]
        conv_out = []
        for c, z in enumerate(conv_in):
            cols = slice(c * f + j * tf, c * f + (j + 1) * tf)
            slot = (j % 2) * convs + c
            ubuf_ref[slot, 0:8] = carry_ref[:, cols]
            ubuf_ref[slot, 8:tm + 8] = z
            w = cw_ref[:, cols]
            conv_out.append(w[0:1] * ubuf_ref[slot, pl.ds(6, tm), :] + w[1:2] * ubuf_ref[slot, pl.ds(7, tm), :]
                            + w[2:3] * z)
            last = z[tm - 8:tm]
            carry_ref[:, cols] = last
            tail_ref[0, 0, :, cols] = last
        if gated_ffn:
            act = conv_out[0] * jax.nn.sigmoid(conv_out[0]) * conv_out[1]
        else:
            act = u[2] * conv_out[0]
        act_ref[:, j * tf:(j + 1) * tf] = act.astype(BF16)
    y = x + _dot(act_ref[...], down_ref[...])
    y_ref[0] = _rms(y, fw_ref[...]) if final else y


def _mixer_prompt(x, nw, w_up, conv_w, w_down, gated_ffn, final_w=None, attn=None, tm=512, tf=256):
    b, s, d = x.shape
    f = w_down.shape[0]
    groups = w_up.shape[1] // f
    convs = conv_w.shape[1] // f
    final = final_w is not None
    const = lambda a: pl.BlockSpec(a.shape, lambda i, t: (0,) * a.ndim, pipeline_mode=pl.Buffered(1))
    tile = lambda a: pl.BlockSpec((1, tm, a.shape[2]), lambda i, t: (i, t, 0))
    in_specs = [tile(x), const(nw), const(w_up), const(conv_w), const(w_down)]
    args = [x, nw, w_up, conv_w, w_down]
    if final:
        in_specs.append(const(final_w))
        args.append(final_w)
    if attn is not None:
        oa, ob, w_out = attn
        in_specs += [tile(oa), tile(ob), const(w_out)]
        args += [oa, ob, w_out]
    y, tail = pl.pallas_call(
        functools.partial(_mixer_prompt_kernel, groups=groups, convs=convs, gated_ffn=gated_ffn,
                          final=final, attn=attn is not None, tm=tm, tf=tf, f=f),
        grid=(b, s // tm),
        in_specs=in_specs,
        out_specs=[pl.BlockSpec((1, tm, d), lambda i, t: (i, t, 0)),
                   pl.BlockSpec((1, 1, 8, convs * f), lambda i, t: (i, t, 0, 0))],
        out_shape=[jax.ShapeDtypeStruct((b, s, d), F32), jax.ShapeDtypeStruct((b, s // tm, 8, convs * f), F32)],
        scratch_shapes=[pltpu.VMEM((tm, f), BF16), pltpu.VMEM((2 * convs, tm + 8, tf), F32),
                        pltpu.VMEM((8, convs * f), F32)],
        compiler_params=_cparams(("parallel", "arbitrary"), 56),
        name="ffn_prompt" if gated_ffn else "sconv_prompt",
    )(*args)
    return y, tail[:, -1, 8 - (CONV_W - 1):, :]


def _mixer_sample_kernel(*refs, groups, convs, gated_ffn, final):
    x_ref, nw_ref = refs[0:2]
    up = refs[2:2 + groups]
    cw = refs[2 + groups:2 + groups + convs]
    st = refs[2 + groups + convs:2 + groups + 2 * convs]
    down_ref = refs[2 + groups + 2 * convs]
    pos = 3 + groups + 2 * convs
    fw_ref = refs[pos] if final else None
    pos += int(final)
    y_ref = refs[pos]
    news = refs[pos + 1:pos + 1 + convs]
    xn_ref, acc_ref = refs[pos + 1 + convs:]
    j = pl.program_id(0)

    @pl.when(j == 0)
    def _():
        xn_ref[...] = _rms(x_ref[...], nw_ref[...]).astype(BF16)
        acc_ref[...] = jnp.zeros_like(acc_ref)

    xn = xn_ref[...]
    u = [_dot(xn, w[...]) for w in up]
    conv_in = [u[0], u[1]] if gated_ffn else [u[1] * u[0]]
    conv_out = []
    for c, z in enumerate(conv_in):
        w = cw[c][...]
        conv_out.append(w[0:1] * st[c][:, 0, :] + w[1:2] * st[c][:, 1, :] + w[2:3] * z)
        news[c][...] = z
    if gated_ffn:
        act = conv_out[0] * jax.nn.sigmoid(conv_out[0]) * conv_out[1]
    else:
        act = u[2] * conv_out[0]
    acc_ref[...] += _dot(act.astype(BF16), down_ref[...])

    @pl.when(j == pl.num_programs(0) - 1)
    def _():
        y = x_ref[...] + acc_ref[...]
        y_ref[...] = _rms(y, fw_ref[...]) if final else y


def _mixer_sample(x, nw, w_up, conv_w, w_down, state, gated_ffn, final_w=None, tf=256):
    bd, d = x.shape
    f = w_down.shape[0]
    groups = w_up.shape[1] // f
    convs = conv_w.shape[1] // f
    nf = f // tf
    final = final_w is not None
    in_specs = [pl.BlockSpec((bd, d), lambda j: (0, 0)), pl.BlockSpec(nw.shape, lambda j: (0, 0))]
    in_specs += [pl.BlockSpec((d, tf), lambda j, g=g: (0, g * nf + j)) for g in range(groups)]
    in_specs += [pl.BlockSpec((CONV_W, tf), lambda j, c=c: (0, c * nf + j)) for c in range(convs)]
    in_specs += [pl.BlockSpec((bd, CONV_W - 1, tf), lambda j, c=c: (0, 0, c * nf + j)) for c in range(convs)]
    in_specs += [pl.BlockSpec((tf, d), lambda j: (j, 0))]
    args = [x, nw] + [w_up] * groups + [conv_w] * convs + [state] * convs + [w_down]
    if final:
        in_specs.append(pl.BlockSpec(final_w.shape, lambda j: (0, 0)))
        args.append(final_w)
    out_specs = [pl.BlockSpec((bd, d), lambda j: (0, 0))]
    out_specs += [pl.BlockSpec((bd, tf), lambda j: (0, j))] * convs
    out_shape = [jax.ShapeDtypeStruct((bd, d), F32)] + [jax.ShapeDtypeStruct((bd, f), F32)] * convs
    res = pl.pallas_call(
        functools.partial(_mixer_sample_kernel, groups=groups, convs=convs, gated_ffn=gated_ffn, final=final),
        grid=(nf,),
        in_specs=in_specs,
        out_specs=out_specs,
        out_shape=out_shape,
        scratch_shapes=[pltpu.VMEM((bd, d), BF16), pltpu.VMEM((bd, d), F32)],
        compiler_params=_cparams(("arbitrary",), 32),
        name="ffn_sample" if gated_ffn else "sconv_sample",
    )(*args)
    new = jnp.concatenate(res[1:], axis=-1)
    return res[0], jnp.concatenate([state[:, 1:], new[:, None, :]], axis=1)


def _pad_in_weight(w):
    c = HG_W + QB_W + 3 * KV_W
    per = 3 * B_GROUP
    gates = [jnp.pad(w[:, c + h * per:c + (h + 1) * per], ((0, 0), (0, GATE_PAD - per)))
             for h in range(B_KV_HEADS)]
    return jnp.concatenate([w[:, :c]] + gates, axis=1).astype(BF16)


def _per_head_rows(z, width):
    bd = z.shape[0]
    z = z.reshape(bd, B_KV_HEADS, B_GROUP, -1)
    return jnp.pad(z, ((0, 0), (0, 0), (0, 8 - B_GROUP), (0, width - z.shape[-1])))


def kernel(x_prompt, x_sample, cache_cmp_kv, cache_slc_kv, state_win_kv, state_hgrn, state_sconv, state_ffn,
           page_table, norm_mix, norm_ffn, norm_final, ab_w_in, ab_w_out, hgrn_lb_logits, hgrn_norm,
           cmp_pe, cmp_w1, cmp_w2, c_w_in, c_conv, c_w_out, ffn_up, ffn_conv, ffn_down):
    b, s, d = x_prompt.shape
    bd, t, _ = x_sample.shape
    assert t == 1 and d == D_MODEL
    depth = norm_mix.shape[0]
    n_a = ab_w_in.shape[0]
    wb = state_win_kv.shape[2]
    n_pages = page_table.shape[1]
    past = n_pages * PAGE_SIZE
    assert wb == WINDOW and past % (PAGES_PER_STEP * PAGE_SIZE) == 0 and s % 1024 == 0 and s >= wb
    hh = jnp.arange(1, B_HEADS + 1, dtype=F32)
    slopes = 2.0 ** (-8.0 * hh / B_HEADS)
    lb_logits = hgrn_lb_logits.astype(F32)
    row2 = lambda v: v.reshape(1, -1).astype(F32)

    xp = x_prompt.reshape(b * s, d)
    xs = x_sample.reshape(bd, d)
    outs_p = {k: [] for k in ("cmp", "slc", "win", "hg", "sc", "ff")}
    outs_s = {k: [] for k in ("cmp", "slc", "win", "hg", "sc", "ff")}
    attn_p = None
    for l in range(depth):
        last = l == depth - 1
        if l % 2 == 0:
            la = l // 2
            w_in = _pad_in_weight(ab_w_in[la])
            w_out = ab_w_out[la].astype(BF16)
            nw = row2(norm_mix[l])
            hnw = row2(hgrn_norm[la])
            pe_b, w1bd, w2_packed = _cmp_weights(cmp_pe[la], cmp_w1[la], cmp_w2[la], packed=True)
            _, _, w2_plain = _cmp_weights(cmp_pe[la], cmp_w1[la], cmp_w2[la], packed=False)
            hg, qb, gb, kck, kcv, kct, kslt, kwt, sk, sv, wk, wv = _inproj(xp, nw, w_in, tm=512, seq=s)
            sk, sv, wk, wv = [z.reshape(B_KV_HEADS, b, s, -1) for z in (sk, sv, wk, wv)]
            oa, hst = _hgrn_prompt(hg, lb_logits, hnw, la, b, s)
            ck, cv = _compress_prompt(kck.reshape(b, s, -1), kcv.reshape(b, s, -1), pe_b, w1bd, w2_packed)
            n_cmp = s // D_CMP
            nblk = s // L_SLC
            cover = _cover_matrix(n_cmp, nblk, nblk)
            ob = _nsa_prompt(slopes, qb.reshape(b, s, QB_W), gb.reshape(b, s, -1), ck, cv, sk, sv, wk, wv, cover)
            attn_p = (oa.reshape(b, s, -1), ob, w_out)
            kv6 = lambda z, n, tt: z.reshape(n, tt, 2, B_KV_HEADS, B_HD)
            from_t = lambda z: z.reshape(b, 2, B_KV_HEADS, B_HD, -1).transpose(0, 4, 1, 2, 3)
            outs_p["cmp"].append(from_t(kct))
            outs_p["slc"].append(from_t(kslt))
            outs_p["win"].append(from_t(kwt[:, :, s - wb:]))
            outs_p["hg"].append(hst)
            hg, qb, gb, kc, ksl, kw = _inproj(xs, nw, w_in, tm=bd)
            oa, hst = _hgrn_sample(hg, state_hgrn[la], lb_logits, hnw, la)
            pre = _cmp_pages(page_table, cache_cmp_kv, la, w1bd)
            n_cmp = pre.shape[1]
            nblk = past // L_SLC + 1
            cover = _cover_matrix(n_cmp, nblk, -(-nblk // 128) * 128)
            q8 = jnp.stack([jnp.pad(_per_head_rows(qb, B_HD)[:, h], ((0, 0), (0, 0), (h * B_HD, KV_W - (h + 1) * B_HD)))
                            for h in range(B_KV_HEADS)], axis=1)
            idx, phys, o_cmp = _nsa_sample_select(slopes, pre, kc.reshape(bd, 1, KV_W), q8, page_table, pe_b, w1bd,
                                                  w2_plain, cover, past)
            picks = lambda z: z[:, :B_KV_HEADS, :N_SEL].reshape(bd, B_KV_HEADS * N_SEL)
            idx, phys = picks(idx), picks(phys)
            gate8 = _per_head_rows(jnp.concatenate([gb[:, h * GATE_PAD:h * GATE_PAD + 3 * B_GROUP]
                                                    for h in range(B_KV_HEADS)], axis=1), 128)
            win_state = state_win_kv[la].transpose(0, 2, 3, 4, 1).reshape(bd, KV_W, wb)
            o8 = _nsa_sample_attend(phys, idx, slopes, cache_slc_kv, la, q8, ksl.reshape(bd, 1, KV_W),
                                    kw.reshape(bd, 1, KV_W), win_state, gate8, o_cmp, past)
            ob = jnp.concatenate([o8[:, h, :B_GROUP, 2 * B_HD + h * B_HD:2 * B_HD + (h + 1) * B_HD]
                                  .reshape(bd, B_GROUP * B_HD) for h in range(B_KV_HEADS)], axis=1)
            xs = _outproj(xs, oa, ob.astype(BF16), w_out, tm=bd)
            outs_s["cmp"].append(kv6(kc, bd, t))
            outs_s["slc"].append(kv6(ksl, bd, t))
            outs_s["win"].append(jnp.concatenate([state_win_kv[la], kv6(kw, bd, t)], axis=1)[:, t:])
            outs_s["hg"].append(hst)
        else:
            lc = l // 2
            xp3, tail = _mixer_prompt(xp.reshape(b, s, d), row2(norm_mix[l]), c_w_in[lc].astype(BF16),
                                      c_conv[lc], c_w_out[lc].astype(BF16), gated_ffn=False)
            xp = xp3.reshape(b * s, d)
            outs_p["sc"].append(tail)
            xs, new = _mixer_sample(xs, row2(norm_mix[l]), c_w_in[lc].astype(BF16), c_conv[lc],
                                    c_w_out[lc].astype(BF16), state_sconv[lc], gated_ffn=False)
            outs_s["sc"].append(new)
        fw = row2(norm_final) if last else None
        xp3, tail = _mixer_prompt(xp.reshape(b, s, d), row2(norm_ffn[l]), ffn_up[l].astype(BF16), ffn_conv[l],
                                  ffn_down[l].astype(BF16), gated_ffn=True, final_w=fw, attn=attn_p)
        attn_p = None
        xp = xp3.reshape(b * s, d)
        outs_p["ff"].append(tail)
        xs, new = _mixer_sample(xs, row2(norm_ffn[l]), ffn_up[l].astype(BF16), ffn_conv[l],
                                ffn_down[l].astype(BF16), state_ffn[l], gated_ffn=True, final_w=fw)
        outs_s["ff"].append(new)
    y_prompt = xp.reshape(b, s, d)
    y_sample = xs.reshape(bd, t, d)
    return (y_prompt, y_sample,
            jnp.stack(outs_p["cmp"], axis=2), jnp.stack(outs_p["slc"], axis=2), jnp.stack(outs_p["win"], axis=0),
            jnp.stack(outs_p["hg"], axis=0), jnp.stack(outs_p["sc"], axis=0), jnp.stack(outs_p["ff"], axis=0),
            jnp.stack(outs_s["cmp"], axis=2), jnp.stack(outs_s["slc"], axis=2), jnp.stack(outs_s["win"], axis=0),
            jnp.stack(outs_s["hg"], axis=0), jnp.stack(outs_s["sc"], axis=0), jnp.stack(outs_s["ff"], axis=0))
```

```python
import functools

import jax
import jax.numpy as jnp
from jax import lax
from jax.experimental import pallas as pl
from jax.experimental.pallas import tpu as pltpu

F32 = jnp.float32
BF16 = jnp.bfloat16
NEG_INF = float("-inf")

D_MODEL = 1024
PAGE_SIZE = 128
A_HEADS = 4
A_DK = 128
A_DV = 128
B_HEADS = 8
B_KV_HEADS = 2
B_GROUP = B_HEADS // B_KV_HEADS
B_HD = 64
L_CMP = 32
D_CMP = 16
CMP_HID = 64
L_SLC = 64
SLC_SHIFT = 6
N_SEL = 16
WINDOW = 512
SEL_FORCE = 1e9
CONV_W = 3
EPS = 1e-6
KV_W = 2 * B_KV_HEADS * B_HD
HG_W = 4 * A_HEADS * A_DK
QB_W = B_HEADS * B_HD
GATE_PAD = 128

V7X_VMEM_BYTES = 64 * 2**20


def _cparams(sem, vmem_mb):
    assert vmem_mb * 2**20 < V7X_VMEM_BYTES
    return pltpu.CompilerParams(dimension_semantics=sem, vmem_limit_bytes=vmem_mb * 2**20)


def _dot(a, b):
    return jnp.dot(a, b, preferred_element_type=F32)


def _dot_nt(a, b):
    return lax.dot_general(a, b, (((1,), (1,)), ((), ())), preferred_element_type=F32)


def _dot_tn(a, b):
    return lax.dot_general(a, b, (((0,), (0,)), ((), ())), preferred_element_type=F32)


def _split3(x):
    hi = x.astype(BF16)
    r1 = x - hi.astype(F32)
    mid = r1.astype(BF16)
    lo = (r1 - mid.astype(F32)).astype(BF16)
    return hi, mid, lo


def _dot3_lhs(x, w):
    hi, mid, lo = _split3(x)
    return _dot(hi, w) + _dot(mid, w) + _dot(lo, w)


def _dot3_rhs(w, x):
    hi, mid, lo = _split3(x)
    return _dot(w, hi) + _dot(w, mid) + _dot(w, lo)


def _rms(x, w):
    return x * lax.rsqrt(jnp.mean(x * x, axis=-1, keepdims=True) + EPS) * w


def _masked_softmax(s, mask):
    s = jnp.where(mask, s, NEG_INF)
    m = jnp.max(s, axis=-1, keepdims=True)
    m = jnp.where(m > NEG_INF, m, 0.0)
    e = jnp.exp(s - m)
    d = jnp.sum(e, axis=-1, keepdims=True)
    return e * (1.0 / jnp.where(d > 0, d, 1.0))


def _iota(shape, dim):
    return lax.broadcasted_iota(jnp.int32, shape, dim)


def _attention_rows(kv, pos, with_blocks):
    tm = kv.shape[0]
    lane = _iota((tm, 128), 1)
    lo = lane < B_HD
    pos_cols = jnp.where(lane == B_HD, (pos >> 7).astype(F32),
                         jnp.where(lane == B_HD + 1, (pos & 127).astype(F32), 0.0))
    k_pair, v_pair = kv[:, 0:128], kv[:, 128:256]
    k_swap, v_swap = pltpu.roll(k_pair, B_HD, axis=1), pltpu.roll(v_pair, B_HD, axis=1)
    onehot = ((pos >> SLC_SHIFT) == lane).astype(BF16)
    keys, values = [], []
    for h, (k, v_lo, v_hi) in enumerate(((k_pair, v_pair, v_swap), (k_swap, v_swap, v_pair))):
        key = jnp.where(lo, k, pos_cols).astype(BF16)
        keys.append(jnp.concatenate([key, onehot], axis=1) if with_blocks else key)
        values.append(jnp.where(lo, v_lo, v_hi).astype(BF16))
    return keys, values


def _inproj_kernel(x_ref, nw_ref, w_ref, hg_ref, qb_ref, gb_ref, *refs, seq):
    xn = _rms(x_ref[...], nw_ref[...]).astype(BF16)

    def proj(c0, n):
        return _dot(xn, w_ref[:, c0:c0 + n])

    hg_ref[...] = proj(0, HG_W)
    qb_ref[...] = (proj(HG_W, QB_W) * (B_HD ** -0.5)).astype(BF16)
    c = HG_W + QB_W
    kc = proj(c, KV_W)
    ksl = proj(c + KV_W, KV_W)
    kw = proj(c + 2 * KV_W, KV_W)
    gb_ref[...] = proj(c + 3 * KV_W, B_KV_HEADS * GATE_PAD)
    if seq is None:
        kc_ref, ksl_ref, kw_ref = refs
        kc_ref[...] = kc
        ksl_ref[...] = ksl
        kw_ref[...] = kw
    else:
        kck_ref, kcv_ref, kct_ref, kslt_ref, kwt_ref, sk_ref, sv_ref, wk_ref, wv_ref = refs
        kck_ref[...] = kc[:, 0:KV_W // 2]
        kcv_ref[...] = kc[:, KV_W // 2:KV_W]
        kct_ref[0] = kc.T
        kslt_ref[0] = ksl.T
        kwt_ref[0] = kw.T
        tm = x_ref.shape[0]
        pos = (pl.program_id(0) * tm + _iota((tm, 1), 0)) & (seq - 1)
        for kv, k_ref, v_ref, with_blocks in ((ksl, sk_ref, sv_ref, True), (kw, wk_ref, wv_ref, False)):
            keys, values = _attention_rows(kv, pos, with_blocks)
            for h in range(B_KV_HEADS):
                k_ref[h] = keys[h]
                v_ref[h] = values[h]


def _inproj(x, nw, w_pad, tm, seq=None):
    m = x.shape[0]
    row = lambda n: pl.BlockSpec((tm, n), lambda i: (i, 0))
    full = lambda a: pl.BlockSpec(a.shape, lambda i: (0,) * a.ndim)
    widths = (HG_W, QB_W, B_KV_HEADS * GATE_PAD)
    dtypes = (F32, BF16, F32)
    out_specs = [row(n) for n in widths]
    out_shape = [jax.ShapeDtypeStruct((m, n), d) for n, d in zip(widths, dtypes)]
    if seq is None:
        out_specs += [row(KV_W)] * 3
        out_shape += [jax.ShapeDtypeStruct((m, KV_W), F32)] * 3
    else:
        assert seq // L_SLC == 128 and seq % tm == 0 and seq & (seq - 1) == 0
        nt = seq // tm
        out_specs += [row(KV_W // 2)] * 2
        out_shape += [jax.ShapeDtypeStruct((m, KV_W // 2), F32)] * 2
        out_specs += [pl.BlockSpec((1, KV_W, tm), lambda i: (i // nt, 0, i % nt))] * 3
        out_shape += [jax.ShapeDtypeStruct((m // seq, KV_W, seq), F32)] * 3
        for n in (256, 128, 128, 128):
            out_specs.append(pl.BlockSpec((B_KV_HEADS, tm, n), lambda i: (0, i, 0)))
            out_shape.append(jax.ShapeDtypeStruct((B_KV_HEADS, m, n), BF16))
    return pl.pallas_call(
        functools.partial(_inproj_kernel, seq=seq),
        grid=(m // tm,),
        in_specs=[row(D_MODEL), full(nw), full(w_pad)],
        out_specs=out_specs,
        out_shape=out_shape,
        compiler_params=_cparams(("parallel",), 48),
        name="inproj",
    )(x, nw, w_pad)


def _lower_bound(logit_ref, la):
    lg = logit_ref[...]
    e = jnp.exp(lg - jnp.max(lg, axis=0, keepdims=True))
    return jnp.sum(e[:la + 1], axis=0, keepdims=True) / jnp.sum(e, axis=0, keepdims=True)


def _hgrn_post(o, nw, ga):
    o = o * lax.rsqrt(jnp.mean(o * o, axis=-1, keepdims=True) + EPS)
    return o * nw * jax.nn.sigmoid(ga)


def _hgrn_prompt_kernel(hg_ref, lg_ref, nw_ref, oa_ref, st_ref, s_ref, *, la, chunk, sub):
    ci = pl.program_id(1)

    @pl.when(ci == 0)
    def _():
        s_ref[...] = jnp.zeros_like(s_ref)

    lb = _lower_bound(lg_ref, la)
    tri = (_iota((chunk, chunk), 0) >= _iota((chunk, chunk), 1)).astype(BF16)
    srow = _iota((sub, 1), 0)
    lane = _iota((sub, chunk), 1)
    for h in range(A_HEADS):
        hs = slice(h * A_DK, (h + 1) * A_DK)
        q = hg_ref[:, h * A_DK:(h + 1) * A_DK]
        fl = hg_ref[:, 512 + h * A_DK:512 + (h + 1) * A_DK]
        v = hg_ref[:, 1024 + h * A_DV:1024 + (h + 1) * A_DV]
        ga = hg_ref[:, 1536 + h * A_DV:1536 + (h + 1) * A_DV]
        lbh = lb[:, hs]
        f = lbh + (1.0 - lbh) * jax.nn.sigmoid(fl)
        a = _dot3_rhs(tri, jnp.log(f))
        k = 1.0 - f
        st = s_ref[h]
        vb = v.astype(BF16)
        o = _dot_nt((q * jnp.exp(a)).astype(BF16), st.astype(BF16))
        rows = []
        for i in range(chunk // sub):
            r0 = i * sub
            qi, ai = q[r0:r0 + sub], a[r0:r0 + sub]
            att = jnp.zeros((sub, chunk), F32)
            for s in range(sub):
                dec = jnp.exp(jnp.where(srow >= s, ai - a[r0 + s:r0 + s + 1], NEG_INF))
                col = jnp.sum(qi * dec * k[r0 + s:r0 + s + 1], axis=-1, keepdims=True)
                att = jnp.where(lane == r0 + s, col, att)
            if i > 0:
                ref_a = a[r0 - 1:r0]
                qt = qi * jnp.exp(ai - ref_a)
                kt = jnp.concatenate([k[:r0] * jnp.exp(ref_a - a[:r0]), jnp.zeros((chunk - r0, A_DK), F32)], axis=0)
                att = att + _dot_nt(qt.astype(BF16), kt.astype(BF16))
            rows.append(att)
        att = jnp.concatenate(rows, axis=0)
        o = o + _dot(att.astype(BF16), vb)
        a_end = a[chunk - 1:chunk]
        khat = k * jnp.exp(a_end - a)
        s_ref[h] = st * jnp.exp(a_end) + _dot_tn(vb, khat.astype(BF16))
        oa_ref[:, hs] = _hgrn_post(o, nw_ref[:, hs], ga).astype(BF16)

    @pl.when(ci == pl.num_programs(1) - 1)
    def _():
        for h in range(A_HEADS):
            st_ref[0, h] = s_ref[h].T


def _hgrn_prompt(hg, lb_logits, nw, la, b, s, chunk=128, sub=16):
    n = s // chunk
    full = lambda a: pl.BlockSpec(a.shape, lambda i, j: (0,) * a.ndim)
    return pl.pallas_call(
        functools.partial(_hgrn_prompt_kernel, la=la, chunk=chunk, sub=sub),
        grid=(b, n),
        in_specs=[pl.BlockSpec((chunk, HG_W), lambda i, j: (i * n + j, 0)), full(lb_logits), full(nw)],
        out_specs=[pl.BlockSpec((chunk, A_HEADS * A_DV), lambda i, j: (i * n + j, 0)),
                   pl.BlockSpec((1, A_HEADS, A_DK, A_DV), lambda i, j: (i, 0, 0, 0))],
        out_shape=[jax.ShapeDtypeStruct((b * s, A_HEADS * A_DV), BF16),
                   jax.ShapeDtypeStruct((b, A_HEADS, A_DK, A_DV), F32)],
        scratch_shapes=[pltpu.VMEM((A_HEADS, A_DV, A_DK), F32)],
        compiler_params=_cparams(("parallel", "arbitrary"), 32),
        name="hgrn_prompt",
    )(hg, lb_logits, nw)


def _hgrn_sample_kernel(hg_ref, s0_ref, lg_ref, nw_ref, oa_ref, st_ref, *, la):
    lb = _lower_bound(lg_ref, la)
    eye = (_iota((A_DK, A_DK), 0) == _iota((A_DK, A_DK), 1)).astype(F32)

    def col(r):
        return jnp.sum(eye * r, axis=1, keepdims=True)

    for h in range(A_HEADS):
        hs = slice(h * A_DK, (h + 1) * A_DK)
        q = hg_ref[0, :, h * A_DK:(h + 1) * A_DK]
        fl = hg_ref[0, :, 512 + h * A_DK:512 + (h + 1) * A_DK]
        v = hg_ref[0, :, 1024 + h * A_DV:1024 + (h + 1) * A_DV]
        ga = hg_ref[0, :, 1536 + h * A_DV:1536 + (h + 1) * A_DV]
        lbh = lb[:, hs]
        f = lbh + (1.0 - lbh) * jax.nn.sigmoid(fl)
        sn = col(f) * s0_ref[0, h] + col(1.0 - f) * v
        st_ref[0, h] = sn
        o = jnp.sum(col(q) * sn, axis=0, keepdims=True)
        oa_ref[0, :, hs] = _hgrn_post(o, nw_ref[:, hs], ga).astype(BF16)


def _hgrn_sample(hg, s0, lb_logits, nw, la):
    bd = hg.shape[0]
    full = lambda a: pl.BlockSpec(a.shape, lambda i: (0,) * a.ndim)
    st_spec = pl.BlockSpec((1, A_HEADS, A_DK, A_DV), lambda i: (i, 0, 0, 0))
    oa, st = pl.pallas_call(
        functools.partial(_hgrn_sample_kernel, la=la),
        grid=(bd,),
        in_specs=[pl.BlockSpec((1, 1, HG_W), lambda i: (i, 0, 0)), st_spec, full(lb_logits), full(nw)],
        out_specs=[pl.BlockSpec((1, 1, A_HEADS * A_DV), lambda i: (i, 0, 0)), st_spec],
        out_shape=[jax.ShapeDtypeStruct((bd, 1, A_HEADS * A_DV), BF16),
                   jax.ShapeDtypeStruct(s0.shape, F32)],
        compiler_params=_cparams(("parallel",), 16),
        name="hgrn_sample",
    )(hg.reshape(bd, 1, HG_W), s0, lb_logits, nw)
    return oa.reshape(bd, A_HEADS * A_DV), st


def _cmp_weights(pe, w1, w2, packed):
    eye = jnp.eye(B_KV_HEADS, dtype=F32)
    w1r = w1.reshape(2, L_CMP, B_HD, CMP_HID)
    w1bd = jnp.einsum("kidm,hH->ikhdHm", w1r, eye).reshape(L_CMP, 2, KV_W // 2, KV_W // 2)
    pe_b = jnp.broadcast_to(pe.transpose(1, 0, 2)[:, :, None, :], (L_CMP, 2, B_KV_HEADS, B_HD))
    if not packed:
        w2bd = jnp.einsum("kmd,kK,hH->khmKHd", w2, eye, eye).reshape(KV_W, KV_W)
        return pe_b.reshape(L_CMP, KV_W), w1bd.astype(BF16), w2bd.astype(BF16)
    half = KV_W // 2
    zero = jnp.zeros((half, KV_W), F32)
    k_blk = jnp.concatenate([w2[0], jnp.zeros_like(w2[0])], axis=1)
    v_blk = jnp.concatenate([w2[1], w2[1]], axis=1)
    w2k = jnp.concatenate([jnp.einsum("ml,hH->hmHl", k_blk, eye).reshape(half, KV_W), zero], axis=0)
    w2v = jnp.concatenate([zero, jnp.einsum("ml,hH->hmHl", v_blk, eye).reshape(half, KV_W)], axis=0)
    return pe_b.reshape(L_CMP, KV_W), w1bd.astype(BF16), (w2k.astype(BF16), w2v.astype(BF16))


def _chunk_preacts(load, n, pe_ref, w1_ref):
    half = KV_W // 2
    p0 = [jnp.zeros((n, half), F32) for _ in range(2)]
    p1 = [jnp.zeros((n, half), F32) for _ in range(2)]
    for i in range(D_CMP):
        for kv in range(2):
            x = load(i, kv)
            ls = slice(kv * half, (kv + 1) * half)
            p0[kv] = p0[kv] + _dot((x + pe_ref[i:i + 1, ls]).astype(BF16), w1_ref[i, kv])
            p1[kv] = p1[kv] + _dot((x + pe_ref[D_CMP + i:D_CMP + i + 1, ls]).astype(BF16), w1_ref[D_CMP + i, kv])
    return jnp.concatenate(p0, axis=1), jnp.concatenate(p1, axis=1)


def _with_position(k, pos, lane):
    col = lane & 127
    return jnp.where(col == B_HD, (pos >> 7).astype(F32), jnp.where(col == B_HD + 1, (pos & 127).astype(F32), k))


def _compress_kernel(kck_ref, kcv_ref, pe_ref, w1_ref, w2k_ref, w2v_ref, ck_ref, cv_ref):
    n = ck_ref.shape[1]
    p0, p1 = _chunk_preacts(lambda i, kv: (kck_ref, kcv_ref)[kv][0, pl.ds(i, n, stride=D_CMP), :], n, pe_ref, w1_ref)
    pre = p0 + pltpu.roll(p1, n - 1, axis=0)
    hid = (pre * jax.nn.sigmoid(pre)).astype(BF16)
    cpos = _iota((n, 1), 0) * D_CMP + (L_CMP - 1)
    ck_ref[0] = _with_position(_dot(hid, w2k_ref[...]), cpos, _iota((n, KV_W), 1)).astype(BF16)
    cv_ref[0] = _dot(hid, w2v_ref[...]).astype(BF16)


def _compress_prompt(kck, kcv, pe_b, w1bd, w2kv):
    b, s, _ = kck.shape
    n = s // D_CMP
    full = lambda a: pl.BlockSpec(a.shape, lambda i: (0,) * a.ndim)
    out = pl.BlockSpec((1, n, KV_W), lambda i: (i, 0, 0))
    seq = pl.BlockSpec((1, s, KV_W // 2), lambda i: (i, 0, 0))
    return pl.pallas_call(
        _compress_kernel,
        grid=(b,),
        in_specs=[seq, seq, full(pe_b), full(w1bd), full(w2kv[0]), full(w2kv[1])],
        out_specs=[out, out],
        out_shape=[jax.ShapeDtypeStruct((b, n, KV_W), BF16)] * 2,
        compiler_params=_cparams(("parallel",), 48),
        name="compress_prompt",
    )(kck, kcv, pe_b, w1bd, *w2kv)


MASK_BIAS = 2.0 ** 100

def _top_blocks_t(score_t, n_sel):
    nblk = score_t.shape[0]
    blk = _iota(score_t.shape, 0).astype(F32)

    def body(_, carry):
        sc, sel = carry
        m = jnp.max(sc, axis=0, keepdims=True)
        idx = jnp.min(jnp.where(sc == m, blk, float(nblk)), axis=0, keepdims=True)
        pick = blk == idx
        return jnp.where(pick, NEG_INF, sc), jnp.where(pick, 1.0, sel)

    _, sel = lax.fori_loop(0, n_sel, body, (score_t, jnp.zeros_like(score_t)), unroll=True)
    return sel


def _nsa_prompt_kernel(sl_ref, q_ref, g_ref, ck_ref, cv_ref, sk_ref, sv_ref, wk_ref, wv_ref, cover_ref, o_ref,
                       m_ref, l_ref, acc_ref, sa_ref, sb_ref, tiles_ref, *, tq, tk):
    kvh = pl.program_id(1)
    q0 = pl.program_id(2) * tq
    rows = B_GROUP * tq
    lane = _iota((tq, 128), 1)
    lo = lane < B_HD
    qf = q_ref[0].astype(F32)
    left = []
    for g in range(B_GROUP):
        pair = qf[:, 128 * (g // 2):128 * (g // 2) + 128]
        if g % 2:
            pair = pltpu.roll(pair, B_HD, axis=1)
        slope = sl_ref[kvh * B_GROUP + g]
        pos_cols = jnp.where(lane == B_HD, slope * 128.0, jnp.where(lane == B_HD + 1, slope, 0.0))
        left.append(jnp.where(lo, pair, pos_cols).astype(BF16))
    ql = jnp.concatenate(left, axis=0)
    qpos = q0 + (_iota((rows, 1), 0) & (tq - 1))

    nc = ck_ref.shape[1]
    cpos = _iota((1, nc), 1) * D_CMP + (L_CMP - 1)
    span = WINDOW + tq
    start = pl.multiple_of(jnp.maximum(q0 - WINDOW, 0), 128)
    wpos = start + _iota((1, span), 1)
    s_cmp = _dot_nt(ql, ck_ref[0])
    s_win = _dot_nt(ql, wk_ref[0, 0, pl.ds(start, span), :])
    p = _masked_softmax(s_cmp, cpos <= qpos)
    pw = _masked_softmax(s_win, (wpos <= qpos) & (wpos > qpos - WINDOW))
    o_cmp = _dot(p.astype(BF16), cv_ref[0])
    o_win = _dot(pw.astype(BF16), wv_ref[0, 0, pl.ds(start, span), :])
    imp = p[0:tq]
    for g in range(1, B_GROUP):
        imp = imp + p[g * tq:(g + 1) * tq]

    nblk = cover_ref.shape[1]
    jb = _iota((1, nblk), 1)
    cur = (q0 + _iota((tq, 1), 0)) >> SLC_SHIFT
    forced = (jb == 0) | (jb == cur) | (jb == cur - 1)
    score = _dot3_lhs(imp, cover_ref[...])
    score = jnp.where(forced | (jb > cur), NEG_INF, score)
    n_forced = 3
    sel = jnp.where(forced, 1.0, _top_blocks_t(score.T, min(N_SEL, nblk) - n_forced).T)

    bias = ((sel - 1.0) * MASK_BIAS).astype(BF16)
    qs = jnp.concatenate([ql, jnp.concatenate([bias] * B_GROUP, axis=0)], axis=1)
    m_ref[...] = jnp.full_like(m_ref, NEG_INF)
    l_ref[...] = jnp.zeros_like(l_ref)
    acc_ref[...] = jnp.zeros_like(acc_ref)

    def scores(kt):
        return _dot_nt(qs, sk_ref[0, 0, pl.ds(pl.multiple_of(kt * tk, tk), tk), :])

    def update(s, kt, causal):
        k0 = pl.multiple_of(kt * tk, tk)
        if causal:
            s = jnp.where(k0 + _iota((1, tk), 1) <= qpos, s, NEG_INF)
        m_old = m_ref[...]
        m_new = jnp.maximum(m_old, jnp.max(s, axis=-1, keepdims=True))
        alpha = jnp.exp(m_old - m_new)
        e = jnp.exp(s - m_new)
        l_ref[...] = alpha * l_ref[...] + jnp.sum(e, axis=-1, keepdims=True)
        acc_ref[...] = alpha * acc_ref[...] + _dot(e.astype(BF16), sv_ref[0, 0, pl.ds(k0, tk), :])
        m_ref[...] = m_new

    n_diag = q0 // tk
    bpt = tk // L_SLC
    picked = jnp.max(sel, axis=0, keepdims=True)
    n_full = jnp.int32(0)
    for kt in range(nblk * L_SLC // tk):
        in_tile = (jb >= kt * bpt) & (jb < (kt + 1) * bpt)
        wanted = (jnp.max(jnp.where(in_tile, picked, 0.0)) > 0.5) & (kt < n_diag)
        tiles_ref[n_full] = jnp.int32(kt)
        n_full = n_full + wanted.astype(jnp.int32)
    tiles_ref[n_full] = n_diag

    sa_ref[...] = scores(tiles_ref[0])

    def tile_pair(i, carry):
        sb_ref[...] = scores(tiles_ref[2 * i + 1])
        update(sa_ref[...], tiles_ref[2 * i], False)
        sa_ref[...] = scores(tiles_ref[2 * i + 2])
        update(sb_ref[...], tiles_ref[2 * i + 1], False)
        return carry

    lax.fori_loop(0, n_full // 2, tile_pair, 0)

    @pl.when(n_full % 2 == 1)
    def _():
        sb_ref[...] = scores(n_diag)
        update(sa_ref[...], tiles_ref[n_full - 1], False)
        update(sb_ref[...], n_diag, True)

    @pl.when(n_full % 2 == 0)
    def _():
        update(sa_ref[...], n_diag, True)

    o_slc = acc_ref[...] * (1.0 / l_ref[...])

    gate = jax.nn.sigmoid(g_ref[0])
    outs = []
    for g in range(B_GROUP):
        rs = slice(g * tq, (g + 1) * tq)
        outs.append(gate[:, 3 * g:3 * g + 1] * o_cmp[rs] + gate[:, 3 * g + 1:3 * g + 2] * o_slc[rs]
                    + gate[:, 3 * g + 2:3 * g + 3] * o_win[rs])
    for j in range(B_GROUP // 2):
        o_ref[0, :, 128 * j:128 * j + 128] = jnp.where(lo, outs[2 * j], outs[2 * j + 1]).astype(BF16)


def _nsa_prompt(slopes, qb, gb, ck, cv, sk, sv, wk, wv, cover, tq=128, tk=512):
    b, s, _ = qb.shape
    half = QB_W // B_KV_HEADS
    rows = B_GROUP * tq
    assert cover.shape[1] == 128 and s % tk == 0 and tk % tq == 0
    seq = lambda a: pl.BlockSpec((1, 1) + a.shape[2:], lambda i, h, t, sl: (h, i, 0, 0))
    grid_spec = pltpu.PrefetchScalarGridSpec(
        num_scalar_prefetch=1,
        grid=(b, B_KV_HEADS, s // tq),
        in_specs=[
            pl.BlockSpec((1, tq, half), lambda i, h, t, sl: (i, t, h)),
            pl.BlockSpec((1, tq, GATE_PAD), lambda i, h, t, sl: (i, t, h)),
            pl.BlockSpec((1, ck.shape[1], 128), lambda i, h, t, sl: (i, 0, h)),
            pl.BlockSpec((1, cv.shape[1], 128), lambda i, h, t, sl: (i, 0, h)),
            seq(sk), seq(sv), seq(wk), seq(wv),
            pl.BlockSpec(cover.shape, lambda i, h, t, sl: (0, 0)),
        ],
        out_specs=pl.BlockSpec((1, tq, half), lambda i, h, t, sl: (i, t, h)),
        scratch_shapes=[pltpu.VMEM((rows, 1), F32), pltpu.VMEM((rows, 1), F32), pltpu.VMEM((rows, 128), F32),
                        pltpu.VMEM((rows, tk), F32), pltpu.VMEM((rows, tk), F32),
                        pltpu.SMEM((s // tk + 1,), jnp.int32)],
    )
    return pl.pallas_call(
        functools.partial(_nsa_prompt_kernel, tq=tq, tk=tk),
        grid_spec=grid_spec,
        out_shape=jax.ShapeDtypeStruct((b, s, QB_W), BF16),
        compiler_params=_cparams(("parallel", "parallel", "arbitrary"), 56),
        name="nsa_prompt",
    )(slopes, qb, gb, ck, cv, sk, sv, wk, wv, cover)


def _cover_matrix(n_cmp, n_blk, n_pad):
    ci = jnp.arange(n_cmp)[:, None] * D_CMP
    sj = jnp.arange(n_pad)[None, :] * L_SLC
    cov = (ci < sj + L_SLC) & (ci + L_CMP > sj) & (jnp.arange(n_pad)[None, :] < n_blk)
    return cov.astype(BF16)


PAGES_PER_STEP = 64
PAGE_GROUPS = 2


def _cmp_pages_kernel(pt_ref, *refs):
    pages = refs[:PAGES_PER_STEP]
    w_ref, out_ref = refs[PAGES_PER_STEP:PAGES_PER_STEP + 2]
    row_refs = refs[PAGES_PER_STEP + 2:]
    per_group = PAGES_PER_STEP // PAGE_GROUPS
    n = per_group * PAGE_SIZE // D_CMP
    half = KV_W // 2
    for grp, rows_ref in enumerate(row_refs):
        for k, pg in enumerate(pages[grp * per_group:(grp + 1) * per_group]):
            for kv in range(2):
                tile_t = pg[0, kv * half:(kv + 1) * half, :].astype(BF16).T
                rows_ref[kv, k * PAGE_SIZE:(k + 1) * PAGE_SIZE, :] = tile_t.astype(F32)
        for kv in range(2):
            x = jnp.concatenate([rows_ref[kv, pl.ds(i, n, stride=D_CMP), :].astype(BF16) for i in range(D_CMP)],
                                axis=1)
            y = _dot(x, w_ref[kv])
            for part in range(2):
                c0 = part * KV_W + kv * half
                out_ref[0, grp * n:(grp + 1) * n, c0:c0 + half] = y[:, part * half:(part + 1) * half]


def _pages_weight(w1bd):
    w = jnp.concatenate([w1bd[:D_CMP], w1bd[D_CMP:]], axis=3)
    return w.transpose(1, 0, 2, 3).reshape(2, D_CMP * (KV_W // 2), KV_W)


def _feature_major_pages(cache):
    return cache.transpose(0, 2, 3, 4, 5, 1).reshape(cache.shape[0], -1, cache.shape[1])


def _cmp_pages(page_table, cache, li, w1bd):
    bd, n_pages = page_table.shape
    pool = _feature_major_pages(cache)
    steps = n_pages // PAGES_PER_STEP
    rows = PAGES_PER_STEP * PAGE_SIZE // D_CMP
    w = _pages_weight(w1bd)

    def page_spec(k):
        return pl.BlockSpec((1, KV_W, PAGE_SIZE), lambda i, j, pt: (pt[i, j * PAGES_PER_STEP + k], li, 0))

    grid_spec = pltpu.PrefetchScalarGridSpec(
        num_scalar_prefetch=1,
        grid=(bd, steps),
        in_specs=[page_spec(k) for k in range(PAGES_PER_STEP)] + [pl.BlockSpec(w.shape, lambda i, j, pt: (0, 0, 0))],
        out_specs=pl.BlockSpec((1, rows, 2 * KV_W), lambda i, j, pt: (i, j, 0)),
        scratch_shapes=[pltpu.VMEM((2, PAGES_PER_STEP // PAGE_GROUPS * PAGE_SIZE, KV_W // 2), F32)] * PAGE_GROUPS,
    )
    return pl.pallas_call(
        _cmp_pages_kernel,
        grid_spec=grid_spec,
        out_shape=jax.ShapeDtypeStruct((bd, steps * rows, 2 * KV_W), F32),
        compiler_params=_cparams(("parallel", "arbitrary"), 48),
        name="cmp_pages",
    )(page_table, *([pool] * PAGES_PER_STEP), w)


def _nsa_sample_select_kernel(sl_ref, pre_ref, new_ref, q_ref, pt_ref, pe_ref, w1_ref, w2_ref, cover_ref,
                              idx_ref, phys_ref, ocmp_ref, tail_ref, score_ref, *, past):
    half = KV_W // 2

    def row_act(i, r):
        r8 = jnp.broadcast_to(r, (8, KV_W)).astype(BF16)
        return jnp.concatenate([_dot(r8[:, kv * half:(kv + 1) * half], w1_ref[i, kv]) for kv in range(2)], axis=1)

    @pl.when(pl.program_id(0) == 0)
    def _():
        t = jnp.zeros((8, KV_W), F32)
        for i in range(L_CMP):
            t = t + row_act(i, pe_ref[i:i + 1])
        tail_ref[...] = t

    n = pre_ref.shape[1]
    p1 = pltpu.roll(pre_ref[0, :, KV_W:2 * KV_W], n - 1, axis=0)
    p1 = jnp.where(_iota((n, 1), 0) == n - 1, row_act(D_CMP, new_ref[0])[0:1], p1)
    pre = pre_ref[0, :, 0:KV_W] + p1 + tail_ref[0:1]
    comp = _dot((pre * jax.nn.sigmoid(pre)).astype(BF16), w2_ref[...]).astype(BF16)

    qpos = past
    cpos = _iota((1, n), 1) * D_CMP + (L_CMP - 1)
    cmask = cpos <= qpos
    cdist = (qpos - cpos).astype(F32)
    row8 = _iota((8, 1), 0)
    imp = jnp.zeros((8, n), F32)
    for kvh in range(B_KV_HEADS):
        slope = jnp.zeros((8, 1), F32)
        for g in range(B_GROUP):
            slope = jnp.where(row8 == g, sl_ref[kvh * B_GROUP + g], slope)
        p = _masked_softmax(_dot_nt(q_ref[0, kvh], comp) - slope * cdist, cmask)
        p = jnp.where(row8 < B_GROUP, p, 0.0)
        ocmp_ref[0, kvh] = _dot(p.astype(BF16), comp)
        imp = jnp.where(row8 == kvh, jnp.sum(p, axis=0, keepdims=True), imp)

    nblk = cover_ref.shape[1]
    jb = _iota((1, nblk), 1)
    cur = qpos // L_SLC
    score = _dot3_lhs(imp, cover_ref[...])
    score = jnp.where((jb == 0) | (jb == cur) | (jb == cur - 1), SEL_FORCE, score)
    b = pl.program_id(0)
    score_ref[b] = jnp.where(jb > cur, NEG_INF, score)

    @pl.when(b == pl.num_programs(0) - 1)
    def _():
        bd = score_ref.shape[0]
        lane = _iota((1, 128), 1)
        jbf = jb.astype(F32)
        pages = jnp.broadcast_to(pt_ref[...].astype(F32), (bd, 8, 128)).reshape(bd * 8, 128)
        bpp_shift = (PAGE_SIZE // L_SLC).bit_length() - 1

        def body(it, carry):
            sc, idx, phys = carry
            m = jnp.max(sc, axis=-1, keepdims=True)
            j = jnp.min(jnp.where(sc == m, jbf, float(nblk)), axis=-1, keepdims=True)
            logical = jnp.minimum(j.astype(jnp.int32), past // L_SLC - 1) >> bpp_shift
            page = jnp.sum(jnp.where(lane == logical, pages, 0.0), axis=-1, keepdims=True)
            return (jnp.where(jbf == j, NEG_INF, sc), jnp.where(lane == it, j, idx),
                    jnp.where(lane == it, page, phys))

        zero = jnp.zeros((bd * 8, 128), F32)
        _, idx, phys = lax.fori_loop(0, N_SEL, body, (score_ref[...].reshape(bd * 8, nblk), zero, zero))
        idx_ref[...] = idx.astype(jnp.int32).reshape(bd, 8, 128)
        phys_ref[...] = phys.astype(jnp.int32).reshape(bd, 8, 128)


def _nsa_sample_select(slopes, pre, cmp_new, q8, page_table, pe_b, w1bd, w2bd, cover, past):
    bd = pre.shape[0]
    assert page_table.shape[1] == 128
    full = lambda a: pl.BlockSpec(a.shape, lambda i, sl: (0,) * a.ndim)
    picks = pl.BlockSpec((bd, 8, 128), lambda i, sl: (0, 0, 0))
    pt = page_table.reshape(bd, 1, 128)
    grid_spec = pltpu.PrefetchScalarGridSpec(
        num_scalar_prefetch=1,
        grid=(bd,),
        in_specs=[
            pl.BlockSpec((1,) + pre.shape[1:], lambda i, sl: (i, 0, 0)),
            pl.BlockSpec((1, 1, KV_W), lambda i, sl: (i, 0, 0)),
            pl.BlockSpec((1, B_KV_HEADS, 8, KV_W), lambda i, sl: (i, 0, 0, 0)),
            full(pt), full(pe_b), full(w1bd), full(w2bd), full(cover),
        ],
        out_specs=[picks, picks, pl.BlockSpec((1, B_KV_HEADS, 8, KV_W), lambda i, sl: (i, 0, 0, 0))],
        scratch_shapes=[pltpu.VMEM((8, KV_W), F32), pltpu.VMEM((bd, 8, cover.shape[1]), F32)],
    )
    return pl.pallas_call(
        functools.partial(_nsa_sample_select_kernel, past=past),
        grid_spec=grid_spec,
        out_shape=[jax.ShapeDtypeStruct((bd, 8, 128), jnp.int32)] * 2
        + [jax.ShapeDtypeStruct((bd, B_KV_HEADS, 8, KV_W), F32)],
        compiler_params=_cparams(("arbitrary",), 48),
        name="nsa_sample_select",
    )(slopes, pre, cmp_new, q8, pt, pe_b, w1bd, w2bd, cover)


def _nsa_sample_attend_kernel(pt_ref, ix_ref, sl_ref, *refs, past, nb_past):
    n_blocks = B_KV_HEADS * N_SEL
    pages = refs[:n_blocks]
    q_ref, slc_new_ref, win_new_ref, win_ref, g_ref, ocmp_ref, o_ref = refs[n_blocks:]
    b = pl.program_id(0)
    bpp = PAGE_SIZE // L_SLC
    row8 = _iota((8, 1), 0)
    page_row = _iota((1, PAGE_SIZE), 1)
    wb = win_ref.shape[2]
    win = win_ref[0].astype(BF16)
    slc_new = slc_new_ref[0]
    win_new = win_new_ref[0]
    for kvh in range(B_KV_HEADS):
        q = q_ref[0, kvh]
        qf = q.astype(F32)
        slope = jnp.zeros((8, 1), F32)
        for g in range(B_GROUP):
            slope = jnp.where(row8 == g, sl_ref[kvh * B_GROUP + g], slope)
        kv = jnp.concatenate([pages[kvh * N_SEL + n][0].astype(BF16) for n in range(N_SEL)], axis=1)
        kpos, blk, live = [], [], []
        j_max = ix_ref[b, kvh * N_SEL]
        for n in range(N_SEL):
            j = ix_ref[b, kvh * N_SEL + n]
            j_max = jnp.maximum(j_max, j)
            jc = jnp.minimum(j, nb_past - 1)
            kpos.append((jc // bpp) * PAGE_SIZE + page_row)
            blk.append(jnp.broadcast_to(jc, (1, PAGE_SIZE)))
            live.append(jnp.broadcast_to((j < nb_past).astype(jnp.int32), (1, PAGE_SIZE)))
        kpos, blk, live = [jnp.concatenate(z, axis=1) for z in (kpos, blk, live)]
        s = _dot(q, kv) - slope * (past - kpos).astype(F32)
        s = jnp.where(((kpos >> SLC_SHIFT) == blk) & (live > 0), s, NEG_INF)
        s_new = jnp.where(j_max >= nb_past, jnp.sum(qf * slc_new, axis=-1, keepdims=True), NEG_INF)
        m = jnp.maximum(jnp.max(s, axis=-1, keepdims=True), s_new)
        m = jnp.where(m > NEG_INF, m, 0.0)
        e = jnp.exp(s - m)
        e_new = jnp.exp(s_new - m)
        l = jnp.sum(e, axis=-1, keepdims=True) + e_new
        o_slc = (_dot_nt(e.astype(BF16), kv) + e_new * slc_new) * (1.0 / jnp.where(l > 0, l, 1.0))
        wpos = past - wb + _iota((1, wb), 1)
        s1 = _dot(q, win) - slope * (past - wpos).astype(F32)
        s1 = jnp.where(wpos > past - WINDOW, s1, NEG_INF)
        s2 = jnp.sum(qf * win_new, axis=-1, keepdims=True)
        mw = jnp.maximum(jnp.max(s1, axis=-1, keepdims=True), s2)
        e1 = jnp.exp(s1 - mw)
        e2 = jnp.exp(s2 - mw)
        dw = jnp.sum(e1, axis=-1, keepdims=True) + e2
        o_win = (_dot_nt(e1.astype(BF16), win) + e2 * win_new) * (1.0 / dw)
        gate = jax.nn.sigmoid(g_ref[0, kvh])
        o_ref[0, kvh] = gate[:, 0:1] * ocmp_ref[0, kvh] + gate[:, 1:2] * o_slc + gate[:, 2:3] * o_win


def _nsa_sample_attend(phys, idx, slopes, cache, li, q8, slc_new, win_new, win_state, gate8, o_cmp, past):
    bd = q8.shape[0]
    bpp = PAGE_SIZE // L_SLC
    nb_past = past // L_SLC
    n_blocks = B_KV_HEADS * N_SEL
    pool = _feature_major_pages(cache)

    def blk_spec(k):
        return pl.BlockSpec((1, KV_W, PAGE_SIZE), lambda i, ph, ix, sl: (ph[i, k], li, 0))

    row = lambda: pl.BlockSpec((1, 1, KV_W), lambda i, pt, ix, sl: (i, 0, 0))
    per_head = lambda w: pl.BlockSpec((1, B_KV_HEADS, 8, w), lambda i, pt, ix, sl: (i, 0, 0, 0))
    grid_spec = pltpu.PrefetchScalarGridSpec(
        num_scalar_prefetch=3,
        grid=(bd,),
        in_specs=[blk_spec(k) for k in range(n_blocks)] + [
            per_head(KV_W), row(), row(),
            pl.BlockSpec((1,) + win_state.shape[1:], lambda i, pt, ix, sl: (i, 0, 0)),
            per_head(128), per_head(KV_W)],
        out_specs=per_head(KV_W),
    )
    return pl.pallas_call(
        functools.partial(_nsa_sample_attend_kernel, past=past, nb_past=nb_past),
        grid_spec=grid_spec,
        out_shape=jax.ShapeDtypeStruct((bd, B_KV_HEADS, 8, KV_W), F32),
        compiler_params=_cparams(("arbitrary",), 48),
        name="nsa_sample_attend",
    )(phys, idx, slopes, *([pool] * n_blocks), q8, slc_new, win_new, win_state, gate8, o_cmp)


def _outproj_kernel(x_ref, a_ref, b_ref, w_ref, y_ref):
    na = a_ref.shape[1]
    y_ref[...] = x_ref[...] + _dot(a_ref[...], w_ref[0:na]) + _dot(b_ref[...], w_ref[na:])


def _outproj(x, oa, ob, w, tm):
    m = x.shape[0]
    row = lambda n: pl.BlockSpec((tm, n), lambda i: (i, 0))
    return pl.pallas_call(
        _outproj_kernel,
        grid=(m // tm,),
        in_specs=[row(D_MODEL), row(oa.shape[1]), row(ob.shape[1]), pl.BlockSpec(w.shape, lambda i: (0, 0))],
        out_specs=row(D_MODEL),
        out_shape=jax.ShapeDtypeStruct((m, D_MODEL), F32),
        compiler_params=_cparams(("parallel",), 32),
        name="outproj",
    )(x, oa, ob, w)


def _mixer_prompt_kernel(*refs, groups, convs, gated_ffn, final, attn, tm, tf, f):
    x_ref, nw_ref, up_ref, cw_ref, down_ref = refs[0:5]
    pos = 5
    fw_ref = refs[pos] if final else None
    pos += int(final)
    attn_refs = refs[pos:pos + 3] if attn else None
    pos += 3 * int(attn)
    y_ref, tail_ref = refs[pos], refs[pos + 1]
    act_ref, ubuf_ref, carry_ref = refs[pos + 2:]

    @pl.when(pl.program_id(1) == 0)
    def _():
        carry_ref[...] = jnp.zeros_like(carry_ref)

    x = x_ref[0]
    if attn:
        oa_ref, ob_ref, wo_ref = attn_refs
        na = oa_ref.shape[2]
        x = x + _dot(oa_ref[0], wo_ref[0:na]) + _dot(ob_ref[0], wo_ref[na:])
    xn = _rms(x, nw_ref[...]).astype(BF16)
    for j in range(f // tf):
        u = [_dot(xn, up_ref[:, g * f + j * tf:g * f + (j + 1) * tf]) for g in range(groups)]
        conv_in = [u[0], u[1]] if gated_ffn else [u[1] * u[0]]
        conv_out = []
        for c, z in enumerate(conv_in):
            cols = slice(c * f + j * tf, c * f + (j + 1) * tf)
            slot = (j % 2) * convs + c
            ubuf_ref[slot, 0:8] = carry_ref[:, cols]
            ubuf_ref[slot, 8:tm + 8] = z
            w = cw_ref[:, cols]
            conv_out.append(w[0:1] * ubuf_ref[slot, pl.ds(6, tm), :] + w[1:2] * ubuf_ref[slot, pl.ds(7, tm), :]
                            + w[2:3] * z)
            last = z[tm - 8:tm]
            carry_ref[:, cols] = last
            tail_ref[0, 0, :, cols] = last
        if gated_ffn:
            act = conv_out[0] * jax.nn.sigmoid(conv_out[0]) * conv_out[1]
        else:
            act = u[2] * conv_out[0]
        act_ref[:, j * tf:(j + 1) * tf] = act.astype(BF16)
    y = x + _dot(act_ref[...], down_ref[...])
    y_ref[0] = _rms(y, fw_ref[...]) if final else y


def _mixer_prompt(x, nw, w_up, conv_w, w_down, gated_ffn, final_w=None, attn=None, tm=512, tf=256):
    b, s, d = x.shape
    f = w_down.shape[0]
    groups = w_up.shape[1] // f
    convs = conv_w.shape[1] // f
    final = final_w is not None
    const = lambda a: pl.BlockSpec(a.shape, lambda i, t: (0,) * a.ndim, pipeline_mode=pl.Buffered(1))
    tile = lambda a: pl.BlockSpec((1, tm, a.shape[2]), lambda i, t: (i, t, 0))
    in_specs = [tile(x), const(nw), const(w_up), const(conv_w), const(w_down)]
    args = [x, nw, w_up, conv_w, w_down]
    if final:
        in_specs.append(const(final_w))
        args.append(final_w)
    if attn is not None:
        oa, ob, w_out = attn
        in_specs += [tile(oa), tile(ob), const(w_out)]
        args += [oa, ob, w_out]
    y, tail = pl.pallas_call(
        functools.partial(_mixer_prompt_kernel, groups=groups, convs=convs, gated_ffn=gated_ffn,
                          final=final, attn=attn is not None, tm=tm, tf=tf, f=f),
        grid=(b, s // tm),
        in_specs=in_specs,
        out_specs=[pl.BlockSpec((1, tm, d), lambda i, t: (i, t, 0)),
                   pl.BlockSpec((1, 1, 8, convs * f), lambda i, t: (i, t, 0, 0))],
        out_shape=[jax.ShapeDtypeStruct((b, s, d), F32), jax.ShapeDtypeStruct((b, s // tm, 8, convs * f), F32)],
        scratch_shapes=[pltpu.VMEM((tm, f), BF16), pltpu.VMEM((2 * convs, tm + 8, tf), F32),
                        pltpu.VMEM((8, convs * f), F32)],
        compiler_params=_cparams(("parallel", "arbitrary"), 56),
        name="ffn_prompt" if gated_ffn else "sconv_prompt",
    )(*args)
    return y, tail[:, -1, 8 - (CONV_W - 1):, :]


def _mixer_sample_kernel(*refs, groups, convs, gated_ffn, final):
    x_ref, nw_ref = refs[0:2]
    up = refs[2:2 + groups]
    cw = refs[2 + groups:2 + groups + convs]
    st = refs[2 + groups + convs:2 + groups + 2 * convs]
    down_ref = refs[2 + groups + 2 * convs]
    pos = 3 + groups + 2 * convs
    fw_ref = refs[pos] if final else None
    pos += int(final)
    y_ref = refs[pos]
    news = refs[pos + 1:pos + 1 + convs]
    xn_ref, acc_ref = refs[pos + 1 + convs:]
    j = pl.program_id(0)

    @pl.when(j == 0)
    def _():
        xn_ref[...] = _rms(x_ref[...], nw_ref[...]).astype(BF16)
        acc_ref[...] = jnp.zeros_like(acc_ref)

    xn = xn_ref[...]
    u = [_dot(xn, w[...]) for w in up]
    conv_in = [u[0], u[1]] if gated_ffn else [u[1] * u[0]]
    conv_out = []
    for c, z in enumerate(conv_in):
        w = cw[c][...]
        conv_out.append(w[0:1] * st[c][:, 0, :] + w[1:2] * st[c][:, 1, :] + w[2:3] * z)
        news[c][...] = z
    if gated_ffn:
        act = conv_out[0] * jax.nn.sigmoid(conv_out[0]) * conv_out[1]
    else:
        act = u[2] * conv_out[0]
    acc_ref[...] += _dot(act.astype(BF16), down_ref[...])

    @pl.when(j == pl.num_programs(0) - 1)
    def _():
        y = x_ref[...] + acc_ref[...]
        y_ref[...] = _rms(y, fw_ref[...]) if final else y


def _mixer_sample(x, nw, w_up, conv_w, w_down, state, gated_ffn, final_w=None, tf=256):
    bd, d = x.shape
    f = w_down.shape[0]
    groups = w_up.shape[1] // f
    convs = conv_w.shape[1] // f
    nf = f // tf
    final = final_w is not None
    in_specs = [pl.BlockSpec((bd, d), lambda j: (0, 0)), pl.BlockSpec(nw.shape, lambda j: (0, 0))]
    in_specs += [pl.BlockSpec((d, tf), lambda j, g=g: (0, g * nf + j)) for g in range(groups)]
    in_specs += [pl.BlockSpec((CONV_W, tf), lambda j, c=c: (0, c * nf + j)) for c in range(convs)]
    in_specs += [pl.BlockSpec((bd, CONV_W - 1, tf), lambda j, c=c: (0, 0, c * nf + j)) for c in range(convs)]
    in_specs += [pl.BlockSpec((tf, d), lambda j: (j, 0))]
    args = [x, nw] + [w_up] * groups + [conv_w] * convs + [state] * convs + [w_down]
    if final:
        in_specs.append(pl.BlockSpec(final_w.shape, lambda j: (0, 0)))
        args.append(final_w)
    out_specs = [pl.BlockSpec((bd, d), lambda j: (0, 0))]
    out_specs += [pl.BlockSpec((bd, tf), lambda j: (0, j))] * convs
    out_shape = [jax.ShapeDtypeStruct((bd, d), F32)] + [jax.ShapeDtypeStruct((bd, f), F32)] * convs
    res = pl.pallas_call(
        functools.partial(_mixer_sample_kernel, groups=groups, convs=convs, gated_ffn=gated_ffn, final=final),
        grid=(nf,),
        in_specs=in_specs,
        out_specs=out_specs,
        out_shape=out_shape,
        scratch_shapes=[pltpu.VMEM((bd, d), BF16), pltpu.VMEM((bd, d), F32)],
        compiler_params=_cparams(("arbitrary",), 32),
        name="ffn_sample" if gated_ffn else "sconv_sample",
    )(*args)
    new = jnp.concatenate(res[1:], axis=-1)
    return res[0], jnp.concatenate([state[:, 1:], new[:, None, :]], axis=1)


def _pad_in_weight(w):
    c = HG_W + QB_W + 3 * KV_W
    per = 3 * B_GROUP
    gates = [jnp.pad(w[:, c + h * per:c + (h + 1) * per], ((0, 0), (0, GATE_PAD - per)))
             for h in range(B_KV_HEADS)]
    return jnp.concatenate([w[:, :c]] + gates, axis=1).astype(BF16)


def _per_head_rows(z, width):
    bd = z.shape[0]
    z = z.reshape(bd, B_KV_HEADS, B_GROUP, -1)
    return jnp.pad(z, ((0, 0), (0, 0), (0, 8 - B_GROUP), (0, width - z.shape[-1])))


def kernel(x_prompt, x_sample, cache_cmp_kv, cache_slc_kv, state_win_kv, state_hgrn, state_sconv, state_ffn,
           page_table, norm_mix, norm_ffn, norm_final, ab_w_in, ab_w_out, hgrn_lb_logits, hgrn_norm,
           cmp_pe, cmp_w1, cmp_w2, c_w_in, c_conv, c_w_out, ffn_up, ffn_conv, ffn_down):
    b, s, d = x_prompt.shape
    bd, t, _ = x_sample.shape
    assert t == 1 and d == D_MODEL
    depth = norm_mix.shape[0]
    n_a = ab_w_in.shape[0]
    wb = state_win_kv.shape[2]
    n_pages = page_table.shape[1]
    past = n_pages * PAGE_SIZE
    assert wb == WINDOW and past % (PAGES_PER_STEP * PAGE_SIZE) == 0 and s % 1024 == 0 and s >= wb
    hh = jnp.arange(1, B_HEADS + 1, dtype=F32)
    slopes = 2.0 ** (-8.0 * hh / B_HEADS)
    lb_logits = hgrn_lb_logits.astype(F32)
    row2 = lambda v: v.reshape(1, -1).astype(F32)

    xp = x_prompt.reshape(b * s, d)
    xs = x_sample.reshape(bd, d)
    outs_p = {k: [] for k in ("cmp", "slc", "win", "hg", "sc", "ff")}
    outs_s = {k: [] for k in ("cmp", "slc", "win", "hg", "sc", "ff")}
    attn_p = None
    for l in range(depth):
        last = l == depth - 1
        if l % 2 == 0:
            la = l // 2
            w_in = _pad_in_weight(ab_w_in[la])
            w_out = ab_w_out[la].astype(BF16)
            nw = row2(norm_mix[l])
            hnw = row2(hgrn_norm[la])
            pe_b, w1bd, w2_packed = _cmp_weights(cmp_pe[la], cmp_w1[la], cmp_w2[la], packed=True)
            _, _, w2_plain = _cmp_weights(cmp_pe[la], cmp_w1[la], cmp_w2[la], packed=False)
            hg, qb, gb, kck, kcv, kct, kslt, kwt, sk, sv, wk, wv = _inproj(xp, nw, w_in, tm=512, seq=s)
            sk, sv, wk, wv = [z.reshape(B_KV_HEADS, b, s, -1) for z in (sk, sv, wk, wv)]
            oa, hst = _hgrn_prompt(hg, lb_logits, hnw, la, b, s)
            ck, cv = _compress_prompt(kck.reshape(b, s, -1), kcv.reshape(b, s, -1), pe_b, w1bd, w2_packed)
            n_cmp = s // D_CMP
            nblk = s // L_SLC
            cover = _cover_matrix(n_cmp, nblk, nblk)
            ob = _nsa_prompt(slopes, qb.reshape(b, s, QB_W), gb.reshape(b, s, -1), ck, cv, sk, sv, wk, wv, cover)
            attn_p = (oa.reshape(b, s, -1), ob, w_out)
            kv6 = lambda z, n, tt: z.reshape(n, tt, 2, B_KV_HEADS, B_HD)
            from_t = lambda z: z.reshape(b, 2, B_KV_HEADS, B_HD, -1).transpose(0, 4, 1, 2, 3)
            outs_p["cmp"].append(from_t(kct))
            outs_p["slc"].append(from_t(kslt))
            outs_p["win"].append(from_t(kwt[:, :, s - wb:]))
            outs_p["hg"].append(hst)
            hg, qb, gb, kc, ksl, kw = _inproj(xs, nw, w_in, tm=bd)
            oa, hst = _hgrn_sample(hg, state_hgrn[la], lb_logits, hnw, la)
            pre = _cmp_pages(page_table, cache_cmp_kv, la, w1bd)
            n_cmp = pre.shape[1]
            nblk = past // L_SLC + 1
            cover = _cover_matrix(n_cmp, nblk, -(-nblk // 128) * 128)
            q8 = jnp.stack([jnp.pad(_per_head_rows(qb, B_HD)[:, h], ((0, 0), (0, 0), (h * B_HD, KV_W - (h + 1) * B_HD)))
                            for h in range(B_KV_HEADS)], axis=1)
            idx, phys, o_cmp = _nsa_sample_select(slopes, pre, kc.reshape(bd, 1, KV_W), q8, page_table, pe_b, w1bd,
                                                  w2_plain, cover, past)
            picks = lambda z: z[:, :B_KV_HEADS, :N_SEL].reshape(bd, B_KV_HEADS * N_SEL)
            idx, phys = picks(idx), picks(phys)
            gate8 = _per_head_rows(jnp.concatenate([gb[:, h * GATE_PAD:h * GATE_PAD + 3 * B_GROUP]
                                                    for h in range(B_KV_HEADS)], axis=1), 128)
            win_state = state_win_kv[la].transpose(0, 2, 3, 4, 1).reshape(bd, KV_W, wb)
            o8 = _nsa_sample_attend(phys, idx, slopes, cache_slc_kv, la, q8, ksl.reshape(bd, 1, KV_W),
                                    kw.reshape(bd, 1, KV_W), win_state, gate8, o_cmp, past)
            ob = jnp.concatenate([o8[:, h, :B_GROUP, 2 * B_HD + h * B_HD:2 * B_HD + (h + 1) * B_HD]
                                  .reshape(bd, B_GROUP * B_HD) for h in range(B_KV_HEADS)], axis=1)
            xs = _outproj(xs, oa, ob.astype(BF16), w_out, tm=bd)
            outs_s["cmp"].append(kv6(kc, bd, t))
            outs_s["slc"].append(kv6(ksl, bd, t))
            outs_s["win"].append(jnp.concatenate([state_win_kv[la], kv6(kw, bd, t)], axis=1)[:, t:])
            outs_s["hg"].append(hst)
        else:
            lc = l // 2
            xp3, tail = _mixer_prompt(xp.reshape(b, s, d), row2(norm_mix[l]), c_w_in[lc].astype(BF16),
                                      c_conv[lc], c_w_out[lc].astype(BF16), gated_ffn=False)
            xp = xp3.reshape(b * s, d)
            outs_p["sc"].append(tail)
            xs, new = _mixer_sample(xs, row2(norm_mix[l]), c_w_in[lc].astype(BF16), c_conv[lc],
                                    c_w_out[lc].astype(BF16), state_sconv[lc], gated_ffn=False)
            outs_s["sc"].append(new)
        fw = row2(norm_final) if last else None
        xp3, tail = _mixer_prompt(xp.reshape(b, s, d), row2(norm_ffn[l]), ffn_up[l].astype(BF16), ffn_conv[l],
                                  ffn_down[l].astype(BF16), gated_ffn=True, final_w=fw, attn=attn_p)
        attn_p = None
        xp = xp3.reshape(b * s, d)
        outs_p["ff"].append(tail)
        xs, new = _mixer_sample(xs, row2(norm_ffn[l]), ffn_up[l].astype(BF16), ffn_conv[l],
                                ffn_down[l].astype(BF16), state_ffn[l], gated_ffn=True, final_w=fw)
        outs_s["ff"].append(new)
    y_prompt = xp.reshape(b, s, d)
    y_sample = xs.reshape(bd, t, d)
    return (y_prompt, y_sample,
            jnp.stack(outs_p["cmp"], axis=2), jnp.stack(outs_p["slc"], axis=2), jnp.stack(outs_p["win"], axis=0),
            jnp.stack(outs_p["hg"], axis=0), jnp.stack(outs_p["sc"], axis=0), jnp.stack(outs_p["ff"], axis=0),
            jnp.stack(outs_s["cmp"], axis=2), jnp.stack(outs_s["slc"], axis=2), jnp.stack(outs_s["win"], axis=0),
            jnp.stack(outs_s["hg"], axis=0), jnp.stack(outs_s["sc"], axis=0), jnp.stack(outs_s["ff"], axis=0))
```
